```python
import math
import jax, jax.numpy as jnp
from jax import lax
import numpy as np

D_MODEL = 1024
BATCH = 8
SEQ = 2048
DEPTH = 2
DEC_BATCH = 128
DEC_SEQ = 8
PAST_LEN = 16384
PAGE_SIZE = 128

N_EVEN = (DEPTH + 1) // 2
N_ODD = DEPTH // 2
H_A = 4
DK_A = 128
DV_A = 128
CONV_A = 4
W_QK_A = H_A * DK_A
W_V_A = H_A * DV_A
D_B = D_MODEL // 2
CONV_B = 3
PROJ_A = 2 * W_QK_A + 2 * W_V_A + 2 * H_A + 3 * D_B
H_C = 4
DK_C = 256
DV_C = 512
PROJ_C = 2 * H_C * DK_C + 2 * H_C * DV_C
ROPE_BASE = 10000.0
CHUNK = 64
N_MEM = 256
H_X = 4
HD_X = D_MODEL // H_X
N_GROUPS = 4
E_PER_GROUP = 8
N_EXPERTS = N_GROUPS * E_PER_GROUP
TOP_K = 2
D_EXPERT = 512
MOE_BLOCK = 128
EPS = 1e-6

kernel_name = "hybrid_gdn_shortconv_retention_hmoe_step"


def rmsnorm(x, g):
    xf = x.astype(jnp.float32)
    y = xf * lax.rsqrt(jnp.mean(xf * xf, axis=-1, keepdims=True) + EPS)
    return (y * g.astype(jnp.float32)).astype(x.dtype)


def l2norm(x):
    xf = x.astype(jnp.float32)
    return xf * lax.rsqrt(jnp.sum(xf * xf, axis=-1, keepdims=True) + EPS)


def split_cols(x, sizes):
    cuts = [int(c) for c in np.cumsum(sizes)[:-1]]
    return jnp.split(x, cuts, axis=-1)


def causal_dwconv(x, buf, w):
    width = w.shape[0]
    L = x.shape[1]
    xp = jnp.concatenate([buf.astype(x.dtype), x], axis=1)
    y = xp[:, 0:L] * w[0]
    for j in range(1, width):
        y = y + xp[:, j:j + L] * w[j]
    return y, xp[:, L:]


def chunk_len(L):
    return math.gcd(L, CHUNK)


def to_chunks(t, n, c):
    B, L, H = t.shape[:3]
    rest = t.shape[3:]
    t = t.reshape((B, n, c, H) + rest)
    return t.transpose((1, 0, 3, 2) + tuple(range(4, t.ndim)))


def from_chunks(t):
    N, B, H, C, D = t.shape
    return t.transpose(1, 0, 3, 2, 4).reshape(B, N * C, H, D)


def gated_delta_rule(q, k, v, g, beta, S0):
    f32 = jnp.float32
    L = q.shape[1]
    c = chunk_len(L)
    n = L // c
    q, k, v = [to_chunks(t.astype(f32), n, c) for t in (q, k, v)]
    g = to_chunks(g.astype(f32), n, c)
    beta = to_chunks(beta.astype(f32), n, c)
    gc = jnp.cumsum(g, axis=-1)
    idx = jnp.arange(c)
    incl = idx[:, None] >= idx[None, :]
    strict = idx[:, None] > idx[None, :]
    decay = jnp.exp(jnp.where(incl, gc[..., :, None] - gc[..., None, :], -jnp.inf))
    k_beta = k * beta[..., None]
    m = jnp.where(strict, jnp.einsum('nbhid,nbhjd->nbhij', k_beta, k) * decay, 0.0)
    u = lax.linalg.triangular_solve(m, v * beta[..., None], left_side=True, lower=True, unit_diagonal=True)
    w = lax.linalg.triangular_solve(m, k_beta * jnp.exp(gc)[..., None], left_side=True, lower=True, unit_diagonal=True)
    qk = jnp.einsum('nbhid,nbhjd->nbhij', q, k) * decay
    q_dec = q * jnp.exp(gc)[..., None]
    k_dec = k * jnp.exp(gc[..., -1:] - gc)[..., None]
    g_last = jnp.exp(gc[..., -1])

    def step(S, xs):
        u_n, w_n, qk_n, qd_n, kd_n, gl_n = xs
        v_new = u_n - jnp.einsum('bhik,bhkv->bhiv', w_n, S)
        o = jnp.einsum('bhik,bhkv->bhiv', qd_n, S) + jnp.einsum('bhij,bhjv->bhiv', qk_n, v_new)
        S = S * gl_n[..., None, None] + jnp.einsum('bhjk,bhjv->bhkv', kd_n, v_new)
        return S, o

    S, o = lax.scan(step, S0.astype(f32), (u, w, qk, q_dec, k_dec, g_last))
    return from_chunks(o), S


def retention(q, k, v, log_gamma, R0):
    f32 = jnp.float32
    L = q.shape[1]
    c = chunk_len(L)
    n = L // c
    q, k, v = [to_chunks(t.astype(f32), n, c) for t in (q, k, v)]
    i = jnp.arange(c, dtype=f32)
    incl = i[:, None] >= i[None, :]
    lg = log_gamma[:, None]
    dmat = jnp.exp(jnp.where(incl[None], (i[:, None] - i[None, :])[None] * lg[..., None], -jnp.inf))
    qk = jnp.einsum('nbhid,nbhjd->nbhij', q, k) * dmat
    q_dec = q * jnp.exp((i + 1.0)[None] * lg)[..., None]
    k_dec = k * jnp.exp((c - 1.0 - i)[None] * lg)[..., None]
    chunk_dec = jnp.exp(c * lg)[..., None]

    def step(R, xs):
        qk_n, qd_n, kd_n, v_n = xs
        o = jnp.einsum('bhij,bhjv->bhiv', qk_n, v_n) + jnp.einsum('bhik,bhkv->bhiv', qd_n, R)
        R = R * chunk_dec + jnp.einsum('bhjk,bhjv->bhkv', kd_n, v_n)
        return R, o

    R, o = lax.scan(step, R0.astype(f32), (qk, q_dec, k_dec, v))
    return from_chunks(o), R


def rotary(x, pos):
    half = x.shape[-1] // 2
    inv = ROPE_BASE ** (-jnp.arange(half, dtype=jnp.float32) / half)
    ang = pos.astype(jnp.float32)[:, None] * inv[None, :]
    cos = jnp.cos(ang)[None, :, None, :]
    sin = jnp.sin(ang)[None, :, None, :]
    xf = x.astype(jnp.float32)
    x1, x2 = xf[..., :half], xf[..., half:]
    return jnp.concatenate([x1 * cos - x2 * sin, x1 * sin + x2 * cos], axis=-1).astype(x.dtype)


def mixer_ab(h, S0, conv_buf, sc_buf, w_in, w_conv_qkv, a_log, dt_bias, gdn_norm, w_conv_sc, w_out):
    B, L, _ = h.shape
    p = h @ w_in
    qkv, z, b, a, sc_h, sc_b, sc_c = split_cols(p, [2 * W_QK_A + W_V_A, W_V_A, H_A, H_A, D_B, D_B, D_B])
    qkv, conv_buf_new = causal_dwconv(qkv, conv_buf, w_conv_qkv)
    qkv = jax.nn.silu(qkv)
    q, k, v = split_cols(qkv, [W_QK_A, W_QK_A, W_V_A])
    q = l2norm(q.reshape(B, L, H_A, DK_A)) * (DK_A ** -0.5)
    k = l2norm(k.reshape(B, L, H_A, DK_A))
    v = v.reshape(B, L, H_A, DV_A)
    beta = jax.nn.sigmoid(b.astype(jnp.float32))
    g = -jnp.exp(a_log.astype(jnp.float32)) * jax.nn.softplus(a.astype(jnp.float32) + dt_bias.astype(jnp.float32))
    o, S_new = gated_delta_rule(q, k, v, g, beta, S0)
    o = rmsnorm(o, gdn_norm) * jax.nn.silu(z.reshape(B, L, H_A, DV_A).astype(jnp.float32))
    o = o.reshape(B, L, W_V_A).astype(h.dtype)
    conv_out, sc_buf_new = causal_dwconv(sc_c * sc_h, sc_buf, w_conv_sc)
    y_b = sc_b * conv_out
    y = jnp.concatenate([o, y_b], axis=-1) @ w_out
    return y, S_new, conv_buf_new, sc_buf_new


def mixer_c(h, pos, R0, w_in, ret_norm, w_out):
    B, L, _ = h.shape
    p = h @ w_in
    q, k, v, gate = split_cols(p, [H_C * DK_C, H_C * DK_C, H_C * DV_C, H_C * DV_C])
    q = rotary(q.reshape(B, L, H_C, DK_C), pos)
    k = rotary(k.reshape(B, L, H_C, DK_C), pos) * (DK_C ** -0.5)
    v = v.reshape(B, L, H_C, DV_C)
    log_gamma = jnp.log(1.0 - 2.0 ** (-5.0 - jnp.arange(H_C, dtype=jnp.float32)))
    o, R_new = retention(q, k, v, log_gamma, R0)
    o = o * lax.rsqrt(jnp.mean(o * o, axis=-1, keepdims=True) + EPS)
    o = o.reshape(B, L, H_C * DV_C) * ret_norm.astype(jnp.float32)
    y = (jax.nn.silu(gate.astype(jnp.float32)) * o).astype(h.dtype) @ w_out
    return y, R_new


def cross_attn(h, mk, mv, w_q, w_o):
    B, L, _ = h.shape
    q = (h @ w_q).reshape(B, L, H_X, HD_X)
    s = jnp.einsum('blhd,bmhd->bhlm', q, mk, preferred_element_type=jnp.float32) * (HD_X ** -0.5)
    p = jax.nn.softmax(s, axis=-1).astype(h.dtype)
    o = jnp.einsum('bhlm,bmhd->blhd', p, mv).reshape(B, L, H_X * HD_X)
    return o @ w_o


def moe(h, w_group, b_group, w_router, b_router, w_gate, w_up, w_down):
    B, L, D = h.shape
    T = B * L
    xf = h.reshape(T, D)
    gl = (xf @ w_group).astype(jnp.float32) + b_group.astype(jnp.float32)
    gp = jax.nn.softmax(gl, axis=-1)
    grp = jnp.argmax(gl, axis=-1)
    p_grp = jnp.take_along_axis(gp, grp[:, None], axis=1)[:, 0]
    el = (xf @ w_router).astype(jnp.float32) + b_router.astype(jnp.float32)
    el = el.reshape(T, N_GROUPS, E_PER_GROUP)
    el_g = jnp.take_along_axis(el, grp[:, None, None], axis=1)[:, 0]
    top_w, top_i = lax.top_k(jax.nn.softmax(el_g, axis=-1), TOP_K)
    top_w = top_w / jnp.sum(top_w, axis=-1, keepdims=True)
    wts = (p_grp[:, None] * top_w).reshape(-1)
    eid = (grp[:, None] * E_PER_GROUP + top_i).reshape(-1).astype(jnp.int32)
    A = T * TOP_K
    tok = jnp.repeat(jnp.arange(T, dtype=jnp.int32), TOP_K)
    onehot = (eid[:, None] == jnp.arange(N_EXPERTS, dtype=jnp.int32)[None, :]).astype(jnp.int32)
    counts = jnp.sum(onehot, axis=0)
    rank = jnp.take_along_axis(jnp.cumsum(onehot, axis=0), eid[:, None], axis=1)[:, 0] - 1
    padded = (counts + MOE_BLOCK - 1) // MOE_BLOCK * MOE_BLOCK
    pad_end = jnp.cumsum(padded)
    pad_start = pad_end - padded
    dest = pad_start[eid] + rank
    n_blocks = -(-A // MOE_BLOCK) + N_EXPERTS
    x_disp = jnp.zeros((n_blocks * MOE_BLOCK, D), h.dtype).at[dest].set(xf[tok])
    block_start = jnp.arange(n_blocks, dtype=jnp.int32) * MOE_BLOCK
    block_expert = jnp.minimum(jnp.sum(block_start[:, None] >= pad_end[None, :], axis=1), N_EXPERTS - 1)

    def run_block(args):
        xb, e = args
        return (jax.nn.silu(xb @ w_gate[e]) * (xb @ w_up[e])) @ w_down[e]

    y_disp = lax.map(run_block, (x_disp.reshape(n_blocks, MOE_BLOCK, D), block_expert)).reshape(-1, D)
    y = jnp.zeros((T, D), jnp.float32).at[tok].add(y_disp[dest].astype(jnp.float32) * wts[:, None])
    return y.reshape(B, L, D).astype(h.dtype)


def trunk(x, pos, st_gdn, st_conv, st_sc, st_ret, mem_k, mem_v, params):
    (norm_mix, norm_x, norm_ffn, norm_final, w_in_a, w_conv_qkv, a_log, dt_bias, gdn_norm,
     w_conv_sc, w_out_a, w_in_c, ret_norm, w_out_c, w_xq, w_xo, w_group, b_group,
     w_router, b_router, w_exp_gate, w_exp_up, w_exp_down) = params
    new_gdn, new_conv, new_sc, new_ret = [], [], [], []
    for layer in range(DEPTH):
        h = rmsnorm(x, norm_mix[layer])
        if layer % 2 == 0:
            i = layer // 2
            y, s, cb, sb = mixer_ab(h, st_gdn[i], st_conv[i], st_sc[i], w_in_a[i], w_conv_qkv[i], a_log[i],
                                    dt_bias[i], gdn_norm[i], w_conv_sc[i], w_out_a[i])
            new_gdn.append(s.astype(st_gdn.dtype))
            new_conv.append(cb.astype(st_conv.dtype))
            new_sc.append(sb.astype(st_sc.dtype))
        else:
            j = layer // 2
            y, r = mixer_c(h, pos, st_ret[j], w_in_c[j], ret_norm[j], w_out_c[j])
            new_ret.append(r.astype(st_ret.dtype))
        x = x + y
        x = x + cross_attn(rmsnorm(x, norm_x[layer]), mem_k[layer], mem_v[layer], w_xq[layer], w_xo[layer])
        x = x + moe(rmsnorm(x, norm_ffn[layer]), w_group[layer], b_group[layer], w_router[layer], b_router[layer],
                    w_exp_gate[layer], w_exp_up[layer], w_exp_down[layer])
    return (rmsnorm(x, norm_final), jnp.stack(new_gdn), jnp.stack(new_conv), jnp.stack(new_sc), jnp.stack(new_ret))


def setup_inputs(seed: int = 0) -> dict:
    key = jax.random.key(seed)
    ks = iter(jax.random.split(key, 48))
    d = D_MODEL

    def nrm(shape, scale=1.0):
        return jax.random.normal(next(ks), shape, jnp.float32) * scale

    return {
        "x_prompt": nrm((BATCH, SEQ, d)),
        "x_sample": nrm((DEC_BATCH, DEC_SEQ, d)),
        "state_gdn": nrm((N_EVEN, DEC_BATCH, H_A, DK_A, DV_A), 0.05),
        "state_gdn_conv": nrm((N_EVEN, DEC_BATCH, CONV_A - 1, 2 * W_QK_A + W_V_A)),
        "state_sconv": nrm((N_EVEN, DEC_BATCH, CONV_B - 1, D_B)),
        "state_ret": nrm((N_ODD, DEC_BATCH, H_C, DK_C, DV_C), 0.05),
        "cache_mem_k": nrm((DEPTH, DEC_BATCH, N_MEM, H_X, HD_X)),
        "cache_mem_v": nrm((DEPTH, DEC_BATCH, N_MEM, H_X, HD_X)),
        "mem_prompt": nrm((BATCH, N_MEM, d)),
        "norm_mix": 1.0 + nrm((DEPTH, d), 0.02),
        "norm_x": 1.0 + nrm((DEPTH, d), 0.02),
        "norm_ffn": 1.0 + nrm((DEPTH, d), 0.02),
        "norm_final": 1.0 + nrm((d,), 0.02),
        "norm_mem": 1.0 + nrm((DEPTH, d), 0.02),
        "w_in_a": nrm((N_EVEN, d, PROJ_A), d ** -0.5),
        "w_conv_qkv": nrm((N_EVEN, CONV_A, 2 * W_QK_A + W_V_A), CONV_A ** -0.5),
        "a_log": jnp.log(jax.random.uniform(next(ks), (N_EVEN, H_A), jnp.float32, 1.0, 16.0)),
        "dt_bias": nrm((N_EVEN, H_A), 0.1),
        "gdn_norm": 1.0 + nrm((N_EVEN, DV_A), 0.02),
        "w_conv_sc": nrm((N_EVEN, CONV_B, D_B), CONV_B ** -0.5),
        "w_out_a": nrm((N_EVEN, W_V_A + D_B, d), (W_V_A + D_B) ** -0.5),
        "w_in_c": nrm((N_ODD, d, PROJ_C), d ** -0.5),
        "ret_norm": 1.0 + nrm((N_ODD, H_C * DV_C), 0.02),
        "w_out_c": nrm((N_ODD, H_C * DV_C, d), (H_C * DV_C) ** -0.5),
        "w_xq": nrm((DEPTH, d, H_X * HD_X), d ** -0.5),
        "w_xk": nrm((DEPTH, d, H_X * HD_X), d ** -0.5),
        "w_xv": nrm((DEPTH, d, H_X * HD_X), d ** -0.5),
        "w_xo": nrm((DEPTH, H_X * HD_X, d), (H_X * HD_X) ** -0.5),
        "w_group": nrm((DEPTH, d, N_GROUPS), d ** -0.5),
        "b_group": nrm((DEPTH, N_GROUPS), 0.01),
        "w_router": nrm((DEPTH, d, N_EXPERTS), d ** -0.5),
        "b_router": nrm((DEPTH, N_EXPERTS), 0.01),
        "w_exp_gate": nrm((DEPTH, N_EXPERTS, d, D_EXPERT), d ** -0.5),
        "w_exp_up": nrm((DEPTH, N_EXPERTS, d, D_EXPERT), d ** -0.5),
        "w_exp_down": nrm((DEPTH, N_EXPERTS, D_EXPERT, d), D_EXPERT ** -0.5),
    }


def reference(x_prompt, x_sample, state_gdn, state_gdn_conv, state_sconv, state_ret, cache_mem_k, cache_mem_v,
              mem_prompt, norm_mix, norm_x, norm_ffn, norm_final, norm_mem, w_in_a, w_conv_qkv, a_log, dt_bias,
              gdn_norm, w_conv_sc, w_out_a, w_in_c, ret_norm, w_out_c, w_xq, w_xk, w_xv, w_xo, w_group, b_group,
              w_router, b_router, w_exp_gate, w_exp_up, w_exp_down):
    params = (norm_mix, norm_x, norm_ffn, norm_final, w_in_a, w_conv_qkv, a_log, dt_bias, gdn_norm,
              w_conv_sc, w_out_a, w_in_c, ret_norm, w_out_c, w_xq, w_xo, w_group, b_group,
              w_router, b_router, w_exp_gate, w_exp_up, w_exp_down)
    bp, lp = x_prompt.shape[0], x_prompt.shape[1]
    ls = x_sample.shape[1]
    n_mem = mem_prompt.shape[1]
    mk_list, mv_list = [], []
    for layer in range(DEPTH):
        mn = rmsnorm(mem_prompt, norm_mem[layer])
        mk_list.append((mn @ w_xk[layer]).reshape(bp, n_mem, H_X, HD_X))
        mv_list.append((mn @ w_xv[layer]).reshape(bp, n_mem, H_X, HD_X))
    p_cache_mem_k = jnp.stack(mk_list)
    p_cache_mem_v = jnp.stack(mv_list)
    dt = x_prompt.dtype
    z_gdn = jnp.zeros((N_EVEN, bp, H_A, DK_A, DV_A), dt)
    z_conv = jnp.zeros((N_EVEN, bp, CONV_A - 1, 2 * W_QK_A + W_V_A), dt)
    z_sc = jnp.zeros((N_EVEN, bp, CONV_B - 1, D_B), dt)
    z_ret = jnp.zeros((N_ODD, bp, H_C, DK_C, DV_C), dt)
    pos_p = jnp.arange(lp, dtype=jnp.int32)
    pos_s = PAST_LEN + jnp.arange(ls, dtype=jnp.int32)
    y_prompt, p_state_gdn, p_state_gdn_conv, p_state_sconv, p_state_ret = trunk(
        x_prompt, pos_p, z_gdn, z_conv, z_sc, z_ret, p_cache_mem_k, p_cache_mem_v, params)
    y_sample, s_state_gdn, s_state_gdn_conv, s_state_sconv, s_state_ret = trunk(
        x_sample, pos_s, state_gdn, state_gdn_conv, state_sconv, state_ret, cache_mem_k, cache_mem_v, params)
    return (y_prompt, y_sample, p_state_gdn, p_state_gdn_conv, p_state_sconv, p_state_ret, p_cache_mem_k,
            p_cache_mem_v, s_state_gdn, s_state_gdn_conv, s_state_sconv, s_state_ret)
```

```python
import functools
import math

import jax
import jax.numpy as jnp
import numpy as np
from jax import lax
from jax.experimental import pallas as pl
from jax.experimental.pallas import tpu as pltpu

F32 = jnp.float32
BF16 = jnp.bfloat16
I32 = jnp.int32

EPS = 1e-6
ROPE_BASE = 10000.0

D_MODEL = 1024
H_A, DK_A, DV_A, CONV_A = 4, 128, 128, 4
W_QKV_A = 3 * H_A * DK_A
D_B, CONV_B = D_MODEL // 2, 3
H_C, DK_C, DV_C = 4, 256, 512
H_X, HD_X, N_MEM = 4, 256, 256
N_GROUPS, E_PER_GROUP, N_EXPERTS, D_EXPERT = 4, 8, 32, 512
GDN_CHUNK = 64

LANES = 128
SUBLANES = 8
GDN_STACK = 256
VMEM_LIMIT = 56 * 1024 * 1024

PA_COLS = 3840
PA_BA = 3584
R_E0 = N_GROUPS
MOE_BLK = 128


def _cparams(sem):
    return pltpu.CompilerParams(dimension_semantics=sem, vmem_limit_bytes=VMEM_LIMIT)


def _dot(a, b, trans_a=False, trans_b=False):
    dn = (((0 if trans_a else 1,), (1 if trans_b else 0,)), ((), ()))
    return lax.dot_general(a.astype(BF16), b.astype(BF16), dn, preferred_element_type=F32)


def _silu(x):
    return x * (1.0 / (1.0 + jnp.exp(-x)))


def _sigmoid(x):
    return 1.0 / (1.0 + jnp.exp(-x))


def _rms_matmul_kernel(x_ref, g_ref, w_ref, o_ref, xn_ref):
    @pl.when(pl.program_id(1) == 0)
    def _():
        x = x_ref[...]
        ms = jnp.mean(x * x, axis=-1, keepdims=True)
        xn_ref[...] = (x * lax.rsqrt(ms + EPS) * g_ref[...]).astype(BF16)

    o_ref[...] = jnp.dot(xn_ref[...], w_ref[...], preferred_element_type=F32).astype(o_ref.dtype)


def rms_matmul(x, g, w, *, tn, out_dtype=F32, tm=1024):
    t, d = x.shape
    n = w.shape[1]
    tm = min(tm, t)
    return pl.pallas_call(
        _rms_matmul_kernel,
        grid=(t // tm, n // tn),
        in_specs=[pl.BlockSpec((tm, d), lambda i, j: (i, 0)),
                  pl.BlockSpec((1, d), lambda i, j: (0, 0)),
                  pl.BlockSpec((d, tn), lambda i, j: (0, j))],
        out_specs=pl.BlockSpec((tm, tn), lambda i, j: (i, j)),
        out_shape=jax.ShapeDtypeStruct((t, n), out_dtype),
        scratch_shapes=[pltpu.VMEM((tm, d), BF16)],
        compiler_params=_cparams(("parallel", "arbitrary")),
        name="rms_matmul",
    )(x, g.reshape(1, d), w)


def _matmul_res_kernel(a_ref, w_ref, r_ref, o_ref):
    o_ref[...] = r_ref[...] + jnp.dot(a_ref[...].astype(BF16), w_ref[...], preferred_element_type=F32)


def matmul_res(a, w, res, *, tm=512):
    t, k = a.shape
    n = w.shape[1]
    tm = min(tm, t)
    return pl.pallas_call(
        _matmul_res_kernel,
        grid=(t // tm,),
        in_specs=[pl.BlockSpec((tm, k), lambda i: (i, 0)),
                  pl.BlockSpec((k, n), lambda i: (0, 0)),
                  pl.BlockSpec((tm, n), lambda i: (i, 0))],
        out_specs=pl.BlockSpec((tm, n), lambda i: (i, 0)),
        out_shape=jax.ShapeDtypeStruct((t, n), F32),
        compiler_params=_cparams(("parallel",)),
        name="matmul_res",
    )(a, w, res)


def _rmsnorm_kernel(x_ref, g_ref, o_ref):
    x = x_ref[...]
    ms = jnp.mean(x * x, axis=-1, keepdims=True)
    o_ref[...] = x * lax.rsqrt(ms + EPS) * g_ref[...]


def rmsnorm_rows(x, g, *, tm=512):
    t, d = x.shape
    tm = min(tm, t)
    return pl.pallas_call(
        _rmsnorm_kernel,
        grid=(t // tm,),
        in_specs=[pl.BlockSpec((tm, d), lambda i: (i, 0)), pl.BlockSpec((1, d), lambda i: (0, 0))],
        out_specs=pl.BlockSpec((tm, d), lambda i: (i, 0)),
        out_shape=jax.ShapeDtypeStruct((t, d), F32),
        compiler_params=_cparams(("parallel",)),
        name="final_rmsnorm",
    )(x, g.reshape(1, d))


def _causal_conv(x, hist, w_ref, width, seq8):
    r = x.shape[0]
    taps = [w_ref[j:j + 1, :] for j in range(width)]

    def head(x8, h8):
        n = x8.shape[0]
        t = lax.broadcasted_iota(I32, (n, 1), 0) % SUBLANES
        y = taps[width - 1] * x8
        for s in range(1, width):
            prev = pltpu.roll(h8, (n + s - (width - 1)) % n, 0) if s != width - 1 else h8
            y = y + taps[width - 1 - s] * jnp.where(t >= s, pltpu.roll(x8, s, 0), prev)
        return y

    if seq8:
        return head(x, hist)
    y = taps[width - 1] * x
    for s in range(1, width):
        y = y + taps[width - 1 - s] * pltpu.roll(x, s, 0)
    return jnp.concatenate([head(x[:SUBLANES], hist), y[SUBLANES:]], axis=0)


def _unit_lower_inverse(m, c, ri, ci):
    base = min(c, 16)
    eye = jnp.where(ri == ci, 1.0, 0.0).astype(F32)
    d = jnp.where((ri // base) == (ci // base), m, 0.0)
    p = eye - d
    dp = d
    k = 2
    while k < base:
        dp = _dot(dp, dp)
        p = _dot(p, eye + dp)
        k *= 2
    s = base
    while s < c:
        off = jnp.where(((ri // (2 * s)) == (ci // (2 * s))) & ((ri // s) != (ci // s)), m, 0.0)
        p = p - _dot(p, _dot(off, p))
        s *= 2
    return p


def _gdn_unit(q, k, v, bfull, gfull, states, c):
    n = GDN_STACK
    ri = lax.broadcasted_iota(I32, (n, n), 0)
    ci = lax.broadcasted_iota(I32, (n, n), 1)
    same = (ri // c) == (ci // c)
    incl = same & (ri >= ci)
    strict = same & (ri > ci)
    g2 = jnp.concatenate([gfull, gfull], axis=1)
    g_row = jnp.sum(jnp.where(ri == ci, g2, 0.0), axis=0, keepdims=True)
    gc_col = jnp.sum(jnp.where(incl, g_row, 0.0), axis=1, keepdims=True)
    gc_row = jnp.sum(jnp.where(same & (ri <= ci), g2, 0.0), axis=0, keepdims=True)
    gl_col = jnp.sum(jnp.where(same, g_row, 0.0), axis=1, keepdims=True)
    decay = jnp.where(incl, jnp.exp(jnp.where(incl, gc_col - gc_row, 0.0)), 0.0)
    egc = jnp.exp(gc_col)
    ekd = jnp.exp(gl_col - gc_col)
    egl = jnp.exp(gl_col)

    kb = k * bfull
    mm = jnp.where(strict, _dot(kb, k, trans_b=True) * decay, 0.0)
    qk = _dot(q, k, trans_b=True) * decay
    tinv = _unit_lower_inverse(mm, c, ri, ci)
    uw = _dot(tinv, jnp.concatenate([v * bfull, kb * egc], axis=1))
    u, w = uw[:, :DV_A], uw[:, DV_A:]
    qd = q * egc
    kd = k * ekd

    nprob = n // c
    ws, qs = [], []
    for p in range(nprob):
        sl = slice(p * c, (p + 1) * c)
        ws.append(_dot(w[sl], states[p]))
        qs.append(_dot(qd[sl], states[p]))
    vn = u - jnp.concatenate(ws, axis=0)
    o = _dot(qk, vn) + jnp.concatenate(qs, axis=0)
    new_states = []
    for p in range(nprob):
        sl = slice(p * c, (p + 1) * c)
        new_states.append(states[p] * egl[p * c:p * c + 1, :] + _dot(kd[sl], vn[sl], trans_a=True))
    return o, new_states


def _gdn_kernel(qkv_ref, z_ref, sch_ref, scb_ref, scc_ref, ba_ref, hq_ref, hs_ref, wq_ref, ws_ref,
                prm_ref, gn_ref, s0_ref, o_ref, sn_ref, cq_ref, cs_ref, s_scr, hq_scr, hs_scr,
                *, seq8, c, nu, nl):
    unit = GDN_CHUNK
    l = jnp.int32(0) if seq8 else pl.program_id(1)

    if seq8:
        hq = hq_ref[...]
        hs = hs_ref[...]
    else:
        @pl.when(l == 0)
        def _():
            hq_scr[...] = hq_ref[0]
            hs_scr[...] = hs_ref[0]
            s_scr[...] = s0_ref[0]
        hq = hq_scr[...]
        hs = hs_scr[...]

    x = qkv_ref[...]
    rows = x.shape[0]
    u_sc = scc_ref[...] * sch_ref[...]
    if seq8:
        cq_ref[...] = pltpu.roll(x, rows - SUBLANES + CONV_A - 1, 0)
        cs_ref[...] = pltpu.roll(u_sc, rows - SUBLANES + CONV_B - 1, 0)
    else:
        hq_scr[...] = pltpu.roll(x[rows - SUBLANES:], CONV_A - 1, 0)
        hs_scr[...] = pltpu.roll(u_sc[rows - SUBLANES:], CONV_B - 1, 0)

    xc = _silu(_causal_conv(x, hq, wq_ref, CONV_A, seq8))
    yb = scb_ref[...] * _causal_conv(u_sc, hs, ws_ref, CONV_B, seq8)
    o_ref[:, H_A * DV_A:] = yb.astype(o_ref.dtype)

    ba = ba_ref[...]
    beta_all = _sigmoid(ba)
    sp = jnp.maximum(ba + prm_ref[1:2, :], 0.0) + jnp.log1p(jnp.exp(-jnp.abs(ba + prm_ref[1:2, :])))
    g_all = -jnp.exp(prm_ref[0:1, :]) * sp
    gn = gn_ref[...]

    def head_cols(a, base):
        return a[:, base * DK_A:(base + 1) * DK_A]

    for ui in range(nu):
        rs = slice(ui * unit, (ui + 1) * unit)
        qs, ks, vs, zs, bs, gs = [], [], [], [], [], []
        for h in range(H_A):
            qh = head_cols(xc, h)[rs]
            kh = head_cols(xc, H_A + h)[rs]
            qs.append(qh * lax.rsqrt(jnp.sum(qh * qh, axis=-1, keepdims=True) + EPS) * (DK_A ** -0.5))
            ks.append(kh * lax.rsqrt(jnp.sum(kh * kh, axis=-1, keepdims=True) + EPS))
            vs.append(head_cols(xc, 2 * H_A + h)[rs])
            zs.append(z_ref[rs, h * DV_A:(h + 1) * DV_A])
            bs.append(jnp.broadcast_to(beta_all[rs, h:h + 1], (unit, LANES)))
            gs.append(jnp.broadcast_to(g_all[rs, H_A + h:H_A + h + 1], (unit, LANES)))
        cat = lambda xs: jnp.concatenate(xs, axis=0)
        nprob = GDN_STACK // c
        if seq8:
            states = [s0_ref[p % SUBLANES, p // SUBLANES] for p in range(nprob)]
        else:
            states = [s_scr[p] for p in range(nprob)]
        o, new_states = _gdn_unit(cat(qs), cat(ks), cat(vs), cat(bs), cat(gs), states, c)
        for p in range(nprob):
            if seq8:
                sn_ref[p % SUBLANES, p // SUBLANES] = new_states[p]
            else:
                s_scr[p] = new_states[p]
        ms = jnp.mean(o * o, axis=-1, keepdims=True)
        og = o * lax.rsqrt(ms + EPS) * gn * _silu(cat(zs))
        for h in range(H_A):
            o_ref[rs, h * DV_A:(h + 1) * DV_A] = og[h * unit:(h + 1) * unit].astype(o_ref.dtype)

    if not seq8:
        @pl.when(l == nl - 1)
        def _():
            sn_ref[0] = s_scr[...]
            cq_ref[0] = hq_scr[...]
            cs_ref[0] = hs_scr[...]


def gdn_core(p, hist_q, hist_s, w_conv_qkv, w_conv_sc, prm, gn, s0, *, batch, seqlen):
    t = batch * seqlen
    seq8 = seqlen == SUBLANES
    if seq8:
        c, nu, rows = SUBLANES, 1, GDN_CHUNK
        nb = rows // seqlen
        grid = (batch // nb,)
        rmap = lambda i: i
        sem = ("parallel",)
        hq_spec = pl.BlockSpec((rows, W_QKV_A), lambda i: (i, 0))
        hs_spec = pl.BlockSpec((rows, D_B), lambda i: (i, 0))
        s_spec = pl.BlockSpec((nb, H_A, DK_A, DV_A), lambda i: (i, 0, 0, 0))
        hist_q = hist_q.reshape(t, W_QKV_A)
        hist_s = hist_s.reshape(t, D_B)
        cq_shape, cs_shape = (t, W_QKV_A), (t, D_B)
        nl = 1
    else:
        c, rows = GDN_CHUNK, min(seqlen, 128)
        nu = rows // GDN_CHUNK
        nl = seqlen // rows
        grid = (batch, nl)
        rmap = lambda b, l: b * nl + l
        sem = ("parallel", "arbitrary")
        hq_spec = pl.BlockSpec((1, SUBLANES, W_QKV_A), lambda b, l: (b, 0, 0))
        hs_spec = pl.BlockSpec((1, SUBLANES, D_B), lambda b, l: (b, 0, 0))
        s_spec = pl.BlockSpec((1, H_A, DK_A, DV_A), lambda b, l: (b, 0, 0, 0))
        cq_shape, cs_shape = (batch, SUBLANES, W_QKV_A), (batch, SUBLANES, D_B)

    def col(width, blk):
        return pl.BlockSpec((rows, width), lambda *ix: (rmap(*ix), blk))

    const = lambda shape: pl.BlockSpec(shape, lambda *ix: (0,) * len(shape))
    kern = functools.partial(_gdn_kernel, seq8=seq8, c=c, nu=nu, nl=nl)
    return pl.pallas_call(
        kern,
        grid=grid,
        in_specs=[col(W_QKV_A, 0), col(D_B, 3), col(D_B, 4), col(D_B, 5), col(D_B, 6),
                  col(LANES, PA_BA // LANES), hq_spec, hs_spec,
                  const((CONV_A, W_QKV_A)), const((CONV_B, D_B)), const((SUBLANES, LANES)),
                  const((1, DV_A)), s_spec],
        out_specs=[pl.BlockSpec((rows, D_MODEL), lambda *ix: (rmap(*ix), 0)), s_spec, hq_spec, hs_spec],
        out_shape=[jax.ShapeDtypeStruct((t, D_MODEL), BF16),
                   jax.ShapeDtypeStruct((batch, H_A, DK_A, DV_A), F32),
                   jax.ShapeDtypeStruct(cq_shape, F32),
                   jax.ShapeDtypeStruct(cs_shape, F32)],
        scratch_shapes=[pltpu.VMEM((H_A, DK_A, DV_A), F32),
                        pltpu.VMEM((SUBLANES, W_QKV_A), F32),
                        pltpu.VMEM((SUBLANES, D_B), F32)],
        compiler_params=_cparams(sem),
        name="gdn_core",
    )(p, p, p, p, p, p, hist_q, hist_s, w_conv_qkv, w_conv_sc, prm, gn.reshape(1, DV_A), s0)


def _ret_kernel(q_ref, k_ref, v_ref, gate_ref, cos_ref, sin_ref, dm_ref, qd_ref, kd_ref, cd_ref,
                rn_ref, r0_ref, o_ref, rnew_ref, r_scr, *, nseq, c, nl):
    l = pl.program_id(1)

    @pl.when(l == 0)
    def _():
        r_scr[...] = r0_ref[...]

    cos = cos_ref[...]
    sin = sin_ref[...]
    half = DK_C // 2

    def rot(x):
        x1, x2 = x[:, :half], x[:, half:]
        return jnp.concatenate([x1 * cos - x2 * sin, x1 * sin + x2 * cos], axis=1)

    for h in range(H_C):
        qh = rot(q_ref[:, h * DK_C:(h + 1) * DK_C])
        kh = rot(k_ref[:, h * DK_C:(h + 1) * DK_C]) * (DK_C ** -0.5)
        vh = v_ref[:, h * DV_C:(h + 1) * DV_C]
        s = _dot(qh, kh, trans_b=True) * dm_ref[h]
        o = _dot(s, vh)
        qdh = qh * qd_ref[h]
        kdh = kh * kd_ref[h]
        cd = cd_ref[h][0:1, 0:1]
        inter = []
        for sq in range(nseq):
            sl = slice(sq * c, (sq + 1) * c)
            r_old = r_scr[sq, h]
            inter.append(_dot(qdh[sl], r_old))
            r_scr[sq, h] = r_old * cd + _dot(kdh[sl], vh[sl], trans_a=True)
        o = o + (inter[0] if nseq == 1 else jnp.concatenate(inter, axis=0))
        ms = jnp.mean(o * o, axis=-1, keepdims=True)
        on = o * lax.rsqrt(ms + EPS) * rn_ref[:, h * DV_C:(h + 1) * DV_C]
        o_ref[:, h * DV_C:(h + 1) * DV_C] = (_silu(gate_ref[:, h * DV_C:(h + 1) * DV_C]) * on).astype(o_ref.dtype)

    @pl.when(l == nl - 1)
    def _():
        rnew_ref[...] = r_scr[...]


def ret_core(p, pos, ret_norm, r0, *, batch, seqlen):
    t = batch * seqlen
    if seqlen == SUBLANES:
        nseq, c = 2, seqlen
        nl = 1
        grid = (batch // nseq, 1)
        rmap = lambda b, l: b
    else:
        nseq, c = 1, min(seqlen, 256)
        nl = seqlen // c
        grid = (batch, nl)
        rmap = lambda b, l: b * nl + l
    rows = nseq * c
    half = DK_C // 2
    inv = ROPE_BASE ** (-jnp.arange(half, dtype=F32) / half)
    ang = pos.astype(F32)[:, None] * inv[None, :]
    cos, sin = jnp.cos(ang), jnp.sin(ang)
    if nseq > 1:
        cos, sin = jnp.tile(cos, (nseq, 1)), jnp.tile(sin, (nseq, 1))
    lg = jnp.log(1.0 - 2.0 ** (-5.0 - jnp.arange(H_C, dtype=F32)))[:, None]
    i = jnp.arange(c, dtype=F32)
    incl = i[:, None] >= i[None, :]
    dmat = jnp.exp(jnp.where(incl[None], (i[:, None] - i[None, :])[None] * lg[..., None], -jnp.inf))
    if nseq > 1:
        dmat = jnp.kron(jnp.eye(nseq, dtype=F32)[None], dmat)
    qd = jnp.tile(jnp.exp((i + 1.0)[None] * lg), (1, nseq))[..., None] * jnp.ones((1, 1, DK_C), F32)
    kd = jnp.tile(jnp.exp((c - 1.0 - i)[None] * lg), (1, nseq))[..., None] * jnp.ones((1, 1, DK_C), F32)
    cd = jnp.exp(c * lg)[..., None] * jnp.ones((1, SUBLANES, LANES), F32)

    const = lambda shape: pl.BlockSpec(shape, lambda b, l: (0,) * len(shape))
    trig_spec = (const((rows, half)) if nseq > 1
                 else pl.BlockSpec((rows, half), lambda b, l: (l, 0)))
    kern = functools.partial(_ret_kernel, nseq=nseq, c=c, nl=nl)
    hk = H_C * DK_C
    hv = H_C * DV_C
    return pl.pallas_call(
        kern,
        grid=grid,
        in_specs=[pl.BlockSpec((rows, hk), lambda b, l: (rmap(b, l), 0)),
                  pl.BlockSpec((rows, hk), lambda b, l: (rmap(b, l), 1)),
                  pl.BlockSpec((rows, hv), lambda b, l: (rmap(b, l), 1)),
                  pl.BlockSpec((rows, hv), lambda b, l: (rmap(b, l), 2)),
                  trig_spec, trig_spec,
                  const((H_C, rows, rows)), const((H_C, rows, DK_C)), const((H_C, rows, DK_C)),
                  const((H_C, SUBLANES, LANES)), const((1, hv)),
                  pl.BlockSpec((nseq, H_C, DK_C, DV_C), lambda b, l: (b, 0, 0, 0))],
        out_specs=[pl.BlockSpec((rows, hv), lambda b, l: (rmap(b, l), 0)),
                   pl.BlockSpec((nseq, H_C, DK_C, DV_C), lambda b, l: (b, 0, 0, 0))],
        out_shape=[jax.ShapeDtypeStruct((t, hv), BF16),
                   jax.ShapeDtypeStruct((batch, H_C, DK_C, DV_C), F32)],
        scratch_shapes=[pltpu.VMEM((nseq, H_C, DK_C, DV_C), F32)],
        compiler_params=_cparams(("parallel", "arbitrary")),
        name="ret_core",
    )(p, p, p, p, cos, sin, dmat, qd, kd, cd, ret_norm.reshape(1, hv), r0)


def _xattn_kernel(q_ref, mk_ref, mv_ref, o_ref, *, nb, lq):
    for b in range(nb):
        rs = slice(b * lq, (b + 1) * lq)
        for h in range(H_X):
            cs = slice(h * HD_X, (h + 1) * HD_X)
            s = _dot(q_ref[rs, cs], mk_ref[b, :, cs], trans_b=True) * (HD_X ** -0.5)
            m = jnp.max(s, axis=-1, keepdims=True)
            e = jnp.exp(s - m)
            p = e / jnp.sum(e, axis=-1, keepdims=True)
            o_ref[rs, cs] = _dot(p, mv_ref[b, :, cs]).astype(o_ref.dtype)


def xattn_core(q, mk, mv, *, batch, seqlen):
    t = batch * seqlen
    d = H_X * HD_X
    if seqlen == SUBLANES:
        nb, lq, nl = 2, seqlen, 1
    else:
        nb, lq = 1, min(seqlen, 512)
        nl = seqlen // lq
    rows = nb * lq
    kern = functools.partial(_xattn_kernel, nb=nb, lq=lq)
    return pl.pallas_call(
        kern,
        grid=(batch // nb, nl),
        in_specs=[pl.BlockSpec((rows, d), lambda b, l: (b * nl + l, 0)),
                  pl.BlockSpec((nb, N_MEM, d), lambda b, l: (b, 0, 0)),
                  pl.BlockSpec((nb, N_MEM, d), lambda b, l: (b, 0, 0))],
        out_specs=pl.BlockSpec((rows, d), lambda b, l: (b * nl + l, 0)),
        out_shape=jax.ShapeDtypeStruct((t, d), q.dtype),
        compiler_params=_cparams(("parallel", "arbitrary")),
        name="xattn_core",
    )(q, mk, mv)


def _route_kernel(x_ref, g_ref, wr_ref, br_ref, h_ref, meta_ref, wts_ref, cnt_ref, cnt_scr):
    i = pl.program_id(0)

    @pl.when(i == 0)
    def _():
        cnt_scr[...] = jnp.zeros_like(cnt_scr)

    x = x_ref[...]
    tm = x.shape[0]
    ms = jnp.mean(x * x, axis=-1, keepdims=True)
    h = x * lax.rsqrt(ms + EPS) * g_ref[...]
    h_ref[...] = h
    logits = jnp.dot(h, wr_ref[...], preferred_element_type=F32, precision=lax.Precision.HIGHEST) + br_ref[...]
    lane_i = lax.broadcasted_iota(I32, (tm, LANES), 1)
    lane = lane_i.astype(F32)
    neg = jnp.float32(-3.0e38)
    big = jnp.float32(LANES)
    is_g = lane_i < N_GROUPS
    gl = jnp.where(is_g, logits, neg)
    gmax = jnp.max(gl, axis=1, keepdims=True)
    grp = jnp.min(jnp.where(gl == gmax, lane, big), axis=1, keepdims=True)
    gsum = jnp.sum(jnp.where(is_g, jnp.exp(jnp.where(is_g, logits - gmax, 0.0)), 0.0), axis=1, keepdims=True)
    p_grp = 1.0 / gsum
    in_grp = ((lane_i >= R_E0) & (lane_i < R_E0 + N_EXPERTS)
              & (jnp.floor((lane - R_E0) * (1.0 / E_PER_GROUP)) == grp))
    el = jnp.where(in_grp, logits, neg)
    m1 = jnp.max(el, axis=1, keepdims=True)
    i1 = jnp.min(jnp.where(el == m1, lane, big), axis=1, keepdims=True)
    el2 = jnp.where(lane == i1, neg, el)
    m2 = jnp.max(el2, axis=1, keepdims=True)
    i2 = jnp.min(jnp.where(el2 == m2, lane, big), axis=1, keepdims=True)
    esum = jnp.sum(jnp.where(in_grp, jnp.exp(jnp.where(in_grp, logits - m1, 0.0)), 0.0), axis=1, keepdims=True)
    p1 = 1.0 / esum
    p2 = jnp.exp(m2 - m1) / esum
    tot = p1 + p2
    w1 = p_grp * (p1 / tot)
    w2 = p_grp * (p2 / tot)
    wts_ref[...] = jnp.where(lane_i == 0, w1, jnp.where(lane_i == 1, w2, 0.0))

    oh1 = (lane == i1).astype(F32)
    oh2 = (lane == i2).astype(F32)
    rr = lax.broadcasted_iota(I32, (tm, tm), 0)
    cc = lax.broadcasted_iota(I32, (tm, tm), 1)
    tri = (rr > cc).astype(BF16)
    base = cnt_scr[0:1, :]
    c1 = jnp.sum(oh1, axis=0, keepdims=True)
    c2 = jnp.sum(oh2, axis=0, keepdims=True)
    r1 = jnp.sum(oh1 * (jnp.dot(tri, oh1.astype(BF16), preferred_element_type=F32) + base), axis=1, keepdims=True)
    r2 = jnp.sum(oh2 * (jnp.dot(tri, oh2.astype(BF16), preferred_element_type=F32) + base + c1), axis=1, keepdims=True)
    new_cnt = base + c1 + c2
    cnt_scr[...] = jnp.broadcast_to(new_cnt, cnt_scr.shape)
    cnt_ref[...] = jnp.broadcast_to(new_cnt, cnt_ref.shape)
    meta = jnp.where(lane_i == 0, i1, jnp.where(lane_i == 1, i2, 0.0))
    meta = jnp.where(lane_i == 2, r1, jnp.where(lane_i == 3, r2, meta))
    meta_ref[...] = meta.astype(I32)


def moe_route(x, g, wr, br, *, tm=256):
    t, d = x.shape
    tm = min(tm, t)
    return pl.pallas_call(
        _route_kernel,
        grid=(t // tm,),
        in_specs=[pl.BlockSpec((tm, d), lambda i: (i, 0)),
                  pl.BlockSpec((1, d), lambda i: (0, 0)),
                  pl.BlockSpec((d, LANES), lambda i: (0, 0)),
                  pl.BlockSpec((1, LANES), lambda i: (0, 0))],
        out_specs=[pl.BlockSpec((tm, d), lambda i: (i, 0)),
                   pl.BlockSpec((tm, LANES), lambda i: (i, 0)),
                   pl.BlockSpec((tm, LANES), lambda i: (i, 0)),
                   pl.BlockSpec((SUBLANES, LANES), lambda i: (0, 0))],
        out_shape=[jax.ShapeDtypeStruct((t, d), F32),
                   jax.ShapeDtypeStruct((t, LANES), I32),
                   jax.ShapeDtypeStruct((t, LANES), F32),
                   jax.ShapeDtypeStruct((SUBLANES, LANES), F32)],
        scratch_shapes=[pltpu.VMEM((SUBLANES, LANES), F32)],
        compiler_params=_cparams(("arbitrary",)),
        name="moe_route",
    )(x, g.reshape(1, d), wr, br)


def _plan_kernel(cnt_ref, meta_ref, dest_ref, be_ref, *, nblk_pad):
    cnt = cnt_ref[...]
    lane8 = lax.broadcasted_iota(I32, (SUBLANES, LANES), 1)
    padded = jnp.ceil(cnt / MOE_BLK) * MOE_BLK
    pend = padded
    s = 1
    while s < LANES:
        pend = pend + jnp.where(lane8 >= s, pltpu.roll(pend, s, 1), 0.0)
        s *= 2
    pstart = (pend - padded)[0:1, :]
    meta = meta_ref[...].astype(F32)
    tm = meta.shape[0]
    lane_i = lax.broadcasted_iota(I32, (tm, LANES), 1)
    lane = lane_i.astype(F32)
    col = lambda j: jnp.sum(jnp.where(lane_i == j, meta, 0.0), axis=1, keepdims=True)
    e1, e2, r1, r2 = col(0), col(1), col(2), col(3)
    d1 = jnp.sum(jnp.where(lane == e1, pstart, 0.0), axis=1, keepdims=True) + r1
    d2 = jnp.sum(jnp.where(lane == e2, pstart, 0.0), axis=1, keepdims=True) + r2
    dest_ref[...] = jnp.where(lane_i == 0, d1, jnp.where(lane_i == 1, d2, 0.0)).astype(I32)
    bi = (lax.broadcasted_iota(I32, (nblk_pad, LANES), 0) * MOE_BLK).astype(F32)
    lane_b = lax.broadcasted_iota(I32, (nblk_pad, LANES), 1)
    is_e = (lane_b >= R_E0) & (lane_b < R_E0 + N_EXPERTS)
    nfull = jnp.sum(jnp.where(is_e & (bi >= pend[0:1, :]), 1.0, 0.0), axis=1, keepdims=True)
    be = jnp.minimum(nfull, N_EXPERTS - 1.0)
    nused = jnp.max(pend[0:1, :], axis=1, keepdims=True) / MOE_BLK
    row = lax.broadcasted_iota(I32, (nblk_pad, LANES), 0)
    be_ref[...] = jnp.where(row == nblk_pad - 1, nused, be).astype(I32)


def moe_plan(cnt, meta, *, nblk_pad, tm=512):
    t = meta.shape[0]
    tm = min(tm, t)
    kern = functools.partial(_plan_kernel, nblk_pad=nblk_pad)
    return pl.pallas_call(
        kern,
        grid=(t // tm,),
        in_specs=[pl.BlockSpec((SUBLANES, LANES), lambda i: (0, 0)),
                  pl.BlockSpec((tm, LANES), lambda i: (i, 0))],
        out_specs=[pl.BlockSpec((tm, LANES), lambda i: (i, 0)),
                   pl.BlockSpec((nblk_pad, LANES), lambda i: (0, 0))],
        out_shape=[jax.ShapeDtypeStruct((t, LANES), I32),
                   jax.ShapeDtypeStruct((nblk_pad, LANES), I32)],
        compiler_params=_cparams(("arbitrary",)),
        name="moe_plan",
    )(cnt, meta)


def _dispatch_kernel(dest_ref, h_hbm, xd_in, xd_hbm, sem, *, tm):
    del xd_in
    base = pl.program_id(0) * tm

    def row_copy(t, k):
        return pltpu.make_async_copy(h_hbm.at[pl.ds(base + t, 1)],
                                     xd_hbm.at[pl.ds(dest_ref[2 * t + k], 1)], sem)

    def start(t, carry):
        row_copy(t, 0).start()
        row_copy(t, 1).start()
        return carry

    def wait(t, carry):
        row_copy(t, 0).wait()
        row_copy(t, 1).wait()
        return carry

    lax.fori_loop(0, tm, start, 0)
    lax.fori_loop(0, tm, wait, 0)


def moe_dispatch(dest_flat, h, xd_zero, *, tm=512):
    t, d = h.shape
    tm = min(tm, t)
    kern = functools.partial(_dispatch_kernel, tm=tm)
    return pl.pallas_call(
        kern,
        grid=(t // tm,),
        in_specs=[pl.BlockSpec((2 * tm,), lambda i: (i,), memory_space=pltpu.SMEM),
                  pl.BlockSpec(memory_space=pl.ANY),
                  pl.BlockSpec(memory_space=pl.ANY)],
        out_specs=pl.BlockSpec(memory_space=pl.ANY),
        out_shape=jax.ShapeDtypeStruct(xd_zero.shape, F32),
        scratch_shapes=[pltpu.SemaphoreType.DMA],
        input_output_aliases={2: 0},
        compiler_params=_cparams(("arbitrary",)),
        name="moe_dispatch",
    )(dest_flat, h, xd_zero)


def _experts_kernel(be_ref, x_ref, wg_ref, wu_ref, wd_ref, o_ref, wg_s, wu_s, wd_s, *, nblk_pad):
    i = pl.program_id(0)
    nused = be_ref[nblk_pad - 1]

    @pl.when(i < nused)
    def _():
        prev = be_ref[jnp.maximum(i - 1, 0)]

        @pl.when((i == 0) | (be_ref[i] != prev))
        def _():
            wg_s[...] = wg_ref[0].astype(BF16)
            wu_s[...] = wu_ref[0].astype(BF16)
            wd_s[...] = wd_ref[0].astype(BF16)

        x = x_ref[...].astype(BF16)
        g = jnp.dot(x, wg_s[...], preferred_element_type=F32)
        u = jnp.dot(x, wu_s[...], preferred_element_type=F32)
        a = (_silu(g) * u).astype(BF16)
        o_ref[...] = jnp.dot(a, wd_s[...], preferred_element_type=F32)

    @pl.when(i >= nused)
    def _():
        o_ref[...] = jnp.zeros_like(o_ref)


def moe_experts(be_flat, xd, w_gate, w_up, w_down, *, nblk, nblk_pad):
    d, f = w_gate.shape[1], w_gate.shape[2]
    kern = functools.partial(_experts_kernel, nblk_pad=nblk_pad)
    grid_spec = pltpu.PrefetchScalarGridSpec(
        num_scalar_prefetch=1,
        grid=(nblk,),
        in_specs=[pl.BlockSpec((MOE_BLK, d), lambda i, be: (i, 0)),
                  pl.BlockSpec((1, d, f), lambda i, be: (be[i], 0, 0)),
                  pl.BlockSpec((1, d, f), lambda i, be: (be[i], 0, 0)),
                  pl.BlockSpec((1, f, d), lambda i, be: (be[i], 0, 0))],
        out_specs=pl.BlockSpec((MOE_BLK, d), lambda i, be: (i, 0)),
        scratch_shapes=[pltpu.VMEM((d, f), BF16), pltpu.VMEM((d, f), BF16), pltpu.VMEM((f, d), BF16)],
    )
    return pl.pallas_call(
        kern,
        grid_spec=grid_spec,
        out_shape=jax.ShapeDtypeStruct(xd.shape, F32),
        compiler_params=_cparams(("arbitrary",)),
        name="moe_experts",
    )(be_flat, xd, w_gate, w_up, w_down)


def _combine_kernel(dest_ref, x_ref, wts_ref, yd_hbm, o_ref, r0, r1, sem, *, tm):
    def row_copy(t, k, buf):
        return pltpu.make_async_copy(yd_hbm.at[pl.ds(dest_ref[2 * t + k], 1)], buf.at[pl.ds(t, 1)], sem)

    def start(t, carry):
        row_copy(t, 0, r0).start()
        row_copy(t, 1, r1).start()
        return carry

    def wait(t, carry):
        row_copy(t, 0, r0).wait()
        row_copy(t, 1, r1).wait()
        return carry

    lax.fori_loop(0, tm, start, 0)
    lax.fori_loop(0, tm, wait, 0)
    w = wts_ref[...]
    o_ref[...] = x_ref[...] + r0[...] * w[:, 0:1] + r1[...] * w[:, 1:2]


def moe_combine(dest_flat, x, wts, yd, *, tm=512):
    t, d = x.shape
    tm = min(tm, t)
    kern = functools.partial(_combine_kernel, tm=tm)
    return pl.pallas_call(
        kern,
        grid=(t // tm,),
        in_specs=[pl.BlockSpec((2 * tm,), lambda i: (i,), memory_space=pltpu.SMEM),
                  pl.BlockSpec((tm, d), lambda i: (i, 0)),
                  pl.BlockSpec((tm, LANES), lambda i: (i, 0)),
                  pl.BlockSpec(memory_space=pl.ANY)],
        out_specs=pl.BlockSpec((tm, d), lambda i: (i, 0)),
        out_shape=jax.ShapeDtypeStruct((t, d), F32),
        scratch_shapes=[pltpu.VMEM((tm, d), F32), pltpu.VMEM((tm, d), F32), pltpu.SemaphoreType.DMA],
        compiler_params=_cparams(("arbitrary",)),
        name="moe_combine",
    )(dest_flat, x, wts, yd)


def moe_block(x, g, wr, br, w_gate, w_up, w_down):
    t, d = x.shape
    nblk = (2 * t) // MOE_BLK + N_EXPERTS
    nblk_pad = -(-(nblk + 1) // SUBLANES) * SUBLANES
    h, meta, wts, cnt = moe_route(x, g, wr, br)
    dest, be = moe_plan(cnt, meta, nblk_pad=nblk_pad)
    dest_flat = dest[:, :2].reshape(2 * t)
    be_flat = be[:, 0]
    xd = moe_dispatch(dest_flat, h, jnp.zeros((nblk * MOE_BLK, d), F32))
    yd = moe_experts(be_flat, xd, w_gate, w_up, w_down, nblk=nblk, nblk_pad=nblk_pad)
    return moe_combine(dest_flat, x, wts, yd)


def _pad_rows(buf):
    return jnp.pad(buf, ((0, 0), (0, SUBLANES - buf.shape[1]), (0, 0)))


def _trunk(x, pos, st_gdn, st_conv, st_sc, st_ret, mem_k, mem_v, wts, *, batch, seqlen):
    depth = len(mem_k)
    new_gdn, new_conv, new_sc, new_ret = [], [], [], []
    for layer in range(depth):
        if layer % 2 == 0:
            i = layer // 2
            p = rms_matmul(x, wts["norm_mix"][layer], wts["w_in_a"][i], tn=768)
            mix, s_new, cq, cs = gdn_core(p, _pad_rows(st_conv[i]), _pad_rows(st_sc[i]),
                                          wts["w_conv_qkv"][i], wts["w_conv_sc"][i], wts["gdn_prm"][i],
                                          wts["gdn_norm"][i], st_gdn[i], batch=batch, seqlen=seqlen)
            new_gdn.append(s_new)
            new_conv.append(cq.reshape(batch, SUBLANES, W_QKV_A)[:, :CONV_A - 1])
            new_sc.append(cs.reshape(batch, SUBLANES, D_B)[:, :CONV_B - 1])
            x = matmul_res(mix, wts["w_out_a"][i], x)
        else:
            j = layer // 2
            p = rms_matmul(x, wts["norm_mix"][layer], wts["w_in_c"][j], tn=768)
            ret, r_new = ret_core(p, pos, wts["ret_norm"][j], st_ret[j], batch=batch, seqlen=seqlen)
            new_ret.append(r_new)
            x = matmul_res(ret, wts["w_out_c"][j], x)
        q = rms_matmul(x, wts["norm_x"][layer], wts["w_xq"][layer], tn=D_MODEL,
                       out_dtype=BF16 if seqlen % 16 == 0 else F32)
        att = xattn_core(q, mem_k[layer], mem_v[layer], batch=batch, seqlen=seqlen)
        x = matmul_res(att, wts["w_xo"][layer], x)
        x = moe_block(x, wts["norm_ffn"][layer], wts["w_route"][layer], wts["b_route"][layer],
                      wts["w_exp_gate"][layer], wts["w_exp_up"][layer], wts["w_exp_down"][layer])
    y = rmsnorm_rows(x, wts["norm_final"])
    return y, jnp.stack(new_gdn), jnp.stack(new_conv), jnp.stack(new_sc), jnp.stack(new_ret)


def kernel(x_prompt, x_sample, state_gdn, state_gdn_conv, state_sconv, state_ret, cache_mem_k, cache_mem_v, mem_prompt, norm_mix, norm_x, norm_ffn, norm_final, norm_mem, w_in_a, w_conv_qkv, a_log, dt_bias, gdn_norm, w_conv_sc, w_out_a, w_in_c, ret_norm, w_out_c, w_xq, w_xk, w_xv, w_xo, w_group, b_group, w_router, b_router, w_exp_gate, w_exp_up, w_exp_down):
    bp, lp, d = x_prompt.shape
    bs, ls, _ = x_sample.shape
    depth = norm_mix.shape[0]
    n_even = w_in_a.shape[0]
    n_mem = mem_prompt.shape[1]

    qkv_w = 2 * H_A * DK_A + H_A * DV_A
    o_z = qkv_w
    o_b = o_z + H_A * DV_A
    o_a = o_b + H_A
    o_sc = o_a + H_A
    w_a = jnp.concatenate([w_in_a[:, :, :o_b], w_in_a[:, :, o_sc:], w_in_a[:, :, o_b:o_sc],
                           jnp.zeros((n_even, d, PA_COLS - PA_BA - 2 * H_A), F32)], axis=-1).astype(BF16)
    prm = jnp.zeros((n_even, SUBLANES, LANES), F32)
    prm = prm.at[:, 0, H_A:2 * H_A].set(a_log).at[:, 1, H_A:2 * H_A].set(dt_bias)
    w_route = jnp.concatenate([w_group, w_router, jnp.zeros((depth, d, LANES - N_GROUPS - N_EXPERTS), F32)], axis=-1)
    b_route = jnp.concatenate([b_group, b_router, jnp.zeros((depth, LANES - N_GROUPS - N_EXPERTS), F32)],
                              axis=-1).reshape(depth, 1, LANES)
    wts = dict(norm_mix=norm_mix, norm_x=norm_x, norm_ffn=norm_ffn, norm_final=norm_final,
               w_in_a=w_a, w_conv_qkv=w_conv_qkv, gdn_prm=prm, gdn_norm=gdn_norm, w_conv_sc=w_conv_sc,
               w_out_a=w_out_a.astype(BF16), w_in_c=w_in_c.astype(BF16), ret_norm=ret_norm,
               w_out_c=w_out_c.astype(BF16), w_xq=w_xq.astype(BF16), w_xo=w_xo.astype(BF16),
               w_route=w_route, b_route=b_route,
               w_exp_gate=w_exp_gate, w_exp_up=w_exp_up, w_exp_down=w_exp_down)

    memf = mem_prompt.reshape(bp * n_mem, d)
    w_kv = jnp.concatenate([w_xk, w_xv], axis=-1).astype(BF16)
    mk_p, mv_p = [], []
    for layer in range(depth):
        kv = rms_matmul(memf, norm_mem[layer], w_kv[layer], tn=d)
        mk_p.append(kv[:, :d].reshape(bp, n_mem, d))
        mv_p.append(kv[:, d:].reshape(bp, n_mem, d))
    p_cache_mem_k = jnp.stack(mk_p).reshape(depth, bp, n_mem, H_X, HD_X)
    p_cache_mem_v = jnp.stack(mv_p).reshape(depth, bp, n_mem, H_X, HD_X)

    n_odd = w_in_c.shape[0]
    z_gdn = jnp.zeros((n_even, bp, H_A, DK_A, DV_A), F32)
    z_conv = jnp.zeros((n_even, bp, CONV_A - 1, qkv_w), F32)
    z_sc = jnp.zeros((n_even, bp, CONV_B - 1, D_B), F32)
    z_ret = jnp.zeros((n_odd, bp, H_C, DK_C, DV_C), F32)
    pos_p = jnp.arange(lp, dtype=I32)
    pos_s = 16384 + jnp.arange(ls, dtype=I32)

    y_p, p_gdn, p_conv, p_sc, p_ret = _trunk(
        x_prompt.reshape(bp * lp, d), pos_p, z_gdn, z_conv, z_sc, z_ret, mk_p, mv_p, wts, batch=bp, seqlen=lp)
    mk_s = [cache_mem_k[layer].reshape(bs, n_mem, d) for layer in range(depth)]
    mv_s = [cache_mem_v[layer].reshape(bs, n_mem, d) for layer in range(depth)]
    y_s, s_gdn, s_conv, s_sc, s_ret = _trunk(
        x_sample.reshape(bs * ls, d), pos_s, state_gdn, state_gdn_conv, state_sconv, state_ret, mk_s, mv_s, wts,
        batch=bs, seqlen=ls)
    return (y_p.reshape(bp, lp, d), y_s.reshape(bs, ls, d), p_gdn, p_conv, p_sc, p_ret, p_cache_mem_k,
            p_cache_mem_v, s_gdn, s_conv, s_sc, s_ret)
```

```python
import functools
import math

import jax
import jax.numpy as jnp
import numpy as np
from jax import lax
from jax.experimental import pallas as pl
from jax.experimental.pallas import tpu as pltpu

F32 = jnp.float32
BF16 = jnp.bfloat16
I32 = jnp.int32

EPS = 1e-6
ROPE_BASE = 10000.0

D_MODEL = 1024
H_A, DK_A, DV_A, CONV_A = 4, 128, 128, 4
W_QKV_A = 3 * H_A * DK_A
D_B, CONV_B = D_MODEL // 2, 3
H_C, DK_C, DV_C = 4, 256, 512
H_X, HD_X, N_MEM = 4, 256, 256
N_GROUPS, E_PER_GROUP, N_EXPERTS, D_EXPERT = 4, 8, 32, 512
GDN_CHUNK = 64

LANES = 128
SUBLANES = 8
GDN_STACK = 256
VMEM_LIMIT = 56 * 1024 * 1024

PA_COLS = 3840
PA_BA = 3584
R_E0 = N_GROUPS
MOE_BLK = 256


def _cparams(sem):
    return pltpu.CompilerParams(dimension_semantics=sem, vmem_limit_bytes=VMEM_LIMIT)


def _dot(a, b, trans_a=False, trans_b=False):
    dn = (((0 if trans_a else 1,), (1 if trans_b else 0,)), ((), ()))
    return lax.dot_general(a.astype(BF16), b.astype(BF16), dn, preferred_element_type=F32)


def _silu(x):
    return x * (1.0 / (1.0 + jnp.exp(-x)))


def _sigmoid(x):
    return 1.0 / (1.0 + jnp.exp(-x))


def _rms_matmul_kernel(x_ref, g_ref, w_ref, o_ref, xn_ref):
    @pl.when(pl.program_id(1) == 0)
    def _():
        x = x_ref[...]
        ms = jnp.mean(x * x, axis=-1, keepdims=True)
        xn_ref[...] = (x * lax.rsqrt(ms + EPS) * g_ref[...]).astype(BF16)

    o_ref[...] = jnp.dot(xn_ref[...], w_ref[...], preferred_element_type=F32).astype(o_ref.dtype)


def rms_matmul(x, g, w, *, tn, out_dtype=F32, tm=1024):
    t, d = x.shape
    n = w.shape[1]
    tm = min(tm, t)
    return pl.pallas_call(
        _rms_matmul_kernel,
        grid=(t // tm, n // tn),
        in_specs=[pl.BlockSpec((tm, d), lambda i, j: (i, 0)),
                  pl.BlockSpec((1, d), lambda i, j: (0, 0)),
                  pl.BlockSpec((d, tn), lambda i, j: (0, j))],
        out_specs=pl.BlockSpec((tm, tn), lambda i, j: (i, j)),
        out_shape=jax.ShapeDtypeStruct((t, n), out_dtype),
        scratch_shapes=[pltpu.VMEM((tm, d), BF16)],
        compiler_params=_cparams(("parallel", "arbitrary")),
        name="rms_matmul",
    )(x, g.reshape(1, d), w)


def _matmul_res_kernel(a_ref, w_ref, r_ref, o_ref):
    o_ref[...] = r_ref[...] + jnp.dot(a_ref[...].astype(BF16), w_ref[...], preferred_element_type=F32)


def matmul_res(a, w, res, *, tm=512):
    t, k = a.shape
    n = w.shape[1]
    tm = min(tm, t)
    return pl.pallas_call(
        _matmul_res_kernel,
        grid=(t // tm,),
        in_specs=[pl.BlockSpec((tm, k), lambda i: (i, 0)),
                  pl.BlockSpec((k, n), lambda i: (0, 0)),
                  pl.BlockSpec((tm, n), lambda i: (i, 0))],
        out_specs=pl.BlockSpec((tm, n), lambda i: (i, 0)),
        out_shape=jax.ShapeDtypeStruct((t, n), F32),
        compiler_params=_cparams(("parallel",)),
        name="matmul_res",
    )(a, w, res)


def _rmsnorm_kernel(x_ref, g_ref, o_ref):
    x = x_ref[...]
    ms = jnp.mean(x * x, axis=-1, keepdims=True)
    o_ref[...] = x * lax.rsqrt(ms + EPS) * g_ref[...]


def rmsnorm_rows(x, g, *, tm=512):
    t, d = x.shape
    tm = min(tm, t)
    return pl.pallas_call(
        _rmsnorm_kernel,
        grid=(t // tm,),
        in_specs=[pl.BlockSpec((tm, d), lambda i: (i, 0)), pl.BlockSpec((1, d), lambda i: (0, 0))],
        out_specs=pl.BlockSpec((tm, d), lambda i: (i, 0)),
        out_shape=jax.ShapeDtypeStruct((t, d), F32),
        compiler_params=_cparams(("parallel",)),
        name="final_rmsnorm",
    )(x, g.reshape(1, d))


def _causal_conv(x, hist, w_ref, width, seq8):
    r = x.shape[0]
    taps = [w_ref[j:j + 1, :] for j in range(width)]

    def head(x8, h8):
        n = x8.shape[0]
        t = lax.broadcasted_iota(I32, (n, 1), 0) % SUBLANES
        y = taps[width - 1] * x8
        for s in range(1, width):
            prev = pltpu.roll(h8, (n + s - (width - 1)) % n, 0) if s != width - 1 else h8
            y = y + taps[width - 1 - s] * jnp.where(t >= s, pltpu.roll(x8, s, 0), prev)
        return y

    if seq8:
        return head(x, hist)
    y = taps[width - 1] * x
    for s in range(1, width):
        y = y + taps[width - 1 - s] * pltpu.roll(x, s, 0)
    return jnp.concatenate([head(x[:SUBLANES], hist), y[SUBLANES:]], axis=0)


def _unit_lower_inverse(m, c, ri, ci):
    base = min(c, 16)
    eye = jnp.where(ri == ci, 1.0, 0.0).astype(F32)
    d = jnp.where((ri // base) == (ci // base), m, 0.0)
    p = eye - d
    dp = d
    k = 2
    while k < base:
        dp = _dot(dp, dp)
        p = _dot(p, eye + dp)
        k *= 2
    s = base
    while s < c:
        off = jnp.where(((ri // (2 * s)) == (ci // (2 * s))) & ((ri // s) != (ci // s)), m, 0.0)
        p = p - _dot(p, _dot(off, p))
        s *= 2
    return p


def _gdn_unit(q, k, v, bfull, gfull, states, c):
    n = GDN_STACK
    ri = lax.broadcasted_iota(I32, (n, n), 0)
    ci = lax.broadcasted_iota(I32, (n, n), 1)
    same = (ri // c) == (ci // c)
    incl = same & (ri >= ci)
    strict = same & (ri > ci)
    g2 = jnp.concatenate([gfull, gfull], axis=1)
    g_row = jnp.sum(jnp.where(ri == ci, g2, 0.0), axis=0, keepdims=True)
    gc_col = jnp.sum(jnp.where(incl, g_row, 0.0), axis=1, keepdims=True)
    gc_row = jnp.sum(jnp.where(same & (ri <= ci), g2, 0.0), axis=0, keepdims=True)
    gl_col = jnp.sum(jnp.where(same, g_row, 0.0), axis=1, keepdims=True)
    decay = jnp.where(incl, jnp.exp(jnp.where(incl, gc_col - gc_row, 0.0)), 0.0)
    egc = jnp.exp(gc_col)
    ekd = jnp.exp(gl_col - gc_col)
    egl = jnp.exp(gl_col)

    kb = k * bfull
    mm = jnp.where(strict, _dot(kb, k, trans_b=True) * decay, 0.0)
    qk = _dot(q, k, trans_b=True) * decay
    tinv = _unit_lower_inverse(mm, c, ri, ci)
    uw = _dot(tinv, jnp.concatenate([v * bfull, kb * egc], axis=1))
    u, w = uw[:, :DV_A], uw[:, DV_A:]
    qd = q * egc
    kd = k * ekd

    nprob = n // c
    ws, qs = [], []
    for p in range(nprob):
        sl = slice(p * c, (p + 1) * c)
        ws.append(_dot(w[sl], states[p]))
        qs.append(_dot(qd[sl], states[p]))
    vn = u - jnp.concatenate(ws, axis=0)
    o = _dot(qk, vn) + jnp.concatenate(qs, axis=0)
    new_states = []
    for p in range(nprob):
        sl = slice(p * c, (p + 1) * c)
        new_states.append(states[p] * egl[p * c:p * c + 1, :] + _dot(kd[sl], vn[sl], trans_a=True))
    return o, new_states


def _gdn_kernel(qkv_ref, z_ref, sch_ref, scb_ref, scc_ref, ba_ref, hq_ref, hs_ref, wq_ref, ws_ref,
                prm_ref, gn_ref, s0_ref, o_ref, sn_ref, cq_ref, cs_ref, s_scr, hq_scr, hs_scr,
                *, seq8, c, nu, nl):
    unit = GDN_CHUNK
    l = jnp.int32(0) if seq8 else pl.program_id(1)

    if seq8:
        hq = hq_ref[...]
        hs = hs_ref[...]
    else:
        @pl.when(l == 0)
        def _():
            hq_scr[...] = hq_ref[0]
            hs_scr[...] = hs_ref[0]
            s_scr[...] = s0_ref[0]
        hq = hq_scr[...]
        hs = hs_scr[...]

    x = qkv_ref[...]
    rows = x.shape[0]
    u_sc = scc_ref[...] * sch_ref[...]
    if seq8:
        cq_ref[...] = pltpu.roll(x, rows - SUBLANES + CONV_A - 1, 0)
        cs_ref[...] = pltpu.roll(u_sc, rows - SUBLANES + CONV_B - 1, 0)
    else:
        hq_scr[...] = pltpu.roll(x[rows - SUBLANES:], CONV_A - 1, 0)
        hs_scr[...] = pltpu.roll(u_sc[rows - SUBLANES:], CONV_B - 1, 0)

    xc = _silu(_causal_conv(x, hq, wq_ref, CONV_A, seq8))
    yb = scb_ref[...] * _causal_conv(u_sc, hs, ws_ref, CONV_B, seq8)
    o_ref[:, H_A * DV_A:] = yb.astype(o_ref.dtype)

    ba = ba_ref[...]
    beta_all = _sigmoid(ba)
    sp = jnp.maximum(ba + prm_ref[1:2, :], 0.0) + jnp.log1p(jnp.exp(-jnp.abs(ba + prm_ref[1:2, :])))
    g_all = -jnp.exp(prm_ref[0:1, :]) * sp
    gn = gn_ref[...]

    def head_cols(a, base):
        return a[:, base * DK_A:(base + 1) * DK_A]

    for ui in range(nu):
        rs = slice(ui * unit, (ui + 1) * unit)
        qs, ks, vs, zs, bs, gs = [], [], [], [], [], []
        for h in range(H_A):
            qh = head_cols(xc, h)[rs]
            kh = head_cols(xc, H_A + h)[rs]
            qs.append(qh * lax.rsqrt(jnp.sum(qh * qh, axis=-1, keepdims=True) + EPS) * (DK_A ** -0.5))
            ks.append(kh * lax.rsqrt(jnp.sum(kh * kh, axis=-1, keepdims=True) + EPS))
            vs.append(head_cols(xc, 2 * H_A + h)[rs])
            zs.append(z_ref[rs, h * DV_A:(h + 1) * DV_A])
            bs.append(jnp.broadcast_to(beta_all[rs, h:h + 1], (unit, LANES)))
            gs.append(jnp.broadcast_to(g_all[rs, H_A + h:H_A + h + 1], (unit, LANES)))
        cat = lambda xs: jnp.concatenate(xs, axis=0)
        nprob = GDN_STACK // c
        if seq8:
            states = [s0_ref[p % SUBLANES, p // SUBLANES] for p in range(nprob)]
        else:
            states = [s_scr[p] for p in range(nprob)]
        o, new_states = _gdn_unit(cat(qs), cat(ks), cat(vs), cat(bs), cat(gs), states, c)
        for p in range(nprob):
            if seq8:
                sn_ref[p % SUBLANES, p // SUBLANES] = new_states[p]
            else:
                s_scr[p] = new_states[p]
        ms = jnp.mean(o * o, axis=-1, keepdims=True)
        og = o * lax.rsqrt(ms + EPS) * gn * _silu(cat(zs))
        for h in range(H_A):
            o_ref[rs, h * DV_A:(h + 1) * DV_A] = og[h * unit:(h + 1) * unit].astype(o_ref.dtype)

    if not seq8:
        @pl.when(l == nl - 1)
        def _():
            sn_ref[0] = s_scr[...]
            cq_ref[0] = hq_scr[...]
            cs_ref[0] = hs_scr[...]


def gdn_core(p, hist_q, hist_s, w_conv_qkv, w_conv_sc, prm, gn, s0, *, batch, seqlen):
    t = batch * seqlen
    seq8 = seqlen == SUBLANES
    if seq8:
        c, nu, rows = SUBLANES, 1, GDN_CHUNK
        nb = rows // seqlen
        grid = (batch // nb,)
        rmap = lambda i: i
        sem = ("parallel",)
        hq_spec = pl.BlockSpec((rows, W_QKV_A), lambda i: (i, 0))
        hs_spec = pl.BlockSpec((rows, D_B), lambda i: (i, 0))
        s_spec = pl.BlockSpec((nb, H_A, DK_A, DV_A), lambda i: (i, 0, 0, 0))
        hist_q = hist_q.reshape(t, W_QKV_A)
        hist_s = hist_s.reshape(t, D_B)
        cq_shape, cs_shape = (t, W_QKV_A), (t, D_B)
        nl = 1
    else:
        c, rows = GDN_CHUNK, min(seqlen, 256)
        nu = rows // GDN_CHUNK
        nl = seqlen // rows
        grid = (batch, nl)
        rmap = lambda b, l: b * nl + l
        sem = ("parallel", "arbitrary")
        hq_spec = pl.BlockSpec((1, SUBLANES, W_QKV_A), lambda b, l: (b, 0, 0))
        hs_spec = pl.BlockSpec((1, SUBLANES, D_B), lambda b, l: (b, 0, 0))
        s_spec = pl.BlockSpec((1, H_A, DK_A, DV_A), lambda b, l: (b, 0, 0, 0))
        cq_shape, cs_shape = (batch, SUBLANES, W_QKV_A), (batch, SUBLANES, D_B)

    def col(width, blk):
        return pl.BlockSpec((rows, width), lambda *ix: (rmap(*ix), blk))

    const = lambda shape: pl.BlockSpec(shape, lambda *ix: (0,) * len(shape))
    kern = functools.partial(_gdn_kernel, seq8=seq8, c=c, nu=nu, nl=nl)
    return pl.pallas_call(
        kern,
        grid=grid,
        in_specs=[col(W_QKV_A, 0), col(D_B, 3), col(D_B, 4), col(D_B, 5), col(D_B, 6),
                  col(LANES, PA_BA // LANES), hq_spec, hs_spec,
                  const((CONV_A, W_QKV_A)), const((CONV_B, D_B)), const((SUBLANES, LANES)),
                  const((1, DV_A)), s_spec],
        out_specs=[pl.BlockSpec((rows, D_MODEL), lambda *ix: (rmap(*ix), 0)), s_spec, hq_spec, hs_spec],
        out_shape=[jax.ShapeDtypeStruct((t, D_MODEL), BF16),
                   jax.ShapeDtypeStruct((batch, H_A, DK_A, DV_A), F32),
                   jax.ShapeDtypeStruct(cq_shape, F32),
                   jax.ShapeDtypeStruct(cs_shape, F32)],
        scratch_shapes=[pltpu.VMEM((H_A, DK_A, DV_A), F32),
                        pltpu.VMEM((SUBLANES, W_QKV_A), F32),
                        pltpu.VMEM((SUBLANES, D_B), F32)],
        compiler_params=_cparams(sem),
        name="gdn_core",
    )(p, p, p, p, p, p, hist_q, hist_s, w_conv_qkv, w_conv_sc, prm, gn.reshape(1, DV_A), s0)


def _ret_kernel(q_ref, k_ref, v_ref, gate_ref, cos_ref, sin_ref, dm_ref, qd_ref, kd_ref, cd_ref,
                rn_ref, r0_ref, o_ref, rnew_ref, r_scr, *, nseq, c, nl):
    l = pl.program_id(1)

    @pl.when(l == 0)
    def _():
        r_scr[...] = r0_ref[...]

    cos = cos_ref[...]
    sin = sin_ref[...]
    half = DK_C // 2

    def rot(x):
        x1, x2 = x[:, :half], x[:, half:]
        return jnp.concatenate([x1 * cos - x2 * sin, x1 * sin + x2 * cos], axis=1)

    for h in range(H_C):
        qh = rot(q_ref[:, h * DK_C:(h + 1) * DK_C])
        kh = rot(k_ref[:, h * DK_C:(h + 1) * DK_C]) * (DK_C ** -0.5)
        vh = v_ref[:, h * DV_C:(h + 1) * DV_C]
        s = _dot(qh, kh, trans_b=True) * dm_ref[h]
        o = _dot(s, vh)
        qdh = qh * qd_ref[h]
        kdh = kh * kd_ref[h]
        cd = cd_ref[h][0:1, 0:1]
        inter = []
        for sq in range(nseq):
            sl = slice(sq * c, (sq + 1) * c)
            r_old = r_scr[sq, h]
            inter.append(_dot(qdh[sl], r_old))
            r_scr[sq, h] = r_old * cd + _dot(kdh[sl], vh[sl], trans_a=True)
        o = o + (inter[0] if nseq == 1 else jnp.concatenate(inter, axis=0))
        ms = jnp.mean(o * o, axis=-1, keepdims=True)
        on = o * lax.rsqrt(ms + EPS) * rn_ref[:, h * DV_C:(h + 1) * DV_C]
        o_ref[:, h * DV_C:(h + 1) * DV_C] = (_silu(gate_ref[:, h * DV_C:(h + 1) * DV_C]) * on).astype(o_ref.dtype)

    @pl.when(l == nl - 1)
    def _():
        rnew_ref[...] = r_scr[...]


def ret_core(p, pos, ret_norm, r0, *, batch, seqlen):
    t = batch * seqlen
    if seqlen == SUBLANES:
        nseq, c = 2, seqlen
        nl = 1
        grid = (batch // nseq, 1)
        rmap = lambda b, l: b
    else:
        nseq, c = 1, min(seqlen, 256)
        nl = seqlen // c
        grid = (batch, nl)
        rmap = lambda b, l: b * nl + l
    rows = nseq * c
    half = DK_C // 2
    inv = ROPE_BASE ** (-jnp.arange(half, dtype=F32) / half)
    ang = pos.astype(F32)[:, None] * inv[None, :]
    cos, sin = jnp.cos(ang), jnp.sin(ang)
    if nseq > 1:
        cos, sin = jnp.tile(cos, (nseq, 1)), jnp.tile(sin, (nseq, 1))
    lg = jnp.log(1.0 - 2.0 ** (-5.0 - jnp.arange(H_C, dtype=F32)))[:, None]
    i = jnp.arange(c, dtype=F32)
    incl = i[:, None] >= i[None, :]
    dmat = jnp.exp(jnp.where(incl[None], (i[:, None] - i[None, :])[None] * lg[..., None], -jnp.inf))
    if nseq > 1:
        dmat = jnp.kron(jnp.eye(nseq, dtype=F32)[None], dmat)
    qd = jnp.tile(jnp.exp((i + 1.0)[None] * lg), (1, nseq))[..., None] * jnp.ones((1, 1, DK_C), F32)
    kd = jnp.tile(jnp.exp((c - 1.0 - i)[None] * lg), (1, nseq))[..., None] * jnp.ones((1, 1, DK_C), F32)
    cd = jnp.exp(c * lg)[..., None] * jnp.ones((1, SUBLANES, LANES), F32)

    const = lambda shape: pl.BlockSpec(shape, lambda b, l: (0,) * len(shape))
    trig_spec = (const((rows, half)) if nseq > 1
                 else pl.BlockSpec((rows, half), lambda b, l: (l, 0)))
    kern = functools.partial(_ret_kernel, nseq=nseq, c=c, nl=nl)
    hk = H_C * DK_C
    hv = H_C * DV_C
    return pl.pallas_call(
        kern,
        grid=grid,
        in_specs=[pl.BlockSpec((rows, hk), lambda b, l: (rmap(b, l), 0)),
                  pl.BlockSpec((rows, hk), lambda b, l: (rmap(b, l), 1)),
                  pl.BlockSpec((rows, hv), lambda b, l: (rmap(b, l), 1)),
                  pl.BlockSpec((rows, hv), lambda b, l: (rmap(b, l), 2)),
                  trig_spec, trig_spec,
                  const((H_C, rows, rows)), const((H_C, rows, DK_C)), const((H_C, rows, DK_C)),
                  const((H_C, SUBLANES, LANES)), const((1, hv)),
                  pl.BlockSpec((nseq, H_C, DK_C, DV_C), lambda b, l: (b, 0, 0, 0))],
        out_specs=[pl.BlockSpec((rows, hv), lambda b, l: (rmap(b, l), 0)),
                   pl.BlockSpec((nseq, H_C, DK_C, DV_C), lambda b, l: (b, 0, 0, 0))],
        out_shape=[jax.ShapeDtypeStruct((t, hv), BF16),
                   jax.ShapeDtypeStruct((batch, H_C, DK_C, DV_C), F32)],
        scratch_shapes=[pltpu.VMEM((nseq, H_C, DK_C, DV_C), F32)],
        compiler_params=_cparams(("parallel", "arbitrary")),
        name="ret_core",
    )(p, p, p, p, cos, sin, dmat, qd, kd, cd, ret_norm.reshape(1, hv), r0)


def _xattn_kernel(q_ref, mk_ref, mv_ref, o_ref, *, nb, lq):
    for b in range(nb):
        rs = slice(b * lq, (b + 1) * lq)
        for h in range(H_X):
            cs = slice(h * HD_X, (h + 1) * HD_X)
            s = _dot(q_ref[rs, cs], mk_ref[0, b, :, h, :], trans_b=True) * (HD_X ** -0.5)
            m = jnp.max(s, axis=-1, keepdims=True)
            e = jnp.exp(s - m)
            p = e / jnp.sum(e, axis=-1, keepdims=True)
            o_ref[rs, cs] = _dot(p, mv_ref[0, b, :, h, :]).astype(o_ref.dtype)


def xattn_core(q, mk, mv, layer, *, batch, seqlen):
    t = batch * seqlen
    d = H_X * HD_X
    if seqlen == SUBLANES:
        nb, lq, nl = 2, seqlen, 1
    else:
        nb, lq = 1, min(seqlen, 512)
        nl = seqlen // lq
    rows = nb * lq
    kern = functools.partial(_xattn_kernel, nb=nb, lq=lq)
    mem_spec = pl.BlockSpec((1, nb, N_MEM, H_X, HD_X), lambda b, l: (layer, b, 0, 0, 0))
    return pl.pallas_call(
        kern,
        grid=(batch // nb, nl),
        in_specs=[pl.BlockSpec((rows, d), lambda b, l: (b * nl + l, 0)), mem_spec, mem_spec],
        out_specs=pl.BlockSpec((rows, d), lambda b, l: (b * nl + l, 0)),
        out_shape=jax.ShapeDtypeStruct((t, d), q.dtype),
        compiler_params=_cparams(("parallel", "arbitrary")),
        name="xattn_core",
    )(q, mk, mv)


def _route_kernel(x_ref, g_ref, wr_ref, br_ref, h_ref, meta_ref, wts_ref, cnt_ref, cnt_scr):
    i = pl.program_id(0)

    @pl.when(i == 0)
    def _():
        cnt_scr[...] = jnp.zeros_like(cnt_scr)

    x = x_ref[...]
    tm = x.shape[0]
    ms = jnp.mean(x * x, axis=-1, keepdims=True)
    h = x * lax.rsqrt(ms + EPS) * g_ref[...]
    h_ref[...] = h
    logits = jnp.dot(h, wr_ref[...], preferred_element_type=F32, precision=lax.Precision.HIGHEST) + br_ref[...]
    lane_i = lax.broadcasted_iota(I32, (tm, LANES), 1)
    lane = lane_i.astype(F32)
    neg = jnp.float32(-3.0e38)
    big = jnp.float32(LANES)
    is_g = lane_i < N_GROUPS
    gl = jnp.where(is_g, logits, neg)
    gmax = jnp.max(gl, axis=1, keepdims=True)
    grp = jnp.min(jnp.where(gl == gmax, lane, big), axis=1, keepdims=True)
    gsum = jnp.sum(jnp.where(is_g, jnp.exp(jnp.where(is_g, logits - gmax, 0.0)), 0.0), axis=1, keepdims=True)
    p_grp = 1.0 / gsum
    in_grp = ((lane_i >= R_E0) & (lane_i < R_E0 + N_EXPERTS)
              & (jnp.floor((lane - R_E0) * (1.0 / E_PER_GROUP)) == grp))
    el = jnp.where(in_grp, logits, neg)
    m1 = jnp.max(el, axis=1, keepdims=True)
    i1 = jnp.min(jnp.where(el == m1, lane, big), axis=1, keepdims=True)
    el2 = jnp.where(lane == i1, neg, el)
    m2 = jnp.max(el2, axis=1, keepdims=True)
    i2 = jnp.min(jnp.where(el2 == m2, lane, big), axis=1, keepdims=True)
    esum = jnp.sum(jnp.where(in_grp, jnp.exp(jnp.where(in_grp, logits - m1, 0.0)), 0.0), axis=1, keepdims=True)
    p1 = 1.0 / esum
    p2 = jnp.exp(m2 - m1) / esum
    tot = p1 + p2
    w1 = p_grp * (p1 / tot)
    w2 = p_grp * (p2 / tot)
    wts_ref[...] = jnp.where(lane_i == 0, w1, jnp.where(lane_i == 1, w2, 0.0))

    oh1 = (lane == i1).astype(F32)
    oh2 = (lane == i2).astype(F32)
    rr = lax.broadcasted_iota(I32, (tm, tm), 0)
    cc = lax.broadcasted_iota(I32, (tm, tm), 1)
    tri = (rr > cc).astype(BF16)
    base = cnt_scr[0:1, :]
    c1 = jnp.sum(oh1, axis=0, keepdims=True)
    c2 = jnp.sum(oh2, axis=0, keepdims=True)
    r1 = jnp.sum(oh1 * (jnp.dot(tri, oh1.astype(BF16), preferred_element_type=F32) + base), axis=1, keepdims=True)
    r2 = jnp.sum(oh2 * (jnp.dot(tri, oh2.astype(BF16), preferred_element_type=F32) + base + c1), axis=1, keepdims=True)
    new_cnt = base + c1 + c2
    cnt_scr[...] = jnp.broadcast_to(new_cnt, cnt_scr.shape)
    cnt_ref[...] = jnp.broadcast_to(new_cnt, cnt_ref.shape)
    meta = jnp.where(lane_i == 0, i1, jnp.where(lane_i == 1, i2, 0.0))
    meta = jnp.where(lane_i == 2, r1, jnp.where(lane_i == 3, r2, meta))
    meta_ref[...] = meta.astype(I32)


def moe_route(x, g, wr, br, *, tm=256):
    t, d = x.shape
    tm = min(tm, t)
    return pl.pallas_call(
        _route_kernel,
        grid=(t // tm,),
        in_specs=[pl.BlockSpec((tm, d), lambda i: (i, 0)),
                  pl.BlockSpec((1, d), lambda i: (0, 0)),
                  pl.BlockSpec((d, LANES), lambda i: (0, 0)),
                  pl.BlockSpec((1, LANES), lambda i: (0, 0))],
        out_specs=[pl.BlockSpec((tm, d), lambda i: (i, 0)),
                   pl.BlockSpec((tm, LANES), lambda i: (i, 0)),
                   pl.BlockSpec((tm, LANES), lambda i: (i, 0)),
                   pl.BlockSpec((SUBLANES, LANES), lambda i: (0, 0))],
        out_shape=[jax.ShapeDtypeStruct((t, d), F32),
                   jax.ShapeDtypeStruct((t, LANES), I32),
                   jax.ShapeDtypeStruct((t, LANES), F32),
                   jax.ShapeDtypeStruct((SUBLANES, LANES), F32)],
        scratch_shapes=[pltpu.VMEM((SUBLANES, LANES), F32)],
        compiler_params=_cparams(("arbitrary",)),
        name="moe_route",
    )(x, g.reshape(1, d), wr, br)


def _plan_kernel(cnt_ref, meta_ref, dest_ref, be_ref, *, nblk_pad):
    cnt = cnt_ref[...]
    lane8 = lax.broadcasted_iota(I32, (SUBLANES, LANES), 1)
    padded = jnp.ceil(cnt / MOE_BLK) * MOE_BLK
    pend = padded
    s = 1
    while s < LANES:
        pend = pend + jnp.where(lane8 >= s, pltpu.roll(pend, s, 1), 0.0)
        s *= 2
    pstart = (pend - padded)[0:1, :]
    meta = meta_ref[...].astype(F32)
    tm = meta.shape[0]
    lane_i = lax.broadcasted_iota(I32, (tm, LANES), 1)
    lane = lane_i.astype(F32)
    col = lambda j: jnp.sum(jnp.where(lane_i == j, meta, 0.0), axis=1, keepdims=True)
    e1, e2, r1, r2 = col(0), col(1), col(2), col(3)
    d1 = jnp.sum(jnp.where(lane == e1, pstart, 0.0), axis=1, keepdims=True) + r1
    d2 = jnp.sum(jnp.where(lane == e2, pstart, 0.0), axis=1, keepdims=True) + r2
    dest_ref[...] = jnp.where(lane_i == 0, d1, jnp.where(lane_i == 1, d2, 0.0)).astype(I32)
    bi = (lax.broadcasted_iota(I32, (nblk_pad, LANES), 0) * MOE_BLK).astype(F32)
    lane_b = lax.broadcasted_iota(I32, (nblk_pad, LANES), 1)
    is_e = (lane_b >= R_E0) & (lane_b < R_E0 + N_EXPERTS)
    nfull = jnp.sum(jnp.where(is_e & (bi >= pend[0:1, :]), 1.0, 0.0), axis=1, keepdims=True)
    be = jnp.minimum(nfull, N_EXPERTS - 1.0)
    nused = jnp.max(pend[0:1, :], axis=1, keepdims=True) / MOE_BLK
    row = lax.broadcasted_iota(I32, (nblk_pad, LANES), 0)
    be_ref[...] = jnp.where(row == nblk_pad - 1, nused, be).astype(I32)


def moe_plan(cnt, meta, *, nblk_pad, tm=512):
    t = meta.shape[0]
    tm = min(tm, t)
    kern = functools.partial(_plan_kernel, nblk_pad=nblk_pad)
    return pl.pallas_call(
        kern,
        grid=(t // tm,),
        in_specs=[pl.BlockSpec((SUBLANES, LANES), lambda i: (0, 0)),
                  pl.BlockSpec((tm, LANES), lambda i: (i, 0))],
        out_specs=[pl.BlockSpec((tm, LANES), lambda i: (i, 0)),
                   pl.BlockSpec((nblk_pad, LANES), lambda i: (0, 0))],
        out_shape=[jax.ShapeDtypeStruct((t, LANES), I32),
                   jax.ShapeDtypeStruct((nblk_pad, LANES), I32)],
        compiler_params=_cparams(("arbitrary",)),
        name="moe_plan",
    )(cnt, meta)


def _dispatch_kernel(dest_ref, h_ref, xd_in, xd_hbm, sem, *, tm):
    del xd_in

    def row_copy(t, k):
        return pltpu.make_async_copy(h_ref.at[pl.ds(t, 1)], xd_hbm.at[pl.ds(dest_ref[2 * t + k], 1)], sem)

    def start(t, carry):
        row_copy(t, 0).start(priority=0)
        row_copy(t, 1).start(priority=1)
        return carry

    def wait(t, carry):
        row_copy(t, 0).wait()
        row_copy(t, 1).wait()
        return carry

    lax.fori_loop(0, tm, start, 0)
    lax.fori_loop(0, tm, wait, 0)


def moe_dispatch(dest_flat, h, xd_zero, *, tm=512):
    t, d = h.shape
    tm = min(tm, t)
    kern = functools.partial(_dispatch_kernel, tm=tm)
    return pl.pallas_call(
        kern,
        grid=(t // tm,),
        in_specs=[pl.BlockSpec((2 * tm,), lambda i: (i,), memory_space=pltpu.SMEM),
                  pl.BlockSpec((tm, d), lambda i: (i, 0)),
                  pl.BlockSpec(memory_space=pl.ANY)],
        out_specs=pl.BlockSpec(memory_space=pl.ANY),
        out_shape=jax.ShapeDtypeStruct(xd_zero.shape, F32),
        scratch_shapes=[pltpu.SemaphoreType.DMA],
        input_output_aliases={2: 0},
        compiler_params=_cparams(("arbitrary",)),
        name="moe_dispatch",
    )(dest_flat, h, xd_zero)


def _experts_kernel(be_ref, x_ref, wg_ref, wu_ref, wd_ref, o_ref, wg_s, wu_s, wd_s, *, nblk_pad):
    i = pl.program_id(0)
    nused = be_ref[nblk_pad - 1]

    @pl.when(i < nused)
    def _():
        prev = be_ref[jnp.maximum(i - 1, 0)]

        @pl.when((i == 0) | (be_ref[i] != prev))
        def _():
            wg_s[...] = wg_ref[0].astype(BF16)
            wu_s[...] = wu_ref[0].astype(BF16)
            wd_s[...] = wd_ref[0].astype(BF16)

        x = x_ref[...].astype(BF16)
        g = jnp.dot(x, wg_s[...], preferred_element_type=F32)
        u = jnp.dot(x, wu_s[...], preferred_element_type=F32)
        a = (_silu(g) * u).astype(BF16)
        o_ref[...] = jnp.dot(a, wd_s[...], preferred_element_type=F32)

    @pl.when(i >= nused)
    def _():
        o_ref[...] = jnp.zeros_like(o_ref)


def moe_experts(be_flat, xd, w_gate, w_up, w_down, e0, *, nblk, nblk_pad):
    d, f = w_gate.shape[1], w_gate.shape[2]
    kern = functools.partial(_experts_kernel, nblk_pad=nblk_pad)
    grid_spec = pltpu.PrefetchScalarGridSpec(
        num_scalar_prefetch=1,
        grid=(nblk,),
        in_specs=[pl.BlockSpec((MOE_BLK, d), lambda i, be: (i, 0)),
                  pl.BlockSpec((1, d, f), lambda i, be: (e0 + be[i], 0, 0)),
                  pl.BlockSpec((1, d, f), lambda i, be: (e0 + be[i], 0, 0)),
                  pl.BlockSpec((1, f, d), lambda i, be: (e0 + be[i], 0, 0))],
        out_specs=pl.BlockSpec((MOE_BLK, d), lambda i, be: (i, 0)),
        scratch_shapes=[pltpu.VMEM((d, f), BF16), pltpu.VMEM((d, f), BF16), pltpu.VMEM((f, d), BF16)],
    )
    return pl.pallas_call(
        kern,
        grid_spec=grid_spec,
        out_shape=jax.ShapeDtypeStruct(xd.shape, F32),
        compiler_params=_cparams(("arbitrary",)),
        name="moe_experts",
    )(be_flat, xd, w_gate, w_up, w_down)


def _combine_kernel(dest_ref, x_ref, wts_ref, yd_hbm, o_ref, r0, r1, sem, *, tm):
    def row_copy(t, k, buf):
        return pltpu.make_async_copy(yd_hbm.at[pl.ds(dest_ref[2 * t + k], 1)], buf.at[pl.ds(t, 1)], sem)

    def start(t, carry):
        row_copy(t, 0, r0).start(priority=0)
        row_copy(t, 1, r1).start(priority=1)
        return carry

    def wait(t, carry):
        row_copy(t, 0, r0).wait()
        row_copy(t, 1, r1).wait()
        return carry

    lax.fori_loop(0, tm, start, 0)
    lax.fori_loop(0, tm, wait, 0)
    w = wts_ref[...]
    o_ref[...] = x_ref[...] + r0[...] * w[:, 0:1] + r1[...] * w[:, 1:2]


def moe_combine(dest_flat, x, wts, yd, *, tm=512):
    t, d = x.shape
    tm = min(tm, t)
    kern = functools.partial(_combine_kernel, tm=tm)
    return pl.pallas_call(
        kern,
        grid=(t // tm,),
        in_specs=[pl.BlockSpec((2 * tm,), lambda i: (i,), memory_space=pltpu.SMEM),
                  pl.BlockSpec((tm, d), lambda i: (i, 0)),
                  pl.BlockSpec((tm, LANES), lambda i: (i, 0)),
                  pl.BlockSpec(memory_space=pl.ANY)],
        out_specs=pl.BlockSpec((tm, d), lambda i: (i, 0)),
        out_shape=jax.ShapeDtypeStruct((t, d), F32),
        scratch_shapes=[pltpu.VMEM((tm, d), F32), pltpu.VMEM((tm, d), F32), pltpu.SemaphoreType.DMA],
        compiler_params=_cparams(("arbitrary",)),
        name="moe_combine",
    )(dest_flat, x, wts, yd)


def moe_block(x, g, wr, br, w_gate, w_up, w_down, e0):
    t, d = x.shape
    nblk = (2 * t) // MOE_BLK + N_EXPERTS
    nblk_pad = -(-(nblk + 1) // SUBLANES) * SUBLANES
    h, meta, wts, cnt = moe_route(x, g, wr, br)
    dest, be = moe_plan(cnt, meta, nblk_pad=nblk_pad)
    dest_flat = dest[:, :2].reshape(2 * t)
    be_flat = be[:, 0]
    xd = moe_dispatch(dest_flat, h, jnp.zeros((nblk * MOE_BLK, d), F32))
    yd = moe_experts(be_flat, xd, w_gate, w_up, w_down, e0, nblk=nblk, nblk_pad=nblk_pad)
    return moe_combine(dest_flat, x, wts, yd)


def _pad_rows(buf):
    return jnp.pad(buf, ((0, 0), (0, SUBLANES - buf.shape[1]), (0, 0)))


def _trunk(x, pos, st_gdn, st_conv, st_sc, st_ret, mem_k, mem_v, wts, *, batch, seqlen):
    depth = mem_k.shape[0]
    new_gdn, new_conv, new_sc, new_ret = [], [], [], []
    for layer in range(depth):
        if layer % 2 == 0:
            i = layer // 2
            p = rms_matmul(x, wts["norm_mix"][layer], wts["w_in_a"][i], tn=768)
            mix, s_new, cq, cs = gdn_core(p, _pad_rows(st_conv[i]), _pad_rows(st_sc[i]),
                                          wts["w_conv_qkv"][i], wts["w_conv_sc"][i], wts["gdn_prm"][i],
                                          wts["gdn_norm"][i], st_gdn[i], batch=batch, seqlen=seqlen)
            new_gdn.append(s_new)
            new_conv.append(cq.reshape(batch, SUBLANES, W_QKV_A)[:, :CONV_A - 1])
            new_sc.append(cs.reshape(batch, SUBLANES, D_B)[:, :CONV_B - 1])
            x = matmul_res(mix, wts["w_out_a"][i], x)
        else:
            j = layer // 2
            p = rms_matmul(x, wts["norm_mix"][layer], wts["w_in_c"][j], tn=768)
            ret, r_new = ret_core(p, pos, wts["ret_norm"][j], st_ret[j], batch=batch, seqlen=seqlen)
            new_ret.append(r_new)
            x = matmul_res(ret, wts["w_out_c"][j], x)
        q = rms_matmul(x, wts["norm_x"][layer], wts["w_xq"][layer], tn=D_MODEL,
                       out_dtype=BF16 if seqlen % 16 == 0 else F32)
        att = xattn_core(q, mem_k, mem_v, layer, batch=batch, seqlen=seqlen)
        x = matmul_res(att, wts["w_xo"][layer], x)
        x = moe_block(x, wts["norm_ffn"][layer], wts["w_route"][layer], wts["b_route"][layer],
                      wts["w_exp_gate"], wts["w_exp_up"], wts["w_exp_down"], layer * N_EXPERTS)
    y = rmsnorm_rows(x, wts["norm_final"])
    return y, jnp.stack(new_gdn), jnp.stack(new_conv), jnp.stack(new_sc), jnp.stack(new_ret)


def kernel(x_prompt, x_sample, state_gdn, state_gdn_conv, state_sconv, state_ret, cache_mem_k, cache_mem_v, mem_prompt, norm_mix, norm_x, norm_ffn, norm_final, norm_mem, w_in_a, w_conv_qkv, a_log, dt_bias, gdn_norm, w_conv_sc, w_out_a, w_in_c, ret_norm, w_out_c, w_xq, w_xk, w_xv, w_xo, w_group, b_group, w_router, b_router, w_exp_gate, w_exp_up, w_exp_down):
    bp, lp, d = x_prompt.shape
    bs, ls, _ = x_sample.shape
    depth = norm_mix.shape[0]
    n_even = w_in_a.shape[0]
    n_mem = mem_prompt.shape[1]

    qkv_w = 2 * H_A * DK_A + H_A * DV_A
    o_z = qkv_w
    o_b = o_z + H_A * DV_A
    o_a = o_b + H_A
    o_sc = o_a + H_A
    w_a = jnp.concatenate([w_in_a[:, :, :o_b], w_in_a[:, :, o_sc:], w_in_a[:, :, o_b:o_sc],
                           jnp.zeros((n_even, d, PA_COLS - PA_BA - 2 * H_A), F32)], axis=-1).astype(BF16)
    prm = jnp.zeros((n_even, SUBLANES, LANES), F32)
    prm = prm.at[:, 0, H_A:2 * H_A].set(a_log).at[:, 1, H_A:2 * H_A].set(dt_bias)
    w_route = jnp.concatenate([w_group, w_router, jnp.zeros((depth, d, LANES - N_GROUPS - N_EXPERTS), F32)], axis=-1)
    b_route = jnp.concatenate([b_group, b_router, jnp.zeros((depth, LANES - N_GROUPS - N_EXPERTS), F32)],
                              axis=-1).reshape(depth, 1, LANES)
    wts = dict(norm_mix=norm_mix, norm_x=norm_x, norm_ffn=norm_ffn, norm_final=norm_final,
               w_in_a=w_a, w_conv_qkv=w_conv_qkv, gdn_prm=prm, gdn_norm=gdn_norm, w_conv_sc=w_conv_sc,
               w_out_a=w_out_a.astype(BF16), w_in_c=w_in_c.astype(BF16), ret_norm=ret_norm,
               w_out_c=w_out_c.astype(BF16), w_xq=w_xq.astype(BF16), w_xo=w_xo.astype(BF16),
               w_route=w_route, b_route=b_route,
               w_exp_gate=w_exp_gate.reshape((depth * N_EXPERTS,) + w_exp_gate.shape[2:]),
               w_exp_up=w_exp_up.reshape((depth * N_EXPERTS,) + w_exp_up.shape[2:]),
               w_exp_down=w_exp_down.reshape((depth * N_EXPERTS,) + w_exp_down.shape[2:]))

    memf = mem_prompt.reshape(bp * n_mem, d)
    w_kv = jnp.concatenate([w_xk, w_xv], axis=-1).astype(BF16)
    mk_p, mv_p = [], []
    for layer in range(depth):
        kv = rms_matmul(memf, norm_mem[layer], w_kv[layer], tn=d)
        mk_p.append(kv[:, :d])
        mv_p.append(kv[:, d:])
    p_cache_mem_k = jnp.stack(mk_p).reshape(depth, bp, n_mem, H_X, HD_X)
    p_cache_mem_v = jnp.stack(mv_p).reshape(depth, bp, n_mem, H_X, HD_X)

    n_odd = w_in_c.shape[0]
    z_gdn = jnp.zeros((n_even, bp, H_A, DK_A, DV_A), F32)
    z_conv = jnp.zeros((n_even, bp, CONV_A - 1, qkv_w), F32)
    z_sc = jnp.zeros((n_even, bp, CONV_B - 1, D_B), F32)
    z_ret = jnp.zeros((n_odd, bp, H_C, DK_C, DV_C), F32)
    pos_p = jnp.arange(lp, dtype=I32)
    pos_s = 16384 + jnp.arange(ls, dtype=I32)

    y_p, p_gdn, p_conv, p_sc, p_ret = _trunk(
        x_prompt.reshape(bp * lp, d), pos_p, z_gdn, z_conv, z_sc, z_ret, p_cache_mem_k, p_cache_mem_v, wts,
        batch=bp, seqlen=lp)
    y_s, s_gdn, s_conv, s_sc, s_ret = _trunk(
        x_sample.reshape(bs * ls, d), pos_s, state_gdn, state_gdn_conv, state_sconv, state_ret,
        cache_mem_k, cache_mem_v, wts, batch=bs, seqlen=ls)
    return (y_p.reshape(bp, lp, d), y_s.reshape(bs, ls, d), p_gdn, p_conv, p_sc, p_ret, p_cache_mem_k,
            p_cache_mem_v, s_gdn, s_conv, s_sc, s_ret)
```

```python
import functools
import math

import jax
import jax.numpy as jnp
import numpy as np
from jax import lax
from jax.experimental import pallas as pl
from jax.experimental.pallas import tpu as pltpu

F32 = jnp.float32
BF16 = jnp.bfloat16
I32 = jnp.int32

EPS = 1e-6
ROPE_BASE = 10000.0

D_MODEL = 1024
H_A, DK_A, DV_A, CONV_A = 4, 128, 128, 4
W_QKV_A = 3 * H_A * DK_A
D_B, CONV_B = D_MODEL // 2, 3
H_C, DK_C, DV_C = 4, 256, 512
H_X, HD_X, N_MEM = 4, 256, 256
N_GROUPS, E_PER_GROUP, N_EXPERTS, D_EXPERT = 4, 8, 32, 512
GDN_CHUNK = 64

LANES = 128
SUBLANES = 8
GDN_STACK = 256
VMEM_LIMIT = 56 * 1024 * 1024

PA_COLS = 3840
PA_BA = 3584
R_E0 = N_GROUPS
MOE_BLK = 256


def _cparams(sem):
    return pltpu.CompilerParams(dimension_semantics=sem, vmem_limit_bytes=VMEM_LIMIT)


def _dot(a, b, trans_a=False, trans_b=False):
    dn = (((0 if trans_a else 1,), (1 if trans_b else 0,)), ((), ()))
    return lax.dot_general(a.astype(BF16), b.astype(BF16), dn, preferred_element_type=F32)


def _silu(x):
    return x * (1.0 / (1.0 + jnp.exp(-x)))


def _sigmoid(x):
    return 1.0 / (1.0 + jnp.exp(-x))


def _rms_matmul_kernel(x_ref, g_ref, w_ref, o_ref, xn_ref):
    @pl.when(pl.program_id(1) == 0)
    def _():
        x = x_ref[...]
        ms = jnp.mean(x * x, axis=-1, keepdims=True)
        xn_ref[...] = (x * lax.rsqrt(ms + EPS) * g_ref[...]).astype(BF16)

    o_ref[...] = jnp.dot(xn_ref[...], w_ref[...], preferred_element_type=F32).astype(o_ref.dtype)


def rms_matmul(x, g, w, *, tn, out_dtype=F32, tm=1024):
    t, d = x.shape
    n = w.shape[1]
    tm = min(tm, t)
    return pl.pallas_call(
        _rms_matmul_kernel,
        grid=(t // tm, n // tn),
        in_specs=[pl.BlockSpec((tm, d), lambda i, j: (i, 0)),
                  pl.BlockSpec((1, d), lambda i, j: (0, 0)),
                  pl.BlockSpec((d, tn), lambda i, j: (0, j))],
        out_specs=pl.BlockSpec((tm, tn), lambda i, j: (i, j)),
        out_shape=jax.ShapeDtypeStruct((t, n), out_dtype),
        scratch_shapes=[pltpu.VMEM((tm, d), BF16)],
        compiler_params=_cparams(("parallel", "arbitrary")),
        name="rms_matmul",
    )(x, g.reshape(1, d), w)


def _matmul_res_kernel(a_ref, w_ref, r_ref, o_ref):
    o_ref[...] = r_ref[...] + jnp.dot(a_ref[...].astype(BF16), w_ref[...], preferred_element_type=F32)


def matmul_res(a, w, res, *, tm=512):
    t, k = a.shape
    n = w.shape[1]
    tm = min(tm, t)
    return pl.pallas_call(
        _matmul_res_kernel,
        grid=(t // tm,),
        in_specs=[pl.BlockSpec((tm, k), lambda i: (i, 0)),
                  pl.BlockSpec((k, n), lambda i: (0, 0)),
                  pl.BlockSpec((tm, n), lambda i: (i, 0))],
        out_specs=pl.BlockSpec((tm, n), lambda i: (i, 0)),
        out_shape=jax.ShapeDtypeStruct((t, n), F32),
        compiler_params=_cparams(("parallel",)),
        name="matmul_res",
    )(a, w, res)


def _rmsnorm_kernel(x_ref, g_ref, o_ref):
    x = x_ref[...]
    ms = jnp.mean(x * x, axis=-1, keepdims=True)
    o_ref[...] = x * lax.rsqrt(ms + EPS) * g_ref[...]


def rmsnorm_rows(x, g, *, tm=512):
    t, d = x.shape
    tm = min(tm, t)
    return pl.pallas_call(
        _rmsnorm_kernel,
        grid=(t // tm,),
        in_specs=[pl.BlockSpec((tm, d), lambda i: (i, 0)), pl.BlockSpec((1, d), lambda i: (0, 0))],
        out_specs=pl.BlockSpec((tm, d), lambda i: (i, 0)),
        out_shape=jax.ShapeDtypeStruct((t, d), F32),
        compiler_params=_cparams(("parallel",)),
        name="final_rmsnorm",
    )(x, g.reshape(1, d))


def _causal_conv(x, hist, w_ref, width, seq8):
    r = x.shape[0]
    taps = [w_ref[j:j + 1, :] for j in range(width)]

    def head(x8, h8):
        n = x8.shape[0]
        t = lax.broadcasted_iota(I32, (n, 1), 0) % SUBLANES
        y = taps[width - 1] * x8
        for s in range(1, width):
            prev = pltpu.roll(h8, (n + s - (width - 1)) % n, 0) if s != width - 1 else h8
            y = y + taps[width - 1 - s] * jnp.where(t >= s, pltpu.roll(x8, s, 0), prev)
        return y

    if seq8:
        return head(x, hist)
    y = taps[width - 1] * x
    for s in range(1, width):
        y = y + taps[width - 1 - s] * pltpu.roll(x, s, 0)
    return jnp.concatenate([head(x[:SUBLANES], hist), y[SUBLANES:]], axis=0)


def _unit_lower_inverse(ms, c, ri, ci):
    base = min(c, 16)
    eye = jnp.where(ri == ci, 1.0, 0.0).astype(F32)
    blk = (ri // base) == (ci // base)
    ds = [jnp.where(blk, m, 0.0) for m in ms]
    ps = [eye - d for d in ds]
    k = 2
    while k < base:
        ds = [_dot(d, d) for d in ds]
        ps = [_dot(p, eye + d) for p, d in zip(ps, ds)]
        k *= 2
    s = base
    while s < c:
        sel = ((ri // (2 * s)) == (ci // (2 * s))) & ((ri // s) != (ci // s))
        ts = [_dot(jnp.where(sel, m, 0.0), p) for m, p in zip(ms, ps)]
        ps = [p - _dot(p, t) for p, t in zip(ps, ts)]
        s *= 2
    return ps


def _gdn_prepare(units, c):
    n = GDN_STACK
    ri = lax.broadcasted_iota(I32, (n, n), 0)
    ci = lax.broadcasted_iota(I32, (n, n), 1)
    same = (ri // c) == (ci // c)
    incl = same & (ri >= ci)
    strict = same & (ri > ci)
    pre = []
    for q, k, v, bfull, gfull in units:
        g2 = jnp.concatenate([gfull, gfull], axis=1)
        g_row = jnp.sum(jnp.where(ri == ci, g2, 0.0), axis=0, keepdims=True)
        gc_col = jnp.sum(jnp.where(incl, g_row, 0.0), axis=1, keepdims=True)
        gc_row = jnp.sum(jnp.where(same & (ri <= ci), g2, 0.0), axis=0, keepdims=True)
        gl_col = jnp.sum(jnp.where(same, g_row, 0.0), axis=1, keepdims=True)
        decay = jnp.where(incl, jnp.exp(jnp.where(incl, gc_col - gc_row, 0.0)), 0.0)
        egc = jnp.exp(gc_col)
        kb = k * bfull
        pre.append(dict(decay=decay, kb=kb, rhs=jnp.concatenate([v * bfull, kb * egc], axis=1),
                        qd=q * egc, kd=k * jnp.exp(gl_col - gc_col), egl=jnp.exp(gl_col)))
    mms = [jnp.where(strict, _dot(e["kb"], u[1], trans_b=True) * e["decay"], 0.0) for e, u in zip(pre, units)]
    qks = [_dot(u[0], u[1], trans_b=True) * e["decay"] for e, u in zip(pre, units)]
    tinvs = _unit_lower_inverse(mms, c, ri, ci)
    uws = [_dot(t, e["rhs"]) for t, e in zip(tinvs, pre)]
    return [dict(u=uw[:, :DV_A], w=uw[:, DV_A:], qk=qk, qd=e["qd"], kd=e["kd"], egl=e["egl"])
            for uw, qk, e in zip(uws, qks, pre)]


def _gdn_recur(e, states, c):
    nprob = GDN_STACK // c
    ws, qs = [], []
    for p in range(nprob):
        sl = slice(p * c, (p + 1) * c)
        ws.append(_dot(e["w"][sl], states[p]))
        qs.append(_dot(e["qd"][sl], states[p]))
    vn = e["u"] - jnp.concatenate(ws, axis=0)
    o = _dot(e["qk"], vn) + jnp.concatenate(qs, axis=0)
    new_states = []
    for p in range(nprob):
        sl = slice(p * c, (p + 1) * c)
        new_states.append(states[p] * e["egl"][p * c:p * c + 1, :] + _dot(e["kd"][sl], vn[sl], trans_a=True))
    return o, new_states


def _gdn_kernel(qkv_ref, z_ref, sch_ref, scb_ref, scc_ref, ba_ref, hq_ref, hs_ref, wq_ref, ws_ref,
                prm_ref, gn_ref, s0_ref, o_ref, sn_ref, cq_ref, cs_ref, s_scr, hq_scr, hs_scr,
                *, seq8, c, nu, nl):
    unit = GDN_CHUNK
    l = jnp.int32(0) if seq8 else pl.program_id(1)

    if seq8:
        hq = hq_ref[...]
        hs = hs_ref[...]
    else:
        @pl.when(l == 0)
        def _():
            hq_scr[...] = hq_ref[0]
            hs_scr[...] = hs_ref[0]
            s_scr[...] = s0_ref[0]
        hq = hq_scr[...]
        hs = hs_scr[...]

    x = qkv_ref[...]
    rows = x.shape[0]
    u_sc = scc_ref[...] * sch_ref[...]
    if seq8:
        cq_ref[...] = pltpu.roll(x, rows - SUBLANES + CONV_A - 1, 0)
        cs_ref[...] = pltpu.roll(u_sc, rows - SUBLANES + CONV_B - 1, 0)
    else:
        hq_scr[...] = pltpu.roll(x[rows - SUBLANES:], CONV_A - 1, 0)
        hs_scr[...] = pltpu.roll(u_sc[rows - SUBLANES:], CONV_B - 1, 0)

    xc = _silu(_causal_conv(x, hq, wq_ref, CONV_A, seq8))
    yb = scb_ref[...] * _causal_conv(u_sc, hs, ws_ref, CONV_B, seq8)
    o_ref[:, H_A * DV_A:] = yb.astype(o_ref.dtype)

    ba = ba_ref[...]
    beta_all = _sigmoid(ba)
    sp = jnp.maximum(ba + prm_ref[1:2, :], 0.0) + jnp.log1p(jnp.exp(-jnp.abs(ba + prm_ref[1:2, :])))
    g_all = -jnp.exp(prm_ref[0:1, :]) * sp
    gn = gn_ref[...]

    def head_cols(a, base):
        return a[:, base * DK_A:(base + 1) * DK_A]

    cat = lambda xs: jnp.concatenate(xs, axis=0)
    units = []
    for ui in range(nu):
        rs = slice(ui * unit, (ui + 1) * unit)
        qs, ks, vs, bs, gs = [], [], [], [], []
        for h in range(H_A):
            qh = head_cols(xc, h)[rs]
            kh = head_cols(xc, H_A + h)[rs]
            qs.append(qh * lax.rsqrt(jnp.sum(qh * qh, axis=-1, keepdims=True) + EPS) * (DK_A ** -0.5))
            ks.append(kh * lax.rsqrt(jnp.sum(kh * kh, axis=-1, keepdims=True) + EPS))
            vs.append(head_cols(xc, 2 * H_A + h)[rs])
            bs.append(jnp.broadcast_to(beta_all[rs, h:h + 1], (unit, LANES)))
            gs.append(jnp.broadcast_to(g_all[rs, H_A + h:H_A + h + 1], (unit, LANES)))
        units.append((cat(qs), cat(ks), cat(vs), cat(bs), cat(gs)))
    prepared = _gdn_prepare(units, c)

    nprob = GDN_STACK // c
    if seq8:
        states = [s0_ref[p % SUBLANES, p // SUBLANES] for p in range(nprob)]
    else:
        states = [s_scr[p] for p in range(nprob)]
    for ui in range(nu):
        rs = slice(ui * unit, (ui + 1) * unit)
        o, states = _gdn_recur(prepared[ui], states, c)
        zst = cat([z_ref[rs, h * DV_A:(h + 1) * DV_A] for h in range(H_A)])
        ms = jnp.mean(o * o, axis=-1, keepdims=True)
        og = o * lax.rsqrt(ms + EPS) * gn * _silu(zst)
        for h in range(H_A):
            o_ref[rs, h * DV_A:(h + 1) * DV_A] = og[h * unit:(h + 1) * unit].astype(o_ref.dtype)
    for p in range(nprob):
        if seq8:
            sn_ref[p % SUBLANES, p // SUBLANES] = states[p]
        else:
            s_scr[p] = states[p]

    if not seq8:
        @pl.when(l == nl - 1)
        def _():
            sn_ref[0] = s_scr[...]
            cq_ref[0] = hq_scr[...]
            cs_ref[0] = hs_scr[...]


def gdn_core(p, hist_q, hist_s, w_conv_qkv, w_conv_sc, prm, gn, s0, *, batch, seqlen):
    t = batch * seqlen
    seq8 = seqlen == SUBLANES
    if seq8:
        c, nu, rows = SUBLANES, 1, GDN_CHUNK
        nb = rows // seqlen
        grid = (batch // nb,)
        rmap = lambda i: i
        sem = ("parallel",)
        hq_spec = pl.BlockSpec((rows, W_QKV_A), lambda i: (i, 0))
        hs_spec = pl.BlockSpec((rows, D_B), lambda i: (i, 0))
        s_spec = pl.BlockSpec((nb, H_A, DK_A, DV_A), lambda i: (i, 0, 0, 0))
        hist_q = hist_q.reshape(t, W_QKV_A)
        hist_s = hist_s.reshape(t, D_B)
        cq_shape, cs_shape = (t, W_QKV_A), (t, D_B)
        nl = 1
    else:
        c, rows = GDN_CHUNK, min(seqlen, 256)
        nu = rows // GDN_CHUNK
        nl = seqlen // rows
        grid = (batch, nl)
        rmap = lambda b, l: b * nl + l
        sem = ("parallel", "arbitrary")
        hq_spec = pl.BlockSpec((1, SUBLANES, W_QKV_A), lambda b, l: (b, 0, 0))
        hs_spec = pl.BlockSpec((1, SUBLANES, D_B), lambda b, l: (b, 0, 0))
        s_spec = pl.BlockSpec((1, H_A, DK_A, DV_A), lambda b, l: (b, 0, 0, 0))
        cq_shape, cs_shape = (batch, SUBLANES, W_QKV_A), (batch, SUBLANES, D_B)

    def col(width, blk):
        return pl.BlockSpec((rows, width), lambda *ix: (rmap(*ix), blk))

    const = lambda shape: pl.BlockSpec(shape, lambda *ix: (0,) * len(shape))
    kern = functools.partial(_gdn_kernel, seq8=seq8, c=c, nu=nu, nl=nl)
    return pl.pallas_call(
        kern,
        grid=grid,
        in_specs=[col(W_QKV_A, 0), col(D_B, 3), col(D_B, 4), col(D_B, 5), col(D_B, 6),
                  col(LANES, PA_BA // LANES), hq_spec, hs_spec,
                  const((CONV_A, W_QKV_A)), const((CONV_B, D_B)), const((SUBLANES, LANES)),
                  const((1, DV_A)), s_spec],
        out_specs=[pl.BlockSpec((rows, D_MODEL), lambda *ix: (rmap(*ix), 0)), s_spec, hq_spec, hs_spec],
        out_shape=[jax.ShapeDtypeStruct((t, D_MODEL), BF16),
                   jax.ShapeDtypeStruct((batch, H_A, DK_A, DV_A), F32),
                   jax.ShapeDtypeStruct(cq_shape, F32),
                   jax.ShapeDtypeStruct(cs_shape, F32)],
        scratch_shapes=[pltpu.VMEM((H_A, DK_A, DV_A), F32),
                        pltpu.VMEM((SUBLANES, W_QKV_A), F32),
                        pltpu.VMEM((SUBLANES, D_B), F32)],
        compiler_params=_cparams(sem),
        name="gdn_core",
    )(p, p, p, p, p, p, hist_q, hist_s, w_conv_qkv, w_conv_sc, prm, gn.reshape(1, DV_A), s0)


def _ret_kernel(q_ref, k_ref, v_ref, gate_ref, cos_ref, sin_ref, dm_ref, qd_ref, kd_ref, cd_ref,
                rn_ref, r0_ref, o_ref, rnew_ref, r_scr, *, nseq, c, nl):
    l = pl.program_id(1)

    @pl.when(l == 0)
    def _():
        r_scr[...] = r0_ref[...]

    cos = cos_ref[...]
    sin = sin_ref[...]
    half = DK_C // 2

    def rot(x):
        x1, x2 = x[:, :half], x[:, half:]
        return jnp.concatenate([x1 * cos - x2 * sin, x1 * sin + x2 * cos], axis=1)

    heads = range(H_C)
    qs = [rot(q_ref[:, h * DK_C:(h + 1) * DK_C]) for h in heads]
    ks = [rot(k_ref[:, h * DK_C:(h + 1) * DK_C]) * (DK_C ** -0.5) for h in heads]
    ss = [_dot(qs[h], ks[h], trans_b=True) * dm_ref[h] for h in heads]
    inters = []
    for h in heads:
        qdh = qs[h] * qd_ref[h]
        parts = [_dot(qdh[sq * c:(sq + 1) * c], r_scr[sq, h]) for sq in range(nseq)]
        inters.append(parts[0] if nseq == 1 else jnp.concatenate(parts, axis=0))
    outs = [_dot(ss[h], v_ref[:, h * DV_C:(h + 1) * DV_C]) + inters[h] for h in heads]
    for h in heads:
        kdh = ks[h] * kd_ref[h]
        cd = cd_ref[h][0:1, 0:1]
        for sq in range(nseq):
            sl = slice(sq * c, (sq + 1) * c)
            r_scr[sq, h] = r_scr[sq, h] * cd + _dot(kdh[sl], v_ref[sl, h * DV_C:(h + 1) * DV_C], trans_a=True)
    for h in heads:
        o = outs[h]
        ms = jnp.mean(o * o, axis=-1, keepdims=True)
        on = o * lax.rsqrt(ms + EPS) * rn_ref[:, h * DV_C:(h + 1) * DV_C]
        o_ref[:, h * DV_C:(h + 1) * DV_C] = (_silu(gate_ref[:, h * DV_C:(h + 1) * DV_C]) * on).astype(o_ref.dtype)

    @pl.when(l == nl - 1)
    def _():
        rnew_ref[...] = r_scr[...]


def ret_core(p, pos, ret_norm, r0, *, batch, seqlen):
    t = batch * seqlen
    if seqlen == SUBLANES:
        nseq, c = 2, seqlen
        nl = 1
        grid = (batch // nseq, 1)
        rmap = lambda b, l: b
    else:
        nseq, c = 1, min(seqlen, 256)
        nl = seqlen // c
        grid = (batch, nl)
        rmap = lambda b, l: b * nl + l
    rows = nseq * c
    half = DK_C // 2
    inv = ROPE_BASE ** (-jnp.arange(half, dtype=F32) / half)
    ang = pos.astype(F32)[:, None] * inv[None, :]
    cos, sin = jnp.cos(ang), jnp.sin(ang)
    if nseq > 1:
        cos, sin = jnp.tile(cos, (nseq, 1)), jnp.tile(sin, (nseq, 1))
    lg = jnp.log(1.0 - 2.0 ** (-5.0 - jnp.arange(H_C, dtype=F32)))[:, None]
    i = jnp.arange(c, dtype=F32)
    incl = i[:, None] >= i[None, :]
    dmat = jnp.exp(jnp.where(incl[None], (i[:, None] - i[None, :])[None] * lg[..., None], -jnp.inf))
    if nseq > 1:
        dmat = jnp.kron(jnp.eye(nseq, dtype=F32)[None], dmat)
    qd = jnp.tile(jnp.exp((i + 1.0)[None] * lg), (1, nseq))[..., None] * jnp.ones((1, 1, DK_C), F32)
    kd = jnp.tile(jnp.exp((c - 1.0 - i)[None] * lg), (1, nseq))[..., None] * jnp.ones((1, 1, DK_C), F32)
    cd = jnp.exp(c * lg)[..., None] * jnp.ones((1, SUBLANES, LANES), F32)

    const = lambda shape: pl.BlockSpec(shape, lambda b, l: (0,) * len(shape))
    trig_spec = (const((rows, half)) if nseq > 1
                 else pl.BlockSpec((rows, half), lambda b, l: (l, 0)))
    kern = functools.partial(_ret_kernel, nseq=nseq, c=c, nl=nl)
    hk = H_C * DK_C
    hv = H_C * DV_C
    return pl.pallas_call(
        kern,
        grid=grid,
        in_specs=[pl.BlockSpec((rows, hk), lambda b, l: (rmap(b, l), 0)),
                  pl.BlockSpec((rows, hk), lambda b, l: (rmap(b, l), 1)),
                  pl.BlockSpec((rows, hv), lambda b, l: (rmap(b, l), 1)),
                  pl.BlockSpec((rows, hv), lambda b, l: (rmap(b, l), 2)),
                  trig_spec, trig_spec,
                  const((H_C, rows, rows)), const((H_C, rows, DK_C)), const((H_C, rows, DK_C)),
                  const((H_C, SUBLANES, LANES)), const((1, hv)),
                  pl.BlockSpec((nseq, H_C, DK_C, DV_C), lambda b, l: (b, 0, 0, 0))],
        out_specs=[pl.BlockSpec((rows, hv), lambda b, l: (rmap(b, l), 0)),
                   pl.BlockSpec((nseq, H_C, DK_C, DV_C), lambda b, l: (b, 0, 0, 0))],
        out_shape=[jax.ShapeDtypeStruct((t, hv), BF16),
                   jax.ShapeDtypeStruct((batch, H_C, DK_C, DV_C), F32)],
        scratch_shapes=[pltpu.VMEM((nseq, H_C, DK_C, DV_C), F32)],
        compiler_params=_cparams(("parallel", "arbitrary")),
        name="ret_core",
    )(p, p, p, p, cos, sin, dmat, qd, kd, cd, ret_norm.reshape(1, hv), r0)


def _xattn_kernel(q_ref, mk_hbm, mv_hbm, o_ref, kbuf, vbuf, sem, *, layer, nb, lq, nsteps):
    i = pl.program_id(0)
    l = pl.program_id(1)
    slot = i % 2

    def copies(step, slot_):
        out = []
        for b in range(nb):
            for h in range(H_X):
                out.append(pltpu.make_async_copy(mk_hbm.at[layer, step * nb + b, :, h, :],
                                                 kbuf.at[slot_, b, h], sem.at[slot_]))
                out.append(pltpu.make_async_copy(mv_hbm.at[layer, step * nb + b, :, h, :],
                                                 vbuf.at[slot_, b, h], sem.at[slot_]))
        return out

    @pl.when(l == 0)
    def _():
        @pl.when(i == 0)
        def _():
            for c in copies(i, slot):
                c.start()

        @pl.when(i + 1 < nsteps)
        def _():
            for c in copies(i + 1, 1 - slot):
                c.start()

        for c in copies(i, slot):
            c.wait()

    probs = [(b, h) for b in range(nb) for h in range(H_X)]
    win = lambda b, h: (slice(b * lq, (b + 1) * lq), slice(h * HD_X, (h + 1) * HD_X))
    ss = [_dot(q_ref[win(b, h)], kbuf[slot, b, h], trans_b=True) * (HD_X ** -0.5) for b, h in probs]
    ps = []
    for s in ss:
        e = jnp.exp(s - jnp.max(s, axis=-1, keepdims=True))
        ps.append(e / jnp.sum(e, axis=-1, keepdims=True))
    os_ = [_dot(p, vbuf[slot, b, h]) for p, (b, h) in zip(ps, probs)]
    for o, (b, h) in zip(os_, probs):
        o_ref[win(b, h)] = o.astype(o_ref.dtype)


def xattn_core(q, mk, mv, layer, *, batch, seqlen):
    t = batch * seqlen
    d = H_X * HD_X
    if seqlen == SUBLANES:
        nb, lq, nl = 2, seqlen, 1
    else:
        nb, lq = 1, min(seqlen, 512)
        nl = seqlen // lq
    rows = nb * lq
    nsteps = batch // nb
    kern = functools.partial(_xattn_kernel, layer=layer, nb=nb, lq=lq, nsteps=nsteps)
    return pl.pallas_call(
        kern,
        grid=(nsteps, nl),
        in_specs=[pl.BlockSpec((rows, d), lambda b, l: (b * nl + l, 0)),
                  pl.BlockSpec(memory_space=pl.ANY), pl.BlockSpec(memory_space=pl.ANY)],
        out_specs=pl.BlockSpec((rows, d), lambda b, l: (b * nl + l, 0)),
        out_shape=jax.ShapeDtypeStruct((t, d), q.dtype),
        scratch_shapes=[pltpu.VMEM((2, nb, H_X, N_MEM, HD_X), F32),
                        pltpu.VMEM((2, nb, H_X, N_MEM, HD_X), F32),
                        pltpu.SemaphoreType.DMA((2,))],
        compiler_params=_cparams(("arbitrary", "arbitrary")),
        name="xattn_core",
    )(q, mk, mv)


def _route_kernel(x_ref, g_ref, wr_ref, br_ref, h_ref, meta_ref, wts_ref, cnt_ref, cnt_scr):
    i = pl.program_id(0)

    @pl.when(i == 0)
    def _():
        cnt_scr[...] = jnp.zeros_like(cnt_scr)

    x = x_ref[...]
    tm = x.shape[0]
    ms = jnp.mean(x * x, axis=-1, keepdims=True)
    h = x * lax.rsqrt(ms + EPS) * g_ref[...]
    h_ref[...] = h
    wr = wr_ref[...]
    h_hi = h.astype(BF16)
    h_lo = (h - h_hi.astype(F32)).astype(BF16)
    w_hi = wr.astype(BF16)
    w_lo = (wr - w_hi.astype(F32)).astype(BF16)
    logits = (jnp.dot(h_hi, w_hi, preferred_element_type=F32) + jnp.dot(h_hi, w_lo, preferred_element_type=F32)
              + jnp.dot(h_lo, w_hi, preferred_element_type=F32)) + br_ref[...]
    lane_i = lax.broadcasted_iota(I32, (tm, LANES), 1)
    lane = lane_i.astype(F32)
    neg = jnp.float32(-3.0e38)
    big = jnp.float32(LANES)
    is_g = lane_i < N_GROUPS
    gl = jnp.where(is_g, logits, neg)
    gmax = jnp.max(gl, axis=1, keepdims=True)
    grp = jnp.min(jnp.where(gl == gmax, lane, big), axis=1, keepdims=True)
    gsum = jnp.sum(jnp.where(is_g, jnp.exp(jnp.where(is_g, logits - gmax, 0.0)), 0.0), axis=1, keepdims=True)
    p_grp = 1.0 / gsum
    in_grp = ((lane_i >= R_E0) & (lane_i < R_E0 + N_EXPERTS)
              & (jnp.floor((lane - R_E0) * (1.0 / E_PER_GROUP)) == grp))
    el = jnp.where(in_grp, logits, neg)
    m1 = jnp.max(el, axis=1, keepdims=True)
    i1 = jnp.min(jnp.where(el == m1, lane, big), axis=1, keepdims=True)
    el2 = jnp.where(lane == i1, neg, el)
    m2 = jnp.max(el2, axis=1, keepdims=True)
    i2 = jnp.min(jnp.where(el2 == m2, lane, big), axis=1, keepdims=True)
    esum = jnp.sum(jnp.where(in_grp, jnp.exp(jnp.where(in_grp, logits - m1, 0.0)), 0.0), axis=1, keepdims=True)
    p1 = 1.0 / esum
    p2 = jnp.exp(m2 - m1) / esum
    tot = p1 + p2
    w1 = p_grp * (p1 / tot)
    w2 = p_grp * (p2 / tot)
    wts_ref[...] = jnp.where(lane_i == 0, w1, jnp.where(lane_i == 1, w2, 0.0))

    oh1 = (lane == i1).astype(F32)
    oh2 = (lane == i2).astype(F32)
    rr = lax.broadcasted_iota(I32, (tm, tm), 0)
    cc = lax.broadcasted_iota(I32, (tm, tm), 1)
    tri = (rr > cc).astype(BF16)
    base = cnt_scr[0:1, :]
    c1 = jnp.sum(oh1, axis=0, keepdims=True)
    c2 = jnp.sum(oh2, axis=0, keepdims=True)
    r1 = jnp.sum(oh1 * (jnp.dot(tri, oh1.astype(BF16), preferred_element_type=F32) + base), axis=1, keepdims=True)
    r2 = jnp.sum(oh2 * (jnp.dot(tri, oh2.astype(BF16), preferred_element_type=F32) + base + c1), axis=1, keepdims=True)
    new_cnt = base + c1 + c2
    cnt_scr[...] = jnp.broadcast_to(new_cnt, cnt_scr.shape)
    cnt_ref[...] = jnp.broadcast_to(new_cnt, cnt_ref.shape)
    meta = jnp.where(lane_i == 0, i1, jnp.where(lane_i == 1, i2, 0.0))
    meta = jnp.where(lane_i == 2, r1, jnp.where(lane_i == 3, r2, meta))
    meta_ref[...] = meta.astype(I32)


def moe_route(x, g, wr, br, *, tm=256):
    t, d = x.shape
    tm = min(tm, t)
    return pl.pallas_call(
        _route_kernel,
        grid=(t // tm,),
        in_specs=[pl.BlockSpec((tm, d), lambda i: (i, 0)),
                  pl.BlockSpec((1, d), lambda i: (0, 0)),
                  pl.BlockSpec((d, LANES), lambda i: (0, 0)),
                  pl.BlockSpec((1, LANES), lambda i: (0, 0))],
        out_specs=[pl.BlockSpec((tm, d), lambda i: (i, 0)),
                   pl.BlockSpec((tm, LANES), lambda i: (i, 0)),
                   pl.BlockSpec((tm, LANES), lambda i: (i, 0)),
                   pl.BlockSpec((SUBLANES, LANES), lambda i: (0, 0))],
        out_shape=[jax.ShapeDtypeStruct((t, d), F32),
                   jax.ShapeDtypeStruct((t, LANES), I32),
                   jax.ShapeDtypeStruct((t, LANES), F32),
                   jax.ShapeDtypeStruct((SUBLANES, LANES), F32)],
        scratch_shapes=[pltpu.VMEM((SUBLANES, LANES), F32)],
        compiler_params=_cparams(("arbitrary",)),
        name="moe_route",
    )(x, g.reshape(1, d), wr, br)


def _plan_kernel(cnt_ref, meta_ref, dest_ref, be_ref, *, nblk_pad):
    cnt = cnt_ref[...]
    lane8 = lax.broadcasted_iota(I32, (SUBLANES, LANES), 1)
    padded = jnp.ceil(cnt / MOE_BLK) * MOE_BLK
    pend = padded
    s = 1
    while s < LANES:
        pend = pend + jnp.where(lane8 >= s, pltpu.roll(pend, s, 1), 0.0)
        s *= 2
    pstart = (pend - padded)[0:1, :]
    meta = meta_ref[...].astype(F32)
    tm = meta.shape[0]
    lane_i = lax.broadcasted_iota(I32, (tm, LANES), 1)
    lane = lane_i.astype(F32)
    col = lambda j: jnp.sum(jnp.where(lane_i == j, meta, 0.0), axis=1, keepdims=True)
    e1, e2, r1, r2 = col(0), col(1), col(2), col(3)
    d1 = jnp.sum(jnp.where(lane == e1, pstart, 0.0), axis=1, keepdims=True) + r1
    d2 = jnp.sum(jnp.where(lane == e2, pstart, 0.0), axis=1, keepdims=True) + r2
    dest_ref[...] = jnp.where(lane_i == 0, d1, jnp.where(lane_i == 1, d2, 0.0)).astype(I32)
    bi = (lax.broadcasted_iota(I32, (nblk_pad, LANES), 0) * MOE_BLK).astype(F32)
    lane_b = lax.broadcasted_iota(I32, (nblk_pad, LANES), 1)
    is_e = (lane_b >= R_E0) & (lane_b < R_E0 + N_EXPERTS)
    nfull = jnp.sum(jnp.where(is_e & (bi >= pend[0:1, :]), 1.0, 0.0), axis=1, keepdims=True)
    be = jnp.minimum(nfull, N_EXPERTS - 1.0)
    nused = jnp.max(pend[0:1, :], axis=1, keepdims=True) / MOE_BLK
    row = lax.broadcasted_iota(I32, (nblk_pad, LANES), 0)
    be_ref[...] = jnp.where(row == nblk_pad - 1, nused, be).astype(I32)


def moe_plan(cnt, meta, *, nblk_pad, tm=512):
    t = meta.shape[0]
    tm = min(tm, t)
    kern = functools.partial(_plan_kernel, nblk_pad=nblk_pad)
    return pl.pallas_call(
        kern,
        grid=(t // tm,),
        in_specs=[pl.BlockSpec((SUBLANES, LANES), lambda i: (0, 0)),
                  pl.BlockSpec((tm, LANES), lambda i: (i, 0))],
        out_specs=[pl.BlockSpec((tm, LANES), lambda i: (i, 0)),
                   pl.BlockSpec((nblk_pad, LANES), lambda i: (0, 0))],
        out_shape=[jax.ShapeDtypeStruct((t, LANES), I32),
                   jax.ShapeDtypeStruct((nblk_pad, LANES), I32)],
        compiler_params=_cparams(("arbitrary",)),
        name="moe_plan",
    )(cnt, meta)


DMA_UNROLL = 8
DISPATCH_SLOTS = 3


def _dispatch_kernel(dest_ref, h_hbm, xd_in, xd_hbm, hbuf, lsem, ssem, *, tm, nsteps):
    del xd_in
    i = pl.program_id(0)
    slot = i % DISPATCH_SLOTS

    def load(step, slot_):
        return pltpu.make_async_copy(h_hbm.at[pl.ds(step * tm, tm)], hbuf.at[slot_], lsem.at[slot_])

    def rows_done(slot_):
        return pltpu.make_async_copy(hbuf.at[slot_], xd_hbm.at[pl.ds(0, tm)], ssem.at[slot_])

    @pl.when(i == 0)
    def _():
        load(0, 0).start()
        if nsteps > 1:
            load(1, 1).start()

    load(i, slot).wait()

    def start(t, carry):
        for k in range(2):
            pltpu.make_async_copy(hbuf.at[slot, pl.ds(t, 1)], xd_hbm.at[pl.ds(dest_ref[2 * t + k], 1)],
                                  ssem.at[slot]).start(priority=k)
        return carry

    lax.fori_loop(0, tm, start, 0, unroll=DMA_UNROLL)

    @pl.when(i >= 1)
    def _():
        prev = (i + DISPATCH_SLOTS - 1) % DISPATCH_SLOTS
        rows_done(prev).wait()
        rows_done(prev).wait()

    @pl.when(i + 2 < nsteps)
    def _():
        load(i + 2, (i + 2) % DISPATCH_SLOTS).start()

    @pl.when(i == nsteps - 1)
    def _():
        rows_done(slot).wait()
        rows_done(slot).wait()


def moe_dispatch(dest_flat, h, xd_zero, *, tm=512):
    t, d = h.shape
    tm = min(tm, t)
    nsteps = t // tm
    kern = functools.partial(_dispatch_kernel, tm=tm, nsteps=nsteps)
    return pl.pallas_call(
        kern,
        grid=(nsteps,),
        in_specs=[pl.BlockSpec((2 * tm,), lambda i: (i,), memory_space=pltpu.SMEM),
                  pl.BlockSpec(memory_space=pl.ANY),
                  pl.BlockSpec(memory_space=pl.ANY)],
        out_specs=pl.BlockSpec(memory_space=pl.ANY),
        out_shape=jax.ShapeDtypeStruct(xd_zero.shape, F32),
        scratch_shapes=[pltpu.VMEM((DISPATCH_SLOTS, tm, d), F32),
                        pltpu.SemaphoreType.DMA((DISPATCH_SLOTS,)),
                        pltpu.SemaphoreType.DMA((DISPATCH_SLOTS,))],
        input_output_aliases={2: 0},
        compiler_params=_cparams(("arbitrary",)),
        name="moe_dispatch",
    )(dest_flat, h, xd_zero)


def _experts_kernel(be_ref, x_ref, wg_ref, wu_ref, wd_ref, o_ref, wg_s, wu_s, wd_s, *, nblk_pad):
    i = pl.program_id(0)
    nused = be_ref[nblk_pad - 1]

    @pl.when(i < nused)
    def _():
        prev = be_ref[jnp.maximum(i - 1, 0)]

        @pl.when((i == 0) | (be_ref[i] != prev))
        def _():
            wg_s[...] = wg_ref[0].astype(BF16)
            wu_s[...] = wu_ref[0].astype(BF16)
            wd_s[...] = wd_ref[0].astype(BF16)

        x = x_ref[...].astype(BF16)
        g = jnp.dot(x, wg_s[...], preferred_element_type=F32)
        u = jnp.dot(x, wu_s[...], preferred_element_type=F32)
        a = (_silu(g) * u).astype(BF16)
        o_ref[...] = jnp.dot(a, wd_s[...], preferred_element_type=F32)

    @pl.when(i >= nused)
    def _():
        o_ref[...] = jnp.zeros_like(o_ref)


def moe_experts(be_flat, xd, w_gate, w_up, w_down, e0, *, nblk, nblk_pad):
    d, f = w_gate.shape[1], w_gate.shape[2]
    kern = functools.partial(_experts_kernel, nblk_pad=nblk_pad)
    grid_spec = pltpu.PrefetchScalarGridSpec(
        num_scalar_prefetch=1,
        grid=(nblk,),
        in_specs=[pl.BlockSpec((MOE_BLK, d), lambda i, be: (i, 0)),
                  pl.BlockSpec((1, d, f), lambda i, be: (e0 + be[i], 0, 0)),
                  pl.BlockSpec((1, d, f), lambda i, be: (e0 + be[i], 0, 0)),
                  pl.BlockSpec((1, f, d), lambda i, be: (e0 + be[i], 0, 0))],
        out_specs=pl.BlockSpec((MOE_BLK, d), lambda i, be: (i, 0)),
        scratch_shapes=[pltpu.VMEM((d, f), BF16), pltpu.VMEM((d, f), BF16), pltpu.VMEM((f, d), BF16)],
    )
    return pl.pallas_call(
        kern,
        grid_spec=grid_spec,
        out_shape=jax.ShapeDtypeStruct(xd.shape, F32),
        compiler_params=_cparams(("arbitrary",)),
        name="moe_experts",
    )(be_flat, xd, w_gate, w_up, w_down)


def _combine_kernel(dest_ref, dest_next_ref, x_ref, wts_ref, yd_hbm, o_ref, rbuf, sem, *, tm, nsteps):
    i = pl.program_id(0)
    slot = i % 2

    def gather(dref, slot_):
        def start(t, carry):
            for k in range(2):
                pltpu.make_async_copy(yd_hbm.at[pl.ds(dref[2 * t + k], 1)], rbuf.at[slot_, k, pl.ds(t, 1)],
                                      sem.at[slot_]).start(priority=k)
            return carry
        lax.fori_loop(0, tm, start, 0, unroll=DMA_UNROLL)

    @pl.when(i == 0)
    def _():
        gather(dest_ref, 0)

    @pl.when(i + 1 < nsteps)
    def _():
        gather(dest_next_ref, 1 - slot)

    for k in range(2):
        pltpu.make_async_copy(yd_hbm.at[pl.ds(0, tm)], rbuf.at[slot, k], sem.at[slot]).wait()
    w = wts_ref[...]
    o_ref[...] = x_ref[...] + rbuf[slot, 0] * w[:, 0:1] + rbuf[slot, 1] * w[:, 1:2]


def moe_combine(dest_flat, x, wts, yd, *, tm=512):
    t, d = x.shape
    tm = min(tm, t)
    nsteps = t // tm
    kern = functools.partial(_combine_kernel, tm=tm, nsteps=nsteps)
    return pl.pallas_call(
        kern,
        grid=(nsteps,),
        in_specs=[pl.BlockSpec((2 * tm,), lambda i: (i,), memory_space=pltpu.SMEM),
                  pl.BlockSpec((2 * tm,), lambda i: (jnp.minimum(i + 1, nsteps - 1),), memory_space=pltpu.SMEM),
                  pl.BlockSpec((tm, d), lambda i: (i, 0)),
                  pl.BlockSpec((tm, LANES), lambda i: (i, 0)),
                  pl.BlockSpec(memory_space=pl.ANY)],
        out_specs=pl.BlockSpec((tm, d), lambda i: (i, 0)),
        out_shape=jax.ShapeDtypeStruct((t, d), F32),
        scratch_shapes=[pltpu.VMEM((2, 2, tm, d), F32), pltpu.SemaphoreType.DMA((2,))],
        compiler_params=_cparams(("arbitrary",)),
        name="moe_combine",
    )(dest_flat, dest_flat, x, wts, yd)


def moe_block(x, g, wr, br, w_gate, w_up, w_down, e0):
    t, d = x.shape
    nblk = (2 * t) // MOE_BLK + N_EXPERTS
    nblk_pad = -(-(nblk + 1) // SUBLANES) * SUBLANES
    h, meta, wts, cnt = moe_route(x, g, wr, br)
    dest, be = moe_plan(cnt, meta, nblk_pad=nblk_pad)
    dest_flat = dest[:, :2].reshape(2 * t)
    be_flat = be[:, 0]
    xd = moe_dispatch(dest_flat, h, jnp.zeros((nblk * MOE_BLK, d), F32))
    yd = moe_experts(be_flat, xd, w_gate, w_up, w_down, e0, nblk=nblk, nblk_pad=nblk_pad)
    return moe_combine(dest_flat, x, wts, yd)


def _pad_rows(buf):
    return jnp.pad(buf, ((0, 0), (0, SUBLANES - buf.shape[1]), (0, 0)))


def _trunk(x, pos, st_gdn, st_conv, st_sc, st_ret, mem_k, mem_v, wts, *, batch, seqlen):
    depth = mem_k.shape[0]
    new_gdn, new_conv, new_sc, new_ret = [], [], [], []
    for layer in range(depth):
        if layer % 2 == 0:
            i = layer // 2
            p = rms_matmul(x, wts["norm_mix"][layer], wts["w_in_a"][i], tn=768)
            mix, s_new, cq, cs = gdn_core(p, _pad_rows(st_conv[i]), _pad_rows(st_sc[i]),
                                          wts["w_conv_qkv"][i], wts["w_conv_sc"][i], wts["gdn_prm"][i],
                                          wts["gdn_norm"][i], st_gdn[i], batch=batch, seqlen=seqlen)
            new_gdn.append(s_new)
            new_conv.append(cq.reshape(batch, SUBLANES, W_QKV_A)[:, :CONV_A - 1])
            new_sc.append(cs.reshape(batch, SUBLANES, D_B)[:, :CONV_B - 1])
            x = matmul_res(mix, wts["w_out_a"][i], x)
        else:
            j = layer // 2
            p = rms_matmul(x, wts["norm_mix"][layer], wts["w_in_c"][j], tn=768)
            ret, r_new = ret_core(p, pos, wts["ret_norm"][j], st_ret[j], batch=batch, seqlen=seqlen)
            new_ret.append(r_new)
            x = matmul_res(ret, wts["w_out_c"][j], x)
        q = rms_matmul(x, wts["norm_x"][layer], wts["w_xq"][layer], tn=D_MODEL,
                       out_dtype=BF16 if seqlen % 16 == 0 else F32)
        att = xattn_core(q, mem_k, mem_v, layer, batch=batch, seqlen=seqlen)
        x = matmul_res(att, wts["w_xo"][layer], x)
        x = moe_block(x, wts["norm_ffn"][layer], wts["w_route"][layer], wts["b_route"][layer],
                      wts["w_exp_gate"], wts["w_exp_up"], wts["w_exp_down"], layer * N_EXPERTS)
    y = rmsnorm_rows(x, wts["norm_final"])
    return y, jnp.stack(new_gdn), jnp.stack(new_conv), jnp.stack(new_sc), jnp.stack(new_ret)


def kernel(x_prompt, x_sample, state_gdn, state_gdn_conv, state_sconv, state_ret, cache_mem_k, cache_mem_v, mem_prompt, norm_mix, norm_x, norm_ffn, norm_final, norm_mem, w_in_a, w_conv_qkv, a_log, dt_bias, gdn_norm, w_conv_sc, w_out_a, w_in_c, ret_norm, w_out_c, w_xq, w_xk, w_xv, w_xo, w_group, b_group, w_router, b_router, w_exp_gate, w_exp_up, w_exp_down):
    bp, lp, d = x_prompt.shape
    bs, ls, _ = x_sample.shape
    depth = norm_mix.shape[0]
    n_even = w_in_a.shape[0]
    n_mem = mem_prompt.shape[1]

    qkv_w = 2 * H_A * DK_A + H_A * DV_A
    o_z = qkv_w
    o_b = o_z + H_A * DV_A
    o_a = o_b + H_A
    o_sc = o_a + H_A
    w_a = jnp.concatenate([w_in_a[:, :, :o_b], w_in_a[:, :, o_sc:], w_in_a[:, :, o_b:o_sc],
                           jnp.zeros((n_even, d, PA_COLS - PA_BA - 2 * H_A), F32)], axis=-1).astype(BF16)
    prm = jnp.zeros((n_even, SUBLANES, LANES), F32)
    prm = prm.at[:, 0, H_A:2 * H_A].set(a_log).at[:, 1, H_A:2 * H_A].set(dt_bias)
    w_route = jnp.concatenate([w_group, w_router, jnp.zeros((depth, d, LANES - N_GROUPS - N_EXPERTS), F32)], axis=-1)
    b_route = jnp.concatenate([b_group, b_router, jnp.zeros((depth, LANES - N_GROUPS - N_EXPERTS), F32)],
                              axis=-1).reshape(depth, 1, LANES)
    wts = dict(norm_mix=norm_mix, norm_x=norm_x, norm_ffn=norm_ffn, norm_final=norm_final,
               w_in_a=w_a, w_conv_qkv=w_conv_qkv, gdn_prm=prm, gdn_norm=gdn_norm, w_conv_sc=w_conv_sc,
               w_out_a=w_out_a.astype(BF16), w_in_c=w_in_c.astype(BF16), ret_norm=ret_norm,
               w_out_c=w_out_c.astype(BF16), w_xq=w_xq.astype(BF16), w_xo=w_xo.astype(BF16),
               w_route=w_route, b_route=b_route,
               w_exp_gate=w_exp_gate.reshape((depth * N_EXPERTS,) + w_exp_gate.shape[2:]),
               w_exp_up=w_exp_up.reshape((depth * N_EXPERTS,) + w_exp_up.shape[2:]),
               w_exp_down=w_exp_down.reshape((depth * N_EXPERTS,) + w_exp_down.shape[2:]))

    memf = mem_prompt.reshape(bp * n_mem, d)
    w_kv = jnp.concatenate([w_xk, w_xv], axis=-1).astype(BF16)
    mk_p, mv_p = [], []
    for layer in range(depth):
        kv = rms_matmul(memf, norm_mem[layer], w_kv[layer], tn=d)
        mk_p.append(kv[:, :d])
        mv_p.append(kv[:, d:])
    p_cache_mem_k = jnp.stack(mk_p).reshape(depth, bp, n_mem, H_X, HD_X)
    p_cache_mem_v = jnp.stack(mv_p).reshape(depth, bp, n_mem, H_X, HD_X)

    n_odd = w_in_c.shape[0]
    z_gdn = jnp.zeros((n_even, bp, H_A, DK_A, DV_A), F32)
    z_conv = jnp.zeros((n_even, bp, CONV_A - 1, qkv_w), F32)
    z_sc = jnp.zeros((n_even, bp, CONV_B - 1, D_B), F32)
    z_ret = jnp.zeros((n_odd, bp, H_C, DK_C, DV_C), F32)
    pos_p = jnp.arange(lp, dtype=I32)
    pos_s = 16384 + jnp.arange(ls, dtype=I32)

    y_p, p_gdn, p_conv, p_sc, p_ret = _trunk(
        x_prompt.reshape(bp * lp, d), pos_p, z_gdn, z_conv, z_sc, z_ret, p_cache_mem_k, p_cache_mem_v, wts,
        batch=bp, seqlen=lp)
    y_s, s_gdn, s_conv, s_sc, s_ret = _trunk(
        x_sample.reshape(bs * ls, d), pos_s, state_gdn, state_gdn_conv, state_sconv, state_ret,
        cache_mem_k, cache_mem_v, wts, batch=bs, seqlen=ls)
    return (y_p.reshape(bp, lp, d), y_s.reshape(bs, ls, d), p_gdn, p_conv, p_sc, p_ret, p_cache_mem_k,
            p_cache_mem_v, s_gdn, s_conv, s_sc, s_ret)
```

```python
import functools
import math

import jax
import jax.numpy as jnp
import numpy as np
from jax import lax
from jax.experimental import pallas as pl
from jax.experimental.pallas import tpu as pltpu

F32 = jnp.float32
BF16 = jnp.bfloat16
I32 = jnp.int32

EPS = 1e-6
ROPE_BASE = 10000.0

D_MODEL = 1024
H_A, DK_A, DV_A, CONV_A = 4, 128, 128, 4
W_QKV_A = 3 * H_A * DK_A
D_B, CONV_B = D_MODEL // 2, 3
H_C, DK_C, DV_C = 4, 256, 512
H_X, HD_X, N_MEM = 4, 256, 256
N_GROUPS, E_PER_GROUP, N_EXPERTS, D_EXPERT = 4, 8, 32, 512
GDN_CHUNK = 64

LANES = 128
SUBLANES = 8
GDN_STACK = 256
VMEM_LIMIT = 56 * 1024 * 1024

PA_COLS = 3840
PA_SC = 2048
PA_BA = 3584
R_E0 = N_GROUPS
MOE_BLK = 256


def _cparams(sem):
    return pltpu.CompilerParams(dimension_semantics=sem, vmem_limit_bytes=VMEM_LIMIT)


def _dot(a, b, trans_a=False, trans_b=False):
    dn = (((0 if trans_a else 1,), (1 if trans_b else 0,)), ((), ()))
    return lax.dot_general(a.astype(BF16), b.astype(BF16), dn, preferred_element_type=F32)


def _silu(x):
    return x * (1.0 / (1.0 + jnp.exp(-x)))


def _sigmoid(x):
    return 1.0 / (1.0 + jnp.exp(-x))


def _rms_matmul_kernel(x_ref, g_ref, w_ref, o_ref, xn_ref):
    @pl.when(pl.program_id(1) == 0)
    def _():
        x = x_ref[...]
        ms = jnp.mean(x * x, axis=-1, keepdims=True)
        xn_ref[...] = (x * lax.rsqrt(ms + EPS) * g_ref[...]).astype(BF16)

    o_ref[...] = jnp.dot(xn_ref[...], w_ref[...], preferred_element_type=F32).astype(o_ref.dtype)


def rms_matmul(x, g, w, *, tn, out_dtype=F32, tm=1024):
    t, d = x.shape
    n = w.shape[1]
    tm = min(tm, t)
    return pl.pallas_call(
        _rms_matmul_kernel,
        grid=(t // tm, n // tn),
        in_specs=[pl.BlockSpec((tm, d), lambda i, j: (i, 0)),
                  pl.BlockSpec((1, d), lambda i, j: (0, 0)),
                  pl.BlockSpec((d, tn), lambda i, j: (0, j))],
        out_specs=pl.BlockSpec((tm, tn), lambda i, j: (i, j)),
        out_shape=jax.ShapeDtypeStruct((t, n), out_dtype),
        scratch_shapes=[pltpu.VMEM((tm, d), BF16)],
        compiler_params=_cparams(("parallel", "arbitrary")),
        name="rms_matmul",
    )(x, g.reshape(1, d), w)


def _matmul_res_kernel(a_ref, w_ref, r_ref, o_ref):
    o_ref[...] = r_ref[...] + jnp.dot(a_ref[...].astype(BF16), w_ref[...], preferred_element_type=F32)


def matmul_res(a, w, res, *, tm=512):
    t, k = a.shape
    n = w.shape[1]
    tm = min(tm, t)
    return pl.pallas_call(
        _matmul_res_kernel,
        grid=(t // tm,),
        in_specs=[pl.BlockSpec((tm, k), lambda i: (i, 0)),
                  pl.BlockSpec((k, n), lambda i: (0, 0)),
                  pl.BlockSpec((tm, n), lambda i: (i, 0))],
        out_specs=pl.BlockSpec((tm, n), lambda i: (i, 0)),
        out_shape=jax.ShapeDtypeStruct((t, n), F32),
        compiler_params=_cparams(("parallel",)),
        name="matmul_res",
    )(a, w, res)


def _causal_conv(x, hist, w_ref, width, seq8):
    r = x.shape[0]
    taps = [w_ref[j:j + 1, :] for j in range(width)]

    def head(x8, h8):
        n = x8.shape[0]
        t = lax.broadcasted_iota(I32, (n, 1), 0) % SUBLANES
        y = taps[width - 1] * x8
        for s in range(1, width):
            prev = pltpu.roll(h8, (n + s - (width - 1)) % n, 0) if s != width - 1 else h8
            y = y + taps[width - 1 - s] * jnp.where(t >= s, pltpu.roll(x8, s, 0), prev)
        return y

    if seq8:
        return head(x, hist)
    y = taps[width - 1] * x
    for s in range(1, width):
        y = y + taps[width - 1 - s] * pltpu.roll(x, s, 0)
    return jnp.concatenate([head(x[:SUBLANES], hist), y[SUBLANES:]], axis=0)


def _unit_lower_inverse(ms, c, ri, ci):
    base = min(c, 16)
    eye = jnp.where(ri == ci, 1.0, 0.0).astype(F32)
    blk = (ri // base) == (ci // base)
    ds = [jnp.where(blk, m, 0.0) for m in ms]
    ps = [eye - d for d in ds]
    k = 2
    while k < base:
        ds = [_dot(d, d) for d in ds]
        ps = [_dot(p, eye + d) for p, d in zip(ps, ds)]
        k *= 2
    s = base
    while s < c:
        sel = ((ri // (2 * s)) == (ci // (2 * s))) & ((ri // s) != (ci // s))
        ts = [_dot(jnp.where(sel, m, 0.0), p) for m, p in zip(ms, ps)]
        ps = [p - _dot(p, t) for p, t in zip(ps, ts)]
        s *= 2
    return ps


def _gdn_prepare(units, c):
    n = GDN_STACK
    ri = lax.broadcasted_iota(I32, (n, n), 0)
    ci = lax.broadcasted_iota(I32, (n, n), 1)
    same = (ri // c) == (ci // c)
    incl = same & (ri >= ci)
    strict = same & (ri > ci)
    pre = []
    for q, k, v, bfull, gfull in units:
        g2 = jnp.concatenate([gfull, gfull], axis=1)
        g_row = jnp.sum(jnp.where(ri == ci, g2, 0.0), axis=0, keepdims=True)
        gc_col = jnp.sum(jnp.where(incl, g_row, 0.0), axis=1, keepdims=True)
        gc_row = jnp.sum(jnp.where(same & (ri <= ci), g2, 0.0), axis=0, keepdims=True)
        gl_col = jnp.sum(jnp.where(same, g_row, 0.0), axis=1, keepdims=True)
        decay = jnp.where(incl, jnp.exp(jnp.where(incl, gc_col - gc_row, 0.0)), 0.0)
        egc = jnp.exp(gc_col)
        kb = k * bfull
        pre.append(dict(decay=decay, kb=kb, rhs=jnp.concatenate([v * bfull, kb * egc], axis=1),
                        qd=q * egc, kd=k * jnp.exp(gl_col - gc_col), egl=jnp.exp(gl_col)))
    mms = [jnp.where(strict, _dot(e["kb"], u[1], trans_b=True) * e["decay"], 0.0) for e, u in zip(pre, units)]
    qks = [_dot(u[0], u[1], trans_b=True) * e["decay"] for e, u in zip(pre, units)]
    tinvs = _unit_lower_inverse(mms, c, ri, ci)
    uws = [_dot(t, e["rhs"]) for t, e in zip(tinvs, pre)]
    return [dict(u=uw[:, :DV_A], w=uw[:, DV_A:], qk=qk, qd=e["qd"], kd=e["kd"], egl=e["egl"])
            for uw, qk, e in zip(uws, qks, pre)]


def _gdn_recur(e, states, c):
    nprob = GDN_STACK // c
    ws, qs = [], []
    for p in range(nprob):
        sl = slice(p * c, (p + 1) * c)
        ws.append(_dot(e["w"][sl], states[p]))
        qs.append(_dot(e["qd"][sl], states[p]))
    vn = e["u"] - jnp.concatenate(ws, axis=0)
    o = _dot(e["qk"], vn) + jnp.concatenate(qs, axis=0)
    new_states = []
    for p in range(nprob):
        sl = slice(p * c, (p + 1) * c)
        new_states.append(states[p] * e["egl"][p * c:p * c + 1, :] + _dot(e["kd"][sl], vn[sl], trans_a=True))
    return o, new_states


def _gdn_compute(x, z, u_sc, scb, ba, hq, hs, wq_ref, ws_ref, prm_ref, gn, states, *, seq8, c, nu):
    unit = GDN_CHUNK
    xc = _silu(_causal_conv(x, hq, wq_ref, CONV_A, seq8))
    yb = scb * _causal_conv(u_sc, hs, ws_ref, CONV_B, seq8)
    beta_all = _sigmoid(ba)
    sp = jnp.maximum(ba + prm_ref[1:2, :], 0.0) + jnp.log1p(jnp.exp(-jnp.abs(ba + prm_ref[1:2, :])))
    g_all = -jnp.exp(prm_ref[0:1, :]) * sp

    def head_cols(a, base):
        return a[:, base * DK_A:(base + 1) * DK_A]

    cat = lambda xs: jnp.concatenate(xs, axis=0)
    units = []
    for ui in range(nu):
        rs = slice(ui * unit, (ui + 1) * unit)
        qs, ks, vs, bs, gs = [], [], [], [], []
        for h in range(H_A):
            qh = head_cols(xc, h)[rs]
            kh = head_cols(xc, H_A + h)[rs]
            qs.append(qh * lax.rsqrt(jnp.sum(qh * qh, axis=-1, keepdims=True) + EPS) * (DK_A ** -0.5))
            ks.append(kh * lax.rsqrt(jnp.sum(kh * kh, axis=-1, keepdims=True) + EPS))
            vs.append(head_cols(xc, 2 * H_A + h)[rs])
            bs.append(jnp.broadcast_to(beta_all[rs, h:h + 1], (unit, LANES)))
            gs.append(jnp.broadcast_to(g_all[rs, H_A + h:H_A + h + 1], (unit, LANES)))
        units.append((cat(qs), cat(ks), cat(vs), cat(bs), cat(gs)))
    prepared = _gdn_prepare(units, c)

    outs = []
    for ui in range(nu):
        rs = slice(ui * unit, (ui + 1) * unit)
        o, states = _gdn_recur(prepared[ui], states, c)
        zst = cat([z[rs, h * DV_A:(h + 1) * DV_A] for h in range(H_A)])
        ms = jnp.mean(o * o, axis=-1, keepdims=True)
        og = o * lax.rsqrt(ms + EPS) * gn * _silu(zst)
        outs.append(jnp.concatenate([og[h * unit:(h + 1) * unit] for h in range(H_A)], axis=1))
    o_all = outs[0] if nu == 1 else cat(outs)
    return jnp.concatenate([o_all, yb], axis=1), states


def _gdn_kernel(qkv_ref, z_ref, sch_ref, scb_ref, scc_ref, ba_ref, hq_ref, hs_ref, wq_ref, ws_ref,
                prm_ref, gn_ref, s0_ref, o_ref, sn_ref, cq_ref, cs_ref):
    x = qkv_ref[...]
    rows = x.shape[0]
    u_sc = scc_ref[...] * sch_ref[...]
    cq_ref[...] = pltpu.roll(x, rows - SUBLANES + CONV_A - 1, 0)
    cs_ref[...] = pltpu.roll(u_sc, rows - SUBLANES + CONV_B - 1, 0)
    nprob = GDN_STACK // SUBLANES
    states = [s0_ref[p % SUBLANES, p // SUBLANES] for p in range(nprob)]
    mix, states = _gdn_compute(x, z_ref[...], u_sc, scb_ref[...], ba_ref[...], hq_ref[...], hs_ref[...],
                               wq_ref, ws_ref, prm_ref, gn_ref[...], states, seq8=True, c=SUBLANES, nu=1)
    o_ref[...] = mix.astype(o_ref.dtype)
    for p in range(nprob):
        sn_ref[p % SUBLANES, p // SUBLANES] = states[p]


def _gdn_block_kernel(x_ref, g_ref, wi_ref, wo_ref, hq_ref, hs_ref, wq_ref, ws_ref, prm_ref, gn_ref, s0_ref,
                      o_ref, sn_ref, cq_ref, cs_ref, s_scr, hq_scr, hs_scr, *, nu, nl):
    l = pl.program_id(1)

    @pl.when(l == 0)
    def _():
        hq_scr[...] = hq_ref[0]
        hs_scr[...] = hs_ref[0]
        s_scr[...] = s0_ref[0]

    xres = x_ref[...]
    rows = xres.shape[0]
    ms = jnp.mean(xres * xres, axis=-1, keepdims=True)
    xn = (xres * lax.rsqrt(ms + EPS) * g_ref[...]).astype(BF16)
    proj = lambda lo, width: jnp.dot(xn, wi_ref[:, lo:lo + width], preferred_element_type=F32)
    x = proj(0, W_QKV_A)
    z = proj(W_QKV_A, H_A * DV_A)
    u_sc = proj(PA_SC + 2 * D_B, D_B) * proj(PA_SC, D_B)
    scb = proj(PA_SC + D_B, D_B)
    ba = proj(PA_BA, LANES)
    hq = hq_scr[...]
    hs = hs_scr[...]
    hq_scr[...] = pltpu.roll(x[rows - SUBLANES:], CONV_A - 1, 0)
    hs_scr[...] = pltpu.roll(u_sc[rows - SUBLANES:], CONV_B - 1, 0)
    states = [s_scr[p] for p in range(H_A)]
    mix, states = _gdn_compute(x, z, u_sc, scb, ba, hq, hs, wq_ref, ws_ref, prm_ref, gn_ref[...], states,
                               seq8=False, c=GDN_CHUNK, nu=nu)
    for p in range(H_A):
        s_scr[p] = states[p]
    o_ref[...] = xres + jnp.dot(mix.astype(BF16), wo_ref[...], preferred_element_type=F32)

    @pl.when(l == nl - 1)
    def _():
        sn_ref[0] = s_scr[...]
        cq_ref[0] = hq_scr[...]
        cs_ref[0] = hs_scr[...]


def gdn_core(p, hist_q, hist_s, w_conv_qkv, w_conv_sc, prm, gn, s0, *, batch, seqlen):
    t = batch * seqlen
    assert seqlen == SUBLANES
    rows = GDN_CHUNK
    nb = rows // seqlen
    hq_spec = pl.BlockSpec((rows, W_QKV_A), lambda i: (i, 0))
    hs_spec = pl.BlockSpec((rows, D_B), lambda i: (i, 0))
    s_spec = pl.BlockSpec((nb, H_A, DK_A, DV_A), lambda i: (i, 0, 0, 0))
    col = lambda width, blk: pl.BlockSpec((rows, width), lambda i: (i, blk))
    const = lambda shape: pl.BlockSpec(shape, lambda i: (0,) * len(shape))
    return pl.pallas_call(
        _gdn_kernel,
        grid=(batch // nb,),
        in_specs=[col(W_QKV_A, 0), col(D_B, 3), col(D_B, 4), col(D_B, 5), col(D_B, 6),
                  col(LANES, PA_BA // LANES), hq_spec, hs_spec,
                  const((CONV_A, W_QKV_A)), const((CONV_B, D_B)), const((SUBLANES, LANES)),
                  const((1, DV_A)), s_spec],
        out_specs=[pl.BlockSpec((rows, D_MODEL), lambda i: (i, 0)), s_spec, hq_spec, hs_spec],
        out_shape=[jax.ShapeDtypeStruct((t, D_MODEL), BF16),
                   jax.ShapeDtypeStruct((batch, H_A, DK_A, DV_A), F32),
                   jax.ShapeDtypeStruct((t, W_QKV_A), F32),
                   jax.ShapeDtypeStruct((t, D_B), F32)],
        compiler_params=_cparams(("parallel",)),
        name="gdn_core",
    )(p, p, p, p, p, p, hist_q.reshape(t, W_QKV_A), hist_s.reshape(t, D_B), w_conv_qkv, w_conv_sc, prm,
      gn.reshape(1, DV_A), s0)


def _resident(shape):
    return pl.BlockSpec(shape, lambda *ix: (0,) * len(shape), pipeline_mode=pl.Buffered(1))


def gdn_block(x, g, w_in, w_out, hist_q, hist_s, w_conv_qkv, w_conv_sc, prm, gn, s0, *, batch, seqlen):
    t, d = x.shape
    rows = min(seqlen, 256)
    nu = rows // GDN_CHUNK
    nl = seqlen // rows
    per_b = lambda shape: pl.BlockSpec((1,) + shape, lambda b, l: (b,) + (0,) * len(shape))
    row_spec = pl.BlockSpec((rows, d), lambda b, l: (b * nl + l, 0))
    kern = functools.partial(_gdn_block_kernel, nu=nu, nl=nl)
    return pl.pallas_call(
        kern,
        grid=(batch, nl),
        in_specs=[row_spec, _resident((1, d)), _resident(w_in.shape), _resident(w_out.shape),
                  per_b((SUBLANES, W_QKV_A)), per_b((SUBLANES, D_B)),
                  _resident((CONV_A, W_QKV_A)), _resident((CONV_B, D_B)), _resident((SUBLANES, LANES)),
                  _resident((1, DV_A)), per_b((H_A, DK_A, DV_A))],
        out_specs=[row_spec, per_b((H_A, DK_A, DV_A)), per_b((SUBLANES, W_QKV_A)), per_b((SUBLANES, D_B))],
        out_shape=[jax.ShapeDtypeStruct((t, d), F32),
                   jax.ShapeDtypeStruct((batch, H_A, DK_A, DV_A), F32),
                   jax.ShapeDtypeStruct((batch, SUBLANES, W_QKV_A), F32),
                   jax.ShapeDtypeStruct((batch, SUBLANES, D_B), F32)],
        scratch_shapes=[pltpu.VMEM((H_A, DK_A, DV_A), F32),
                        pltpu.VMEM((SUBLANES, W_QKV_A), F32),
                        pltpu.VMEM((SUBLANES, D_B), F32)],
        compiler_params=_cparams(("parallel", "arbitrary")),
        name="gdn_block",
    )(x, g.reshape(1, d), w_in, w_out, hist_q, hist_s, w_conv_qkv, w_conv_sc, prm, gn.reshape(1, DV_A), s0)


def _ret_compute(get_q, get_k, get_v, get_gate, cos, sin, dm_ref, qd_ref, kd_ref, cd_ref, rn_ref, r_scr,
                 *, nseq, c):
    half = DK_C // 2

    def rot(x):
        x1, x2 = x[:, :half], x[:, half:]
        return jnp.concatenate([x1 * cos - x2 * sin, x1 * sin + x2 * cos], axis=1)

    heads = range(H_C)
    qs = [rot(get_q(h)) for h in heads]
    ks = [rot(get_k(h)) * (DK_C ** -0.5) for h in heads]
    vs = [get_v(h).astype(BF16) for h in heads]
    ss = [_dot(qs[h], ks[h], trans_b=True) * dm_ref[h] for h in heads]
    inters = []
    for h in heads:
        qdh = qs[h] * qd_ref[h]
        parts = [_dot(qdh[sq * c:(sq + 1) * c], r_scr[sq, h]) for sq in range(nseq)]
        inters.append(parts[0] if nseq == 1 else jnp.concatenate(parts, axis=0))
    outs = [_dot(ss[h], vs[h]) + inters[h] for h in heads]
    for h in heads:
        kdh = ks[h] * kd_ref[h]
        cd = cd_ref[h][0:1, 0:1]
        for sq in range(nseq):
            sl = slice(sq * c, (sq + 1) * c)
            r_scr[sq, h] = r_scr[sq, h] * cd + _dot(kdh[sl], vs[h][sl], trans_a=True)
    gated = []
    for h in heads:
        o = outs[h]
        ms = jnp.mean(o * o, axis=-1, keepdims=True)
        on = o * lax.rsqrt(ms + EPS) * rn_ref[:, h * DV_C:(h + 1) * DV_C]
        gated.append(_silu(get_gate(h)) * on)
    return gated


def _ret_kernel(q_ref, k_ref, v_ref, gate_ref, cos_ref, sin_ref, dm_ref, qd_ref, kd_ref, cd_ref,
                rn_ref, r0_ref, o_ref, rnew_ref, r_scr, *, nseq, c, nl):
    l = pl.program_id(1)

    @pl.when(l == 0)
    def _():
        r_scr[...] = r0_ref[...]

    gated = _ret_compute(lambda h: q_ref[:, h * DK_C:(h + 1) * DK_C], lambda h: k_ref[:, h * DK_C:(h + 1) * DK_C],
                         lambda h: v_ref[:, h * DV_C:(h + 1) * DV_C], lambda h: gate_ref[:, h * DV_C:(h + 1) * DV_C],
                         cos_ref[...], sin_ref[...], dm_ref, qd_ref, kd_ref, cd_ref, rn_ref, r_scr, nseq=nseq, c=c)
    for h in range(H_C):
        o_ref[:, h * DV_C:(h + 1) * DV_C] = gated[h].astype(o_ref.dtype)

    @pl.when(l == nl - 1)
    def _():
        rnew_ref[...] = r_scr[...]


def _ret_block_kernel(x_ref, g_ref, wi_ref, wo_ref, cos_ref, sin_ref, dm_ref, qd_ref, kd_ref, cd_ref,
                      rn_ref, r0_ref, o_ref, rnew_ref, r_scr, *, c, nl):
    l = pl.program_id(1)

    @pl.when(l == 0)
    def _():
        r_scr[...] = r0_ref[...]

    xres = x_ref[...]
    ms = jnp.mean(xres * xres, axis=-1, keepdims=True)
    xn = (xres * lax.rsqrt(ms + EPS) * g_ref[...]).astype(BF16)
    proj = lambda lo, width: jnp.dot(xn, wi_ref[:, lo:lo + width], preferred_element_type=F32)
    hk, hv = H_C * DK_C, H_C * DV_C
    gated = _ret_compute(lambda h: proj(h * DK_C, DK_C), lambda h: proj(hk + h * DK_C, DK_C),
                         lambda h: proj(2 * hk + h * DV_C, DV_C), lambda h: proj(2 * hk + hv + h * DV_C, DV_C),
                         cos_ref[...], sin_ref[...], dm_ref, qd_ref, kd_ref, cd_ref, rn_ref, r_scr, nseq=1, c=c)
    y = xres
    for h in range(H_C):
        y = y + jnp.dot(gated[h].astype(BF16), wo_ref[h * DV_C:(h + 1) * DV_C, :], preferred_element_type=F32)
    o_ref[...] = y

    @pl.when(l == nl - 1)
    def _():
        rnew_ref[...] = r_scr[...]


def ret_core(p, pos, ret_norm, r0, *, batch, seqlen):
    t = batch * seqlen
    assert seqlen == SUBLANES
    nseq, c = 2, seqlen
    rows = nseq * c
    cos, sin, dmat, qd, kd, cd = _ret_tables(pos, nseq, c)
    const = lambda shape: pl.BlockSpec(shape, lambda b, l: (0,) * len(shape))
    kern = functools.partial(_ret_kernel, nseq=nseq, c=c, nl=1)
    hk = H_C * DK_C
    hv = H_C * DV_C
    return pl.pallas_call(
        kern,
        grid=(batch // nseq, 1),
        in_specs=[pl.BlockSpec((rows, hk), lambda b, l: (b, 0)),
                  pl.BlockSpec((rows, hk), lambda b, l: (b, 1)),
                  pl.BlockSpec((rows, hv), lambda b, l: (b, 1)),
                  pl.BlockSpec((rows, hv), lambda b, l: (b, 2)),
                  const((rows, DK_C // 2)), const((rows, DK_C // 2)),
                  const((H_C, rows, rows)), const((H_C, rows, DK_C)), const((H_C, rows, DK_C)),
                  const((H_C, SUBLANES, LANES)), const((1, hv)),
                  pl.BlockSpec((nseq, H_C, DK_C, DV_C), lambda b, l: (b, 0, 0, 0))],
        out_specs=[pl.BlockSpec((rows, hv), lambda b, l: (b, 0)),
                   pl.BlockSpec((nseq, H_C, DK_C, DV_C), lambda b, l: (b, 0, 0, 0))],
        out_shape=[jax.ShapeDtypeStruct((t, hv), BF16),
                   jax.ShapeDtypeStruct((batch, H_C, DK_C, DV_C), F32)],
        scratch_shapes=[pltpu.VMEM((nseq, H_C, DK_C, DV_C), F32)],
        compiler_params=_cparams(("parallel", "arbitrary")),
        name="ret_core",
    )(p, p, p, p, cos, sin, dmat, qd, kd, cd, ret_norm.reshape(1, hv), r0)


def _ret_tables(pos, nseq, c):
    half = DK_C // 2
    inv = ROPE_BASE ** (-jnp.arange(half, dtype=F32) / half)
    ang = pos.astype(F32)[:, None] * inv[None, :]
    cos, sin = jnp.cos(ang), jnp.sin(ang)
    if nseq > 1:
        cos, sin = jnp.tile(cos, (nseq, 1)), jnp.tile(sin, (nseq, 1))
    lg = jnp.log(1.0 - 2.0 ** (-5.0 - jnp.arange(H_C, dtype=F32)))[:, None]
    i = jnp.arange(c, dtype=F32)
    incl = i[:, None] >= i[None, :]
    dmat = jnp.exp(jnp.where(incl[None], (i[:, None] - i[None, :])[None] * lg[..., None], -jnp.inf))
    if nseq > 1:
        dmat = jnp.kron(jnp.eye(nseq, dtype=F32)[None], dmat)
    qd = jnp.tile(jnp.exp((i + 1.0)[None] * lg), (1, nseq))[..., None] * jnp.ones((1, 1, DK_C), F32)
    kd = jnp.tile(jnp.exp((c - 1.0 - i)[None] * lg), (1, nseq))[..., None] * jnp.ones((1, 1, DK_C), F32)
    cd = jnp.exp(c * lg)[..., None] * jnp.ones((1, SUBLANES, LANES), F32)
    return cos, sin, dmat, qd, kd, cd


def ret_block(x, g, w_in, w_out, pos, ret_norm, r0, *, batch, seqlen):
    t, d = x.shape
    c = min(seqlen, 256)
    nl = seqlen // c
    cos, sin, dmat, qd, kd, cd = _ret_tables(pos, 1, c)
    hv = H_C * DV_C
    row_spec = pl.BlockSpec((c, d), lambda b, l: (b * nl + l, 0))
    trig_spec = pl.BlockSpec((c, DK_C // 2), lambda b, l: (l, 0))
    state_spec = pl.BlockSpec((1, H_C, DK_C, DV_C), lambda b, l: (b, 0, 0, 0))
    kern = functools.partial(_ret_block_kernel, c=c, nl=nl)
    return pl.pallas_call(
        kern,
        grid=(batch, nl),
        in_specs=[row_spec, _resident((1, d)), _resident(w_in.shape), _resident(w_out.shape),
                  trig_spec, trig_spec,
                  _resident((H_C, c, c)), _resident((H_C, c, DK_C)), _resident((H_C, c, DK_C)),
                  _resident((H_C, SUBLANES, LANES)), _resident((1, hv)), state_spec],
        out_specs=[row_spec, state_spec],
        out_shape=[jax.ShapeDtypeStruct((t, d), F32),
                   jax.ShapeDtypeStruct((batch, H_C, DK_C, DV_C), F32)],
        scratch_shapes=[pltpu.VMEM((1, H_C, DK_C, DV_C), F32)],
        compiler_params=_cparams(("parallel", "arbitrary")),
        name="ret_block",
    )(x, g.reshape(1, d), w_in, w_out, cos, sin, dmat, qd, kd, cd, ret_norm.reshape(1, hv), r0)


def _xattn_fetch(mk_hbm, mv_hbm, kbuf, vbuf, sem, *, layer, nb, nsteps):
    i = pl.program_id(0)
    l = pl.program_id(1)
    slot = i % 2

    def copies(step, slot_):
        out = []
        for b in range(nb):
            for h in range(H_X):
                out.append(pltpu.make_async_copy(mk_hbm.at[layer, step * nb + b, :, h, :],
                                                 kbuf.at[slot_, b, h], sem.at[slot_]))
                out.append(pltpu.make_async_copy(mv_hbm.at[layer, step * nb + b, :, h, :],
                                                 vbuf.at[slot_, b, h], sem.at[slot_]))
        return out

    @pl.when(l == 0)
    def _():
        @pl.when(i == 0)
        def _():
            for c in copies(i, slot):
                c.start()

        @pl.when(i + 1 < nsteps)
        def _():
            for c in copies(i + 1, 1 - slot):
                c.start()

        for c in copies(i, slot):
            c.wait()

    return slot


def _xattn_heads(q_of, kbuf, vbuf, slot, probs):
    ss = [_dot(q_of(b, h), kbuf[slot, b, h], trans_b=True) * (HD_X ** -0.5) for b, h in probs]
    ps = []
    for s in ss:
        e = jnp.exp(s - jnp.max(s, axis=-1, keepdims=True))
        ps.append(e / jnp.sum(e, axis=-1, keepdims=True))
    return [_dot(p, vbuf[slot, b, h]) for p, (b, h) in zip(ps, probs)]


def _xattn_kernel(q_ref, mk_hbm, mv_hbm, o_ref, kbuf, vbuf, sem, *, layer, nb, lq, nsteps):
    slot = _xattn_fetch(mk_hbm, mv_hbm, kbuf, vbuf, sem, layer=layer, nb=nb, nsteps=nsteps)
    probs = [(b, h) for b in range(nb) for h in range(H_X)]
    win = lambda b, h: (slice(b * lq, (b + 1) * lq), slice(h * HD_X, (h + 1) * HD_X))
    outs = _xattn_heads(lambda b, h: q_ref[win(b, h)], kbuf, vbuf, slot, probs)
    for o, (b, h) in zip(outs, probs):
        o_ref[win(b, h)] = o.astype(o_ref.dtype)


def _xattn_block_kernel(x_ref, g_ref, wq_ref, wo_ref, mk_hbm, mv_hbm, o_ref, kbuf, vbuf, sem,
                        *, layer, nsteps):
    slot = _xattn_fetch(mk_hbm, mv_hbm, kbuf, vbuf, sem, layer=layer, nb=1, nsteps=nsteps)
    xres = x_ref[...]
    ms = jnp.mean(xres * xres, axis=-1, keepdims=True)
    xn = (xres * lax.rsqrt(ms + EPS) * g_ref[...]).astype(BF16)
    probs = [(0, h) for h in range(H_X)]
    qs = [jnp.dot(xn, wq_ref[:, h * HD_X:(h + 1) * HD_X], preferred_element_type=F32) for h in range(H_X)]
    outs = _xattn_heads(lambda b, h: qs[h], kbuf, vbuf, slot, probs)
    y = xres
    for h in range(H_X):
        y = y + jnp.dot(outs[h].astype(BF16), wo_ref[h * HD_X:(h + 1) * HD_X, :], preferred_element_type=F32)
    o_ref[...] = y


def xattn_core(q, mk, mv, layer, *, batch, seqlen):
    t = batch * seqlen
    d = H_X * HD_X
    assert seqlen == SUBLANES
    nb, lq = 2, seqlen
    rows = nb * lq
    nsteps = batch // nb
    kern = functools.partial(_xattn_kernel, layer=layer, nb=nb, lq=lq, nsteps=nsteps)
    return pl.pallas_call(
        kern,
        grid=(nsteps, 1),
        in_specs=[pl.BlockSpec((rows, d), lambda b, l: (b, 0)),
                  pl.BlockSpec(memory_space=pl.ANY), pl.BlockSpec(memory_space=pl.ANY)],
        out_specs=pl.BlockSpec((rows, d), lambda b, l: (b, 0)),
        out_shape=jax.ShapeDtypeStruct((t, d), q.dtype),
        scratch_shapes=[pltpu.VMEM((2, nb, H_X, N_MEM, HD_X), F32),
                        pltpu.VMEM((2, nb, H_X, N_MEM, HD_X), F32),
                        pltpu.SemaphoreType.DMA((2,))],
        compiler_params=_cparams(("arbitrary", "arbitrary")),
        name="xattn_core",
    )(q, mk, mv)


def xattn_block(x, g, w_q, w_o, mk, mv, layer, *, batch, seqlen):
    t, d = x.shape
    lq = min(seqlen, 512)
    nl = seqlen // lq
    row_spec = pl.BlockSpec((lq, d), lambda b, l: (b * nl + l, 0))
    kern = functools.partial(_xattn_block_kernel, layer=layer, nsteps=batch)
    return pl.pallas_call(
        kern,
        grid=(batch, nl),
        in_specs=[row_spec, _resident((1, d)), _resident(w_q.shape), _resident(w_o.shape),
                  pl.BlockSpec(memory_space=pl.ANY), pl.BlockSpec(memory_space=pl.ANY)],
        out_specs=row_spec,
        out_shape=jax.ShapeDtypeStruct((t, d), F32),
        scratch_shapes=[pltpu.VMEM((2, 1, H_X, N_MEM, HD_X), F32),
                        pltpu.VMEM((2, 1, H_X, N_MEM, HD_X), F32),
                        pltpu.SemaphoreType.DMA((2,))],
        compiler_params=_cparams(("arbitrary", "arbitrary")),
        name="xattn_block",
    )(x, g.reshape(1, d), w_q, w_o, mk, mv)


def _route_kernel(x_ref, g_ref, wr_ref, br_ref, h_ref, meta_ref, wts_ref, cnt_ref, cnt_scr):
    i = pl.program_id(0)

    @pl.when(i == 0)
    def _():
        cnt_scr[...] = jnp.zeros_like(cnt_scr)

    x = x_ref[...]
    tm = x.shape[0]
    ms = jnp.mean(x * x, axis=-1, keepdims=True)
    h = x * lax.rsqrt(ms + EPS) * g_ref[...]
    h_ref[...] = h
    wr = wr_ref[...]
    h_hi = h.astype(BF16)
    h_lo = (h - h_hi.astype(F32)).astype(BF16)
    w_hi = wr.astype(BF16)
    w_lo = (wr - w_hi.astype(F32)).astype(BF16)
    logits = (jnp.dot(h_hi, w_hi, preferred_element_type=F32) + jnp.dot(h_hi, w_lo, preferred_element_type=F32)
              + jnp.dot(h_lo, w_hi, preferred_element_type=F32)) + br_ref[...]
    lane_i = lax.broadcasted_iota(I32, (tm, LANES), 1)
    lane = lane_i.astype(F32)
    neg = jnp.float32(-3.0e38)
    big = jnp.float32(LANES)
    is_g = lane_i < N_GROUPS
    gl = jnp.where(is_g, logits, neg)
    gmax = jnp.max(gl, axis=1, keepdims=True)
    grp = jnp.min(jnp.where(gl == gmax, lane, big), axis=1, keepdims=True)
    gsum = jnp.sum(jnp.where(is_g, jnp.exp(jnp.where(is_g, logits - gmax, 0.0)), 0.0), axis=1, keepdims=True)
    p_grp = 1.0 / gsum
    in_grp = ((lane_i >= R_E0) & (lane_i < R_E0 + N_EXPERTS)
              & (jnp.floor((lane - R_E0) * (1.0 / E_PER_GROUP)) == grp))
    el = jnp.where(in_grp, logits, neg)
    m1 = jnp.max(el, axis=1, keepdims=True)
    i1 = jnp.min(jnp.where(el == m1, lane, big), axis=1, keepdims=True)
    el2 = jnp.where(lane == i1, neg, el)
    m2 = jnp.max(el2, axis=1, keepdims=True)
    i2 = jnp.min(jnp.where(el2 == m2, lane, big), axis=1, keepdims=True)
    esum = jnp.sum(jnp.where(in_grp, jnp.exp(jnp.where(in_grp, logits - m1, 0.0)), 0.0), axis=1, keepdims=True)
    p1 = 1.0 / esum
    p2 = jnp.exp(m2 - m1) / esum
    tot = p1 + p2
    w1 = p_grp * (p1 / tot)
    w2 = p_grp * (p2 / tot)
    wts_ref[...] = jnp.where(lane_i == 0, w1, jnp.where(lane_i == 1, w2, 0.0))

    oh1 = (lane == i1).astype(F32)
    oh2 = (lane == i2).astype(F32)
    rr = lax.broadcasted_iota(I32, (tm, tm), 0)
    cc = lax.broadcasted_iota(I32, (tm, tm), 1)
    tri = (rr > cc).astype(BF16)
    base = cnt_scr[0:1, :]
    c1 = jnp.sum(oh1, axis=0, keepdims=True)
    c2 = jnp.sum(oh2, axis=0, keepdims=True)
    r1 = jnp.sum(oh1 * (jnp.dot(tri, oh1.astype(BF16), preferred_element_type=F32) + base), axis=1, keepdims=True)
    r2 = jnp.sum(oh2 * (jnp.dot(tri, oh2.astype(BF16), preferred_element_type=F32) + base + c1), axis=1, keepdims=True)
    new_cnt = base + c1 + c2
    cnt_scr[...] = jnp.broadcast_to(new_cnt, cnt_scr.shape)
    cnt_ref[...] = jnp.broadcast_to(new_cnt, cnt_ref.shape)
    meta = jnp.where(lane_i == 0, i1, jnp.where(lane_i == 1, i2, 0.0))
    meta = jnp.where(lane_i == 2, r1, jnp.where(lane_i == 3, r2, meta))
    meta_ref[...] = meta.astype(I32)


def moe_route(x, g, wr, br, *, tm=256):
    t, d = x.shape
    tm = min(tm, t)
    return pl.pallas_call(
        _route_kernel,
        grid=(t // tm,),
        in_specs=[pl.BlockSpec((tm, d), lambda i: (i, 0)),
                  pl.BlockSpec((1, d), lambda i: (0, 0)),
                  pl.BlockSpec((d, LANES), lambda i: (0, 0)),
                  pl.BlockSpec((1, LANES), lambda i: (0, 0))],
        out_specs=[pl.BlockSpec((tm, d), lambda i: (i, 0)),
                   pl.BlockSpec((tm, LANES), lambda i: (i, 0)),
                   pl.BlockSpec((tm, LANES), lambda i: (i, 0)),
                   pl.BlockSpec((SUBLANES, LANES), lambda i: (0, 0))],
        out_shape=[jax.ShapeDtypeStruct((t, d), F32),
                   jax.ShapeDtypeStruct((t, LANES), I32),
                   jax.ShapeDtypeStruct((t, LANES), F32),
                   jax.ShapeDtypeStruct((SUBLANES, LANES), F32)],
        scratch_shapes=[pltpu.VMEM((SUBLANES, LANES), F32)],
        compiler_params=_cparams(("arbitrary",)),
        name="moe_route",
    )(x, g.reshape(1, d), wr, br)


def _plan_kernel(cnt_ref, meta_ref, dest_ref, be_ref, *, nblk_pad):
    cnt = cnt_ref[...]
    lane8 = lax.broadcasted_iota(I32, (SUBLANES, LANES), 1)
    padded = jnp.ceil(cnt / MOE_BLK) * MOE_BLK
    pend = padded
    s = 1
    while s < LANES:
        pend = pend + jnp.where(lane8 >= s, pltpu.roll(pend, s, 1), 0.0)
        s *= 2
    pstart = (pend - padded)[0:1, :]
    meta = meta_ref[...].astype(F32)
    tm = meta.shape[0]
    lane_i = lax.broadcasted_iota(I32, (tm, LANES), 1)
    lane = lane_i.astype(F32)
    col = lambda j: jnp.sum(jnp.where(lane_i == j, meta, 0.0), axis=1, keepdims=True)
    e1, e2, r1, r2 = col(0), col(1), col(2), col(3)
    d1 = jnp.sum(jnp.where(lane == e1, pstart, 0.0), axis=1, keepdims=True) + r1
    d2 = jnp.sum(jnp.where(lane == e2, pstart, 0.0), axis=1, keepdims=True) + r2
    dest_ref[...] = jnp.where(lane_i == 0, d1, jnp.where(lane_i == 1, d2, 0.0)).astype(I32)
    bi = (lax.broadcasted_iota(I32, (nblk_pad, LANES), 0) * MOE_BLK).astype(F32)
    lane_b = lax.broadcasted_iota(I32, (nblk_pad, LANES), 1)
    is_e = (lane_b >= R_E0) & (lane_b < R_E0 + N_EXPERTS)
    nfull = jnp.sum(jnp.where(is_e & (bi >= pend[0:1, :]), 1.0, 0.0), axis=1, keepdims=True)
    be = jnp.minimum(nfull, N_EXPERTS - 1.0)
    nused = jnp.max(pend[0:1, :], axis=1, keepdims=True) / MOE_BLK
    row = lax.broadcasted_iota(I32, (nblk_pad, LANES), 0)
    be_ref[...] = jnp.where(row == nblk_pad - 1, nused, be).astype(I32)


def moe_plan(cnt, meta, *, nblk_pad, tm=512):
    t = meta.shape[0]
    tm = min(tm, t)
    kern = functools.partial(_plan_kernel, nblk_pad=nblk_pad)
    return pl.pallas_call(
        kern,
        grid=(t // tm,),
        in_specs=[pl.BlockSpec((SUBLANES, LANES), lambda i: (0, 0)),
                  pl.BlockSpec((tm, LANES), lambda i: (i, 0))],
        out_specs=[pl.BlockSpec((tm, LANES), lambda i: (i, 0)),
                   pl.BlockSpec((nblk_pad, LANES), lambda i: (0, 0))],
        out_shape=[jax.ShapeDtypeStruct((t, LANES), I32),
                   jax.ShapeDtypeStruct((nblk_pad, LANES), I32)],
        compiler_params=_cparams(("arbitrary",)),
        name="moe_plan",
    )(cnt, meta)


DMA_UNROLL = 8
DISPATCH_SLOTS = 3


def _dispatch_kernel(dest_ref, h_hbm, xd_in, xd_hbm, hbuf, lsem, ssem, *, tm, nsteps):
    del xd_in
    i = pl.program_id(0)
    slot = i % DISPATCH_SLOTS

    def load(step, slot_):
        return pltpu.make_async_copy(h_hbm.at[pl.ds(step * tm, tm)], hbuf.at[slot_], lsem.at[slot_])

    def rows_done(slot_):
        return pltpu.make_async_copy(hbuf.at[slot_], xd_hbm.at[pl.ds(0, tm)], ssem.at[slot_])

    @pl.when(i == 0)
    def _():
        load(0, 0).start()
        if nsteps > 1:
            load(1, 1).start()

    load(i, slot).wait()

    def start(t, carry):
        for k in range(2):
            pltpu.make_async_copy(hbuf.at[slot, pl.ds(t, 1)], xd_hbm.at[pl.ds(dest_ref[2 * t + k], 1)],
                                  ssem.at[slot]).start(priority=k)
        return carry

    lax.fori_loop(0, tm, start, 0, unroll=DMA_UNROLL)

    @pl.when(i >= 1)
    def _():
        prev = (i + DISPATCH_SLOTS - 1) % DISPATCH_SLOTS
        rows_done(prev).wait()
        rows_done(prev).wait()

    @pl.when(i + 2 < nsteps)
    def _():
        load(i + 2, (i + 2) % DISPATCH_SLOTS).start()

    @pl.when(i == nsteps - 1)
    def _():
        rows_done(slot).wait()
        rows_done(slot).wait()


def moe_dispatch(dest_flat, h, xd_zero, *, tm=512):
    t, d = h.shape
    tm = min(tm, t)
    nsteps = t // tm
    kern = functools.partial(_dispatch_kernel, tm=tm, nsteps=nsteps)
    return pl.pallas_call(
        kern,
        grid=(nsteps,),
        in_specs=[pl.BlockSpec((2 * tm,), lambda i: (i,), memory_space=pltpu.SMEM),
                  pl.BlockSpec(memory_space=pl.ANY),
                  pl.BlockSpec(memory_space=pl.ANY)],
        out_specs=pl.BlockSpec(memory_space=pl.ANY),
        out_shape=jax.ShapeDtypeStruct(xd_zero.shape, F32),
        scratch_shapes=[pltpu.VMEM((DISPATCH_SLOTS, tm, d), F32),
                        pltpu.SemaphoreType.DMA((DISPATCH_SLOTS,)),
                        pltpu.SemaphoreType.DMA((DISPATCH_SLOTS,))],
        input_output_aliases={2: 0},
        compiler_params=_cparams(("arbitrary",)),
        name="moe_dispatch",
    )(dest_flat, h, xd_zero)


def _experts_kernel(be_ref, x_ref, wg_ref, wu_ref, wd_ref, o_ref, wg_s, wu_s, wd_s, *, nblk_pad):
    i = pl.program_id(0)
    nused = be_ref[nblk_pad - 1]

    @pl.when(i < nused)
    def _():
        prev = be_ref[jnp.maximum(i - 1, 0)]

        @pl.when((i == 0) | (be_ref[i] != prev))
        def _():
            wg_s[...] = wg_ref[0].astype(BF16)
            wu_s[...] = wu_ref[0].astype(BF16)
            wd_s[...] = wd_ref[0].astype(BF16)

        half = MOE_BLK // 2
        xs = [x_ref[r * half:(r + 1) * half, :].astype(BF16) for r in range(2)]
        gs = [jnp.dot(x, wg_s[...], preferred_element_type=F32) for x in xs]
        us = [jnp.dot(x, wu_s[...], preferred_element_type=F32) for x in xs]
        acts = [(_silu(g) * u).astype(BF16) for g, u in zip(gs, us)]
        for r in range(2):
            o_ref[r * half:(r + 1) * half, :] = jnp.dot(acts[r], wd_s[...], preferred_element_type=F32)

    @pl.when(i >= nused)
    def _():
        o_ref[...] = jnp.zeros_like(o_ref)


def moe_experts(be_flat, xd, w_gate, w_up, w_down, e0, *, nblk, nblk_pad):
    d, f = w_gate.shape[1], w_gate.shape[2]
    kern = functools.partial(_experts_kernel, nblk_pad=nblk_pad)
    grid_spec = pltpu.PrefetchScalarGridSpec(
        num_scalar_prefetch=1,
        grid=(nblk,),
        in_specs=[pl.BlockSpec((MOE_BLK, d), lambda i, be: (i, 0)),
                  pl.BlockSpec((1, d, f), lambda i, be: (e0 + be[i], 0, 0)),
                  pl.BlockSpec((1, d, f), lambda i, be: (e0 + be[i], 0, 0)),
                  pl.BlockSpec((1, f, d), lambda i, be: (e0 + be[i], 0, 0))],
        out_specs=pl.BlockSpec((MOE_BLK, d), lambda i, be: (i, 0)),
        scratch_shapes=[pltpu.VMEM((d, f), BF16), pltpu.VMEM((d, f), BF16), pltpu.VMEM((f, d), BF16)],
    )
    return pl.pallas_call(
        kern,
        grid_spec=grid_spec,
        out_shape=jax.ShapeDtypeStruct(xd.shape, F32),
        compiler_params=_cparams(("arbitrary",)),
        name="moe_experts",
    )(be_flat, xd, w_gate, w_up, w_down)


def _combine_kernel(dest_ref, dest_next_ref, x_ref, wts_ref, gf_ref, yd_hbm, o_ref, rbuf, sem,
                    *, tm, nsteps, final_norm):
    i = pl.program_id(0)
    slot = i % 2

    def gather(dref, slot_):
        def start(t, carry):
            for k in range(2):
                pltpu.make_async_copy(yd_hbm.at[pl.ds(dref[2 * t + k], 1)], rbuf.at[slot_, k, pl.ds(t, 1)],
                                      sem.at[slot_]).start(priority=k)
            return carry
        lax.fori_loop(0, tm, start, 0, unroll=DMA_UNROLL)

    @pl.when(i == 0)
    def _():
        gather(dest_ref, 0)

    @pl.when(i + 1 < nsteps)
    def _():
        gather(dest_next_ref, 1 - slot)

    for k in range(2):
        pltpu.make_async_copy(yd_hbm.at[pl.ds(0, tm)], rbuf.at[slot, k], sem.at[slot]).wait()
    w = wts_ref[...]
    y = x_ref[...] + rbuf[slot, 0] * w[:, 0:1] + rbuf[slot, 1] * w[:, 1:2]
    if final_norm:
        y = y * lax.rsqrt(jnp.mean(y * y, axis=-1, keepdims=True) + EPS) * gf_ref[...]
    o_ref[...] = y


def moe_combine(dest_flat, x, wts, yd, g_final, *, tm=512):
    t, d = x.shape
    tm = min(tm, t)
    nsteps = t // tm
    final_norm = g_final is not None
    gf = g_final.reshape(1, d) if final_norm else jnp.ones((1, d), F32)
    kern = functools.partial(_combine_kernel, tm=tm, nsteps=nsteps, final_norm=final_norm)
    return pl.pallas_call(
        kern,
        grid=(nsteps,),
        in_specs=[pl.BlockSpec((2 * tm,), lambda i: (i,), memory_space=pltpu.SMEM),
                  pl.BlockSpec((2 * tm,), lambda i: (jnp.minimum(i + 1, nsteps - 1),), memory_space=pltpu.SMEM),
                  pl.BlockSpec((tm, d), lambda i: (i, 0)),
                  pl.BlockSpec((tm, LANES), lambda i: (i, 0)),
                  pl.BlockSpec((1, d), lambda i: (0, 0)),
                  pl.BlockSpec(memory_space=pl.ANY)],
        out_specs=pl.BlockSpec((tm, d), lambda i: (i, 0)),
        out_shape=jax.ShapeDtypeStruct((t, d), F32),
        scratch_shapes=[pltpu.VMEM((2, 2, tm, d), F32), pltpu.SemaphoreType.DMA((2,))],
        compiler_params=_cparams(("arbitrary",)),
        name="moe_combine",
    )(dest_flat, dest_flat, x, wts, gf, yd)


def moe_block(x, g, wr, br, w_gate, w_up, w_down, e0, g_final=None):
    t, d = x.shape
    nblk = (2 * t) // MOE_BLK + N_EXPERTS
    nblk_pad = -(-(nblk + 1) // SUBLANES) * SUBLANES
    h, meta, wts, cnt = moe_route(x, g, wr, br)
    dest, be = moe_plan(cnt, meta, nblk_pad=nblk_pad)
    dest_flat = dest[:, :2].reshape(2 * t)
    be_flat = be[:, 0]
    xd = moe_dispatch(dest_flat, h, jnp.zeros((nblk * MOE_BLK, d), F32))
    yd = moe_experts(be_flat, xd, w_gate, w_up, w_down, e0, nblk=nblk, nblk_pad=nblk_pad)
    return moe_combine(dest_flat, x, wts, yd, g_final)


def _pad_rows(buf):
    return jnp.pad(buf, ((0, 0), (0, SUBLANES - buf.shape[1]), (0, 0)))


def _trunk(x, pos, st_gdn, st_conv, st_sc, st_ret, mem_k, mem_v, wts, *, batch, seqlen):
    depth = mem_k.shape[0]
    fused = seqlen != SUBLANES
    new_gdn, new_conv, new_sc, new_ret = [], [], [], []
    for layer in range(depth):
        if layer % 2 == 0:
            i = layer // 2
            hist = (_pad_rows(st_conv[i]), _pad_rows(st_sc[i]), wts["w_conv_qkv"][i], wts["w_conv_sc"][i],
                    wts["gdn_prm"][i], wts["gdn_norm"][i], st_gdn[i])
            if fused:
                x, s_new, cq, cs = gdn_block(x, wts["norm_mix"][layer], wts["w_in_a"][i], wts["w_out_a"][i],
                                             *hist, batch=batch, seqlen=seqlen)
            else:
                p = rms_matmul(x, wts["norm_mix"][layer], wts["w_in_a"][i], tn=768)
                mix, s_new, cq, cs = gdn_core(p, *hist, batch=batch, seqlen=seqlen)
                x = matmul_res(mix, wts["w_out_a"][i], x)
            new_gdn.append(s_new)
            new_conv.append(cq.reshape(batch, SUBLANES, W_QKV_A)[:, :CONV_A - 1])
            new_sc.append(cs.reshape(batch, SUBLANES, D_B)[:, :CONV_B - 1])
        else:
            j = layer // 2
            if fused:
                x, r_new = ret_block(x, wts["norm_mix"][layer], wts["w_in_c"][j], wts["w_out_c"][j], pos,
                                     wts["ret_norm"][j], st_ret[j], batch=batch, seqlen=seqlen)
            else:
                p = rms_matmul(x, wts["norm_mix"][layer], wts["w_in_c"][j], tn=768)
                ret, r_new = ret_core(p, pos, wts["ret_norm"][j], st_ret[j], batch=batch, seqlen=seqlen)
                x = matmul_res(ret, wts["w_out_c"][j], x)
            new_ret.append(r_new)
        if fused:
            x = xattn_block(x, wts["norm_x"][layer], wts["w_xq"][layer], wts["w_xo"][layer], mem_k, mem_v, layer,
                            batch=batch, seqlen=seqlen)
        else:
            q = rms_matmul(x, wts["norm_x"][layer], wts["w_xq"][layer], tn=D_MODEL)
            att = xattn_core(q, mem_k, mem_v, layer, batch=batch, seqlen=seqlen)
            x = matmul_res(att, wts["w_xo"][layer], x)
        x = moe_block(x, wts["norm_ffn"][layer], wts["w_route"][layer], wts["b_route"][layer],
                      wts["w_exp_gate"], wts["w_exp_up"], wts["w_exp_down"], layer * N_EXPERTS,
                      g_final=wts["norm_final"] if layer == depth - 1 else None)
    return x, jnp.stack(new_gdn), jnp.stack(new_conv), jnp.stack(new_sc), jnp.stack(new_ret)


def kernel(x_prompt, x_sample, state_gdn, state_gdn_conv, state_sconv, state_ret, cache_mem_k, cache_mem_v, mem_prompt, norm_mix, norm_x, norm_ffn, norm_final, norm_mem, w_in_a, w_conv_qkv, a_log, dt_bias, gdn_norm, w_conv_sc, w_out_a, w_in_c, ret_norm, w_out_c, w_xq, w_xk, w_xv, w_xo, w_group, b_group, w_router, b_router, w_exp_gate, w_exp_up, w_exp_down):
    bp, lp, d = x_prompt.shape
    bs, ls, _ = x_sample.shape
    depth = norm_mix.shape[0]
    n_even = w_in_a.shape[0]
    n_mem = mem_prompt.shape[1]

    qkv_w = 2 * H_A * DK_A + H_A * DV_A
    o_z = qkv_w
    o_b = o_z + H_A * DV_A
    o_a = o_b + H_A
    o_sc = o_a + H_A
    w_a = jnp.concatenate([w_in_a[:, :, :o_b], w_in_a[:, :, o_sc:], w_in_a[:, :, o_b:o_sc],
                           jnp.zeros((n_even, d, PA_COLS - PA_BA - 2 * H_A), F32)], axis=-1).astype(BF16)
    prm = jnp.zeros((n_even, SUBLANES, LANES), F32)
    prm = prm.at[:, 0, H_A:2 * H_A].set(a_log).at[:, 1, H_A:2 * H_A].set(dt_bias)
    w_route = jnp.concatenate([w_group, w_router, jnp.zeros((depth, d, LANES - N_GROUPS - N_EXPERTS), F32)], axis=-1)
    b_route = jnp.concatenate([b_group, b_router, jnp.zeros((depth, LANES - N_GROUPS - N_EXPERTS), F32)],
                              axis=-1).reshape(depth, 1, LANES)
    wts = dict(norm_mix=norm_mix, norm_x=norm_x, norm_ffn=norm_ffn, norm_final=norm_final,
               w_in_a=w_a, w_conv_qkv=w_conv_qkv, gdn_prm=prm, gdn_norm=gdn_norm, w_conv_sc=w_conv_sc,
               w_out_a=w_out_a.astype(BF16), w_in_c=w_in_c.astype(BF16), ret_norm=ret_norm,
               w_out_c=w_out_c.astype(BF16), w_xq=w_xq.astype(BF16), w_xo=w_xo.astype(BF16),
               w_route=w_route, b_route=b_route,
               w_exp_gate=w_exp_gate.reshape((depth * N_EXPERTS,) + w_exp_gate.shape[2:]),
               w_exp_up=w_exp_up.reshape((depth * N_EXPERTS,) + w_exp_up.shape[2:]),
               w_exp_down=w_exp_down.reshape((depth * N_EXPERTS,) + w_exp_down.shape[2:]))

    memf = mem_prompt.reshape(bp * n_mem, d)
    w_kv = jnp.concatenate([w_xk, w_xv], axis=-1).astype(BF16)
    mk_p, mv_p = [], []
    for layer in range(depth):
        kv = rms_matmul(memf, norm_mem[layer], w_kv[layer], tn=d)
        mk_p.append(kv[:, :d])
        mv_p.append(kv[:, d:])
    p_cache_mem_k = jnp.stack(mk_p).reshape(depth, bp, n_mem, H_X, HD_X)
    p_cache_mem_v = jnp.stack(mv_p).reshape(depth, bp, n_mem, H_X, HD_X)

    n_odd = w_in_c.shape[0]
    z_gdn = jnp.zeros((n_even, bp, H_A, DK_A, DV_A), F32)
    z_conv = jnp.zeros((n_even, bp, CONV_A - 1, qkv_w), F32)
    z_sc = jnp.zeros((n_even, bp, CONV_B - 1, D_B), F32)
    z_ret = jnp.zeros((n_odd, bp, H_C, DK_C, DV_C), F32)
    pos_p = jnp.arange(lp, dtype=I32)
    pos_s = 16384 + jnp.arange(ls, dtype=I32)

    y_p, p_gdn, p_conv, p_sc, p_ret = _trunk(
        x_prompt.reshape(bp * lp, d), pos_p, z_gdn, z_conv, z_sc, z_ret, p_cache_mem_k, p_cache_mem_v, wts,
        batch=bp, seqlen=lp)
    y_s, s_gdn, s_conv, s_sc, s_ret = _trunk(
        x_sample.reshape(bs * ls, d), pos_s, state_gdn, state_gdn_conv, state_sconv, state_ret,
        cache_mem_k, cache_mem_v, wts, batch=bs, seqlen=ls)
    return (y_p.reshape(bp, lp, d), y_s.reshape(bs, ls, d), p_gdn, p_conv, p_sc, p_ret, p_cache_mem_k,
            p_cache_mem_v, s_gdn, s_conv, s_sc, s_ret)
```

```python
import functools
import math

import jax
import jax.numpy as jnp
import numpy as np
from jax import lax
from jax.experimental import pallas as pl
from jax.experimental.pallas import tpu as pltpu

F32 = jnp.float32
BF16 = jnp.bfloat16
I32 = jnp.int32

EPS = 1e-6
ROPE_BASE = 10000.0

D_MODEL = 1024
H_A, DK_A, DV_A, CONV_A = 4, 128, 128, 4
W_QKV_A = 3 * H_A * DK_A
D_B, CONV_B = D_MODEL // 2, 3
H_C, DK_C, DV_C = 4, 256, 512
H_X, HD_X, N_MEM = 4, 256, 256
N_GROUPS, E_PER_GROUP, N_EXPERTS, D_EXPERT = 4, 8, 32, 512
GDN_CHUNK = 64

LANES = 128
SUBLANES = 8
GDN_STACK = 256
VMEM_LIMIT = 56 * 1024 * 1024

PA_COLS = 3840
PA_SC = 2048
PA_BA = 3584
R_E0 = N_GROUPS
MOE_BLK = 256


def _cparams(sem):
    return pltpu.CompilerParams(dimension_semantics=sem, vmem_limit_bytes=VMEM_LIMIT)


def _dot(a, b, trans_a=False, trans_b=False):
    dn = (((0 if trans_a else 1,), (1 if trans_b else 0,)), ((), ()))
    return lax.dot_general(a.astype(BF16), b.astype(BF16), dn, preferred_element_type=F32)


def _silu(x):
    return x * (1.0 / (1.0 + jnp.exp(-x)))


def _sigmoid(x):
    return 1.0 / (1.0 + jnp.exp(-x))


def _rms_matmul_kernel(x_ref, g_ref, w_ref, o_ref, xn_ref):
    @pl.when(pl.program_id(1) == 0)
    def _():
        x = x_ref[...]
        ms = jnp.mean(x * x, axis=-1, keepdims=True)
        xn_ref[...] = (x * lax.rsqrt(ms + EPS) * g_ref[...]).astype(BF16)

    o_ref[...] = jnp.dot(xn_ref[...], w_ref[...], preferred_element_type=F32).astype(o_ref.dtype)


def rms_matmul(x, g, w, *, tn, rows=None, row0=0, out_dtype=F32, tm=1024):
    t, d = (rows or x.shape[0]), x.shape[1]
    n = w.shape[1]
    tm = min(tm, t)
    blk0 = row0 // tm
    return pl.pallas_call(
        _rms_matmul_kernel,
        grid=(t // tm, n // tn),
        in_specs=[pl.BlockSpec((tm, d), lambda i, j: (i + blk0, 0)),
                  pl.BlockSpec((1, d), lambda i, j: (0, 0)),
                  pl.BlockSpec((d, tn), lambda i, j: (0, j))],
        out_specs=pl.BlockSpec((tm, tn), lambda i, j: (i, j)),
        out_shape=jax.ShapeDtypeStruct((t, n), out_dtype),
        scratch_shapes=[pltpu.VMEM((tm, d), BF16)],
        compiler_params=_cparams(("parallel", "arbitrary")),
        name="rms_matmul",
    )(x, g.reshape(1, d), w)


def _matmul_res_kernel(a_ref, w_ref, r_ref, *rest):
    o_ref = rest[-1]
    o_ref[...] = r_ref[...] + jnp.dot(a_ref[...].astype(BF16), w_ref[...], preferred_element_type=F32)


def matmul_res(a, w, res, *, res_row0=0, into=None, out_row0=0, tm=512):
    t, k = a.shape
    n = w.shape[1]
    tm = min(tm, t)
    rblk, oblk = res_row0 // tm, out_row0 // tm
    in_specs = [pl.BlockSpec((tm, k), lambda i: (i, 0)),
                pl.BlockSpec((k, n), lambda i: (0, 0)),
                pl.BlockSpec((tm, n), lambda i: (i + rblk, 0))]
    args = (a, w, res)
    if into is not None:
        in_specs.append(pl.BlockSpec(memory_space=pl.ANY))
        args += (into,)
    return pl.pallas_call(
        _matmul_res_kernel,
        grid=(t // tm,),
        in_specs=in_specs,
        out_specs=pl.BlockSpec((tm, n), lambda i: (i + oblk, 0)),
        out_shape=jax.ShapeDtypeStruct((t, n) if into is None else into.shape, F32),
        input_output_aliases={} if into is None else {3: 0},
        compiler_params=_cparams(("parallel",)),
        name="matmul_res",
    )(*args)


def _causal_conv(x, hist, w_ref, width, seq8):
    r = x.shape[0]
    taps = [w_ref[j:j + 1, :] for j in range(width)]

    def head(x8, h8):
        n = x8.shape[0]
        t = lax.broadcasted_iota(I32, (n, 1), 0) % SUBLANES
        y = taps[width - 1] * x8
        for s in range(1, width):
            prev = pltpu.roll(h8, (n + s - (width - 1)) % n, 0) if s != width - 1 else h8
            y = y + taps[width - 1 - s] * jnp.where(t >= s, pltpu.roll(x8, s, 0), prev)
        return y

    if seq8:
        return head(x, hist)
    y = taps[width - 1] * x
    for s in range(1, width):
        y = y + taps[width - 1 - s] * pltpu.roll(x, s, 0)
    return jnp.concatenate([head(x[:SUBLANES], hist), y[SUBLANES:]], axis=0)


def _unit_lower_inverse(ms, c, ri, ci):
    base = min(c, 16)
    eye = jnp.where(ri == ci, 1.0, 0.0).astype(F32)
    blk = (ri // base) == (ci // base)
    ds = [jnp.where(blk, m, 0.0) for m in ms]
    ps = [eye - d for d in ds]
    k = 2
    while k < base:
        ds = [_dot(d, d) for d in ds]
        ps = [_dot(p, eye + d) for p, d in zip(ps, ds)]
        k *= 2
    s = base
    while s < c:
        sel = ((ri // (2 * s)) == (ci // (2 * s))) & ((ri // s) != (ci // s))
        ts = [_dot(jnp.where(sel, m, 0.0), p) for m, p in zip(ms, ps)]
        ps = [p - _dot(p, t) for p, t in zip(ps, ts)]
        s *= 2
    return ps


def _gdn_prepare(units, c):
    n = GDN_STACK
    ri = lax.broadcasted_iota(I32, (n, n), 0)
    ci = lax.broadcasted_iota(I32, (n, n), 1)
    same = (ri // c) == (ci // c)
    incl = same & (ri >= ci)
    strict = same & (ri > ci)
    pre = []
    for q, k, v, bfull, gfull in units:
        g2 = jnp.concatenate([gfull, gfull], axis=1)
        g_row = jnp.sum(jnp.where(ri == ci, g2, 0.0), axis=0, keepdims=True)
        gc_col = jnp.sum(jnp.where(incl, g_row, 0.0), axis=1, keepdims=True)
        gc_row = jnp.sum(jnp.where(same & (ri <= ci), g2, 0.0), axis=0, keepdims=True)
        gl_col = jnp.sum(jnp.where(same, g_row, 0.0), axis=1, keepdims=True)
        decay = jnp.where(incl, jnp.exp(jnp.where(incl, gc_col - gc_row, 0.0)), 0.0)
        egc = jnp.exp(gc_col)
        kb = k * bfull
        pre.append(dict(decay=decay, kb=kb, rhs=jnp.concatenate([v * bfull, kb * egc], axis=1),
                        qd=q * egc, kd=k * jnp.exp(gl_col - gc_col), egl=jnp.exp(gl_col)))
    mms = [jnp.where(strict, _dot(e["kb"], u[1], trans_b=True) * e["decay"], 0.0) for e, u in zip(pre, units)]
    qks = [_dot(u[0], u[1], trans_b=True) * e["decay"] for e, u in zip(pre, units)]
    tinvs = _unit_lower_inverse(mms, c, ri, ci)
    uws = [_dot(t, e["rhs"]) for t, e in zip(tinvs, pre)]
    return [dict(u=uw[:, :DV_A], w=uw[:, DV_A:], qk=qk, qd=e["qd"], kd=e["kd"], egl=e["egl"])
            for uw, qk, e in zip(uws, qks, pre)]


def _gdn_recur(e, states, c):
    nprob = GDN_STACK // c
    ws, qs = [], []
    for p in range(nprob):
        sl = slice(p * c, (p + 1) * c)
        ws.append(_dot(e["w"][sl], states[p]))
        qs.append(_dot(e["qd"][sl], states[p]))
    vn = e["u"] - jnp.concatenate(ws, axis=0)
    o = _dot(e["qk"], vn) + jnp.concatenate(qs, axis=0)
    new_states = []
    for p in range(nprob):
        sl = slice(p * c, (p + 1) * c)
        new_states.append(states[p] * e["egl"][p * c:p * c + 1, :] + _dot(e["kd"][sl], vn[sl], trans_a=True))
    return o, new_states


def _gdn_compute(x, z, u_sc, scb, ba, hq, hs, wq_ref, ws_ref, prm_ref, gn, states, *, seq8, c, nu):
    unit = GDN_CHUNK
    xc = _silu(_causal_conv(x, hq, wq_ref, CONV_A, seq8))
    yb = scb * _causal_conv(u_sc, hs, ws_ref, CONV_B, seq8)
    beta_all = _sigmoid(ba)
    sp = jnp.maximum(ba + prm_ref[1:2, :], 0.0) + jnp.log1p(jnp.exp(-jnp.abs(ba + prm_ref[1:2, :])))
    g_all = -jnp.exp(prm_ref[0:1, :]) * sp

    def head_cols(a, base):
        return a[:, base * DK_A:(base + 1) * DK_A]

    cat = lambda xs: jnp.concatenate(xs, axis=0)
    units = []
    for ui in range(nu):
        rs = slice(ui * unit, (ui + 1) * unit)
        qs, ks, vs, bs, gs = [], [], [], [], []
        for h in range(H_A):
            qh = head_cols(xc, h)[rs]
            kh = head_cols(xc, H_A + h)[rs]
            qs.append(qh * lax.rsqrt(jnp.sum(qh * qh, axis=-1, keepdims=True) + EPS) * (DK_A ** -0.5))
            ks.append(kh * lax.rsqrt(jnp.sum(kh * kh, axis=-1, keepdims=True) + EPS))
            vs.append(head_cols(xc, 2 * H_A + h)[rs])
            bs.append(jnp.broadcast_to(beta_all[rs, h:h + 1], (unit, LANES)))
            gs.append(jnp.broadcast_to(g_all[rs, H_A + h:H_A + h + 1], (unit, LANES)))
        units.append((cat(qs), cat(ks), cat(vs), cat(bs), cat(gs)))
    prepared = _gdn_prepare(units, c)

    outs = []
    for ui in range(nu):
        rs = slice(ui * unit, (ui + 1) * unit)
        o, states = _gdn_recur(prepared[ui], states, c)
        zst = cat([z[rs, h * DV_A:(h + 1) * DV_A] for h in range(H_A)])
        ms = jnp.mean(o * o, axis=-1, keepdims=True)
        og = o * lax.rsqrt(ms + EPS) * gn * _silu(zst)
        outs.append(jnp.concatenate([og[h * unit:(h + 1) * unit] for h in range(H_A)], axis=1))
    o_all = outs[0] if nu == 1 else cat(outs)
    return jnp.concatenate([o_all, yb], axis=1), states


def _gdn_kernel(qkv_ref, z_ref, sch_ref, scb_ref, scc_ref, ba_ref, hq_ref, hs_ref, wq_ref, ws_ref,
                prm_ref, gn_ref, s0_ref, o_ref, sn_ref, cq_ref, cs_ref):
    x = qkv_ref[...]
    rows = x.shape[0]
    u_sc = scc_ref[...] * sch_ref[...]
    cq_ref[...] = pltpu.roll(x, rows - SUBLANES + CONV_A - 1, 0)
    cs_ref[...] = pltpu.roll(u_sc, rows - SUBLANES + CONV_B - 1, 0)
    nprob = GDN_STACK // SUBLANES
    states = [s0_ref[p % SUBLANES, p // SUBLANES] for p in range(nprob)]
    mix, states = _gdn_compute(x, z_ref[...], u_sc, scb_ref[...], ba_ref[...], hq_ref[...], hs_ref[...],
                               wq_ref, ws_ref, prm_ref, gn_ref[...], states, seq8=True, c=SUBLANES, nu=1)
    o_ref[...] = mix.astype(o_ref.dtype)
    for p in range(nprob):
        sn_ref[p % SUBLANES, p // SUBLANES] = states[p]


def _gdn_block_kernel(x_ref, g_ref, wi_ref, wo_ref, hq_ref, hs_ref, wq_ref, ws_ref, prm_ref, gn_ref, s0_ref,
                      o_ref, sn_ref, cq_ref, cs_ref, s_scr, hq_scr, hs_scr, *, nu, nl):
    l = pl.program_id(1)

    @pl.when(l == 0)
    def _():
        hq_scr[...] = hq_ref[0]
        hs_scr[...] = hs_ref[0]
        s_scr[...] = s0_ref[0]

    xres = x_ref[...]
    rows = xres.shape[0]
    ms = jnp.mean(xres * xres, axis=-1, keepdims=True)
    xn = (xres * lax.rsqrt(ms + EPS) * g_ref[...]).astype(BF16)
    proj = lambda lo, width: jnp.dot(xn, wi_ref[:, lo:lo + width], preferred_element_type=F32)
    x = proj(0, W_QKV_A)
    z = proj(W_QKV_A, H_A * DV_A)
    u_sc = proj(PA_SC + 2 * D_B, D_B) * proj(PA_SC, D_B)
    scb = proj(PA_SC + D_B, D_B)
    ba = proj(PA_BA, LANES)
    hq = hq_scr[...]
    hs = hs_scr[...]
    hq_scr[...] = pltpu.roll(x[rows - SUBLANES:], CONV_A - 1, 0)
    hs_scr[...] = pltpu.roll(u_sc[rows - SUBLANES:], CONV_B - 1, 0)
    states = [s_scr[p] for p in range(H_A)]
    mix, states = _gdn_compute(x, z, u_sc, scb, ba, hq, hs, wq_ref, ws_ref, prm_ref, gn_ref[...], states,
                               seq8=False, c=GDN_CHUNK, nu=nu)
    for p in range(H_A):
        s_scr[p] = states[p]
    o_ref[...] = xres + jnp.dot(mix.astype(BF16), wo_ref[...], preferred_element_type=F32)

    @pl.when(l == nl - 1)
    def _():
        sn_ref[0] = s_scr[...]
        cq_ref[0] = hq_scr[...]
        cs_ref[0] = hs_scr[...]


def gdn_core(p, hist_q, hist_s, w_conv_qkv, w_conv_sc, prm, gn, s0, *, batch, seqlen):
    t = batch * seqlen
    assert seqlen == SUBLANES
    rows = GDN_CHUNK
    nb = rows // seqlen
    hq_spec = pl.BlockSpec((rows, W_QKV_A), lambda i: (i, 0))
    hs_spec = pl.BlockSpec((rows, D_B), lambda i: (i, 0))
    s_spec = pl.BlockSpec((nb, H_A, DK_A, DV_A), lambda i: (i, 0, 0, 0))
    col = lambda width, blk: pl.BlockSpec((rows, width), lambda i: (i, blk))
    const = lambda shape: pl.BlockSpec(shape, lambda i: (0,) * len(shape))
    return pl.pallas_call(
        _gdn_kernel,
        grid=(batch // nb,),
        in_specs=[col(W_QKV_A, 0), col(D_B, 3), col(D_B, 4), col(D_B, 5), col(D_B, 6),
                  col(LANES, PA_BA // LANES), hq_spec, hs_spec,
                  const((CONV_A, W_QKV_A)), const((CONV_B, D_B)), const((SUBLANES, LANES)),
                  const((1, DV_A)), s_spec],
        out_specs=[pl.BlockSpec((rows, D_MODEL), lambda i: (i, 0)), s_spec, hq_spec, hs_spec],
        out_shape=[jax.ShapeDtypeStruct((t, D_MODEL), BF16),
                   jax.ShapeDtypeStruct((batch, H_A, DK_A, DV_A), F32),
                   jax.ShapeDtypeStruct((t, W_QKV_A), F32),
                   jax.ShapeDtypeStruct((t, D_B), F32)],
        compiler_params=_cparams(("parallel",)),
        name="gdn_core",
    )(p, p, p, p, p, p, hist_q.reshape(t, W_QKV_A), hist_s.reshape(t, D_B), w_conv_qkv, w_conv_sc, prm,
      gn.reshape(1, DV_A), s0)


def _resident(shape):
    return pl.BlockSpec(shape, lambda *ix: (0,) * len(shape), pipeline_mode=pl.Buffered(1))


def gdn_block(x, g, w_in, w_out, hist_q, hist_s, w_conv_qkv, w_conv_sc, prm, gn, s0, *, batch, seqlen):
    t, d = batch * seqlen, x.shape[1]
    rows = min(seqlen, 256)
    nu = rows // GDN_CHUNK
    nl = seqlen // rows
    per_b = lambda shape: pl.BlockSpec((1,) + shape, lambda b, l: (b,) + (0,) * len(shape))
    row_spec = pl.BlockSpec((rows, d), lambda b, l: (b * nl + l, 0))
    kern = functools.partial(_gdn_block_kernel, nu=nu, nl=nl)
    return pl.pallas_call(
        kern,
        grid=(batch, nl),
        in_specs=[row_spec, _resident((1, d)), _resident(w_in.shape), _resident(w_out.shape),
                  per_b((SUBLANES, W_QKV_A)), per_b((SUBLANES, D_B)),
                  _resident((CONV_A, W_QKV_A)), _resident((CONV_B, D_B)), _resident((SUBLANES, LANES)),
                  _resident((1, DV_A)), per_b((H_A, DK_A, DV_A))],
        out_specs=[row_spec, per_b((H_A, DK_A, DV_A)), per_b((SUBLANES, W_QKV_A)), per_b((SUBLANES, D_B))],
        out_shape=[jax.ShapeDtypeStruct((t, d), F32),
                   jax.ShapeDtypeStruct((batch, H_A, DK_A, DV_A), F32),
                   jax.ShapeDtypeStruct((batch, SUBLANES, W_QKV_A), F32),
                   jax.ShapeDtypeStruct((batch, SUBLANES, D_B), F32)],
        scratch_shapes=[pltpu.VMEM((H_A, DK_A, DV_A), F32),
                        pltpu.VMEM((SUBLANES, W_QKV_A), F32),
                        pltpu.VMEM((SUBLANES, D_B), F32)],
        compiler_params=_cparams(("parallel", "arbitrary")),
        name="gdn_block",
    )(x, g.reshape(1, d), w_in, w_out, hist_q, hist_s, w_conv_qkv, w_conv_sc, prm, gn.reshape(1, DV_A), s0)


def _ret_compute(get_q, get_k, get_v, get_gate, cos, sin, dm_ref, qd_ref, kd_ref, cd_ref, rn_ref, r_scr,
                 *, nseq, c):
    half = DK_C // 2

    def rot(x):
        x1, x2 = x[:, :half], x[:, half:]
        return jnp.concatenate([x1 * cos - x2 * sin, x1 * sin + x2 * cos], axis=1)

    heads = range(H_C)
    qs = [rot(get_q(h)) for h in heads]
    ks = [rot(get_k(h)) * (DK_C ** -0.5) for h in heads]
    vs = [get_v(h).astype(BF16) for h in heads]
    ss = [_dot(qs[h], ks[h], trans_b=True) * dm_ref[h] for h in heads]
    inters = []
    for h in heads:
        qdh = qs[h] * qd_ref[h]
        parts = [_dot(qdh[sq * c:(sq + 1) * c], r_scr[sq, h]) for sq in range(nseq)]
        inters.append(parts[0] if nseq == 1 else jnp.concatenate(parts, axis=0))
    outs = [_dot(ss[h], vs[h]) + inters[h] for h in heads]
    for h in heads:
        kdh = ks[h] * kd_ref[h]
        cd = cd_ref[h][0:1, 0:1]
        for sq in range(nseq):
            sl = slice(sq * c, (sq + 1) * c)
            r_scr[sq, h] = r_scr[sq, h] * cd + _dot(kdh[sl], vs[h][sl], trans_a=True)
    gated = []
    for h in heads:
        o = outs[h]
        ms = jnp.mean(o * o, axis=-1, keepdims=True)
        on = o * lax.rsqrt(ms + EPS) * rn_ref[:, h * DV_C:(h + 1) * DV_C]
        gated.append(_silu(get_gate(h)) * on)
    return gated


def _ret_kernel(q_ref, k_ref, v_ref, gate_ref, cos_ref, sin_ref, dm_ref, qd_ref, kd_ref, cd_ref,
                rn_ref, r0_ref, o_ref, rnew_ref, r_scr, *, nseq, c, nl):
    l = pl.program_id(1)

    @pl.when(l == 0)
    def _():
        r_scr[...] = r0_ref[...]

    gated = _ret_compute(lambda h: q_ref[:, h * DK_C:(h + 1) * DK_C], lambda h: k_ref[:, h * DK_C:(h + 1) * DK_C],
                         lambda h: v_ref[:, h * DV_C:(h + 1) * DV_C], lambda h: gate_ref[:, h * DV_C:(h + 1) * DV_C],
                         cos_ref[...], sin_ref[...], dm_ref, qd_ref, kd_ref, cd_ref, rn_ref, r_scr, nseq=nseq, c=c)
    for h in range(H_C):
        o_ref[:, h * DV_C:(h + 1) * DV_C] = gated[h].astype(o_ref.dtype)

    @pl.when(l == nl - 1)
    def _():
        rnew_ref[...] = r_scr[...]


def _ret_block_kernel(x_ref, g_ref, wi_ref, wo_ref, cos_ref, sin_ref, dm_ref, qd_ref, kd_ref, cd_ref,
                      rn_ref, r0_ref, o_ref, rnew_ref, r_scr, *, c, nl):
    l = pl.program_id(1)

    @pl.when(l == 0)
    def _():
        r_scr[...] = r0_ref[...]

    xres = x_ref[...]
    ms = jnp.mean(xres * xres, axis=-1, keepdims=True)
    xn = (xres * lax.rsqrt(ms + EPS) * g_ref[...]).astype(BF16)
    proj = lambda lo, width: jnp.dot(xn, wi_ref[:, lo:lo + width], preferred_element_type=F32)
    hk, hv = H_C * DK_C, H_C * DV_C
    gated = _ret_compute(lambda h: proj(h * DK_C, DK_C), lambda h: proj(hk + h * DK_C, DK_C),
                         lambda h: proj(2 * hk + h * DV_C, DV_C), lambda h: proj(2 * hk + hv + h * DV_C, DV_C),
                         cos_ref[...], sin_ref[...], dm_ref, qd_ref, kd_ref, cd_ref, rn_ref, r_scr, nseq=1, c=c)
    y = xres
    for h in range(H_C):
        y = y + jnp.dot(gated[h].astype(BF16), wo_ref[h * DV_C:(h + 1) * DV_C, :], preferred_element_type=F32)
    o_ref[...] = y

    @pl.when(l == nl - 1)
    def _():
        rnew_ref[...] = r_scr[...]


def ret_core(p, pos, ret_norm, r0, *, batch, seqlen):
    t = batch * seqlen
    assert seqlen == SUBLANES
    nseq, c = 2, seqlen
    rows = nseq * c
    cos, sin, dmat, qd, kd, cd = _ret_tables(pos, nseq, c)
    const = lambda shape: pl.BlockSpec(shape, lambda b, l: (0,) * len(shape))
    kern = functools.partial(_ret_kernel, nseq=nseq, c=c, nl=1)
    hk = H_C * DK_C
    hv = H_C * DV_C
    return pl.pallas_call(
        kern,
        grid=(batch // nseq, 1),
        in_specs=[pl.BlockSpec((rows, hk), lambda b, l: (b, 0)),
                  pl.BlockSpec((rows, hk), lambda b, l: (b, 1)),
                  pl.BlockSpec((rows, hv), lambda b, l: (b, 1)),
                  pl.BlockSpec((rows, hv), lambda b, l: (b, 2)),
                  const((rows, DK_C // 2)), const((rows, DK_C // 2)),
                  const((H_C, rows, rows)), const((H_C, rows, DK_C)), const((H_C, rows, DK_C)),
                  const((H_C, SUBLANES, LANES)), const((1, hv)),
                  pl.BlockSpec((nseq, H_C, DK_C, DV_C), lambda b, l: (b, 0, 0, 0))],
        out_specs=[pl.BlockSpec((rows, hv), lambda b, l: (b, 0)),
                   pl.BlockSpec((nseq, H_C, DK_C, DV_C), lambda b, l: (b, 0, 0, 0))],
        out_shape=[jax.ShapeDtypeStruct((t, hv), BF16),
                   jax.ShapeDtypeStruct((batch, H_C, DK_C, DV_C), F32)],
        scratch_shapes=[pltpu.VMEM((nseq, H_C, DK_C, DV_C), F32)],
        compiler_params=_cparams(("parallel", "arbitrary")),
        name="ret_core",
    )(p, p, p, p, cos, sin, dmat, qd, kd, cd, ret_norm.reshape(1, hv), r0)


def _ret_tables(pos, nseq, c):
    half = DK_C // 2
    inv = ROPE_BASE ** (-jnp.arange(half, dtype=F32) / half)
    ang = pos.astype(F32)[:, None] * inv[None, :]
    cos, sin = jnp.cos(ang), jnp.sin(ang)
    if nseq > 1:
        cos, sin = jnp.tile(cos, (nseq, 1)), jnp.tile(sin, (nseq, 1))
    lg = jnp.log(1.0 - 2.0 ** (-5.0 - jnp.arange(H_C, dtype=F32)))[:, None]
    i = jnp.arange(c, dtype=F32)
    incl = i[:, None] >= i[None, :]
    dmat = jnp.exp(jnp.where(incl[None], (i[:, None] - i[None, :])[None] * lg[..., None], -jnp.inf))
    if nseq > 1:
        dmat = jnp.kron(jnp.eye(nseq, dtype=F32)[None], dmat)
    qd = jnp.tile(jnp.exp((i + 1.0)[None] * lg), (1, nseq))[..., None] * jnp.ones((1, 1, DK_C), F32)
    kd = jnp.tile(jnp.exp((c - 1.0 - i)[None] * lg), (1, nseq))[..., None] * jnp.ones((1, 1, DK_C), F32)
    cd = jnp.exp(c * lg)[..., None] * jnp.ones((1, SUBLANES, LANES), F32)
    return cos, sin, dmat, qd, kd, cd


def ret_block(x, g, w_in, w_out, pos, ret_norm, r0, *, batch, seqlen):
    t, d = batch * seqlen, x.shape[1]
    c = min(seqlen, 256)
    nl = seqlen // c
    cos, sin, dmat, qd, kd, cd = _ret_tables(pos, 1, c)
    hv = H_C * DV_C
    row_spec = pl.BlockSpec((c, d), lambda b, l: (b * nl + l, 0))
    trig_spec = pl.BlockSpec((c, DK_C // 2), lambda b, l: (l, 0))
    state_spec = pl.BlockSpec((1, H_C, DK_C, DV_C), lambda b, l: (b, 0, 0, 0))
    kern = functools.partial(_ret_block_kernel, c=c, nl=nl)
    return pl.pallas_call(
        kern,
        grid=(batch, nl),
        in_specs=[row_spec, _resident((1, d)), _resident(w_in.shape), _resident(w_out.shape),
                  trig_spec, trig_spec,
                  _resident((H_C, c, c)), _resident((H_C, c, DK_C)), _resident((H_C, c, DK_C)),
                  _resident((H_C, SUBLANES, LANES)), _resident((1, hv)), state_spec],
        out_specs=[row_spec, state_spec],
        out_shape=[jax.ShapeDtypeStruct((t, d), F32),
                   jax.ShapeDtypeStruct((batch, H_C, DK_C, DV_C), F32)],
        scratch_shapes=[pltpu.VMEM((1, H_C, DK_C, DV_C), F32)],
        compiler_params=_cparams(("parallel", "arbitrary")),
        name="ret_block",
    )(x, g.reshape(1, d), w_in, w_out, cos, sin, dmat, qd, kd, cd, ret_norm.reshape(1, hv), r0)


def _xattn_fetch(mk_hbm, mv_hbm, kbuf, vbuf, sem, *, layer, nb, nsteps):
    i = pl.program_id(0)
    l = pl.program_id(1)
    slot = i % 2

    def copies(step, slot_):
        out = []
        for b in range(nb):
            for h in range(H_X):
                out.append(pltpu.make_async_copy(mk_hbm.at[layer, step * nb + b, :, h, :],
                                                 kbuf.at[slot_, b, h], sem.at[slot_]))
                out.append(pltpu.make_async_copy(mv_hbm.at[layer, step * nb + b, :, h, :],
                                                 vbuf.at[slot_, b, h], sem.at[slot_]))
        return out

    @pl.when(l == 0)
    def _():
        @pl.when(i == 0)
        def _():
            for c in copies(i, slot):
                c.start()

        @pl.when(i + 1 < nsteps)
        def _():
            for c in copies(i + 1, 1 - slot):
                c.start()

        for c in copies(i, slot):
            c.wait()

    return slot


def _xattn_heads(q_of, kbuf, vbuf, slot, probs):
    ss = [_dot(q_of(b, h), kbuf[slot, b, h], trans_b=True) * (HD_X ** -0.5) for b, h in probs]
    ps = []
    for s in ss:
        e = jnp.exp(s - jnp.max(s, axis=-1, keepdims=True))
        ps.append(e / jnp.sum(e, axis=-1, keepdims=True))
    return [_dot(p, vbuf[slot, b, h]) for p, (b, h) in zip(ps, probs)]


def _xattn_kernel(q_ref, mk_hbm, mv_hbm, o_ref, kbuf, vbuf, sem, *, layer, nb, lq, nsteps):
    slot = _xattn_fetch(mk_hbm, mv_hbm, kbuf, vbuf, sem, layer=layer, nb=nb, nsteps=nsteps)
    probs = [(b, h) for b in range(nb) for h in range(H_X)]
    win = lambda b, h: (slice(b * lq, (b + 1) * lq), slice(h * HD_X, (h + 1) * HD_X))
    outs = _xattn_heads(lambda b, h: q_ref[win(b, h)], kbuf, vbuf, slot, probs)
    for o, (b, h) in zip(outs, probs):
        o_ref[win(b, h)] = o.astype(o_ref.dtype)


def _xattn_block_kernel(x_ref, g_ref, wq_ref, wo_ref, mk_hbm, mv_hbm, into_hbm, o_ref, kbuf, vbuf, sem,
                        *, layer, nsteps):
    del into_hbm
    slot = _xattn_fetch(mk_hbm, mv_hbm, kbuf, vbuf, sem, layer=layer, nb=1, nsteps=nsteps)
    xres = x_ref[...]
    ms = jnp.mean(xres * xres, axis=-1, keepdims=True)
    xn = (xres * lax.rsqrt(ms + EPS) * g_ref[...]).astype(BF16)
    probs = [(0, h) for h in range(H_X)]
    qs = [jnp.dot(xn, wq_ref[:, h * HD_X:(h + 1) * HD_X], preferred_element_type=F32) for h in range(H_X)]
    outs = _xattn_heads(lambda b, h: qs[h], kbuf, vbuf, slot, probs)
    y = xres
    for h in range(H_X):
        y = y + jnp.dot(outs[h].astype(BF16), wo_ref[h * HD_X:(h + 1) * HD_X, :], preferred_element_type=F32)
    o_ref[...] = y


def xattn_core(q, mk, mv, layer, *, batch, seqlen):
    t = batch * seqlen
    d = H_X * HD_X
    assert seqlen == SUBLANES
    nb, lq = 2, seqlen
    rows = nb * lq
    nsteps = batch // nb
    kern = functools.partial(_xattn_kernel, layer=layer, nb=nb, lq=lq, nsteps=nsteps)
    return pl.pallas_call(
        kern,
        grid=(nsteps, 1),
        in_specs=[pl.BlockSpec((rows, d), lambda b, l: (b, 0)),
                  pl.BlockSpec(memory_space=pl.ANY), pl.BlockSpec(memory_space=pl.ANY)],
        out_specs=pl.BlockSpec((rows, d), lambda b, l: (b, 0)),
        out_shape=jax.ShapeDtypeStruct((t, d), q.dtype),
        scratch_shapes=[pltpu.VMEM((2, nb, H_X, N_MEM, HD_X), F32),
                        pltpu.VMEM((2, nb, H_X, N_MEM, HD_X), F32),
                        pltpu.SemaphoreType.DMA((2,))],
        compiler_params=_cparams(("arbitrary", "arbitrary")),
        name="xattn_core",
    )(q, mk, mv)


def xattn_block(x, g, w_q, w_o, mk, mv, layer, into, *, batch, seqlen):
    d = x.shape[1]
    lq = min(seqlen, 512)
    nl = seqlen // lq
    row_spec = pl.BlockSpec((lq, d), lambda b, l: (b * nl + l, 0))
    kern = functools.partial(_xattn_block_kernel, layer=layer, nsteps=batch)
    return pl.pallas_call(
        kern,
        grid=(batch, nl),
        in_specs=[row_spec, _resident((1, d)), _resident(w_q.shape), _resident(w_o.shape),
                  pl.BlockSpec(memory_space=pl.ANY), pl.BlockSpec(memory_space=pl.ANY),
                  pl.BlockSpec(memory_space=pl.ANY)],
        out_specs=row_spec,
        out_shape=jax.ShapeDtypeStruct(into.shape, F32),
        scratch_shapes=[pltpu.VMEM((2, 1, H_X, N_MEM, HD_X), F32),
                        pltpu.VMEM((2, 1, H_X, N_MEM, HD_X), F32),
                        pltpu.SemaphoreType.DMA((2,))],
        input_output_aliases={6: 0},
        compiler_params=_cparams(("arbitrary", "arbitrary")),
        name="xattn_block",
    )(x, g.reshape(1, d), w_q, w_o, mk, mv, into)


def _route_kernel(x_ref, g_ref, wr_ref, br_ref, h_ref, meta_ref, wts_ref, cnt_ref, cnt_scr):
    i = pl.program_id(0)

    @pl.when(i == 0)
    def _():
        cnt_scr[...] = jnp.zeros_like(cnt_scr)

    x = x_ref[...]
    tm = x.shape[0]
    ms = jnp.mean(x * x, axis=-1, keepdims=True)
    h = x * lax.rsqrt(ms + EPS) * g_ref[...]
    wr = wr_ref[...]
    h_hi = h.astype(BF16)
    bits = pltpu.bitcast(h_hi.astype(F32), jnp.uint32)
    half_d = bits.shape[1] // 2
    h_ref[...] = bits[:, :half_d] | (bits[:, half_d:] >> 16)
    h_lo = (h - h_hi.astype(F32)).astype(BF16)
    w_hi = wr.astype(BF16)
    w_lo = (wr - w_hi.astype(F32)).astype(BF16)
    logits = (jnp.dot(h_hi, w_hi, preferred_element_type=F32) + jnp.dot(h_hi, w_lo, preferred_element_type=F32)
              + jnp.dot(h_lo, w_hi, preferred_element_type=F32)) + br_ref[...]
    lane_i = lax.broadcasted_iota(I32, (tm, LANES), 1)
    lane = lane_i.astype(F32)
    neg = jnp.float32(-3.0e38)
    big = jnp.float32(LANES)
    is_g = lane_i < N_GROUPS
    gl = jnp.where(is_g, logits, neg)
    gmax = jnp.max(gl, axis=1, keepdims=True)
    grp = jnp.min(jnp.where(gl == gmax, lane, big), axis=1, keepdims=True)
    gsum = jnp.sum(jnp.where(is_g, jnp.exp(jnp.where(is_g, logits - gmax, 0.0)), 0.0), axis=1, keepdims=True)
    p_grp = 1.0 / gsum
    in_grp = ((lane_i >= R_E0) & (lane_i < R_E0 + N_EXPERTS)
              & (jnp.floor((lane - R_E0) * (1.0 / E_PER_GROUP)) == grp))
    el = jnp.where(in_grp, logits, neg)
    m1 = jnp.max(el, axis=1, keepdims=True)
    i1 = jnp.min(jnp.where(el == m1, lane, big), axis=1, keepdims=True)
    el2 = jnp.where(lane == i1, neg, el)
    m2 = jnp.max(el2, axis=1, keepdims=True)
    i2 = jnp.min(jnp.where(el2 == m2, lane, big), axis=1, keepdims=True)
    esum = jnp.sum(jnp.where(in_grp, jnp.exp(jnp.where(in_grp, logits - m1, 0.0)), 0.0), axis=1, keepdims=True)
    p1 = 1.0 / esum
    p2 = jnp.exp(m2 - m1) / esum
    tot = p1 + p2
    w1 = p_grp * (p1 / tot)
    w2 = p_grp * (p2 / tot)
    wts_ref[...] = jnp.where(lane_i == 0, w1, jnp.where(lane_i == 1, w2, 0.0))

    oh1 = (lane == i1).astype(F32)
    oh2 = (lane == i2).astype(F32)
    rr = lax.broadcasted_iota(I32, (tm, tm), 0)
    cc = lax.broadcasted_iota(I32, (tm, tm), 1)
    tri = (rr > cc).astype(BF16)
    base = cnt_scr[0:1, :]
    c1 = jnp.sum(oh1, axis=0, keepdims=True)
    c2 = jnp.sum(oh2, axis=0, keepdims=True)
    r1 = jnp.sum(oh1 * (jnp.dot(tri, oh1.astype(BF16), preferred_element_type=F32) + base), axis=1, keepdims=True)
    r2 = jnp.sum(oh2 * (jnp.dot(tri, oh2.astype(BF16), preferred_element_type=F32) + base + c1), axis=1, keepdims=True)
    new_cnt = base + c1 + c2
    cnt_scr[...] = jnp.broadcast_to(new_cnt, cnt_scr.shape)
    cnt_ref[...] = jnp.broadcast_to(new_cnt, cnt_ref.shape)
    meta = jnp.where(lane_i == 0, i1, jnp.where(lane_i == 1, i2, 0.0))
    meta = jnp.where(lane_i == 2, r1, jnp.where(lane_i == 3, r2, meta))
    meta_ref[...] = meta.astype(I32)


def moe_route(x, g, wr, br, *, tm=256):
    t, d = x.shape
    tm = min(tm, t)
    return pl.pallas_call(
        _route_kernel,
        grid=(t // tm,),
        in_specs=[pl.BlockSpec((tm, d), lambda i: (i, 0)),
                  pl.BlockSpec((1, d), lambda i: (0, 0)),
                  pl.BlockSpec((d, LANES), lambda i: (0, 0)),
                  pl.BlockSpec((1, LANES), lambda i: (0, 0))],
        out_specs=[pl.BlockSpec((tm, d // 2), lambda i: (i, 0)),
                   pl.BlockSpec((tm, LANES), lambda i: (i, 0)),
                   pl.BlockSpec((tm, LANES), lambda i: (i, 0)),
                   pl.BlockSpec((SUBLANES, LANES), lambda i: (0, 0))],
        out_shape=[jax.ShapeDtypeStruct((t, d // 2), jnp.uint32),
                   jax.ShapeDtypeStruct((t, LANES), I32),
                   jax.ShapeDtypeStruct((t, LANES), F32),
                   jax.ShapeDtypeStruct((SUBLANES, LANES), F32)],
        scratch_shapes=[pltpu.VMEM((SUBLANES, LANES), F32)],
        compiler_params=_cparams(("arbitrary",)),
        name="moe_route",
    )(x, g.reshape(1, d), wr, br)


def _plan_kernel(cnt_ref, meta_ref, dest_ref, be_ref, *, nblk_pad):
    cnt = cnt_ref[...]
    lane8 = lax.broadcasted_iota(I32, (SUBLANES, LANES), 1)
    padded = jnp.ceil(cnt / MOE_BLK) * MOE_BLK
    pend = padded
    s = 1
    while s < LANES:
        pend = pend + jnp.where(lane8 >= s, pltpu.roll(pend, s, 1), 0.0)
        s *= 2
    pstart = (pend - padded)[0:1, :]
    meta = meta_ref[...].astype(F32)
    tm = meta.shape[0]
    lane_i = lax.broadcasted_iota(I32, (tm, LANES), 1)
    lane = lane_i.astype(F32)
    col = lambda j: jnp.sum(jnp.where(lane_i == j, meta, 0.0), axis=1, keepdims=True)
    e1, e2, r1, r2 = col(0), col(1), col(2), col(3)
    d1 = jnp.sum(jnp.where(lane == e1, pstart, 0.0), axis=1, keepdims=True) + r1
    d2 = jnp.sum(jnp.where(lane == e2, pstart, 0.0), axis=1, keepdims=True) + r2
    dest_ref[...] = jnp.where(lane_i == 0, d1, jnp.where(lane_i == 1, d2, 0.0)).astype(I32)
    bi = (lax.broadcasted_iota(I32, (nblk_pad, LANES), 0) * MOE_BLK).astype(F32)
    lane_b = lax.broadcasted_iota(I32, (nblk_pad, LANES), 1)
    is_e = (lane_b >= R_E0) & (lane_b < R_E0 + N_EXPERTS)
    nfull = jnp.sum(jnp.where(is_e & (bi >= pend[0:1, :]), 1.0, 0.0), axis=1, keepdims=True)
    be = jnp.minimum(nfull, N_EXPERTS - 1.0)
    nused = jnp.max(pend[0:1, :], axis=1, keepdims=True) / MOE_BLK
    row = lax.broadcasted_iota(I32, (nblk_pad, LANES), 0)
    be_ref[...] = jnp.where(row == nblk_pad - 1, nused, be).astype(I32)


def moe_plan(cnt, meta, *, nblk_pad, tm=512):
    t = meta.shape[0]
    tm = min(tm, t)
    kern = functools.partial(_plan_kernel, nblk_pad=nblk_pad)
    return pl.pallas_call(
        kern,
        grid=(t // tm,),
        in_specs=[pl.BlockSpec((SUBLANES, LANES), lambda i: (0, 0)),
                  pl.BlockSpec((tm, LANES), lambda i: (i, 0))],
        out_specs=[pl.BlockSpec((tm, LANES), lambda i: (i, 0)),
                   pl.BlockSpec((nblk_pad, LANES), lambda i: (0, 0))],
        out_shape=[jax.ShapeDtypeStruct((t, LANES), I32),
                   jax.ShapeDtypeStruct((nblk_pad, LANES), I32)],
        compiler_params=_cparams(("arbitrary",)),
        name="moe_plan",
    )(cnt, meta)


DMA_UNROLL = 8
DISPATCH_SLOTS = 3


def _dispatch_kernel(dest_ref, h_hbm, xd_in, xd_hbm, hbuf, lsem, ssem, *, tm, nsteps):
    del xd_in
    i = pl.program_id(0)
    slot = i % DISPATCH_SLOTS

    def load(step, slot_):
        return pltpu.make_async_copy(h_hbm.at[pl.ds(step * tm, tm)], hbuf.at[slot_], lsem.at[slot_])

    def rows_done(slot_):
        return pltpu.make_async_copy(hbuf.at[slot_], xd_hbm.at[pl.ds(0, tm)], ssem.at[slot_])

    @pl.when(i == 0)
    def _():
        load(0, 0).start()
        if nsteps > 1:
            load(1, 1).start()

    load(i, slot).wait()

    def start(t, carry):
        for k in range(2):
            pltpu.make_async_copy(hbuf.at[slot, pl.ds(t, 1)], xd_hbm.at[pl.ds(dest_ref[2 * t + k], 1)],
                                  ssem.at[slot]).start(priority=k)
        return carry

    lax.fori_loop(0, tm, start, 0, unroll=DMA_UNROLL)

    @pl.when(i >= 1)
    def _():
        prev = (i + DISPATCH_SLOTS - 1) % DISPATCH_SLOTS
        rows_done(prev).wait()
        rows_done(prev).wait()

    @pl.when(i + 2 < nsteps)
    def _():
        load(i + 2, (i + 2) % DISPATCH_SLOTS).start()

    @pl.when(i == nsteps - 1)
    def _():
        rows_done(slot).wait()
        rows_done(slot).wait()


def moe_dispatch(dest_flat, h, xd_zero, *, tm=512):
    t, d = h.shape
    tm = min(tm, t)
    nsteps = t // tm
    kern = functools.partial(_dispatch_kernel, tm=tm, nsteps=nsteps)
    return pl.pallas_call(
        kern,
        grid=(nsteps,),
        in_specs=[pl.BlockSpec((2 * tm,), lambda i: (i,), memory_space=pltpu.SMEM),
                  pl.BlockSpec(memory_space=pl.ANY),
                  pl.BlockSpec(memory_space=pl.ANY)],
        out_specs=pl.BlockSpec(memory_space=pl.ANY),
        out_shape=jax.ShapeDtypeStruct(xd_zero.shape, h.dtype),
        scratch_shapes=[pltpu.VMEM((DISPATCH_SLOTS, tm, d), h.dtype),
                        pltpu.SemaphoreType.DMA((DISPATCH_SLOTS,)),
                        pltpu.SemaphoreType.DMA((DISPATCH_SLOTS,))],
        input_output_aliases={2: 0},
        compiler_params=_cparams(("arbitrary",)),
        name="moe_dispatch",
    )(dest_flat, h, xd_zero)


def _experts_kernel(be_ref, x_ref, wg_ref, wu_ref, wd_ref, o_ref, wg_s, wu_s, wd_s, *, nblk_pad):
    i = pl.program_id(0)
    nused = be_ref[nblk_pad - 1]

    @pl.when(i < nused)
    def _():
        prev = be_ref[jnp.maximum(i - 1, 0)]

        @pl.when((i == 0) | (be_ref[i] != prev))
        def _():
            wg_s[...] = wg_ref[0].astype(BF16)
            wu_s[...] = wu_ref[0].astype(BF16)
            wd_s[...] = wd_ref[0].astype(BF16)

        half = MOE_BLK // 2

        def unpack(w):
            hi = pltpu.bitcast(w & jnp.uint32(0xFFFF0000), F32)
            lo = pltpu.bitcast(w << 16, F32)
            return jnp.concatenate([hi, lo], axis=1).astype(BF16)

        xs = [unpack(x_ref[r * half:(r + 1) * half, :]) for r in range(2)]
        gs = [jnp.dot(x, wg_s[...], preferred_element_type=F32) for x in xs]
        us = [jnp.dot(x, wu_s[...], preferred_element_type=F32) for x in xs]
        acts = [(_silu(g) * u).astype(BF16) for g, u in zip(gs, us)]
        for r in range(2):
            o_ref[r * half:(r + 1) * half, :] = jnp.dot(acts[r], wd_s[...], preferred_element_type=F32)

    @pl.when(i >= nused)
    def _():
        o_ref[...] = jnp.zeros_like(o_ref)


def moe_experts(be_flat, xd, w_gate, w_up, w_down, e0, *, nblk, nblk_pad):
    d, f = w_gate.shape[1], w_gate.shape[2]
    kern = functools.partial(_experts_kernel, nblk_pad=nblk_pad)
    grid_spec = pltpu.PrefetchScalarGridSpec(
        num_scalar_prefetch=1,
        grid=(nblk,),
        in_specs=[pl.BlockSpec((MOE_BLK, d // 2), lambda i, be: (i, 0)),
                  pl.BlockSpec((1, d, f), lambda i, be: (e0 + be[i], 0, 0)),
                  pl.BlockSpec((1, d, f), lambda i, be: (e0 + be[i], 0, 0)),
                  pl.BlockSpec((1, f, d), lambda i, be: (e0 + be[i], 0, 0))],
        out_specs=pl.BlockSpec((MOE_BLK, d), lambda i, be: (i, 0)),
        scratch_shapes=[pltpu.VMEM((d, f), BF16), pltpu.VMEM((d, f), BF16), pltpu.VMEM((f, d), BF16)],
    )
    return pl.pallas_call(
        kern,
        grid_spec=grid_spec,
        out_shape=jax.ShapeDtypeStruct((xd.shape[0], d), F32),
        compiler_params=_cparams(("arbitrary",)),
        name="moe_experts",
    )(be_flat, xd, w_gate, w_up, w_down)


def _combine_kernel(dest_ref, dest_next_ref, x_ref, wts_ref, gf_ref, yd_hbm, o_ref, rbuf, sem,
                    *, tm, nsteps, final_norm):
    i = pl.program_id(0)
    slot = i % 2

    def gather(dref, slot_):
        def start(t, carry):
            for k in range(2):
                pltpu.make_async_copy(yd_hbm.at[pl.ds(dref[2 * t + k], 1)], rbuf.at[slot_, k, pl.ds(t, 1)],
                                      sem.at[slot_]).start(priority=k)
            return carry
        lax.fori_loop(0, tm, start, 0, unroll=DMA_UNROLL)

    @pl.when(i == 0)
    def _():
        gather(dest_ref, 0)

    @pl.when(i + 1 < nsteps)
    def _():
        gather(dest_next_ref, 1 - slot)

    for k in range(2):
        pltpu.make_async_copy(yd_hbm.at[pl.ds(0, tm)], rbuf.at[slot, k], sem.at[slot]).wait()
    w = wts_ref[...]
    y = x_ref[...] + rbuf[slot, 0] * w[:, 0:1] + rbuf[slot, 1] * w[:, 1:2]
    if final_norm:
        y = y * lax.rsqrt(jnp.mean(y * y, axis=-1, keepdims=True) + EPS) * gf_ref[...]
    o_ref[...] = y


def moe_combine(dest_flat, x, wts, yd, g_final, *, row0=0, rows=None, tm=512):
    t, d = (rows or x.shape[0]), x.shape[1]
    tm = min(tm, t)
    nsteps = t // tm
    blk0 = row0 // tm
    final_norm = g_final is not None
    gf = g_final.reshape(1, d) if final_norm else jnp.ones((1, d), F32)
    kern = functools.partial(_combine_kernel, tm=tm, nsteps=nsteps, final_norm=final_norm)
    return pl.pallas_call(
        kern,
        grid=(nsteps,),
        in_specs=[pl.BlockSpec((2 * tm,), lambda i: (i + blk0,), memory_space=pltpu.SMEM),
                  pl.BlockSpec((2 * tm,), lambda i: (jnp.minimum(i + 1, nsteps - 1) + blk0,),
                               memory_space=pltpu.SMEM),
                  pl.BlockSpec((tm, d), lambda i: (i + blk0, 0)),
                  pl.BlockSpec((tm, LANES), lambda i: (i + blk0, 0)),
                  pl.BlockSpec((1, d), lambda i: (0, 0)),
                  pl.BlockSpec(memory_space=pl.ANY)],
        out_specs=pl.BlockSpec((tm, d), lambda i: (i, 0)),
        out_shape=jax.ShapeDtypeStruct((t, d), F32),
        scratch_shapes=[pltpu.VMEM((2, 2, tm, d), F32), pltpu.SemaphoreType.DMA((2,))],
        compiler_params=_cparams(("arbitrary",)),
        name="moe_combine",
    )(dest_flat, dest_flat, x, wts, gf, yd)


def moe_block(x, g, wr, br, w_gate, w_up, w_down, e0, g_final=None, splits=None):
    t, d = x.shape
    nblk = (2 * t) // MOE_BLK + N_EXPERTS
    nblk_pad = -(-(nblk + 1) // SUBLANES) * SUBLANES
    h, meta, wts, cnt = moe_route(x, g, wr, br)
    dest, be = moe_plan(cnt, meta, nblk_pad=nblk_pad)
    dest_flat = dest[:, :2].reshape(2 * t)
    be_flat = be[:, 0]
    xd = moe_dispatch(dest_flat, h, jnp.zeros((nblk * MOE_BLK, h.shape[1]), h.dtype))
    yd = moe_experts(be_flat, xd, w_gate, w_up, w_down, e0, nblk=nblk, nblk_pad=nblk_pad)
    if splits is None:
        return moe_combine(dest_flat, x, wts, yd, g_final)
    return [moe_combine(dest_flat, x, wts, yd, g_final, row0=r0, rows=n) for r0, n in splits]


def _pad_rows(buf):
    return jnp.pad(buf, ((0, 0), (0, SUBLANES - buf.shape[1]), (0, 0)))


def _forward(x_long, x_short, grp_long, grp_short, wts):
    depth = grp_long["mem_k"].shape[0]
    bl, ll = grp_long["batch"], grp_long["seqlen"]
    bs, ls = grp_short["batch"], grp_short["seqlen"]
    t_long, t_short = bl * ll, bs * ls
    d = x_long.shape[1]
    new = {id(grp_long): dict(gdn=[], conv=[], sc=[], ret=[]), id(grp_short): dict(gdn=[], conv=[], sc=[], ret=[])}

    def record_even(grp, s_new, cq, cs):
        rec = new[id(grp)]
        rec["gdn"].append(s_new)
        rec["conv"].append(cq.reshape(grp["batch"], SUBLANES, W_QKV_A)[:, :CONV_A - 1])
        rec["sc"].append(cs.reshape(grp["batch"], SUBLANES, D_B)[:, :CONV_B - 1])

    x_all = None
    for layer in range(depth):
        src_long = x_long if x_all is None else x_all
        src_short, row0 = (x_short, 0) if x_all is None else (x_all, t_long)
        if layer % 2 == 0:
            i = layer // 2
            hist = lambda grp: (_pad_rows(grp["conv"][i]), _pad_rows(grp["sc"][i]), wts["w_conv_qkv"][i],
                                wts["w_conv_sc"][i], wts["gdn_prm"][i], wts["gdn_norm"][i], grp["gdn"][i])
            p = rms_matmul(src_short, wts["norm_mix"][layer], wts["w_in_a"][i], tn=768, rows=t_short, row0=row0)
            mix, s_new, cq, cs = gdn_core(p, *hist(grp_short), batch=bs, seqlen=ls)
            xs = matmul_res(mix, wts["w_out_a"][i], src_short, res_row0=row0)
            record_even(grp_short, s_new, cq, cs)
            xl, s_new, cq, cs = gdn_block(src_long, wts["norm_mix"][layer], wts["w_in_a"][i], wts["w_out_a"][i],
                                          *hist(grp_long), batch=bl, seqlen=ll)
            record_even(grp_long, s_new, cq, cs)
        else:
            j = layer // 2
            p = rms_matmul(src_short, wts["norm_mix"][layer], wts["w_in_c"][j], tn=768, rows=t_short, row0=row0)
            ret, r_new = ret_core(p, grp_short["pos"], wts["ret_norm"][j], grp_short["ret"][j], batch=bs, seqlen=ls)
            xs = matmul_res(ret, wts["w_out_c"][j], src_short, res_row0=row0)
            new[id(grp_short)]["ret"].append(r_new)
            xl, r_new = ret_block(src_long, wts["norm_mix"][layer], wts["w_in_c"][j], wts["w_out_c"][j],
                                  grp_long["pos"], wts["ret_norm"][j], grp_long["ret"][j], batch=bl, seqlen=ll)
            new[id(grp_long)]["ret"].append(r_new)
        q = rms_matmul(xs, wts["norm_x"][layer], wts["w_xq"][layer], tn=D_MODEL)
        att = xattn_core(q, grp_short["mem_k"], grp_short["mem_v"], layer, batch=bs, seqlen=ls)
        joint = jnp.zeros((t_long + t_short, d), F32) if x_all is None else x_all
        joint = matmul_res(att, wts["w_xo"][layer], xs, into=joint, out_row0=t_long)
        joint = xattn_block(xl, wts["norm_x"][layer], wts["w_xq"][layer], wts["w_xo"][layer],
                            grp_long["mem_k"], grp_long["mem_v"], layer, joint, batch=bl, seqlen=ll)
        last = layer == depth - 1
        x_all = moe_block(joint, wts["norm_ffn"][layer], wts["w_route"][layer], wts["b_route"][layer],
                          wts["w_exp_gate"], wts["w_exp_up"], wts["w_exp_down"], layer * N_EXPERTS,
                          g_final=wts["norm_final"] if last else None,
                          splits=[(0, t_long), (t_long, t_short)] if last else None)
    y_long, y_short = x_all
    stack = lambda grp: tuple(jnp.stack(new[id(grp)][k]) for k in ("gdn", "conv", "sc", "ret"))
    return (y_long,) + stack(grp_long), (y_short,) + stack(grp_short)


def kernel(x_prompt, x_sample, state_gdn, state_gdn_conv, state_sconv, state_ret, cache_mem_k, cache_mem_v, mem_prompt, norm_mix, norm_x, norm_ffn, norm_final, norm_mem, w_in_a, w_conv_qkv, a_log, dt_bias, gdn_norm, w_conv_sc, w_out_a, w_in_c, ret_norm, w_out_c, w_xq, w_xk, w_xv, w_xo, w_group, b_group, w_router, b_router, w_exp_gate, w_exp_up, w_exp_down):
    bp, lp, d = x_prompt.shape
    bs, ls, _ = x_sample.shape
    depth = norm_mix.shape[0]
    n_even = w_in_a.shape[0]
    n_mem = mem_prompt.shape[1]

    qkv_w = 2 * H_A * DK_A + H_A * DV_A
    o_z = qkv_w
    o_b = o_z + H_A * DV_A
    o_a = o_b + H_A
    o_sc = o_a + H_A
    w_a = jnp.concatenate([w_in_a[:, :, :o_b], w_in_a[:, :, o_sc:], w_in_a[:, :, o_b:o_sc],
                           jnp.zeros((n_even, d, PA_COLS - PA_BA - 2 * H_A), F32)], axis=-1).astype(BF16)
    prm = jnp.zeros((n_even, SUBLANES, LANES), F32)
    prm = prm.at[:, 0, H_A:2 * H_A].set(a_log).at[:, 1, H_A:2 * H_A].set(dt_bias)
    w_route = jnp.concatenate([w_group, w_router, jnp.zeros((depth, d, LANES - N_GROUPS - N_EXPERTS), F32)], axis=-1)
    b_route = jnp.concatenate([b_group, b_router, jnp.zeros((depth, LANES - N_GROUPS - N_EXPERTS), F32)],
                              axis=-1).reshape(depth, 1, LANES)
    wts = dict(norm_mix=norm_mix, norm_x=norm_x, norm_ffn=norm_ffn, norm_final=norm_final,
               w_in_a=w_a, w_conv_qkv=w_conv_qkv, gdn_prm=prm, gdn_norm=gdn_norm, w_conv_sc=w_conv_sc,
               w_out_a=w_out_a.astype(BF16), w_in_c=w_in_c.astype(BF16), ret_norm=ret_norm,
               w_out_c=w_out_c.astype(BF16), w_xq=w_xq.astype(BF16), w_xo=w_xo.astype(BF16),
               w_route=w_route, b_route=b_route,
               w_exp_gate=w_exp_gate.reshape((depth * N_EXPERTS,) + w_exp_gate.shape[2:]),
               w_exp_up=w_exp_up.reshape((depth * N_EXPERTS,) + w_exp_up.shape[2:]),
               w_exp_down=w_exp_down.reshape((depth * N_EXPERTS,) + w_exp_down.shape[2:]))

    memf = mem_prompt.reshape(bp * n_mem, d)
    w_kv = jnp.concatenate([w_xk, w_xv], axis=-1).astype(BF16)
    mk_p, mv_p = [], []
    for layer in range(depth):
        kv = rms_matmul(memf, norm_mem[layer], w_kv[layer], tn=d)
        mk_p.append(kv[:, :d])
        mv_p.append(kv[:, d:])
    p_cache_mem_k = jnp.stack(mk_p).reshape(depth, bp, n_mem, H_X, HD_X)
    p_cache_mem_v = jnp.stack(mv_p).reshape(depth, bp, n_mem, H_X, HD_X)

    n_odd = w_in_c.shape[0]
    z_gdn = jnp.zeros((n_even, bp, H_A, DK_A, DV_A), F32)
    z_conv = jnp.zeros((n_even, bp, CONV_A - 1, qkv_w), F32)
    z_sc = jnp.zeros((n_even, bp, CONV_B - 1, D_B), F32)
    z_ret = jnp.zeros((n_odd, bp, H_C, DK_C, DV_C), F32)
    pos_p = jnp.arange(lp, dtype=I32)
    pos_s = 16384 + jnp.arange(ls, dtype=I32)

    grp_p = dict(batch=bp, seqlen=lp, pos=pos_p, gdn=z_gdn, conv=z_conv, sc=z_sc, ret=z_ret,
                 mem_k=p_cache_mem_k, mem_v=p_cache_mem_v)
    grp_s = dict(batch=bs, seqlen=ls, pos=pos_s, gdn=state_gdn, conv=state_gdn_conv, sc=state_sconv,
                 ret=state_ret, mem_k=cache_mem_k, mem_v=cache_mem_v)
    (y_p, p_gdn, p_conv, p_sc, p_ret), (y_s, s_gdn, s_conv, s_sc, s_ret) = _forward(
        x_prompt.reshape(bp * lp, d), x_sample.reshape(bs * ls, d), grp_p, grp_s, wts)
    return (y_p.reshape(bp, lp, d), y_s.reshape(bs, ls, d), p_gdn, p_conv, p_sc, p_ret, p_cache_mem_k,
            p_cache_mem_v, s_gdn, s_conv, s_sc, s_ret)
```

```python
import functools
import math

import jax
import jax.numpy as jnp
import numpy as np
from jax import lax
from jax.experimental import pallas as pl
from jax.experimental.pallas import tpu as pltpu

F32 = jnp.float32
BF16 = jnp.bfloat16
I32 = jnp.int32

EPS = 1e-6
ROPE_BASE = 10000.0

D_MODEL = 1024
H_A, DK_A, DV_A, CONV_A = 4, 128, 128, 4
W_QKV_A = 3 * H_A * DK_A
D_B, CONV_B = D_MODEL // 2, 3
H_C, DK_C, DV_C = 4, 256, 512
H_X, HD_X, N_MEM = 4, 256, 256
N_GROUPS, E_PER_GROUP, N_EXPERTS, D_EXPERT = 4, 8, 32, 512
GDN_CHUNK = 64

LANES = 128
SUBLANES = 8
GDN_STACK = 256
VMEM_LIMIT = 56 * 1024 * 1024

PA_COLS = 3840
PA_SC = 2048
PA_BA = 3584
R_E0 = N_GROUPS
MOE_BLK = 256


def _cparams(sem):
    return pltpu.CompilerParams(dimension_semantics=sem, vmem_limit_bytes=VMEM_LIMIT)


def _dot(a, b, trans_a=False, trans_b=False):
    dn = (((0 if trans_a else 1,), (1 if trans_b else 0,)), ((), ()))
    return lax.dot_general(a.astype(BF16), b.astype(BF16), dn, preferred_element_type=F32)


def _silu(x):
    return x * (1.0 / (1.0 + jnp.exp(-x)))


def _sigmoid(x):
    return 1.0 / (1.0 + jnp.exp(-x))


def _rms_matmul_kernel(x_ref, g_ref, w_ref, o_ref, xn_ref):
    @pl.when(pl.program_id(1) == 0)
    def _():
        x = x_ref[...]
        ms = jnp.mean(x * x, axis=-1, keepdims=True)
        xn_ref[...] = (x * lax.rsqrt(ms + EPS) * g_ref[...]).astype(BF16)

    o_ref[...] = jnp.dot(xn_ref[...], w_ref[...], preferred_element_type=F32).astype(o_ref.dtype)


def rms_matmul(x, g, w, *, tn, rows=None, row0=0, out_dtype=F32, tm=1024):
    t, d = (rows or x.shape[0]), x.shape[1]
    n = w.shape[1]
    tm = min(tm, t)
    blk0 = row0 // tm
    return pl.pallas_call(
        _rms_matmul_kernel,
        grid=(t // tm, n // tn),
        in_specs=[pl.BlockSpec((tm, d), lambda i, j: (i + blk0, 0)),
                  pl.BlockSpec((1, d), lambda i, j: (0, 0)),
                  pl.BlockSpec((d, tn), lambda i, j: (0, j))],
        out_specs=pl.BlockSpec((tm, tn), lambda i, j: (i, j)),
        out_shape=jax.ShapeDtypeStruct((t, n), out_dtype),
        scratch_shapes=[pltpu.VMEM((tm, d), BF16)],
        compiler_params=_cparams(("parallel", "arbitrary")),
        name="rms_matmul",
    )(x, g.reshape(1, d), w)


def _matmul_res_kernel(a_ref, w_ref, r_ref, *rest):
    o_ref = rest[-1]
    o_ref[...] = r_ref[...] + jnp.dot(a_ref[...].astype(BF16), w_ref[...], preferred_element_type=F32)


def matmul_res(a, w, res, *, res_row0=0, into=None, out_row0=0, tm=512):
    t, k = a.shape
    n = w.shape[1]
    tm = min(tm, t)
    rblk, oblk = res_row0 // tm, out_row0 // tm
    in_specs = [pl.BlockSpec((tm, k), lambda i: (i, 0)),
                pl.BlockSpec((k, n), lambda i: (0, 0)),
                pl.BlockSpec((tm, n), lambda i: (i + rblk, 0))]
    args = (a, w, res)
    if into is not None:
        in_specs.append(pl.BlockSpec(memory_space=pl.ANY))
        args += (into,)
    return pl.pallas_call(
        _matmul_res_kernel,
        grid=(t // tm,),
        in_specs=in_specs,
        out_specs=pl.BlockSpec((tm, n), lambda i: (i + oblk, 0)),
        out_shape=jax.ShapeDtypeStruct((t, n) if into is None else into.shape, F32),
        input_output_aliases={} if into is None else {3: 0},
        compiler_params=_cparams(("parallel",)),
        name="matmul_res",
    )(*args)


def _causal_conv(x, hist, w_ref, width, seq8):
    r = x.shape[0]
    taps = [w_ref[j:j + 1, :] for j in range(width)]

    def head(x8, h8):
        n = x8.shape[0]
        t = lax.broadcasted_iota(I32, (n, 1), 0) % SUBLANES
        y = taps[width - 1] * x8
        for s in range(1, width):
            prev = pltpu.roll(h8, (n + s - (width - 1)) % n, 0) if s != width - 1 else h8
            y = y + taps[width - 1 - s] * jnp.where(t >= s, pltpu.roll(x8, s, 0), prev)
        return y

    if seq8:
        return head(x, hist)
    y = taps[width - 1] * x
    for s in range(1, width):
        y = y + taps[width - 1 - s] * pltpu.roll(x, s, 0)
    return jnp.concatenate([head(x[:SUBLANES], hist), y[SUBLANES:]], axis=0)


def _unit_lower_inverse(ms, c, ri, ci):
    base = min(c, 16)
    eye = jnp.where(ri == ci, 1.0, 0.0).astype(F32)
    blk = (ri // base) == (ci // base)
    ds = [jnp.where(blk, m, 0.0) for m in ms]
    ps = [eye - d for d in ds]
    k = 2
    while k < base:
        ds = [_dot(d, d) for d in ds]
        ps = [_dot(p, eye + d) for p, d in zip(ps, ds)]
        k *= 2
    s = base
    while s < c:
        sel = ((ri // (2 * s)) == (ci // (2 * s))) & ((ri // s) != (ci // s))
        ts = [_dot(jnp.where(sel, m, 0.0), p) for m, p in zip(ms, ps)]
        ps = [p - _dot(p, t) for p, t in zip(ps, ts)]
        s *= 2
    return ps


def _gdn_prepare(units, c):
    n = GDN_STACK
    ri = lax.broadcasted_iota(I32, (n, n), 0)
    ci = lax.broadcasted_iota(I32, (n, n), 1)
    same = (ri // c) == (ci // c)
    incl = same & (ri >= ci)
    strict = same & (ri > ci)
    pre = []
    for q, k, v, bfull, gfull in units:
        g2 = jnp.concatenate([gfull, gfull], axis=1)
        g_row = jnp.sum(jnp.where(ri == ci, g2, 0.0), axis=0, keepdims=True)
        gc_col = jnp.sum(jnp.where(incl, g_row, 0.0), axis=1, keepdims=True)
        gc_row = jnp.sum(jnp.where(same & (ri <= ci), g2, 0.0), axis=0, keepdims=True)
        gl_col = jnp.sum(jnp.where(same, g_row, 0.0), axis=1, keepdims=True)
        decay = jnp.where(incl, jnp.exp(jnp.where(incl, gc_col - gc_row, 0.0)), 0.0)
        egc = jnp.exp(gc_col)
        kb = k * bfull
        pre.append(dict(decay=decay, kb=kb, rhs=jnp.concatenate([v * bfull, kb * egc], axis=1),
                        qd=q * egc, kd=k * jnp.exp(gl_col - gc_col), egl=jnp.exp(gl_col)))
    mms = [jnp.where(strict, _dot(e["kb"], u[1], trans_b=True) * e["decay"], 0.0) for e, u in zip(pre, units)]
    qks = [_dot(u[0], u[1], trans_b=True) * e["decay"] for e, u in zip(pre, units)]
    tinvs = _unit_lower_inverse(mms, c, ri, ci)
    uws = [_dot(t, e["rhs"]) for t, e in zip(tinvs, pre)]
    return [dict(u=uw[:, :DV_A], w=uw[:, DV_A:], qk=qk, qd=e["qd"], kd=e["kd"], egl=e["egl"])
            for uw, qk, e in zip(uws, qks, pre)]


def _gdn_recur(e, states, c):
    nprob = GDN_STACK // c
    ws, qs = [], []
    for p in range(nprob):
        sl = slice(p * c, (p + 1) * c)
        ws.append(_dot(e["w"][sl], states[p]))
        qs.append(_dot(e["qd"][sl], states[p]))
    vn = e["u"] - jnp.concatenate(ws, axis=0)
    o = _dot(e["qk"], vn) + jnp.concatenate(qs, axis=0)
    new_states = []
    for p in range(nprob):
        sl = slice(p * c, (p + 1) * c)
        new_states.append(states[p] * e["egl"][p * c:p * c + 1, :] + _dot(e["kd"][sl], vn[sl], trans_a=True))
    return o, new_states


def _gdn_compute(x, z, u_sc, scb, ba, hq, hs, wq_ref, ws_ref, prm_ref, gn, states, *, seq8, c, nu):
    unit = GDN_CHUNK
    xc = _silu(_causal_conv(x, hq, wq_ref, CONV_A, seq8))
    yb = scb * _causal_conv(u_sc, hs, ws_ref, CONV_B, seq8)
    beta_all = _sigmoid(ba)
    sp = jnp.maximum(ba + prm_ref[1:2, :], 0.0) + jnp.log1p(jnp.exp(-jnp.abs(ba + prm_ref[1:2, :])))
    g_all = -jnp.exp(prm_ref[0:1, :]) * sp

    def head_cols(a, base):
        return a[:, base * DK_A:(base + 1) * DK_A]

    cat = lambda xs: jnp.concatenate(xs, axis=0)
    units = []
    for ui in range(nu):
        rs = slice(ui * unit, (ui + 1) * unit)
        qs, ks, vs, bs, gs = [], [], [], [], []
        for h in range(H_A):
            qh = head_cols(xc, h)[rs]
            kh = head_cols(xc, H_A + h)[rs]
            qs.append(qh * lax.rsqrt(jnp.sum(qh * qh, axis=-1, keepdims=True) + EPS) * (DK_A ** -0.5))
            ks.append(kh * lax.rsqrt(jnp.sum(kh * kh, axis=-1, keepdims=True) + EPS))
            vs.append(head_cols(xc, 2 * H_A + h)[rs])
            bs.append(jnp.broadcast_to(beta_all[rs, h:h + 1], (unit, LANES)))
            gs.append(jnp.broadcast_to(g_all[rs, H_A + h:H_A + h + 1], (unit, LANES)))
        units.append((cat(qs), cat(ks), cat(vs), cat(bs), cat(gs)))
    prepared = _gdn_prepare(units, c)

    outs = []
    for ui in range(nu):
        rs = slice(ui * unit, (ui + 1) * unit)
        o, states = _gdn_recur(prepared[ui], states, c)
        zst = cat([z[rs, h * DV_A:(h + 1) * DV_A] for h in range(H_A)])
        ms = jnp.mean(o * o, axis=-1, keepdims=True)
        og = o * lax.rsqrt(ms + EPS) * gn * _silu(zst)
        outs.append(jnp.concatenate([og[h * unit:(h + 1) * unit] for h in range(H_A)], axis=1))
    o_all = outs[0] if nu == 1 else cat(outs)
    return jnp.concatenate([o_all, yb], axis=1), states


def _gdn_kernel(qkv_ref, z_ref, sch_ref, scb_ref, scc_ref, ba_ref, hq_ref, hs_ref, wq_ref, ws_ref,
                prm_ref, gn_ref, s0_ref, o_ref, sn_ref, cq_ref, cs_ref):
    x = qkv_ref[...]
    rows = x.shape[0]
    u_sc = scc_ref[...] * sch_ref[...]
    cq_ref[...] = pltpu.roll(x, rows - SUBLANES + CONV_A - 1, 0)
    cs_ref[...] = pltpu.roll(u_sc, rows - SUBLANES + CONV_B - 1, 0)
    nprob = GDN_STACK // SUBLANES
    states = [s0_ref[p % SUBLANES, p // SUBLANES] for p in range(nprob)]
    mix, states = _gdn_compute(x, z_ref[...], u_sc, scb_ref[...], ba_ref[...], hq_ref[...], hs_ref[...],
                               wq_ref, ws_ref, prm_ref, gn_ref[...], states, seq8=True, c=SUBLANES, nu=1)
    o_ref[...] = mix.astype(o_ref.dtype)
    for p in range(nprob):
        sn_ref[p % SUBLANES, p // SUBLANES] = states[p]


def _gdn_block_kernel(x_ref, g_ref, wi_ref, wo_ref, hq_ref, hs_ref, wq_ref, ws_ref, prm_ref, gn_ref, s0_ref,
                      o_ref, sn_ref, cq_ref, cs_ref, s_scr, hq_scr, hs_scr, *, nu, nl):
    l = pl.program_id(1)

    @pl.when(l == 0)
    def _():
        hq_scr[...] = hq_ref[0]
        hs_scr[...] = hs_ref[0]
        s_scr[...] = s0_ref[0]

    xres = x_ref[...]
    rows = xres.shape[0]
    ms = jnp.mean(xres * xres, axis=-1, keepdims=True)
    xn = (xres * lax.rsqrt(ms + EPS) * g_ref[...]).astype(BF16)
    proj = lambda lo, width: jnp.dot(xn, wi_ref[:, lo:lo + width], preferred_element_type=F32)
    x = proj(0, W_QKV_A)
    z = proj(W_QKV_A, H_A * DV_A)
    u_sc = proj(PA_SC + 2 * D_B, D_B) * proj(PA_SC, D_B)
    scb = proj(PA_SC + D_B, D_B)
    ba = proj(PA_BA, LANES)
    hq = hq_scr[...]
    hs = hs_scr[...]
    hq_scr[...] = pltpu.roll(x[rows - SUBLANES:], CONV_A - 1, 0)
    hs_scr[...] = pltpu.roll(u_sc[rows - SUBLANES:], CONV_B - 1, 0)
    states = [s_scr[p] for p in range(H_A)]
    mix, states = _gdn_compute(x, z, u_sc, scb, ba, hq, hs, wq_ref, ws_ref, prm_ref, gn_ref[...], states,
                               seq8=False, c=GDN_CHUNK, nu=nu)
    for p in range(H_A):
        s_scr[p] = states[p]
    o_ref[...] = xres + jnp.dot(mix.astype(BF16), wo_ref[...], preferred_element_type=F32)

    @pl.when(l == nl - 1)
    def _():
        sn_ref[0] = s_scr[...]
        cq_ref[0] = hq_scr[...]
        cs_ref[0] = hs_scr[...]


def gdn_core(p, hist_q, hist_s, w_conv_qkv, w_conv_sc, prm, gn, s0, *, batch, seqlen):
    t = batch * seqlen
    assert seqlen == SUBLANES
    rows = GDN_CHUNK
    nb = rows // seqlen
    hq_spec = pl.BlockSpec((rows, W_QKV_A), lambda i: (i, 0))
    hs_spec = pl.BlockSpec((rows, D_B), lambda i: (i, 0))
    s_spec = pl.BlockSpec((nb, H_A, DK_A, DV_A), lambda i: (i, 0, 0, 0))
    col = lambda width, blk: pl.BlockSpec((rows, width), lambda i: (i, blk))
    const = lambda shape: pl.BlockSpec(shape, lambda i: (0,) * len(shape))
    return pl.pallas_call(
        _gdn_kernel,
        grid=(batch // nb,),
        in_specs=[col(W_QKV_A, 0), col(D_B, 3), col(D_B, 4), col(D_B, 5), col(D_B, 6),
                  col(LANES, PA_BA // LANES), hq_spec, hs_spec,
                  const((CONV_A, W_QKV_A)), const((CONV_B, D_B)), const((SUBLANES, LANES)),
                  const((1, DV_A)), s_spec],
        out_specs=[pl.BlockSpec((rows, D_MODEL), lambda i: (i, 0)), s_spec, hq_spec, hs_spec],
        out_shape=[jax.ShapeDtypeStruct((t, D_MODEL), BF16),
                   jax.ShapeDtypeStruct((batch, H_A, DK_A, DV_A), F32),
                   jax.ShapeDtypeStruct((t, W_QKV_A), F32),
                   jax.ShapeDtypeStruct((t, D_B), F32)],
        compiler_params=_cparams(("parallel",)),
        name="gdn_core",
    )(p, p, p, p, p, p, hist_q.reshape(t, W_QKV_A), hist_s.reshape(t, D_B), w_conv_qkv, w_conv_sc, prm,
      gn.reshape(1, DV_A), s0)


def _resident(shape):
    return pl.BlockSpec(shape, lambda *ix: (0,) * len(shape), pipeline_mode=pl.Buffered(1))


def gdn_block(x, g, w_in, w_out, hist_q, hist_s, w_conv_qkv, w_conv_sc, prm, gn, s0, *, batch, seqlen):
    t, d = batch * seqlen, x.shape[1]
    rows = min(seqlen, 256)
    nu = rows // GDN_CHUNK
    nl = seqlen // rows
    per_b = lambda shape: pl.BlockSpec((1,) + shape, lambda b, l: (b,) + (0,) * len(shape))
    row_spec = pl.BlockSpec((rows, d), lambda b, l: (b * nl + l, 0))
    kern = functools.partial(_gdn_block_kernel, nu=nu, nl=nl)
    return pl.pallas_call(
        kern,
        grid=(batch, nl),
        in_specs=[row_spec, _resident((1, d)), _resident(w_in.shape), _resident(w_out.shape),
                  per_b((SUBLANES, W_QKV_A)), per_b((SUBLANES, D_B)),
                  _resident((CONV_A, W_QKV_A)), _resident((CONV_B, D_B)), _resident((SUBLANES, LANES)),
                  _resident((1, DV_A)), per_b((H_A, DK_A, DV_A))],
        out_specs=[row_spec, per_b((H_A, DK_A, DV_A)), per_b((SUBLANES, W_QKV_A)), per_b((SUBLANES, D_B))],
        out_shape=[jax.ShapeDtypeStruct((t, d), F32),
                   jax.ShapeDtypeStruct((batch, H_A, DK_A, DV_A), F32),
                   jax.ShapeDtypeStruct((batch, SUBLANES, W_QKV_A), F32),
                   jax.ShapeDtypeStruct((batch, SUBLANES, D_B), F32)],
        scratch_shapes=[pltpu.VMEM((H_A, DK_A, DV_A), F32),
                        pltpu.VMEM((SUBLANES, W_QKV_A), F32),
                        pltpu.VMEM((SUBLANES, D_B), F32)],
        compiler_params=_cparams(("parallel", "arbitrary")),
        name="gdn_block",
    )(x, g.reshape(1, d), w_in, w_out, hist_q, hist_s, w_conv_qkv, w_conv_sc, prm, gn.reshape(1, DV_A), s0)


def _ret_compute(get_q, get_k, get_v, get_gate, cos, sin, dm_ref, qd_ref, kd_ref, cd_ref, rn_ref, r_scr,
                 *, nseq, c):
    half = DK_C // 2

    def rot(x):
        x1, x2 = x[:, :half], x[:, half:]
        return jnp.concatenate([x1 * cos - x2 * sin, x1 * sin + x2 * cos], axis=1)

    heads = range(H_C)
    qs = [rot(get_q(h)) for h in heads]
    ks = [rot(get_k(h)) * (DK_C ** -0.5) for h in heads]
    vs = [get_v(h).astype(BF16) for h in heads]
    ss = [_dot(qs[h], ks[h], trans_b=True) * dm_ref[h] for h in heads]
    inters = []
    for h in heads:
        qdh = qs[h] * qd_ref[h]
        parts = [_dot(qdh[sq * c:(sq + 1) * c], r_scr[sq, h]) for sq in range(nseq)]
        inters.append(parts[0] if nseq == 1 else jnp.concatenate(parts, axis=0))
    outs = [_dot(ss[h], vs[h]) + inters[h] for h in heads]
    for h in heads:
        kdh = ks[h] * kd_ref[h]
        cd = cd_ref[h][0:1, 0:1]
        for sq in range(nseq):
            sl = slice(sq * c, (sq + 1) * c)
            r_scr[sq, h] = r_scr[sq, h] * cd + _dot(kdh[sl], vs[h][sl], trans_a=True)
    gated = []
    for h in heads:
        o = outs[h]
        ms = jnp.mean(o * o, axis=-1, keepdims=True)
        on = o * lax.rsqrt(ms + EPS) * rn_ref[:, h * DV_C:(h + 1) * DV_C]
        gated.append(_silu(get_gate(h)) * on)
    return gated


def _ret_kernel(q_ref, k_ref, v_ref, gate_ref, cos_ref, sin_ref, dm_ref, qd_ref, kd_ref, cd_ref,
                rn_ref, r0_ref, o_ref, rnew_ref, r_scr, *, nseq, c, nl):
    l = pl.program_id(1)

    @pl.when(l == 0)
    def _():
        r_scr[...] = r0_ref[...]

    gated = _ret_compute(lambda h: q_ref[:, h * DK_C:(h + 1) * DK_C], lambda h: k_ref[:, h * DK_C:(h + 1) * DK_C],
                         lambda h: v_ref[:, h * DV_C:(h + 1) * DV_C], lambda h: gate_ref[:, h * DV_C:(h + 1) * DV_C],
                         cos_ref[...], sin_ref[...], dm_ref, qd_ref, kd_ref, cd_ref, rn_ref, r_scr, nseq=nseq, c=c)
    for h in range(H_C):
        o_ref[:, h * DV_C:(h + 1) * DV_C] = gated[h].astype(o_ref.dtype)

    @pl.when(l == nl - 1)
    def _():
        rnew_ref[...] = r_scr[...]


def _ret_block_kernel(x_ref, g_ref, wi_ref, wo_ref, cos_ref, sin_ref, dm_ref, qd_ref, kd_ref, cd_ref,
                      rn_ref, r0_ref, o_ref, rnew_ref, r_scr, *, c, nl):
    l = pl.program_id(1)

    @pl.when(l == 0)
    def _():
        r_scr[...] = r0_ref[...]

    xres = x_ref[...]
    ms = jnp.mean(xres * xres, axis=-1, keepdims=True)
    xn = (xres * lax.rsqrt(ms + EPS) * g_ref[...]).astype(BF16)
    proj = lambda lo, width: jnp.dot(xn, wi_ref[:, lo:lo + width], preferred_element_type=F32)
    hk, hv = H_C * DK_C, H_C * DV_C
    gated = _ret_compute(lambda h: proj(h * DK_C, DK_C), lambda h: proj(hk + h * DK_C, DK_C),
                         lambda h: proj(2 * hk + h * DV_C, DV_C), lambda h: proj(2 * hk + hv + h * DV_C, DV_C),
                         cos_ref[...], sin_ref[...], dm_ref, qd_ref, kd_ref, cd_ref, rn_ref, r_scr, nseq=1, c=c)
    y = xres
    for h in range(H_C):
        y = y + jnp.dot(gated[h].astype(BF16), wo_ref[h * DV_C:(h + 1) * DV_C, :], preferred_element_type=F32)
    o_ref[...] = y

    @pl.when(l == nl - 1)
    def _():
        rnew_ref[...] = r_scr[...]


def ret_core(p, pos, ret_norm, r0, *, batch, seqlen):
    t = batch * seqlen
    assert seqlen == SUBLANES
    nseq, c = 2, seqlen
    rows = nseq * c
    cos, sin, dmat, qd, kd, cd = _ret_tables(pos, nseq, c)
    const = lambda shape: pl.BlockSpec(shape, lambda b, l: (0,) * len(shape))
    kern = functools.partial(_ret_kernel, nseq=nseq, c=c, nl=1)
    hk = H_C * DK_C
    hv = H_C * DV_C
    return pl.pallas_call(
        kern,
        grid=(batch // nseq, 1),
        in_specs=[pl.BlockSpec((rows, hk), lambda b, l: (b, 0)),
                  pl.BlockSpec((rows, hk), lambda b, l: (b, 1)),
                  pl.BlockSpec((rows, hv), lambda b, l: (b, 1)),
                  pl.BlockSpec((rows, hv), lambda b, l: (b, 2)),
                  const((rows, DK_C // 2)), const((rows, DK_C // 2)),
                  const((H_C, rows, rows)), const((H_C, rows, DK_C)), const((H_C, rows, DK_C)),
                  const((H_C, SUBLANES, LANES)), const((1, hv)),
                  pl.BlockSpec((nseq, H_C, DK_C, DV_C), lambda b, l: (b, 0, 0, 0))],
        out_specs=[pl.BlockSpec((rows, hv), lambda b, l: (b, 0)),
                   pl.BlockSpec((nseq, H_C, DK_C, DV_C), lambda b, l: (b, 0, 0, 0))],
        out_shape=[jax.ShapeDtypeStruct((t, hv), BF16),
                   jax.ShapeDtypeStruct((batch, H_C, DK_C, DV_C), F32)],
        scratch_shapes=[pltpu.VMEM((nseq, H_C, DK_C, DV_C), F32)],
        compiler_params=_cparams(("parallel", "arbitrary")),
        name="ret_core",
    )(p, p, p, p, cos, sin, dmat, qd, kd, cd, ret_norm.reshape(1, hv), r0)


def _ret_tables(pos, nseq, c):
    half = DK_C // 2
    inv = ROPE_BASE ** (-jnp.arange(half, dtype=F32) / half)
    ang = pos.astype(F32)[:, None] * inv[None, :]
    cos, sin = jnp.cos(ang), jnp.sin(ang)
    if nseq > 1:
        cos, sin = jnp.tile(cos, (nseq, 1)), jnp.tile(sin, (nseq, 1))
    lg = jnp.log(1.0 - 2.0 ** (-5.0 - jnp.arange(H_C, dtype=F32)))[:, None]
    i = jnp.arange(c, dtype=F32)
    incl = i[:, None] >= i[None, :]
    dmat = jnp.exp(jnp.where(incl[None], (i[:, None] - i[None, :])[None] * lg[..., None], -jnp.inf))
    if nseq > 1:
        dmat = jnp.kron(jnp.eye(nseq, dtype=F32)[None], dmat)
    qd = jnp.tile(jnp.exp((i + 1.0)[None] * lg), (1, nseq))[..., None] * jnp.ones((1, 1, DK_C), F32)
    kd = jnp.tile(jnp.exp((c - 1.0 - i)[None] * lg), (1, nseq))[..., None] * jnp.ones((1, 1, DK_C), F32)
    cd = jnp.exp(c * lg)[..., None] * jnp.ones((1, SUBLANES, LANES), F32)
    return cos, sin, dmat, qd, kd, cd


def ret_block(x, g, w_in, w_out, pos, ret_norm, r0, *, batch, seqlen):
    t, d = batch * seqlen, x.shape[1]
    c = min(seqlen, 256)
    nl = seqlen // c
    cos, sin, dmat, qd, kd, cd = _ret_tables(pos, 1, c)
    hv = H_C * DV_C
    row_spec = pl.BlockSpec((c, d), lambda b, l: (b * nl + l, 0))
    trig_spec = pl.BlockSpec((c, DK_C // 2), lambda b, l: (l, 0))
    state_spec = pl.BlockSpec((1, H_C, DK_C, DV_C), lambda b, l: (b, 0, 0, 0))
    kern = functools.partial(_ret_block_kernel, c=c, nl=nl)
    return pl.pallas_call(
        kern,
        grid=(batch, nl),
        in_specs=[row_spec, _resident((1, d)), _resident(w_in.shape), _resident(w_out.shape),
                  trig_spec, trig_spec,
                  _resident((H_C, c, c)), _resident((H_C, c, DK_C)), _resident((H_C, c, DK_C)),
                  _resident((H_C, SUBLANES, LANES)), _resident((1, hv)), state_spec],
        out_specs=[row_spec, state_spec],
        out_shape=[jax.ShapeDtypeStruct((t, d), F32),
                   jax.ShapeDtypeStruct((batch, H_C, DK_C, DV_C), F32)],
        scratch_shapes=[pltpu.VMEM((1, H_C, DK_C, DV_C), F32)],
        compiler_params=_cparams(("parallel", "arbitrary")),
        name="ret_block",
    )(x, g.reshape(1, d), w_in, w_out, cos, sin, dmat, qd, kd, cd, ret_norm.reshape(1, hv), r0)


def _xattn_fetch(mk_hbm, mv_hbm, kbuf, vbuf, sem, *, layer, nb, nsteps):
    i = pl.program_id(0)
    l = pl.program_id(1)
    slot = i % 2

    def copies(step, slot_):
        out = []
        for b in range(nb):
            for h in range(H_X):
                out.append(pltpu.make_async_copy(mk_hbm.at[layer, step * nb + b, :, h, :],
                                                 kbuf.at[slot_, b, h], sem.at[slot_]))
                out.append(pltpu.make_async_copy(mv_hbm.at[layer, step * nb + b, :, h, :],
                                                 vbuf.at[slot_, b, h], sem.at[slot_]))
        return out

    @pl.when(l == 0)
    def _():
        @pl.when(i == 0)
        def _():
            for c in copies(i, slot):
                c.start()

        @pl.when(i + 1 < nsteps)
        def _():
            for c in copies(i + 1, 1 - slot):
                c.start()

        for c in copies(i, slot):
            c.wait()

    return slot


def _xattn_heads(q_of, kbuf, vbuf, slot, probs):
    ss = [_dot(q_of(b, h), kbuf[slot, b, h], trans_b=True) * (HD_X ** -0.5) for b, h in probs]
    ps = []
    for s in ss:
        e = jnp.exp(s - jnp.max(s, axis=-1, keepdims=True))
        ps.append(e / jnp.sum(e, axis=-1, keepdims=True))
    return [_dot(p, vbuf[slot, b, h]) for p, (b, h) in zip(ps, probs)]


def _xattn_kernel(q_ref, mk_hbm, mv_hbm, o_ref, kbuf, vbuf, sem, *, layer, nb, lq, nsteps):
    slot = _xattn_fetch(mk_hbm, mv_hbm, kbuf, vbuf, sem, layer=layer, nb=nb, nsteps=nsteps)
    probs = [(b, h) for b in range(nb) for h in range(H_X)]
    win = lambda b, h: (slice(b * lq, (b + 1) * lq), slice(h * HD_X, (h + 1) * HD_X))
    outs = _xattn_heads(lambda b, h: q_ref[win(b, h)], kbuf, vbuf, slot, probs)
    for o, (b, h) in zip(outs, probs):
        o_ref[win(b, h)] = o.astype(o_ref.dtype)


def _xattn_block_kernel(x_ref, g_ref, wq_ref, wo_ref, mk_hbm, mv_hbm, into_hbm, o_ref, kbuf, vbuf, sem,
                        *, layer, nsteps):
    del into_hbm
    slot = _xattn_fetch(mk_hbm, mv_hbm, kbuf, vbuf, sem, layer=layer, nb=1, nsteps=nsteps)
    xres = x_ref[...]
    ms = jnp.mean(xres * xres, axis=-1, keepdims=True)
    xn = (xres * lax.rsqrt(ms + EPS) * g_ref[...]).astype(BF16)
    probs = [(0, h) for h in range(H_X)]
    qs = [jnp.dot(xn, wq_ref[:, h * HD_X:(h + 1) * HD_X], preferred_element_type=F32) for h in range(H_X)]
    outs = _xattn_heads(lambda b, h: qs[h], kbuf, vbuf, slot, probs)
    y = xres
    for h in range(H_X):
        y = y + jnp.dot(outs[h].astype(BF16), wo_ref[h * HD_X:(h + 1) * HD_X, :], preferred_element_type=F32)
    o_ref[...] = y


def xattn_core(q, mk, mv, layer, *, batch, seqlen):
    t = batch * seqlen
    d = H_X * HD_X
    assert seqlen == SUBLANES
    nb, lq = 2, seqlen
    rows = nb * lq
    nsteps = batch // nb
    kern = functools.partial(_xattn_kernel, layer=layer, nb=nb, lq=lq, nsteps=nsteps)
    return pl.pallas_call(
        kern,
        grid=(nsteps, 1),
        in_specs=[pl.BlockSpec((rows, d), lambda b, l: (b, 0)),
                  pl.BlockSpec(memory_space=pl.ANY), pl.BlockSpec(memory_space=pl.ANY)],
        out_specs=pl.BlockSpec((rows, d), lambda b, l: (b, 0)),
        out_shape=jax.ShapeDtypeStruct((t, d), q.dtype),
        scratch_shapes=[pltpu.VMEM((2, nb, H_X, N_MEM, HD_X), F32),
                        pltpu.VMEM((2, nb, H_X, N_MEM, HD_X), F32),
                        pltpu.SemaphoreType.DMA((2,))],
        compiler_params=_cparams(("arbitrary", "arbitrary")),
        name="xattn_core",
    )(q, mk, mv)


def xattn_block(x, g, w_q, w_o, mk, mv, layer, into, *, batch, seqlen):
    d = x.shape[1]
    lq = min(seqlen, 512)
    nl = seqlen // lq
    row_spec = pl.BlockSpec((lq, d), lambda b, l: (b * nl + l, 0))
    kern = functools.partial(_xattn_block_kernel, layer=layer, nsteps=batch)
    return pl.pallas_call(
        kern,
        grid=(batch, nl),
        in_specs=[row_spec, _resident((1, d)), _resident(w_q.shape), _resident(w_o.shape),
                  pl.BlockSpec(memory_space=pl.ANY), pl.BlockSpec(memory_space=pl.ANY),
                  pl.BlockSpec(memory_space=pl.ANY)],
        out_specs=row_spec,
        out_shape=jax.ShapeDtypeStruct(into.shape, F32),
        scratch_shapes=[pltpu.VMEM((2, 1, H_X, N_MEM, HD_X), F32),
                        pltpu.VMEM((2, 1, H_X, N_MEM, HD_X), F32),
                        pltpu.SemaphoreType.DMA((2,))],
        input_output_aliases={6: 0},
        compiler_params=_cparams(("arbitrary", "arbitrary")),
        name="xattn_block",
    )(x, g.reshape(1, d), w_q, w_o, mk, mv, into)


def _route_kernel(x_ref, g_ref, wr_ref, br_ref, h_ref, meta_ref, wts_ref, cnt_ref, cnt_scr):
    i = pl.program_id(0)

    @pl.when(i == 0)
    def _():
        cnt_scr[...] = jnp.zeros_like(cnt_scr)

    x = x_ref[...]
    tm = x.shape[0]
    ms = jnp.mean(x * x, axis=-1, keepdims=True)
    h = x * lax.rsqrt(ms + EPS) * g_ref[...]
    hb = h.astype(BF16)
    bits = pltpu.bitcast(hb.astype(F32), jnp.uint32)
    half_d = bits.shape[1] // 2
    h_ref[...] = bits[:, :half_d] | (bits[:, half_d:] >> 16)
    logits = jnp.dot(hb, wr_ref[...], preferred_element_type=F32) + br_ref[...]
    lane_i = lax.broadcasted_iota(I32, (tm, LANES), 1)
    lane = lane_i.astype(F32)
    neg = jnp.float32(-3.0e38)
    big = jnp.float32(LANES)
    is_g = lane_i < N_GROUPS
    gl = jnp.where(is_g, logits, neg)
    gmax = jnp.max(gl, axis=1, keepdims=True)
    grp = jnp.min(jnp.where(gl == gmax, lane, big), axis=1, keepdims=True)
    gsum = jnp.sum(jnp.where(is_g, jnp.exp(jnp.where(is_g, logits - gmax, 0.0)), 0.0), axis=1, keepdims=True)
    p_grp = 1.0 / gsum
    in_grp = ((lane_i >= R_E0) & (lane_i < R_E0 + N_EXPERTS)
              & (jnp.floor((lane - R_E0) * (1.0 / E_PER_GROUP)) == grp))
    el = jnp.where(in_grp, logits, neg)
    m1 = jnp.max(el, axis=1, keepdims=True)
    i1 = jnp.min(jnp.where(el == m1, lane, big), axis=1, keepdims=True)
    el2 = jnp.where(lane == i1, neg, el)
    m2 = jnp.max(el2, axis=1, keepdims=True)
    i2 = jnp.min(jnp.where(el2 == m2, lane, big), axis=1, keepdims=True)
    esum = jnp.sum(jnp.where(in_grp, jnp.exp(jnp.where(in_grp, logits - m1, 0.0)), 0.0), axis=1, keepdims=True)
    p1 = 1.0 / esum
    p2 = jnp.exp(m2 - m1) / esum
    tot = p1 + p2
    w1 = p_grp * (p1 / tot)
    w2 = p_grp * (p2 / tot)
    wts_ref[...] = jnp.where(lane_i == 0, w1, jnp.where(lane_i == 1, w2, 0.0))

    oh1 = (lane == i1).astype(F32)
    oh2 = (lane == i2).astype(F32)
    rr = lax.broadcasted_iota(I32, (tm, tm), 0)
    cc = lax.broadcasted_iota(I32, (tm, tm), 1)
    tri = (rr > cc).astype(BF16)
    base = cnt_scr[0:1, :]
    c1 = jnp.sum(oh1, axis=0, keepdims=True)
    c2 = jnp.sum(oh2, axis=0, keepdims=True)
    r1 = jnp.sum(oh1 * (jnp.dot(tri, oh1.astype(BF16), preferred_element_type=F32) + base), axis=1, keepdims=True)
    r2 = jnp.sum(oh2 * (jnp.dot(tri, oh2.astype(BF16), preferred_element_type=F32) + base + c1), axis=1, keepdims=True)
    new_cnt = base + c1 + c2
    cnt_scr[...] = jnp.broadcast_to(new_cnt, cnt_scr.shape)
    cnt_ref[...] = jnp.broadcast_to(new_cnt, cnt_ref.shape)
    meta = jnp.where(lane_i == 0, i1, jnp.where(lane_i == 1, i2, 0.0))
    meta = jnp.where(lane_i == 2, r1, jnp.where(lane_i == 3, r2, meta))
    meta_ref[...] = meta.astype(I32)


def moe_route(x, g, wr, br, *, tm=256):
    t, d = x.shape
    tm = min(tm, t)
    return pl.pallas_call(
        _route_kernel,
        grid=(t // tm,),
        in_specs=[pl.BlockSpec((tm, d), lambda i: (i, 0)),
                  pl.BlockSpec((1, d), lambda i: (0, 0)),
                  pl.BlockSpec((d, LANES), lambda i: (0, 0)),
                  pl.BlockSpec((1, LANES), lambda i: (0, 0))],
        out_specs=[pl.BlockSpec((tm, d // 2), lambda i: (i, 0)),
                   pl.BlockSpec((tm, LANES), lambda i: (i, 0)),
                   pl.BlockSpec((tm, LANES), lambda i: (i, 0)),
                   pl.BlockSpec((SUBLANES, LANES), lambda i: (0, 0))],
        out_shape=[jax.ShapeDtypeStruct((t, d // 2), jnp.uint32),
                   jax.ShapeDtypeStruct((t, LANES), I32),
                   jax.ShapeDtypeStruct((t, LANES), F32),
                   jax.ShapeDtypeStruct((SUBLANES, LANES), F32)],
        scratch_shapes=[pltpu.VMEM((SUBLANES, LANES), F32)],
        compiler_params=_cparams(("arbitrary",)),
        name="moe_route",
    )(x, g.reshape(1, d), wr, br)


def _plan_kernel(cnt_ref, meta_ref, dest_ref, ex_ref):
    cnt = cnt_ref[...]
    lane8 = lax.broadcasted_iota(I32, (SUBLANES, LANES), 1)
    padded = jnp.ceil(cnt / MOE_BLK) * MOE_BLK
    pend = padded
    s = 1
    while s < LANES:
        pend = pend + jnp.where(lane8 >= s, pltpu.roll(pend, s, 1), 0.0)
        s *= 2
    pstart = (pend - padded)[0:1, :]
    meta = meta_ref[...].astype(F32)
    tm = meta.shape[0]
    lane_i = lax.broadcasted_iota(I32, (tm, LANES), 1)
    lane = lane_i.astype(F32)
    col = lambda j: jnp.sum(jnp.where(lane_i == j, meta, 0.0), axis=1, keepdims=True)
    e1, e2, r1, r2 = col(0), col(1), col(2), col(3)
    d1 = jnp.sum(jnp.where(lane == e1, pstart, 0.0), axis=1, keepdims=True) + r1
    d2 = jnp.sum(jnp.where(lane == e2, pstart, 0.0), axis=1, keepdims=True) + r2
    dest_ref[...] = jnp.where(lane_i == 0, d1, jnp.where(lane_i == 1, d2, 0.0)).astype(I32)
    sub8 = lax.broadcasted_iota(I32, (SUBLANES, LANES), 0)
    ex_ref[...] = jnp.where(sub8 == 0, pend - padded, padded / MOE_BLK).astype(I32)


def moe_plan(cnt, meta, *, tm=512):
    t = meta.shape[0]
    tm = min(tm, t)
    return pl.pallas_call(
        _plan_kernel,
        grid=(t // tm,),
        in_specs=[pl.BlockSpec((SUBLANES, LANES), lambda i: (0, 0)),
                  pl.BlockSpec((tm, LANES), lambda i: (i, 0))],
        out_specs=[pl.BlockSpec((tm, LANES), lambda i: (i, 0)),
                   pl.BlockSpec((SUBLANES, LANES), lambda i: (0, 0))],
        out_shape=[jax.ShapeDtypeStruct((t, LANES), I32),
                   jax.ShapeDtypeStruct((SUBLANES, LANES), I32)],
        compiler_params=_cparams(("arbitrary",)),
        name="moe_plan",
    )(cnt, meta)


DMA_UNROLL = 8
DISPATCH_SLOTS = 3


def _dispatch_kernel(dest_ref, h_hbm, xd_in, xd_hbm, hbuf, lsem, ssem, *, tm, nsteps):
    del xd_in
    i = pl.program_id(0)
    slot = i % DISPATCH_SLOTS

    def load(step, slot_):
        return pltpu.make_async_copy(h_hbm.at[pl.ds(step * tm, tm)], hbuf.at[slot_], lsem.at[slot_])

    def rows_done(slot_):
        return pltpu.make_async_copy(hbuf.at[slot_], xd_hbm.at[pl.ds(0, tm)], ssem.at[slot_])

    @pl.when(i == 0)
    def _():
        load(0, 0).start()
        if nsteps > 1:
            load(1, 1).start()

    load(i, slot).wait()

    def start(t, carry):
        for k in range(2):
            pltpu.make_async_copy(hbuf.at[slot, pl.ds(t, 1)], xd_hbm.at[pl.ds(dest_ref[2 * t + k], 1)],
                                  ssem.at[slot]).start(priority=k)
        return carry

    lax.fori_loop(0, tm, start, 0, unroll=DMA_UNROLL)

    @pl.when(i >= 1)
    def _():
        prev = (i + DISPATCH_SLOTS - 1) % DISPATCH_SLOTS
        rows_done(prev).wait()
        rows_done(prev).wait()

    @pl.when(i + 2 < nsteps)
    def _():
        load(i + 2, (i + 2) % DISPATCH_SLOTS).start()

    @pl.when(i == nsteps - 1)
    def _():
        rows_done(slot).wait()
        rows_done(slot).wait()


def moe_dispatch(dest_flat, h, xd_zero, *, tm=512):
    t, d = h.shape
    tm = min(tm, t)
    nsteps = t // tm
    kern = functools.partial(_dispatch_kernel, tm=tm, nsteps=nsteps)
    return pl.pallas_call(
        kern,
        grid=(nsteps,),
        in_specs=[pl.BlockSpec((2 * tm,), lambda i: (i,), memory_space=pltpu.SMEM),
                  pl.BlockSpec(memory_space=pl.ANY),
                  pl.BlockSpec(memory_space=pl.ANY)],
        out_specs=pl.BlockSpec(memory_space=pl.ANY),
        out_shape=jax.ShapeDtypeStruct(xd_zero.shape, h.dtype),
        scratch_shapes=[pltpu.VMEM((DISPATCH_SLOTS, tm, d), h.dtype),
                        pltpu.SemaphoreType.DMA((DISPATCH_SLOTS,)),
                        pltpu.SemaphoreType.DMA((DISPATCH_SLOTS,))],
        input_output_aliases={2: 0},
        compiler_params=_cparams(("arbitrary",)),
        name="moe_dispatch",
    )(dest_flat, h, xd_zero)


def _experts_kernel(row0_ref, nblk_ref, wg_ref, wu_ref, wd_ref, xd_hbm, yd_hbm,
                    wg_s, wu_s, wd_s, xbuf, ybuf, xsem, ysem, *, nblk_total):
    e = pl.program_id(0)
    nb = nblk_ref[e]
    row0 = row0_ref[e]

    def rows(j):
        return pl.ds(pl.multiple_of(row0 + j * MOE_BLK, MOE_BLK), MOE_BLK)

    def x_copy(j, slot):
        return pltpu.make_async_copy(xd_hbm.at[rows(j)], xbuf.at[slot], xsem.at[slot])

    def y_copy(j, slot):
        return pltpu.make_async_copy(ybuf.at[slot], yd_hbm.at[rows(j)], ysem.at[slot])

    def unpack(w):
        hi = pltpu.bitcast(w & jnp.uint32(0xFFFF0000), F32)
        lo = pltpu.bitcast(w << 16, F32)
        return jnp.concatenate([hi, lo], axis=1).astype(BF16)

    @pl.when(nb > 0)
    def _():
        x_copy(0, 0).start()
        wg_s[...] = wg_ref[0].astype(BF16)
        wu_s[...] = wu_ref[0].astype(BF16)
        wd_s[...] = wd_ref[0].astype(BF16)

    def block(j, carry):
        slot = j % 2

        @pl.when(j + 1 < nb)
        def _():
            x_copy(j + 1, 1 - slot).start()

        x_copy(j, slot).wait()

        @pl.when(j >= 2)
        def _():
            y_copy(j - 2, slot).wait()

        half = MOE_BLK // 2
        xs = [unpack(xbuf[slot, r * half:(r + 1) * half, :]) for r in range(2)]
        gs = [jnp.dot(x, wg_s[...], preferred_element_type=F32) for x in xs]
        us = [jnp.dot(x, wu_s[...], preferred_element_type=F32) for x in xs]
        acts = [(_silu(g) * u).astype(BF16) for g, u in zip(gs, us)]
        for r in range(2):
            ybuf[slot, r * half:(r + 1) * half, :] = jnp.dot(acts[r], wd_s[...], preferred_element_type=F32)
        y_copy(j, slot).start()
        return carry

    lax.fori_loop(0, nb, block, 0)

    @pl.when(nb >= 2)
    def _():
        y_copy(nb - 2, nb % 2).wait()

    @pl.when(nb >= 1)
    def _():
        y_copy(nb - 1, (nb - 1) % 2).wait()

    @pl.when(e == N_EXPERTS - 1)
    def _():
        ntail = nblk_total - (row0 // MOE_BLK + nb)
        ybuf[0] = jnp.zeros(ybuf.shape[1:], F32)
        lax.fori_loop(0, ntail, lambda j, c: (y_copy(nb + j, 0).start(), c)[1], 0)
        lax.fori_loop(0, ntail, lambda j, c: (y_copy(nb + j, 0).wait(), c)[1], 0)


def moe_experts(row0, nblk, xd, w_gate, w_up, w_down, e0):
    d, f = w_gate.shape[1], w_gate.shape[2]
    kern = functools.partial(_experts_kernel, nblk_total=xd.shape[0] // MOE_BLK)
    grid_spec = pltpu.PrefetchScalarGridSpec(
        num_scalar_prefetch=2,
        grid=(N_EXPERTS,),
        in_specs=[pl.BlockSpec((1, d, f), lambda e, r, n: (e0 + e, 0, 0)),
                  pl.BlockSpec((1, d, f), lambda e, r, n: (e0 + e, 0, 0)),
                  pl.BlockSpec((1, f, d), lambda e, r, n: (e0 + e, 0, 0)),
                  pl.BlockSpec(memory_space=pl.ANY)],
        out_specs=pl.BlockSpec(memory_space=pl.ANY),
        scratch_shapes=[pltpu.VMEM((d, f), BF16), pltpu.VMEM((d, f), BF16), pltpu.VMEM((f, d), BF16),
                        pltpu.VMEM((2, MOE_BLK, d // 2), xd.dtype), pltpu.VMEM((2, MOE_BLK, d), F32),
                        pltpu.SemaphoreType.DMA((2,)), pltpu.SemaphoreType.DMA((2,))],
    )
    return pl.pallas_call(
        kern,
        grid_spec=grid_spec,
        out_shape=jax.ShapeDtypeStruct((xd.shape[0], d), F32),
        compiler_params=_cparams(("arbitrary",)),
        name="moe_experts",
    )(row0, nblk, w_gate, w_up, w_down, xd)


def _combine_kernel(dest_ref, dest_next_ref, x_ref, wts_ref, gf_ref, yd_hbm, o_ref, rbuf, sem,
                    *, tm, nsteps, final_norm):
    i = pl.program_id(0)
    slot = i % 2

    def gather(dref, slot_):
        def start(t, carry):
            for k in range(2):
                pltpu.make_async_copy(yd_hbm.at[pl.ds(dref[2 * t + k], 1)], rbuf.at[slot_, k, pl.ds(t, 1)],
                                      sem.at[slot_]).start(priority=k)
            return carry
        lax.fori_loop(0, tm, start, 0, unroll=DMA_UNROLL)

    @pl.when(i == 0)
    def _():
        gather(dest_ref, 0)

    @pl.when(i + 1 < nsteps)
    def _():
        gather(dest_next_ref, 1 - slot)

    for k in range(2):
        pltpu.make_async_copy(yd_hbm.at[pl.ds(0, tm)], rbuf.at[slot, k], sem.at[slot]).wait()
    w = wts_ref[...]
    y = x_ref[...] + rbuf[slot, 0] * w[:, 0:1] + rbuf[slot, 1] * w[:, 1:2]
    if final_norm:
        y = y * lax.rsqrt(jnp.mean(y * y, axis=-1, keepdims=True) + EPS) * gf_ref[...]
    o_ref[...] = y


def moe_combine(dest_flat, x, wts, yd, g_final, *, row0=0, rows=None, tm=512):
    t, d = (rows or x.shape[0]), x.shape[1]
    tm = min(tm, t)
    nsteps = t // tm
    blk0 = row0 // tm
    final_norm = g_final is not None
    gf = g_final.reshape(1, d) if final_norm else jnp.ones((1, d), F32)
    kern = functools.partial(_combine_kernel, tm=tm, nsteps=nsteps, final_norm=final_norm)
    return pl.pallas_call(
        kern,
        grid=(nsteps,),
        in_specs=[pl.BlockSpec((2 * tm,), lambda i: (i + blk0,), memory_space=pltpu.SMEM),
                  pl.BlockSpec((2 * tm,), lambda i: (jnp.minimum(i + 1, nsteps - 1) + blk0,),
                               memory_space=pltpu.SMEM),
                  pl.BlockSpec((tm, d), lambda i: (i + blk0, 0)),
                  pl.BlockSpec((tm, LANES), lambda i: (i + blk0, 0)),
                  pl.BlockSpec((1, d), lambda i: (0, 0)),
                  pl.BlockSpec(memory_space=pl.ANY)],
        out_specs=pl.BlockSpec((tm, d), lambda i: (i, 0)),
        out_shape=jax.ShapeDtypeStruct((t, d), F32),
        scratch_shapes=[pltpu.VMEM((2, 2, tm, d), F32), pltpu.SemaphoreType.DMA((2,))],
        compiler_params=_cparams(("arbitrary",)),
        name="moe_combine",
    )(dest_flat, dest_flat, x, wts, gf, yd)


def moe_block(x, g, wr, br, w_gate, w_up, w_down, e0, g_final=None, splits=None):
    t, d = x.shape
    nblk = (2 * t) // MOE_BLK + N_EXPERTS
    h, meta, wts, cnt = moe_route(x, g, wr, br)
    dest, ex = moe_plan(cnt, meta)
    dest_flat = dest[:, :2].reshape(2 * t)
    xd = moe_dispatch(dest_flat, h, jnp.zeros((nblk * MOE_BLK, h.shape[1]), h.dtype))
    yd = moe_experts(ex[0, R_E0:R_E0 + N_EXPERTS], ex[1, R_E0:R_E0 + N_EXPERTS], xd, w_gate, w_up, w_down, e0)
    if splits is None:
        return moe_combine(dest_flat, x, wts, yd, g_final)
    return [moe_combine(dest_flat, x, wts, yd, g_final, row0=r0, rows=n) for r0, n in splits]


def _pad_rows(buf):
    return jnp.pad(buf, ((0, 0), (0, SUBLANES - buf.shape[1]), (0, 0)))


def _forward(x_long, x_short, grp_long, grp_short, wts):
    depth = grp_long["mem_k"].shape[0]
    bl, ll = grp_long["batch"], grp_long["seqlen"]
    bs, ls = grp_short["batch"], grp_short["seqlen"]
    t_long, t_short = bl * ll, bs * ls
    d = x_long.shape[1]
    new = {id(grp_long): dict(gdn=[], conv=[], sc=[], ret=[]), id(grp_short): dict(gdn=[], conv=[], sc=[], ret=[])}

    def record_even(grp, s_new, cq, cs):
        rec = new[id(grp)]
        rec["gdn"].append(s_new)
        rec["conv"].append(cq.reshape(grp["batch"], SUBLANES, W_QKV_A)[:, :CONV_A - 1])
        rec["sc"].append(cs.reshape(grp["batch"], SUBLANES, D_B)[:, :CONV_B - 1])

    x_all = None
    for layer in range(depth):
        src_long = x_long if x_all is None else x_all
        src_short, row0 = (x_short, 0) if x_all is None else (x_all, t_long)
        if layer % 2 == 0:
            i = layer // 2
            hist = lambda grp: (_pad_rows(grp["conv"][i]), _pad_rows(grp["sc"][i]), wts["w_conv_qkv"][i],
                                wts["w_conv_sc"][i], wts["gdn_prm"][i], wts["gdn_norm"][i], grp["gdn"][i])
            p = rms_matmul(src_short, wts["norm_mix"][layer], wts["w_in_a"][i], tn=768, rows=t_short, row0=row0)
            mix, s_new, cq, cs = gdn_core(p, *hist(grp_short), batch=bs, seqlen=ls)
            xs = matmul_res(mix, wts["w_out_a"][i], src_short, res_row0=row0)
            record_even(grp_short, s_new, cq, cs)
            xl, s_new, cq, cs = gdn_block(src_long, wts["norm_mix"][layer], wts["w_in_a"][i], wts["w_out_a"][i],
                                          *hist(grp_long), batch=bl, seqlen=ll)
            record_even(grp_long, s_new, cq, cs)
        else:
            j = layer // 2
            p = rms_matmul(src_short, wts["norm_mix"][layer], wts["w_in_c"][j], tn=768, rows=t_short, row0=row0)
            ret, r_new = ret_core(p, grp_short["pos"], wts["ret_norm"][j], grp_short["ret"][j], batch=bs, seqlen=ls)
            xs = matmul_res(ret, wts["w_out_c"][j], src_short, res_row0=row0)
            new[id(grp_short)]["ret"].append(r_new)
            xl, r_new = ret_block(src_long, wts["norm_mix"][layer], wts["w_in_c"][j], wts["w_out_c"][j],
                                  grp_long["pos"], wts["ret_norm"][j], grp_long["ret"][j], batch=bl, seqlen=ll)
            new[id(grp_long)]["ret"].append(r_new)
        q = rms_matmul(xs, wts["norm_x"][layer], wts["w_xq"][layer], tn=D_MODEL)
        att = xattn_core(q, grp_short["mem_k"], grp_short["mem_v"], layer, batch=bs, seqlen=ls)
        joint = jnp.zeros((t_long + t_short, d), F32) if x_all is None else x_all
        joint = matmul_res(att, wts["w_xo"][layer], xs, into=joint, out_row0=t_long)
        joint = xattn_block(xl, wts["norm_x"][layer], wts["w_xq"][layer], wts["w_xo"][layer],
                            grp_long["mem_k"], grp_long["mem_v"], layer, joint, batch=bl, seqlen=ll)
        last = layer == depth - 1
        x_all = moe_block(joint, wts["norm_ffn"][layer], wts["w_route"][layer], wts["b_route"][layer],
                          wts["w_exp_gate"], wts["w_exp_up"], wts["w_exp_down"], layer * N_EXPERTS,
                          g_final=wts["norm_final"] if last else None,
                          splits=[(0, t_long), (t_long, t_short)] if last else None)
    y_long, y_short = x_all
    stack = lambda grp: tuple(jnp.stack(new[id(grp)][k]) for k in ("gdn", "conv", "sc", "ret"))
    return (y_long,) + stack(grp_long), (y_short,) + stack(grp_short)


def kernel(x_prompt, x_sample, state_gdn, state_gdn_conv, state_sconv, state_ret, cache_mem_k, cache_mem_v, mem_prompt, norm_mix, norm_x, norm_ffn, norm_final, norm_mem, w_in_a, w_conv_qkv, a_log, dt_bias, gdn_norm, w_conv_sc, w_out_a, w_in_c, ret_norm, w_out_c, w_xq, w_xk, w_xv, w_xo, w_group, b_group, w_router, b_router, w_exp_gate, w_exp_up, w_exp_down):
    bp, lp, d = x_prompt.shape
    bs, ls, _ = x_sample.shape
    depth = norm_mix.shape[0]
    n_even = w_in_a.shape[0]
    n_mem = mem_prompt.shape[1]

    qkv_w = 2 * H_A * DK_A + H_A * DV_A
    o_z = qkv_w
    o_b = o_z + H_A * DV_A
    o_a = o_b + H_A
    o_sc = o_a + H_A
    w_a = jnp.concatenate([w_in_a[:, :, :o_b], w_in_a[:, :, o_sc:], w_in_a[:, :, o_b:o_sc],
                           jnp.zeros((n_even, d, PA_COLS - PA_BA - 2 * H_A), F32)], axis=-1).astype(BF16)
    prm = jnp.zeros((n_even, SUBLANES, LANES), F32)
    prm = prm.at[:, 0, H_A:2 * H_A].set(a_log).at[:, 1, H_A:2 * H_A].set(dt_bias)
    w_route = jnp.concatenate([w_group, w_router, jnp.zeros((depth, d, LANES - N_GROUPS - N_EXPERTS), F32)],
                              axis=-1).astype(BF16)
    b_route = jnp.concatenate([b_group, b_router, jnp.zeros((depth, LANES - N_GROUPS - N_EXPERTS), F32)],
                              axis=-1).reshape(depth, 1, LANES)
    wts = dict(norm_mix=norm_mix, norm_x=norm_x, norm_ffn=norm_ffn, norm_final=norm_final,
               w_in_a=w_a, w_conv_qkv=w_conv_qkv, gdn_prm=prm, gdn_norm=gdn_norm, w_conv_sc=w_conv_sc,
               w_out_a=w_out_a.astype(BF16), w_in_c=w_in_c.astype(BF16), ret_norm=ret_norm,
               w_out_c=w_out_c.astype(BF16), w_xq=w_xq.astype(BF16), w_xo=w_xo.astype(BF16),
               w_route=w_route, b_route=b_route,
               w_exp_gate=w_exp_gate.reshape((depth * N_EXPERTS,) + w_exp_gate.shape[2:]),
               w_exp_up=w_exp_up.reshape((depth * N_EXPERTS,) + w_exp_up.shape[2:]),
               w_exp_down=w_exp_down.reshape((depth * N_EXPERTS,) + w_exp_down.shape[2:]))

    memf = mem_prompt.reshape(bp * n_mem, d)
    w_kv = jnp.concatenate([w_xk, w_xv], axis=-1).astype(BF16)
    mk_p, mv_p = [], []
    for layer in range(depth):
        kv = rms_matmul(memf, norm_mem[layer], w_kv[layer], tn=d)
        mk_p.append(kv[:, :d])
        mv_p.append(kv[:, d:])
    p_cache_mem_k = jnp.stack(mk_p).reshape(depth, bp, n_mem, H_X, HD_X)
    p_cache_mem_v = jnp.stack(mv_p).reshape(depth, bp, n_mem, H_X, HD_X)

    n_odd = w_in_c.shape[0]
    z_gdn = jnp.zeros((n_even, bp, H_A, DK_A, DV_A), F32)
    z_conv = jnp.zeros((n_even, bp, CONV_A - 1, qkv_w), F32)
    z_sc = jnp.zeros((n_even, bp, CONV_B - 1, D_B), F32)
    z_ret = jnp.zeros((n_odd, bp, H_C, DK_C, DV_C), F32)
    pos_p = jnp.arange(lp, dtype=I32)
    pos_s = 16384 + jnp.arange(ls, dtype=I32)

    grp_p = dict(batch=bp, seqlen=lp, pos=pos_p, gdn=z_gdn, conv=z_conv, sc=z_sc, ret=z_ret,
                 mem_k=p_cache_mem_k, mem_v=p_cache_mem_v)
    grp_s = dict(batch=bs, seqlen=ls, pos=pos_s, gdn=state_gdn, conv=state_gdn_conv, sc=state_sconv,
                 ret=state_ret, mem_k=cache_mem_k, mem_v=cache_mem_v)
    (y_p, p_gdn, p_conv, p_sc, p_ret), (y_s, s_gdn, s_conv, s_sc, s_ret) = _forward(
        x_prompt.reshape(bp * lp, d), x_sample.reshape(bs * ls, d), grp_p, grp_s, wts)
    return (y_p.reshape(bp, lp, d), y_s.reshape(bs, ls, d), p_gdn, p_conv, p_sc, p_ret, p_cache_mem_k,
            p_cache_mem_v, s_gdn, s_conv, s_sc, s_ret)
```

```python
import functools
import math

import jax
import jax.numpy as jnp
import numpy as np
from jax import lax
from jax.experimental import pallas as pl
from jax.experimental.pallas import tpu as pltpu

F32 = jnp.float32
BF16 = jnp.bfloat16
I32 = jnp.int32

EPS = 1e-6
ROPE_BASE = 10000.0

D_MODEL = 1024
H_A, DK_A, DV_A, CONV_A = 4, 128, 128, 4
W_QKV_A = 3 * H_A * DK_A
D_B, CONV_B = D_MODEL // 2, 3
H_C, DK_C, DV_C = 4, 256, 512
H_X, HD_X, N_MEM = 4, 256, 256
N_GROUPS, E_PER_GROUP, N_EXPERTS, D_EXPERT = 4, 8, 32, 512
GDN_CHUNK = 64

LANES = 128
SUBLANES = 8
GDN_STACK = 256
VMEM_LIMIT = 56 * 1024 * 1024

PA_COLS = 3840
PA_SC = 2048
PA_BA = 3584
R_E0 = N_GROUPS
MOE_BLK = 512


def _cparams(sem):
    return pltpu.CompilerParams(dimension_semantics=sem, vmem_limit_bytes=VMEM_LIMIT)


def _dot(a, b, trans_a=False, trans_b=False):
    dn = (((0 if trans_a else 1,), (1 if trans_b else 0,)), ((), ()))
    return lax.dot_general(a.astype(BF16), b.astype(BF16), dn, preferred_element_type=F32)


def _silu(x):
    return x * (1.0 / (1.0 + jnp.exp(-x)))


def _sigmoid(x):
    return 1.0 / (1.0 + jnp.exp(-x))


def _rms_matmul_kernel(x_ref, g_ref, w_ref, o_ref, xn_ref):
    @pl.when(pl.program_id(1) == 0)
    def _():
        x = x_ref[...]
        ms = jnp.mean(x * x, axis=-1, keepdims=True)
        xn_ref[...] = (x * lax.rsqrt(ms + EPS) * g_ref[...]).astype(BF16)

    o_ref[...] = jnp.dot(xn_ref[...], w_ref[...], preferred_element_type=F32).astype(o_ref.dtype)


def rms_matmul(x, g, w, *, tn, rows=None, row0=0, out_dtype=F32, tm=1024):
    t, d = (rows or x.shape[0]), x.shape[1]
    n = w.shape[1]
    tm = min(tm, t)
    blk0 = row0 // tm
    return pl.pallas_call(
        _rms_matmul_kernel,
        grid=(t // tm, n // tn),
        in_specs=[pl.BlockSpec((tm, d), lambda i, j: (i + blk0, 0)),
                  pl.BlockSpec((1, d), lambda i, j: (0, 0)),
                  pl.BlockSpec((d, tn), lambda i, j: (0, j))],
        out_specs=pl.BlockSpec((tm, tn), lambda i, j: (i, j)),
        out_shape=jax.ShapeDtypeStruct((t, n), out_dtype),
        scratch_shapes=[pltpu.VMEM((tm, d), BF16)],
        compiler_params=_cparams(("parallel", "arbitrary")),
        name="rms_matmul",
    )(x, g.reshape(1, d), w)


def _matmul_res_kernel(a_ref, w_ref, r_ref, *rest):
    o_ref = rest[-1]
    o_ref[...] = r_ref[...] + jnp.dot(a_ref[...].astype(BF16), w_ref[...], preferred_element_type=F32)


def matmul_res(a, w, res, *, res_row0=0, into=None, out_row0=0, tm=512):
    t, k = a.shape
    n = w.shape[1]
    tm = min(tm, t)
    rblk, oblk = res_row0 // tm, out_row0 // tm
    in_specs = [pl.BlockSpec((tm, k), lambda i: (i, 0)),
                pl.BlockSpec((k, n), lambda i: (0, 0)),
                pl.BlockSpec((tm, n), lambda i: (i + rblk, 0))]
    args = (a, w, res)
    if into is not None:
        in_specs.append(pl.BlockSpec(memory_space=pl.ANY))
        args += (into,)
    return pl.pallas_call(
        _matmul_res_kernel,
        grid=(t // tm,),
        in_specs=in_specs,
        out_specs=pl.BlockSpec((tm, n), lambda i: (i + oblk, 0)),
        out_shape=jax.ShapeDtypeStruct((t, n) if into is None else into.shape, F32),
        input_output_aliases={} if into is None else {3: 0},
        compiler_params=_cparams(("parallel",)),
        name="matmul_res",
    )(*args)


def _causal_conv(x, hist, w_ref, width, seq8):
    r = x.shape[0]
    taps = [w_ref[j:j + 1, :] for j in range(width)]

    def head(x8, h8):
        n = x8.shape[0]
        t = lax.broadcasted_iota(I32, (n, 1), 0) % SUBLANES
        y = taps[width - 1] * x8
        for s in range(1, width):
            prev = pltpu.roll(h8, (n + s - (width - 1)) % n, 0) if s != width - 1 else h8
            y = y + taps[width - 1 - s] * jnp.where(t >= s, pltpu.roll(x8, s, 0), prev)
        return y

    if seq8:
        return head(x, hist)
    y = taps[width - 1] * x
    for s in range(1, width):
        y = y + taps[width - 1 - s] * pltpu.roll(x, s, 0)
    return jnp.concatenate([head(x[:SUBLANES], hist), y[SUBLANES:]], axis=0)


def _unit_lower_inverse(ms, c, ri, ci):
    base = min(c, 16)
    eye = jnp.where(ri == ci, 1.0, 0.0).astype(F32)
    blk = (ri // base) == (ci // base)
    ds = [jnp.where(blk, m, 0.0) for m in ms]
    ps = [eye - d for d in ds]
    k = 2
    while k < base:
        ds = [_dot(d, d) for d in ds]
        ps = [_dot(p, eye + d) for p, d in zip(ps, ds)]
        k *= 2
    s = base
    while s < c:
        sel = ((ri // (2 * s)) == (ci // (2 * s))) & ((ri // s) != (ci // s))
        ts = [_dot(jnp.where(sel, m, 0.0), p) for m, p in zip(ms, ps)]
        ps = [p - _dot(p, t) for p, t in zip(ps, ts)]
        s *= 2
    return ps


def _gdn_prepare(units, c):
    n = GDN_STACK
    ri = lax.broadcasted_iota(I32, (n, n), 0)
    ci = lax.broadcasted_iota(I32, (n, n), 1)
    same = (ri // c) == (ci // c)
    incl = same & (ri >= ci)
    strict = same & (ri > ci)
    pre = []
    for q, k, v, bfull, gfull in units:
        g2 = jnp.concatenate([gfull, gfull], axis=1)
        g_row = jnp.sum(jnp.where(ri == ci, g2, 0.0), axis=0, keepdims=True)
        gc_col = jnp.sum(jnp.where(incl, g_row, 0.0), axis=1, keepdims=True)
        gc_row = jnp.sum(jnp.where(same & (ri <= ci), g2, 0.0), axis=0, keepdims=True)
        gl_col = jnp.sum(jnp.where(same, g_row, 0.0), axis=1, keepdims=True)
        decay = jnp.where(incl, jnp.exp(jnp.where(incl, gc_col - gc_row, 0.0)), 0.0)
        egc = jnp.exp(gc_col)
        kb = k * bfull
        pre.append(dict(decay=decay, kb=kb, rhs=jnp.concatenate([v * bfull, kb * egc], axis=1),
                        qd=q * egc, kd=k * jnp.exp(gl_col - gc_col), egl=jnp.exp(gl_col)))
    mms = [jnp.where(strict, _dot(e["kb"], u[1], trans_b=True) * e["decay"], 0.0) for e, u in zip(pre, units)]
    qks = [_dot(u[0], u[1], trans_b=True) * e["decay"] for e, u in zip(pre, units)]
    tinvs = _unit_lower_inverse(mms, c, ri, ci)
    uws = [_dot(t, e["rhs"]) for t, e in zip(tinvs, pre)]
    return [dict(u=uw[:, :DV_A], w=uw[:, DV_A:], qk=qk, qd=e["qd"], kd=e["kd"], egl=e["egl"])
            for uw, qk, e in zip(uws, qks, pre)]


def _gdn_recur(e, states, c):
    nprob = GDN_STACK // c
    ws, qs = [], []
    for p in range(nprob):
        sl = slice(p * c, (p + 1) * c)
        ws.append(_dot(e["w"][sl], states[p]))
        qs.append(_dot(e["qd"][sl], states[p]))
    vn = e["u"] - jnp.concatenate(ws, axis=0)
    o = _dot(e["qk"], vn) + jnp.concatenate(qs, axis=0)
    new_states = []
    for p in range(nprob):
        sl = slice(p * c, (p + 1) * c)
        new_states.append(states[p] * e["egl"][p * c:p * c + 1, :] + _dot(e["kd"][sl], vn[sl], trans_a=True))
    return o, new_states


def _gdn_compute(x, z, u_sc, scb, ba, hq, hs, wq_ref, ws_ref, prm_ref, gn, states, *, seq8, c, nu):
    unit = GDN_CHUNK
    xc = _silu(_causal_conv(x, hq, wq_ref, CONV_A, seq8))
    yb = scb * _causal_conv(u_sc, hs, ws_ref, CONV_B, seq8)
    beta_all = _sigmoid(ba)
    sp = jnp.maximum(ba + prm_ref[1:2, :], 0.0) + jnp.log1p(jnp.exp(-jnp.abs(ba + prm_ref[1:2, :])))
    g_all = -jnp.exp(prm_ref[0:1, :]) * sp

    def head_cols(a, base):
        return a[:, base * DK_A:(base + 1) * DK_A]

    cat = lambda xs: jnp.concatenate(xs, axis=0)
    units = []
    for ui in range(nu):
        rs = slice(ui * unit, (ui + 1) * unit)
        qs, ks, vs, bs, gs = [], [], [], [], []
        for h in range(H_A):
            qh = head_cols(xc, h)[rs]
            kh = head_cols(xc, H_A + h)[rs]
            qs.append(qh * lax.rsqrt(jnp.sum(qh * qh, axis=-1, keepdims=True) + EPS) * (DK_A ** -0.5))
            ks.append(kh * lax.rsqrt(jnp.sum(kh * kh, axis=-1, keepdims=True) + EPS))
            vs.append(head_cols(xc, 2 * H_A + h)[rs])
            bs.append(jnp.broadcast_to(beta_all[rs, h:h + 1], (unit, LANES)))
            gs.append(jnp.broadcast_to(g_all[rs, H_A + h:H_A + h + 1], (unit, LANES)))
        units.append((cat(qs), cat(ks), cat(vs), cat(bs), cat(gs)))
    prepared = _gdn_prepare(units, c)

    outs = []
    for ui in range(nu):
        rs = slice(ui * unit, (ui + 1) * unit)
        o, states = _gdn_recur(prepared[ui], states, c)
        zst = cat([z[rs, h * DV_A:(h + 1) * DV_A] for h in range(H_A)])
        ms = jnp.mean(o * o, axis=-1, keepdims=True)
        og = o * lax.rsqrt(ms + EPS) * gn * _silu(zst)
        outs.append(jnp.concatenate([og[h * unit:(h + 1) * unit] for h in range(H_A)], axis=1))
    o_all = outs[0] if nu == 1 else cat(outs)
    return jnp.concatenate([o_all, yb], axis=1), states


def _gdn_kernel(qkv_ref, z_ref, sch_ref, scb_ref, scc_ref, ba_ref, hq_ref, hs_ref, wq_ref, ws_ref,
                prm_ref, gn_ref, s0_ref, o_ref, sn_ref, cq_ref, cs_ref):
    x = qkv_ref[...]
    rows = x.shape[0]
    u_sc = scc_ref[...] * sch_ref[...]
    cq_ref[...] = pltpu.roll(x, rows - SUBLANES + CONV_A - 1, 0)
    cs_ref[...] = pltpu.roll(u_sc, rows - SUBLANES + CONV_B - 1, 0)
    nprob = GDN_STACK // SUBLANES
    states = [s0_ref[p % SUBLANES, p // SUBLANES] for p in range(nprob)]
    mix, states = _gdn_compute(x, z_ref[...], u_sc, scb_ref[...], ba_ref[...], hq_ref[...], hs_ref[...],
                               wq_ref, ws_ref, prm_ref, gn_ref[...], states, seq8=True, c=SUBLANES, nu=1)
    o_ref[...] = mix.astype(o_ref.dtype)
    for p in range(nprob):
        sn_ref[p % SUBLANES, p // SUBLANES] = states[p]


def _gdn_block_kernel(x_ref, g_ref, wi_ref, wo_ref, hq_ref, hs_ref, wq_ref, ws_ref, prm_ref, gn_ref, s0_ref,
                      o_ref, sn_ref, cq_ref, cs_ref, s_scr, hq_scr, hs_scr, *, nu, nl):
    l = pl.program_id(1)

    @pl.when(l == 0)
    def _():
        hq_scr[...] = hq_ref[0]
        hs_scr[...] = hs_ref[0]
        s_scr[...] = s0_ref[0]

    xres = x_ref[...]
    rows = xres.shape[0]
    ms = jnp.mean(xres * xres, axis=-1, keepdims=True)
    xn = (xres * lax.rsqrt(ms + EPS) * g_ref[...]).astype(BF16)
    proj = lambda lo, width: jnp.dot(xn, wi_ref[:, lo:lo + width], preferred_element_type=F32)
    x = proj(0, W_QKV_A)
    z = proj(W_QKV_A, H_A * DV_A)
    u_sc = proj(PA_SC + 2 * D_B, D_B) * proj(PA_SC, D_B)
    scb = proj(PA_SC + D_B, D_B)
    ba = proj(PA_BA, LANES)
    hq = hq_scr[...]
    hs = hs_scr[...]
    hq_scr[...] = pltpu.roll(x[rows - SUBLANES:], CONV_A - 1, 0)
    hs_scr[...] = pltpu.roll(u_sc[rows - SUBLANES:], CONV_B - 1, 0)
    states = [s_scr[p] for p in range(H_A)]
    mix, states = _gdn_compute(x, z, u_sc, scb, ba, hq, hs, wq_ref, ws_ref, prm_ref, gn_ref[...], states,
                               seq8=False, c=GDN_CHUNK, nu=nu)
    for p in range(H_A):
        s_scr[p] = states[p]
    o_ref[...] = xres + jnp.dot(mix.astype(BF16), wo_ref[...], preferred_element_type=F32)

    @pl.when(l == nl - 1)
    def _():
        sn_ref[0] = s_scr[...]
        cq_ref[0] = hq_scr[...]
        cs_ref[0] = hs_scr[...]


def gdn_core(p, hist_q, hist_s, w_conv_qkv, w_conv_sc, prm, gn, s0, *, batch, seqlen):
    t = batch * seqlen
    assert seqlen == SUBLANES
    rows = GDN_CHUNK
    nb = rows // seqlen
    hq_spec = pl.BlockSpec((rows, W_QKV_A), lambda i: (i, 0))
    hs_spec = pl.BlockSpec((rows, D_B), lambda i: (i, 0))
    s_spec = pl.BlockSpec((nb, H_A, DK_A, DV_A), lambda i: (i, 0, 0, 0))
    col = lambda width, blk: pl.BlockSpec((rows, width), lambda i: (i, blk))
    const = lambda shape: pl.BlockSpec(shape, lambda i: (0,) * len(shape))
    return pl.pallas_call(
        _gdn_kernel,
        grid=(batch // nb,),
        in_specs=[col(W_QKV_A, 0), col(D_B, 3), col(D_B, 4), col(D_B, 5), col(D_B, 6),
                  col(LANES, PA_BA // LANES), hq_spec, hs_spec,
                  const((CONV_A, W_QKV_A)), const((CONV_B, D_B)), const((SUBLANES, LANES)),
                  const((1, DV_A)), s_spec],
        out_specs=[pl.BlockSpec((rows, D_MODEL), lambda i: (i, 0)), s_spec, hq_spec, hs_spec],
        out_shape=[jax.ShapeDtypeStruct((t, D_MODEL), BF16),
                   jax.ShapeDtypeStruct((batch, H_A, DK_A, DV_A), F32),
                   jax.ShapeDtypeStruct((t, W_QKV_A), F32),
                   jax.ShapeDtypeStruct((t, D_B), F32)],
        compiler_params=_cparams(("parallel",)),
        name="gdn_core",
    )(p, p, p, p, p, p, hist_q.reshape(t, W_QKV_A), hist_s.reshape(t, D_B), w_conv_qkv, w_conv_sc, prm,
      gn.reshape(1, DV_A), s0)


def _resident(shape):
    return pl.BlockSpec(shape, lambda *ix: (0,) * len(shape), pipeline_mode=pl.Buffered(1))


def gdn_block(x, g, w_in, w_out, hist_q, hist_s, w_conv_qkv, w_conv_sc, prm, gn, s0, *, batch, seqlen):
    t, d = batch * seqlen, x.shape[1]
    rows = min(seqlen, 256)
    nu = rows // GDN_CHUNK
    nl = seqlen // rows
    per_b = lambda shape: pl.BlockSpec((1,) + shape, lambda b, l: (b,) + (0,) * len(shape))
    row_spec = pl.BlockSpec((rows, d), lambda b, l: (b * nl + l, 0))
    kern = functools.partial(_gdn_block_kernel, nu=nu, nl=nl)
    return pl.pallas_call(
        kern,
        grid=(batch, nl),
        in_specs=[row_spec, _resident((1, d)), _resident(w_in.shape), _resident(w_out.shape),
                  per_b((SUBLANES, W_QKV_A)), per_b((SUBLANES, D_B)),
                  _resident((CONV_A, W_QKV_A)), _resident((CONV_B, D_B)), _resident((SUBLANES, LANES)),
                  _resident((1, DV_A)), per_b((H_A, DK_A, DV_A))],
        out_specs=[row_spec, per_b((H_A, DK_A, DV_A)), per_b((SUBLANES, W_QKV_A)), per_b((SUBLANES, D_B))],
        out_shape=[jax.ShapeDtypeStruct((t, d), F32),
                   jax.ShapeDtypeStruct((batch, H_A, DK_A, DV_A), F32),
                   jax.ShapeDtypeStruct((batch, SUBLANES, W_QKV_A), F32),
                   jax.ShapeDtypeStruct((batch, SUBLANES, D_B), F32)],
        scratch_shapes=[pltpu.VMEM((H_A, DK_A, DV_A), F32),
                        pltpu.VMEM((SUBLANES, W_QKV_A), F32),
                        pltpu.VMEM((SUBLANES, D_B), F32)],
        compiler_params=_cparams(("parallel", "arbitrary")),
        name="gdn_block",
    )(x, g.reshape(1, d), w_in, w_out, hist_q, hist_s, w_conv_qkv, w_conv_sc, prm, gn.reshape(1, DV_A), s0)


def _ret_compute(get_q, get_k, get_v, get_gate, cos, sin, dm_ref, qd_ref, kd_ref, cd_ref, rn_ref, r_scr,
                 *, nseq, c):
    half = DK_C // 2

    def rot(x):
        x1, x2 = x[:, :half], x[:, half:]
        return jnp.concatenate([x1 * cos - x2 * sin, x1 * sin + x2 * cos], axis=1)

    heads = range(H_C)
    qs = [rot(get_q(h)) for h in heads]
    ks = [rot(get_k(h)) * (DK_C ** -0.5) for h in heads]
    vs = [get_v(h).astype(BF16) for h in heads]
    ss = [_dot(qs[h], ks[h], trans_b=True) * dm_ref[h] for h in heads]
    inters = []
    for h in heads:
        qdh = qs[h] * qd_ref[h]
        parts = [_dot(qdh[sq * c:(sq + 1) * c], r_scr[sq, h]) for sq in range(nseq)]
        inters.append(parts[0] if nseq == 1 else jnp.concatenate(parts, axis=0))
    outs = [_dot(ss[h], vs[h]) + inters[h] for h in heads]
    for h in heads:
        kdh = ks[h] * kd_ref[h]
        cd = cd_ref[h][0:1, 0:1]
        for sq in range(nseq):
            sl = slice(sq * c, (sq + 1) * c)
            r_scr[sq, h] = r_scr[sq, h] * cd + _dot(kdh[sl], vs[h][sl], trans_a=True)
    gated = []
    for h in heads:
        o = outs[h]
        ms = jnp.mean(o * o, axis=-1, keepdims=True)
        on = o * lax.rsqrt(ms + EPS) * rn_ref[:, h * DV_C:(h + 1) * DV_C]
        gated.append(_silu(get_gate(h)) * on)
    return gated


def _ret_kernel(q_ref, k_ref, v_ref, gate_ref, cos_ref, sin_ref, dm_ref, qd_ref, kd_ref, cd_ref,
                rn_ref, r0_ref, o_ref, rnew_ref, r_scr, *, nseq, c, nl):
    l = pl.program_id(1)

    @pl.when(l == 0)
    def _():
        r_scr[...] = r0_ref[...]

    gated = _ret_compute(lambda h: q_ref[:, h * DK_C:(h + 1) * DK_C], lambda h: k_ref[:, h * DK_C:(h + 1) * DK_C],
                         lambda h: v_ref[:, h * DV_C:(h + 1) * DV_C], lambda h: gate_ref[:, h * DV_C:(h + 1) * DV_C],
                         cos_ref[...], sin_ref[...], dm_ref, qd_ref, kd_ref, cd_ref, rn_ref, r_scr, nseq=nseq, c=c)
    for h in range(H_C):
        o_ref[:, h * DV_C:(h + 1) * DV_C] = gated[h].astype(o_ref.dtype)

    @pl.when(l == nl - 1)
    def _():
        rnew_ref[...] = r_scr[...]


def _ret_block_kernel(x_ref, g_ref, wi_ref, wo_ref, cos_ref, sin_ref, dm_ref, qd_ref, kd_ref, cd_ref,
                      rn_ref, r0_ref, o_ref, rnew_ref, r_scr, *, c, nl):
    l = pl.program_id(1)

    @pl.when(l == 0)
    def _():
        r_scr[...] = r0_ref[...]

    xres = x_ref[...]
    ms = jnp.mean(xres * xres, axis=-1, keepdims=True)
    xn = (xres * lax.rsqrt(ms + EPS) * g_ref[...]).astype(BF16)
    proj = lambda lo, width: jnp.dot(xn, wi_ref[:, lo:lo + width], preferred_element_type=F32)
    hk, hv = H_C * DK_C, H_C * DV_C
    gated = _ret_compute(lambda h: proj(h * DK_C, DK_C), lambda h: proj(hk + h * DK_C, DK_C),
                         lambda h: proj(2 * hk + h * DV_C, DV_C), lambda h: proj(2 * hk + hv + h * DV_C, DV_C),
                         cos_ref[...], sin_ref[...], dm_ref, qd_ref, kd_ref, cd_ref, rn_ref, r_scr, nseq=1, c=c)
    y = xres
    for h in range(H_C):
        y = y + jnp.dot(gated[h].astype(BF16), wo_ref[h * DV_C:(h + 1) * DV_C, :], preferred_element_type=F32)
    o_ref[...] = y

    @pl.when(l == nl - 1)
    def _():
        rnew_ref[...] = r_scr[...]


def ret_core(p, pos, ret_norm, r0, *, batch, seqlen):
    t = batch * seqlen
    assert seqlen == SUBLANES
    nseq, c = 2, seqlen
    rows = nseq * c
    cos, sin, dmat, qd, kd, cd = _ret_tables(pos, nseq, c)
    const = lambda shape: pl.BlockSpec(shape, lambda b, l: (0,) * len(shape))
    kern = functools.partial(_ret_kernel, nseq=nseq, c=c, nl=1)
    hk = H_C * DK_C
    hv = H_C * DV_C
    return pl.pallas_call(
        kern,
        grid=(batch // nseq, 1),
        in_specs=[pl.BlockSpec((rows, hk), lambda b, l: (b, 0)),
                  pl.BlockSpec((rows, hk), lambda b, l: (b, 1)),
                  pl.BlockSpec((rows, hv), lambda b, l: (b, 1)),
                  pl.BlockSpec((rows, hv), lambda b, l: (b, 2)),
                  const((rows, DK_C // 2)), const((rows, DK_C // 2)),
                  const((H_C, rows, rows)), const((H_C, rows, DK_C)), const((H_C, rows, DK_C)),
                  const((H_C, SUBLANES, LANES)), const((1, hv)),
                  pl.BlockSpec((nseq, H_C, DK_C, DV_C), lambda b, l: (b, 0, 0, 0))],
        out_specs=[pl.BlockSpec((rows, hv), lambda b, l: (b, 0)),
                   pl.BlockSpec((nseq, H_C, DK_C, DV_C), lambda b, l: (b, 0, 0, 0))],
        out_shape=[jax.ShapeDtypeStruct((t, hv), BF16),
                   jax.ShapeDtypeStruct((batch, H_C, DK_C, DV_C), F32)],
        scratch_shapes=[pltpu.VMEM((nseq, H_C, DK_C, DV_C), F32)],
        compiler_params=_cparams(("parallel", "arbitrary")),
        name="ret_core",
    )(p, p, p, p, cos, sin, dmat, qd, kd, cd, ret_norm.reshape(1, hv), r0)


def _ret_tables(pos, nseq, c):
    half = DK_C // 2
    inv = ROPE_BASE ** (-jnp.arange(half, dtype=F32) / half)
    ang = pos.astype(F32)[:, None] * inv[None, :]
    cos, sin = jnp.cos(ang), jnp.sin(ang)
    if nseq > 1:
        cos, sin = jnp.tile(cos, (nseq, 1)), jnp.tile(sin, (nseq, 1))
    lg = jnp.log(1.0 - 2.0 ** (-5.0 - jnp.arange(H_C, dtype=F32)))[:, None]
    i = jnp.arange(c, dtype=F32)
    incl = i[:, None] >= i[None, :]
    dmat = jnp.exp(jnp.where(incl[None], (i[:, None] - i[None, :])[None] * lg[..., None], -jnp.inf))
    if nseq > 1:
        dmat = jnp.kron(jnp.eye(nseq, dtype=F32)[None], dmat)
    qd = jnp.tile(jnp.exp((i + 1.0)[None] * lg), (1, nseq))[..., None] * jnp.ones((1, 1, DK_C), F32)
    kd = jnp.tile(jnp.exp((c - 1.0 - i)[None] * lg), (1, nseq))[..., None] * jnp.ones((1, 1, DK_C), F32)
    cd = jnp.exp(c * lg)[..., None] * jnp.ones((1, SUBLANES, LANES), F32)
    return cos, sin, dmat, qd, kd, cd


def ret_block(x, g, w_in, w_out, pos, ret_norm, r0, *, batch, seqlen):
    t, d = batch * seqlen, x.shape[1]
    c = min(seqlen, 256)
    nl = seqlen // c
    cos, sin, dmat, qd, kd, cd = _ret_tables(pos, 1, c)
    hv = H_C * DV_C
    row_spec = pl.BlockSpec((c, d), lambda b, l: (b * nl + l, 0))
    trig_spec = pl.BlockSpec((c, DK_C // 2), lambda b, l: (l, 0))
    state_spec = pl.BlockSpec((1, H_C, DK_C, DV_C), lambda b, l: (b, 0, 0, 0))
    kern = functools.partial(_ret_block_kernel, c=c, nl=nl)
    return pl.pallas_call(
        kern,
        grid=(batch, nl),
        in_specs=[row_spec, _resident((1, d)), _resident(w_in.shape), _resident(w_out.shape),
                  trig_spec, trig_spec,
                  _resident((H_C, c, c)), _resident((H_C, c, DK_C)), _resident((H_C, c, DK_C)),
                  _resident((H_C, SUBLANES, LANES)), _resident((1, hv)), state_spec],
        out_specs=[row_spec, state_spec],
        out_shape=[jax.ShapeDtypeStruct((t, d), F32),
                   jax.ShapeDtypeStruct((batch, H_C, DK_C, DV_C), F32)],
        scratch_shapes=[pltpu.VMEM((1, H_C, DK_C, DV_C), F32)],
        compiler_params=_cparams(("parallel", "arbitrary")),
        name="ret_block",
    )(x, g.reshape(1, d), w_in, w_out, cos, sin, dmat, qd, kd, cd, ret_norm.reshape(1, hv), r0)


def _xattn_fetch(mk_hbm, mv_hbm, kbuf, vbuf, sem, *, layer, nb, nsteps):
    i = pl.program_id(0)
    l = pl.program_id(1)
    slot = i % 2

    def copies(step, slot_):
        out = []
        for b in range(nb):
            for h in range(H_X):
                out.append(pltpu.make_async_copy(mk_hbm.at[layer, step * nb + b, :, h, :],
                                                 kbuf.at[slot_, b, h], sem.at[slot_]))
                out.append(pltpu.make_async_copy(mv_hbm.at[layer, step * nb + b, :, h, :],
                                                 vbuf.at[slot_, b, h], sem.at[slot_]))
        return out

    @pl.when(l == 0)
    def _():
        @pl.when(i == 0)
        def _():
            for c in copies(i, slot):
                c.start()

        @pl.when(i + 1 < nsteps)
        def _():
            for c in copies(i + 1, 1 - slot):
                c.start()

        for c in copies(i, slot):
            c.wait()

    return slot


def _xattn_heads(q_of, kbuf, vbuf, slot, probs):
    ss = [_dot(q_of(b, h), kbuf[slot, b, h], trans_b=True) * (HD_X ** -0.5) for b, h in probs]
    ps = []
    for s in ss:
        e = jnp.exp(s - jnp.max(s, axis=-1, keepdims=True))
        ps.append(e / jnp.sum(e, axis=-1, keepdims=True))
    return [_dot(p, vbuf[slot, b, h]) for p, (b, h) in zip(ps, probs)]


def _xattn_kernel(q_ref, mk_hbm, mv_hbm, o_ref, kbuf, vbuf, sem, *, layer, nb, lq, nsteps):
    slot = _xattn_fetch(mk_hbm, mv_hbm, kbuf, vbuf, sem, layer=layer, nb=nb, nsteps=nsteps)
    probs = [(b, h) for b in range(nb) for h in range(H_X)]
    win = lambda b, h: (slice(b * lq, (b + 1) * lq), slice(h * HD_X, (h + 1) * HD_X))
    outs = _xattn_heads(lambda b, h: q_ref[win(b, h)], kbuf, vbuf, slot, probs)
    for o, (b, h) in zip(outs, probs):
        o_ref[win(b, h)] = o.astype(o_ref.dtype)


def _xattn_block_kernel(x_ref, g_ref, wq_ref, wo_ref, mk_hbm, mv_hbm, into_hbm, o_ref, kbuf, vbuf, sem,
                        *, layer, nsteps):
    del into_hbm
    slot = _xattn_fetch(mk_hbm, mv_hbm, kbuf, vbuf, sem, layer=layer, nb=1, nsteps=nsteps)
    xres = x_ref[...]
    ms = jnp.mean(xres * xres, axis=-1, keepdims=True)
    xn = (xres * lax.rsqrt(ms + EPS) * g_ref[...]).astype(BF16)
    probs = [(0, h) for h in range(H_X)]
    qs = [jnp.dot(xn, wq_ref[:, h * HD_X:(h + 1) * HD_X], preferred_element_type=F32) for h in range(H_X)]
    outs = _xattn_heads(lambda b, h: qs[h], kbuf, vbuf, slot, probs)
    y = xres
    for h in range(H_X):
        y = y + jnp.dot(outs[h].astype(BF16), wo_ref[h * HD_X:(h + 1) * HD_X, :], preferred_element_type=F32)
    o_ref[...] = y


def xattn_core(q, mk, mv, layer, *, batch, seqlen):
    t = batch * seqlen
    d = H_X * HD_X
    assert seqlen == SUBLANES
    nb, lq = 2, seqlen
    rows = nb * lq
    nsteps = batch // nb
    kern = functools.partial(_xattn_kernel, layer=layer, nb=nb, lq=lq, nsteps=nsteps)
    return pl.pallas_call(
        kern,
        grid=(nsteps, 1),
        in_specs=[pl.BlockSpec((rows, d), lambda b, l: (b, 0)),
                  pl.BlockSpec(memory_space=pl.ANY), pl.BlockSpec(memory_space=pl.ANY)],
        out_specs=pl.BlockSpec((rows, d), lambda b, l: (b, 0)),
        out_shape=jax.ShapeDtypeStruct((t, d), q.dtype),
        scratch_shapes=[pltpu.VMEM((2, nb, H_X, N_MEM, HD_X), F32),
                        pltpu.VMEM((2, nb, H_X, N_MEM, HD_X), F32),
                        pltpu.SemaphoreType.DMA((2,))],
        compiler_params=_cparams(("arbitrary", "arbitrary")),
        name="xattn_core",
    )(q, mk, mv)


def xattn_block(x, g, w_q, w_o, mk, mv, layer, into, *, batch, seqlen):
    d = x.shape[1]
    lq = min(seqlen, 512)
    nl = seqlen // lq
    row_spec = pl.BlockSpec((lq, d), lambda b, l: (b * nl + l, 0))
    kern = functools.partial(_xattn_block_kernel, layer=layer, nsteps=batch)
    return pl.pallas_call(
        kern,
        grid=(batch, nl),
        in_specs=[row_spec, _resident((1, d)), _resident(w_q.shape), _resident(w_o.shape),
                  pl.BlockSpec(memory_space=pl.ANY), pl.BlockSpec(memory_space=pl.ANY),
                  pl.BlockSpec(memory_space=pl.ANY)],
        out_specs=row_spec,
        out_shape=jax.ShapeDtypeStruct(into.shape, F32),
        scratch_shapes=[pltpu.VMEM((2, 1, H_X, N_MEM, HD_X), F32),
                        pltpu.VMEM((2, 1, H_X, N_MEM, HD_X), F32),
                        pltpu.SemaphoreType.DMA((2,))],
        input_output_aliases={6: 0},
        compiler_params=_cparams(("arbitrary", "arbitrary")),
        name="xattn_block",
    )(x, g.reshape(1, d), w_q, w_o, mk, mv, into)


def _route_kernel(x_ref, g_ref, wr_ref, br_ref, h_ref, meta_ref, wts_ref, cnt_ref, cnt_scr):
    i = pl.program_id(0)

    @pl.when(i == 0)
    def _():
        cnt_scr[...] = jnp.zeros_like(cnt_scr)

    x = x_ref[...]
    tm = x.shape[0]
    ms = jnp.mean(x * x, axis=-1, keepdims=True)
    h = x * lax.rsqrt(ms + EPS) * g_ref[...]
    hb = h.astype(BF16)
    bits = pltpu.bitcast(hb.astype(F32), jnp.uint32)
    half_d = bits.shape[1] // 2
    h_ref[...] = bits[:, :half_d] | (bits[:, half_d:] >> 16)
    logits = jnp.dot(hb, wr_ref[...], preferred_element_type=F32) + br_ref[...]
    lane_i = lax.broadcasted_iota(I32, (tm, LANES), 1)
    lane = lane_i.astype(F32)
    neg = jnp.float32(-3.0e38)
    big = jnp.float32(LANES)
    is_g = lane_i < N_GROUPS
    gl = jnp.where(is_g, logits, neg)
    gmax = jnp.max(gl, axis=1, keepdims=True)
    grp = jnp.min(jnp.where(gl == gmax, lane, big), axis=1, keepdims=True)
    gsum = jnp.sum(jnp.where(is_g, jnp.exp(jnp.where(is_g, logits - gmax, 0.0)), 0.0), axis=1, keepdims=True)
    p_grp = 1.0 / gsum
    in_grp = ((lane_i >= R_E0) & (lane_i < R_E0 + N_EXPERTS)
              & (jnp.floor((lane - R_E0) * (1.0 / E_PER_GROUP)) == grp))
    el = jnp.where(in_grp, logits, neg)
    m1 = jnp.max(el, axis=1, keepdims=True)
    i1 = jnp.min(jnp.where(el == m1, lane, big), axis=1, keepdims=True)
    el2 = jnp.where(lane == i1, neg, el)
    m2 = jnp.max(el2, axis=1, keepdims=True)
    i2 = jnp.min(jnp.where(el2 == m2, lane, big), axis=1, keepdims=True)
    esum = jnp.sum(jnp.where(in_grp, jnp.exp(jnp.where(in_grp, logits - m1, 0.0)), 0.0), axis=1, keepdims=True)
    p1 = 1.0 / esum
    p2 = jnp.exp(m2 - m1) / esum
    tot = p1 + p2
    w1 = p_grp * (p1 / tot)
    w2 = p_grp * (p2 / tot)
    wts_ref[...] = jnp.where(lane_i == 0, w1, jnp.where(lane_i == 1, w2, 0.0))

    oh1 = (lane == i1).astype(F32)
    oh2 = (lane == i2).astype(F32)
    rr = lax.broadcasted_iota(I32, (tm, tm), 0)
    cc = lax.broadcasted_iota(I32, (tm, tm), 1)
    tri = (rr > cc).astype(BF16)
    base = cnt_scr[0:1, :]
    c1 = jnp.sum(oh1, axis=0, keepdims=True)
    c2 = jnp.sum(oh2, axis=0, keepdims=True)
    r1 = jnp.sum(oh1 * (jnp.dot(tri, oh1.astype(BF16), preferred_element_type=F32) + base), axis=1, keepdims=True)
    r2 = jnp.sum(oh2 * (jnp.dot(tri, oh2.astype(BF16), preferred_element_type=F32) + base + c1), axis=1, keepdims=True)
    new_cnt = base + c1 + c2
    cnt_scr[...] = jnp.broadcast_to(new_cnt, cnt_scr.shape)
    cnt_ref[...] = jnp.broadcast_to(new_cnt, cnt_ref.shape)
    meta = jnp.where(lane_i == 0, i1, jnp.where(lane_i == 1, i2, 0.0))
    meta = jnp.where(lane_i == 2, r1, jnp.where(lane_i == 3, r2, meta))
    meta_ref[...] = meta.astype(I32)


def moe_route(x, g, wr, br, *, tm=256):
    t, d = x.shape
    tm = min(tm, t)
    return pl.pallas_call(
        _route_kernel,
        grid=(t // tm,),
        in_specs=[pl.BlockSpec((tm, d), lambda i: (i, 0)),
                  pl.BlockSpec((1, d), lambda i: (0, 0)),
                  pl.BlockSpec((d, LANES), lambda i: (0, 0)),
                  pl.BlockSpec((1, LANES), lambda i: (0, 0))],
        out_specs=[pl.BlockSpec((tm, d // 2), lambda i: (i, 0)),
                   pl.BlockSpec((tm, LANES), lambda i: (i, 0)),
                   pl.BlockSpec((tm, LANES), lambda i: (i, 0)),
                   pl.BlockSpec((SUBLANES, LANES), lambda i: (0, 0))],
        out_shape=[jax.ShapeDtypeStruct((t, d // 2), jnp.uint32),
                   jax.ShapeDtypeStruct((t, LANES), I32),
                   jax.ShapeDtypeStruct((t, LANES), F32),
                   jax.ShapeDtypeStruct((SUBLANES, LANES), F32)],
        scratch_shapes=[pltpu.VMEM((SUBLANES, LANES), F32)],
        compiler_params=_cparams(("arbitrary",)),
        name="moe_route",
    )(x, g.reshape(1, d), wr, br)


def _plan_kernel(cnt_ref, meta_ref, dest_ref, ex_ref):
    cnt = cnt_ref[...]
    lane8 = lax.broadcasted_iota(I32, (SUBLANES, LANES), 1)
    padded = jnp.ceil(cnt / MOE_BLK) * MOE_BLK
    pend = padded
    s = 1
    while s < LANES:
        pend = pend + jnp.where(lane8 >= s, pltpu.roll(pend, s, 1), 0.0)
        s *= 2
    pstart = (pend - padded)[0:1, :]
    meta = meta_ref[...].astype(F32)
    tm = meta.shape[0]
    lane_i = lax.broadcasted_iota(I32, (tm, LANES), 1)
    lane = lane_i.astype(F32)
    col = lambda j: jnp.sum(jnp.where(lane_i == j, meta, 0.0), axis=1, keepdims=True)
    e1, e2, r1, r2 = col(0), col(1), col(2), col(3)
    d1 = jnp.sum(jnp.where(lane == e1, pstart, 0.0), axis=1, keepdims=True) + r1
    d2 = jnp.sum(jnp.where(lane == e2, pstart, 0.0), axis=1, keepdims=True) + r2
    dest_ref[...] = jnp.where(lane_i == 0, d1, jnp.where(lane_i == 1, d2, 0.0)).astype(I32)
    sub8 = lax.broadcasted_iota(I32, (SUBLANES, LANES), 0)
    zrow = jnp.floor((pend - padded + cnt) / SUBLANES) * SUBLANES
    ex = jnp.where(sub8 == 0, pend - padded, padded / MOE_BLK)
    ex = jnp.where(sub8 == 2, zrow, jnp.where(sub8 == 3, (pend - zrow) / SUBLANES, ex))
    ex_ref[...] = jnp.where(sub8 == 4, pend, ex).astype(I32)


def moe_plan(cnt, meta, *, tm=512):
    t = meta.shape[0]
    tm = min(tm, t)
    return pl.pallas_call(
        _plan_kernel,
        grid=(t // tm,),
        in_specs=[pl.BlockSpec((SUBLANES, LANES), lambda i: (0, 0)),
                  pl.BlockSpec((tm, LANES), lambda i: (i, 0))],
        out_specs=[pl.BlockSpec((tm, LANES), lambda i: (i, 0)),
                   pl.BlockSpec((SUBLANES, LANES), lambda i: (0, 0))],
        out_shape=[jax.ShapeDtypeStruct((t, LANES), I32),
                   jax.ShapeDtypeStruct((SUBLANES, LANES), I32)],
        compiler_params=_cparams(("arbitrary",)),
        name="moe_plan",
    )(cnt, meta)


DMA_UNROLL = 8
DISPATCH_SLOTS = 3


def _dispatch_kernel(dest_ref, zrow_ref, zcnt_ref, end_ref, h_hbm, xd_hbm, hbuf, zbuf, zblk, lsem, ssem, zsem,
                     *, tm, nsteps, nrows):
    i = pl.program_id(0)
    slot = i % DISPATCH_SLOTS

    def load(step, slot_):
        return pltpu.make_async_copy(h_hbm.at[pl.ds(step * tm, tm)], hbuf.at[slot_], lsem.at[slot_])

    def rows_done(slot_):
        return pltpu.make_async_copy(hbuf.at[slot_], xd_hbm.at[pl.ds(0, tm)], ssem.at[slot_])

    def zero_group(e, g):
        row = pl.multiple_of(zrow_ref[e] + g * SUBLANES, SUBLANES)
        return pltpu.make_async_copy(zbuf, xd_hbm.at[pl.ds(row, SUBLANES)], zsem)

    @pl.when(i == 0)
    def _():
        load(0, 0).start()
        if nsteps > 1:
            load(1, 1).start()
        zbuf[...] = jnp.zeros(zbuf.shape, zbuf.dtype)

        def per_expert(fn):
            def body(e, carry):
                lax.fori_loop(0, zcnt_ref[e], lambda g, c: (fn(zero_group(e, g)), c)[1], 0)
                return carry
            lax.fori_loop(0, N_EXPERTS, body, 0)

        per_expert(lambda cp: cp.start())
        per_expert(lambda cp: cp.wait())

        zblk[...] = jnp.zeros(zblk.shape, zblk.dtype)
        ntail = (nrows - end_ref[0]) // MOE_BLK

        def tail_block(j):
            row = pl.multiple_of(end_ref[0] + j * MOE_BLK, MOE_BLK)
            return pltpu.make_async_copy(zblk, xd_hbm.at[pl.ds(row, MOE_BLK)], zsem)

        lax.fori_loop(0, ntail, lambda j, c: (tail_block(j).start(), c)[1], 0)
        lax.fori_loop(0, ntail, lambda j, c: (tail_block(j).wait(), c)[1], 0)

    load(i, slot).wait()

    def start(t, carry):
        for k in range(2):
            pltpu.make_async_copy(hbuf.at[slot, pl.ds(t, 1)], xd_hbm.at[pl.ds(dest_ref[2 * t + k], 1)],
                                  ssem.at[slot]).start(priority=k)
        return carry

    lax.fori_loop(0, tm, start, 0, unroll=DMA_UNROLL)

    @pl.when(i >= 1)
    def _():
        prev = (i + DISPATCH_SLOTS - 1) % DISPATCH_SLOTS
        rows_done(prev).wait()
        rows_done(prev).wait()

    @pl.when(i + 2 < nsteps)
    def _():
        load(i + 2, (i + 2) % DISPATCH_SLOTS).start()

    @pl.when(i == nsteps - 1)
    def _():
        rows_done(slot).wait()
        rows_done(slot).wait()


def moe_dispatch(dest_flat, zrow, zcnt, end, h, nrows, *, tm=512):
    t, d = h.shape
    tm = min(tm, t)
    nsteps = t // tm
    kern = functools.partial(_dispatch_kernel, tm=tm, nsteps=nsteps, nrows=nrows)
    smem_all = pl.BlockSpec(memory_space=pltpu.SMEM)
    return pl.pallas_call(
        kern,
        grid=(nsteps,),
        in_specs=[pl.BlockSpec((2 * tm,), lambda i: (i,), memory_space=pltpu.SMEM), smem_all, smem_all, smem_all,
                  pl.BlockSpec(memory_space=pl.ANY)],
        out_specs=pl.BlockSpec(memory_space=pl.ANY),
        out_shape=jax.ShapeDtypeStruct((nrows, d), h.dtype),
        scratch_shapes=[pltpu.VMEM((DISPATCH_SLOTS, tm, d), h.dtype),
                        pltpu.VMEM((SUBLANES, d), h.dtype),
                        pltpu.VMEM((MOE_BLK, d), h.dtype),
                        pltpu.SemaphoreType.DMA((DISPATCH_SLOTS,)),
                        pltpu.SemaphoreType.DMA((DISPATCH_SLOTS,)),
                        pltpu.SemaphoreType.DMA],
        compiler_params=_cparams(("arbitrary",)),
        name="moe_dispatch",
    )(dest_flat, zrow, zcnt, end, h)


def _experts_kernel(row0_ref, nblk_ref, wg_ref, wu_ref, wd_ref, xd_hbm, yd_hbm,
                    wg_s, wu_s, wd_s, xbuf, ybuf, xsem, ysem, *, nblk_total):
    e = pl.program_id(0)
    nb = nblk_ref[e]
    row0 = row0_ref[e]

    def rows(j):
        return pl.ds(pl.multiple_of(row0 + j * MOE_BLK, MOE_BLK), MOE_BLK)

    def x_copy(j, slot):
        return pltpu.make_async_copy(xd_hbm.at[rows(j)], xbuf.at[slot], xsem.at[slot])

    def y_copy(j, slot):
        return pltpu.make_async_copy(ybuf.at[slot], yd_hbm.at[rows(j)], ysem.at[slot])

    def unpack(w):
        hi = pltpu.bitcast(w & jnp.uint32(0xFFFF0000), F32)
        lo = pltpu.bitcast(w << 16, F32)
        return jnp.concatenate([hi, lo], axis=1).astype(BF16)

    @pl.when(nb > 0)
    def _():
        x_copy(0, 0).start()
        wg_s[...] = wg_ref[0].astype(BF16)
        wu_s[...] = wu_ref[0].astype(BF16)
        wd_s[...] = wd_ref[0].astype(BF16)

    def block(j, carry):
        slot = j % 2

        @pl.when(j + 1 < nb)
        def _():
            x_copy(j + 1, 1 - slot).start()

        x_copy(j, slot).wait()

        @pl.when(j >= 2)
        def _():
            y_copy(j - 2, slot).wait()

        x = unpack(xbuf[slot])
        g = jnp.dot(x, wg_s[...], preferred_element_type=F32)
        u = jnp.dot(x, wu_s[...], preferred_element_type=F32)
        ybuf[slot] = jnp.dot((_silu(g) * u).astype(BF16), wd_s[...], preferred_element_type=F32)
        y_copy(j, slot).start()
        return carry

    lax.fori_loop(0, nb, block, 0)

    @pl.when(nb >= 2)
    def _():
        y_copy(nb - 2, nb % 2).wait()

    @pl.when(nb >= 1)
    def _():
        y_copy(nb - 1, (nb - 1) % 2).wait()

    @pl.when(e == N_EXPERTS - 1)
    def _():
        ntail = nblk_total - (row0 // MOE_BLK + nb)
        ybuf[0] = jnp.zeros(ybuf.shape[1:], F32)
        lax.fori_loop(0, ntail, lambda j, c: (y_copy(nb + j, 0).start(), c)[1], 0)
        lax.fori_loop(0, ntail, lambda j, c: (y_copy(nb + j, 0).wait(), c)[1], 0)


def moe_experts(row0, nblk, xd, w_gate, w_up, w_down, e0):
    d, f = w_gate.shape[1], w_gate.shape[2]
    kern = functools.partial(_experts_kernel, nblk_total=xd.shape[0] // MOE_BLK)
    grid_spec = pltpu.PrefetchScalarGridSpec(
        num_scalar_prefetch=2,
        grid=(N_EXPERTS,),
        in_specs=[pl.BlockSpec((1, d, f), lambda e, r, n: (e0 + e, 0, 0)),
                  pl.BlockSpec((1, d, f), lambda e, r, n: (e0 + e, 0, 0)),
                  pl.BlockSpec((1, f, d), lambda e, r, n: (e0 + e, 0, 0)),
                  pl.BlockSpec(memory_space=pl.ANY)],
        out_specs=pl.BlockSpec(memory_space=pl.ANY),
        scratch_shapes=[pltpu.VMEM((d, f), BF16), pltpu.VMEM((d, f), BF16), pltpu.VMEM((f, d), BF16),
                        pltpu.VMEM((2, MOE_BLK, d // 2), xd.dtype), pltpu.VMEM((2, MOE_BLK, d), F32),
                        pltpu.SemaphoreType.DMA((2,)), pltpu.SemaphoreType.DMA((2,))],
    )
    return pl.pallas_call(
        kern,
        grid_spec=grid_spec,
        out_shape=jax.ShapeDtypeStruct((xd.shape[0], d), F32),
        compiler_params=_cparams(("arbitrary",)),
        name="moe_experts",
    )(row0, nblk, w_gate, w_up, w_down, xd)


def _combine_kernel(dest_ref, dest_next_ref, x_ref, wts_ref, gf_ref, yd_hbm, o_ref, rbuf, sem,
                    *, tm, nsteps, final_norm):
    i = pl.program_id(0)
    slot = i % 2

    def gather(dref, slot_):
        def start(t, carry):
            for k in range(2):
                pltpu.make_async_copy(yd_hbm.at[pl.ds(dref[2 * t + k], 1)], rbuf.at[slot_, k, pl.ds(t, 1)],
                                      sem.at[slot_]).start(priority=k)
            return carry
        lax.fori_loop(0, tm, start, 0, unroll=DMA_UNROLL)

    @pl.when(i == 0)
    def _():
        gather(dest_ref, 0)

    @pl.when(i + 1 < nsteps)
    def _():
        gather(dest_next_ref, 1 - slot)

    for k in range(2):
        pltpu.make_async_copy(yd_hbm.at[pl.ds(0, tm)], rbuf.at[slot, k], sem.at[slot]).wait()
    w = wts_ref[...]
    y = x_ref[...] + rbuf[slot, 0] * w[:, 0:1] + rbuf[slot, 1] * w[:, 1:2]
    if final_norm:
        y = y * lax.rsqrt(jnp.mean(y * y, axis=-1, keepdims=True) + EPS) * gf_ref[...]
    o_ref[...] = y


def moe_combine(dest_flat, x, wts, yd, g_final, *, row0=0, rows=None, tm=512):
    t, d = (rows or x.shape[0]), x.shape[1]
    tm = min(tm, t)
    nsteps = t // tm
    blk0 = row0 // tm
    final_norm = g_final is not None
    gf = g_final.reshape(1, d) if final_norm else jnp.ones((1, d), F32)
    kern = functools.partial(_combine_kernel, tm=tm, nsteps=nsteps, final_norm=final_norm)
    return pl.pallas_call(
        kern,
        grid=(nsteps,),
        in_specs=[pl.BlockSpec((2 * tm,), lambda i: (i + blk0,), memory_space=pltpu.SMEM),
                  pl.BlockSpec((2 * tm,), lambda i: (jnp.minimum(i + 1, nsteps - 1) + blk0,),
                               memory_space=pltpu.SMEM),
                  pl.BlockSpec((tm, d), lambda i: (i + blk0, 0)),
                  pl.BlockSpec((tm, LANES), lambda i: (i + blk0, 0)),
                  pl.BlockSpec((1, d), lambda i: (0, 0)),
                  pl.BlockSpec(memory_space=pl.ANY)],
        out_specs=pl.BlockSpec((tm, d), lambda i: (i, 0)),
        out_shape=jax.ShapeDtypeStruct((t, d), F32),
        scratch_shapes=[pltpu.VMEM((2, 2, tm, d), F32), pltpu.SemaphoreType.DMA((2,))],
        compiler_params=_cparams(("arbitrary",)),
        name="moe_combine",
    )(dest_flat, dest_flat, x, wts, gf, yd)


def moe_block(x, g, wr, br, w_gate, w_up, w_down, e0, g_final=None, splits=None):
    t, d = x.shape
    nblk = (2 * t) // MOE_BLK + N_EXPERTS
    h, meta, wts, cnt = moe_route(x, g, wr, br)
    dest, ex = moe_plan(cnt, meta)
    dest_flat = dest[:, :2].reshape(2 * t)
    per_expert = lambda r: ex[r, R_E0:R_E0 + N_EXPERTS]
    end_last = ex[4, R_E0 + N_EXPERTS - 1:R_E0 + N_EXPERTS]
    xd = moe_dispatch(dest_flat, per_expert(2), per_expert(3), end_last, h, nblk * MOE_BLK)
    yd = moe_experts(per_expert(0), per_expert(1), xd, w_gate, w_up, w_down, e0)
    if splits is None:
        return moe_combine(dest_flat, x, wts, yd, g_final)
    return [moe_combine(dest_flat, x, wts, yd, g_final, row0=r0, rows=n) for r0, n in splits]


def _pad_rows(buf):
    return jnp.pad(buf, ((0, 0), (0, SUBLANES - buf.shape[1]), (0, 0)))


def _forward(x_long, x_short, grp_long, grp_short, wts):
    depth = grp_long["mem_k"].shape[0]
    bl, ll = grp_long["batch"], grp_long["seqlen"]
    bs, ls = grp_short["batch"], grp_short["seqlen"]
    t_long, t_short = bl * ll, bs * ls
    d = x_long.shape[1]
    new = {id(grp_long): dict(gdn=[], conv=[], sc=[], ret=[]), id(grp_short): dict(gdn=[], conv=[], sc=[], ret=[])}

    def record_even(grp, s_new, cq, cs):
        rec = new[id(grp)]
        rec["gdn"].append(s_new)
        rec["conv"].append(cq.reshape(grp["batch"], SUBLANES, W_QKV_A)[:, :CONV_A - 1])
        rec["sc"].append(cs.reshape(grp["batch"], SUBLANES, D_B)[:, :CONV_B - 1])

    x_all = None
    for layer in range(depth):
        src_long = x_long if x_all is None else x_all
        src_short, row0 = (x_short, 0) if x_all is None else (x_all, t_long)
        if layer % 2 == 0:
            i = layer // 2
            hist = lambda grp: (_pad_rows(grp["conv"][i]), _pad_rows(grp["sc"][i]), wts["w_conv_qkv"][i],
                                wts["w_conv_sc"][i], wts["gdn_prm"][i], wts["gdn_norm"][i], grp["gdn"][i])
            p = rms_matmul(src_short, wts["norm_mix"][layer], wts["w_in_a"][i], tn=768, rows=t_short, row0=row0)
            mix, s_new, cq, cs = gdn_core(p, *hist(grp_short), batch=bs, seqlen=ls)
            xs = matmul_res(mix, wts["w_out_a"][i], src_short, res_row0=row0)
            record_even(grp_short, s_new, cq, cs)
            xl, s_new, cq, cs = gdn_block(src_long, wts["norm_mix"][layer], wts["w_in_a"][i], wts["w_out_a"][i],
                                          *hist(grp_long), batch=bl, seqlen=ll)
            record_even(grp_long, s_new, cq, cs)
        else:
            j = layer // 2
            p = rms_matmul(src_short, wts["norm_mix"][layer], wts["w_in_c"][j], tn=768, rows=t_short, row0=row0)
            ret, r_new = ret_core(p, grp_short["pos"], wts["ret_norm"][j], grp_short["ret"][j], batch=bs, seqlen=ls)
            xs = matmul_res(ret, wts["w_out_c"][j], src_short, res_row0=row0)
            new[id(grp_short)]["ret"].append(r_new)
            xl, r_new = ret_block(src_long, wts["norm_mix"][layer], wts["w_in_c"][j], wts["w_out_c"][j],
                                  grp_long["pos"], wts["ret_norm"][j], grp_long["ret"][j], batch=bl, seqlen=ll)
            new[id(grp_long)]["ret"].append(r_new)
        q = rms_matmul(xs, wts["norm_x"][layer], wts["w_xq"][layer], tn=D_MODEL)
        att = xattn_core(q, grp_short["mem_k"], grp_short["mem_v"], layer, batch=bs, seqlen=ls)
        joint = jnp.zeros((t_long + t_short, d), F32) if x_all is None else x_all
        joint = matmul_res(att, wts["w_xo"][layer], xs, into=joint, out_row0=t_long)
        joint = xattn_block(xl, wts["norm_x"][layer], wts["w_xq"][layer], wts["w_xo"][layer],
                            grp_long["mem_k"], grp_long["mem_v"], layer, joint, batch=bl, seqlen=ll)
        last = layer == depth - 1
        x_all = moe_block(joint, wts["norm_ffn"][layer], wts["w_route"][layer], wts["b_route"][layer],
                          wts["w_exp_gate"], wts["w_exp_up"], wts["w_exp_down"], layer * N_EXPERTS,
                          g_final=wts["norm_final"] if last else None,
                          splits=[(0, t_long), (t_long, t_short)] if last else None)
    y_long, y_short = x_all
    stack = lambda grp: tuple(jnp.stack(new[id(grp)][k]) for k in ("gdn", "conv", "sc", "ret"))
    return (y_long,) + stack(grp_long), (y_short,) + stack(grp_short)


def kernel(x_prompt, x_sample, state_gdn, state_gdn_conv, state_sconv, state_ret, cache_mem_k, cache_mem_v, mem_prompt, norm_mix, norm_x, norm_ffn, norm_final, norm_mem, w_in_a, w_conv_qkv, a_log, dt_bias, gdn_norm, w_conv_sc, w_out_a, w_in_c, ret_norm, w_out_c, w_xq, w_xk, w_xv, w_xo, w_group, b_group, w_router, b_router, w_exp_gate, w_exp_up, w_exp_down):
    bp, lp, d = x_prompt.shape
    bs, ls, _ = x_sample.shape
    depth = norm_mix.shape[0]
    n_even = w_in_a.shape[0]
    n_mem = mem_prompt.shape[1]

    qkv_w = 2 * H_A * DK_A + H_A * DV_A
    o_z = qkv_w
    o_b = o_z + H_A * DV_A
    o_a = o_b + H_A
    o_sc = o_a + H_A
    w_a = jnp.concatenate([w_in_a[:, :, :o_b], w_in_a[:, :, o_sc:], w_in_a[:, :, o_b:o_sc],
                           jnp.zeros((n_even, d, PA_COLS - PA_BA - 2 * H_A), F32)], axis=-1).astype(BF16)
    prm = jnp.zeros((n_even, SUBLANES, LANES), F32)
    prm = prm.at[:, 0, H_A:2 * H_A].set(a_log).at[:, 1, H_A:2 * H_A].set(dt_bias)
    w_route = jnp.concatenate([w_group, w_router, jnp.zeros((depth, d, LANES - N_GROUPS - N_EXPERTS), F32)],
                              axis=-1).astype(BF16)
    b_route = jnp.concatenate([b_group, b_router, jnp.zeros((depth, LANES - N_GROUPS - N_EXPERTS), F32)],
                              axis=-1).reshape(depth, 1, LANES)
    wts = dict(norm_mix=norm_mix, norm_x=norm_x, norm_ffn=norm_ffn, norm_final=norm_final,
               w_in_a=w_a, w_conv_qkv=w_conv_qkv, gdn_prm=prm, gdn_norm=gdn_norm, w_conv_sc=w_conv_sc,
               w_out_a=w_out_a.astype(BF16), w_in_c=w_in_c.astype(BF16), ret_norm=ret_norm,
               w_out_c=w_out_c.astype(BF16), w_xq=w_xq.astype(BF16), w_xo=w_xo.astype(BF16),
               w_route=w_route, b_route=b_route,
               w_exp_gate=w_exp_gate.reshape((depth * N_EXPERTS,) + w_exp_gate.shape[2:]),
               w_exp_up=w_exp_up.reshape((depth * N_EXPERTS,) + w_exp_up.shape[2:]),
               w_exp_down=w_exp_down.reshape((depth * N_EXPERTS,) + w_exp_down.shape[2:]))

    memf = mem_prompt.reshape(bp * n_mem, d)
    w_kv = jnp.concatenate([w_xk, w_xv], axis=-1).astype(BF16)
    mk_p, mv_p = [], []
    for layer in range(depth):
        kv = rms_matmul(memf, norm_mem[layer], w_kv[layer], tn=d)
        mk_p.append(kv[:, :d])
        mv_p.append(kv[:, d:])
    p_cache_mem_k = jnp.stack(mk_p).reshape(depth, bp, n_mem, H_X, HD_X)
    p_cache_mem_v = jnp.stack(mv_p).reshape(depth, bp, n_mem, H_X, HD_X)

    n_odd = w_in_c.shape[0]
    z_gdn = jnp.zeros((n_even, bp, H_A, DK_A, DV_A), F32)
    z_conv = jnp.zeros((n_even, bp, CONV_A - 1, qkv_w), F32)
    z_sc = jnp.zeros((n_even, bp, CONV_B - 1, D_B), F32)
    z_ret = jnp.zeros((n_odd, bp, H_C, DK_C, DV_C), F32)
    pos_p = jnp.arange(lp, dtype=I32)
    pos_s = 16384 + jnp.arange(ls, dtype=I32)

    grp_p = dict(batch=bp, seqlen=lp, pos=pos_p, gdn=z_gdn, conv=z_conv, sc=z_sc, ret=z_ret,
                 mem_k=p_cache_mem_k, mem_v=p_cache_mem_v)
    grp_s = dict(batch=bs, seqlen=ls, pos=pos_s, gdn=state_gdn, conv=state_gdn_conv, sc=state_sconv,
                 ret=state_ret, mem_k=cache_mem_k, mem_v=cache_mem_v)
    (y_p, p_gdn, p_conv, p_sc, p_ret), (y_s, s_gdn, s_conv, s_sc, s_ret) = _forward(
        x_prompt.reshape(bp * lp, d), x_sample.reshape(bs * ls, d), grp_p, grp_s, wts)
    return (y_p.reshape(bp, lp, d), y_s.reshape(bs, ls, d), p_gdn, p_conv, p_sc, p_ret, p_cache_mem_k,
            p_cache_mem_v, s_gdn, s_conv, s_sc, s_ret)
```

```python
import functools
import math

import jax
import jax.numpy as jnp
import numpy as np
from jax import lax
from jax.experimental import pallas as pl
from jax.experimental.pallas import tpu as pltpu

F32 = jnp.float32
BF16 = jnp.bfloat16
I32 = jnp.int32

EPS = 1e-6
ROPE_BASE = 10000.0

D_MODEL = 1024
H_A, DK_A, DV_A, CONV_A = 4, 128, 128, 4
W_QKV_A = 3 * H_A * DK_A
D_B, CONV_B = D_MODEL // 2, 3
H_C, DK_C, DV_C = 4, 256, 512
H_X, HD_X, N_MEM = 4, 256, 256
N_GROUPS, E_PER_GROUP, N_EXPERTS, D_EXPERT = 4, 8, 32, 512
GDN_CHUNK = 64

LANES = 128
SUBLANES = 8
GDN_STACK = 256
VMEM_LIMIT = 56 * 1024 * 1024

PA_COLS = 3840
PA_SC = 2048
PA_BA = 3584
R_E0 = N_GROUPS
MOE_BLK = 256


def _cparams(sem):
    return pltpu.CompilerParams(dimension_semantics=sem, vmem_limit_bytes=VMEM_LIMIT)


def _dot(a, b, trans_a=False, trans_b=False):
    dn = (((0 if trans_a else 1,), (1 if trans_b else 0,)), ((), ()))
    return lax.dot_general(a.astype(BF16), b.astype(BF16), dn, preferred_element_type=F32)


def _silu(x):
    return x * (1.0 / (1.0 + jnp.exp(-x)))


def _sigmoid(x):
    return 1.0 / (1.0 + jnp.exp(-x))


def _rms_matmul_kernel(x_ref, g_ref, w_ref, o_ref, xn_ref):
    @pl.when(pl.program_id(1) == 0)
    def _():
        x = x_ref[...]
        ms = jnp.mean(x * x, axis=-1, keepdims=True)
        xn_ref[...] = (x * lax.rsqrt(ms + EPS) * g_ref[...]).astype(BF16)

    o_ref[...] = jnp.dot(xn_ref[...], w_ref[...], preferred_element_type=F32).astype(o_ref.dtype)


def rms_matmul(x, g, w, *, tn, rows=None, row0=0, out_dtype=F32, tm=1024):
    t, d = (rows or x.shape[0]), x.shape[1]
    n = w.shape[1]
    tm = min(tm, t)
    blk0 = row0 // tm
    return pl.pallas_call(
        _rms_matmul_kernel,
        grid=(t // tm, n // tn),
        in_specs=[pl.BlockSpec((tm, d), lambda i, j: (i + blk0, 0)),
                  pl.BlockSpec((1, d), lambda i, j: (0, 0)),
                  pl.BlockSpec((d, tn), lambda i, j: (0, j))],
        out_specs=pl.BlockSpec((tm, tn), lambda i, j: (i, j)),
        out_shape=jax.ShapeDtypeStruct((t, n), out_dtype),
        scratch_shapes=[pltpu.VMEM((tm, d), BF16)],
        compiler_params=_cparams(("parallel", "arbitrary")),
        name="rms_matmul",
    )(x, g.reshape(1, d), w)


def _kv_proj_kernel(x_ref, g_ref, w_ref, k_ref, v_ref):
    x = x_ref[...]
    ms = jnp.mean(x * x, axis=-1, keepdims=True)
    xn = (x * lax.rsqrt(ms + EPS) * g_ref[0]).astype(BF16)
    d = x.shape[1]
    k_ref[0] = jnp.dot(xn, w_ref[0, :, :d], preferred_element_type=F32)
    v_ref[0] = jnp.dot(xn, w_ref[0, :, d:], preferred_element_type=F32)


def kv_proj(x, g, w_kv, *, tm=1024):
    t, d = x.shape
    depth = g.shape[0]
    tm = min(tm, t)
    out_spec = pl.BlockSpec((1, tm, d), lambda l, i: (l, i, 0))
    return pl.pallas_call(
        _kv_proj_kernel,
        grid=(depth, t // tm),
        in_specs=[pl.BlockSpec((tm, d), lambda l, i: (i, 0)),
                  pl.BlockSpec((1, 1, d), lambda l, i: (l, 0, 0)),
                  pl.BlockSpec((1, d, 2 * d), lambda l, i: (l, 0, 0))],
        out_specs=[out_spec, out_spec],
        out_shape=[jax.ShapeDtypeStruct((depth, t, d), F32), jax.ShapeDtypeStruct((depth, t, d), F32)],
        compiler_params=_cparams(("parallel", "parallel")),
        name="kv_proj",
    )(x, g.reshape(depth, 1, d), w_kv)


def _matmul_res_kernel(a_ref, w_ref, r_ref, *rest):
    o_ref = rest[-1]
    o_ref[...] = r_ref[...] + jnp.dot(a_ref[...].astype(BF16), w_ref[...], preferred_element_type=F32)


def matmul_res(a, w, res, *, res_row0=0, into=None, out_row0=0, tm=512):
    t, k = a.shape
    n = w.shape[1]
    tm = min(tm, t)
    rblk, oblk = res_row0 // tm, out_row0 // tm
    in_specs = [pl.BlockSpec((tm, k), lambda i: (i, 0)),
                pl.BlockSpec((k, n), lambda i: (0, 0)),
                pl.BlockSpec((tm, n), lambda i: (i + rblk, 0))]
    args = (a, w, res)
    if into is not None:
        in_specs.append(pl.BlockSpec(memory_space=pl.ANY))
        args += (into,)
    return pl.pallas_call(
        _matmul_res_kernel,
        grid=(t // tm,),
        in_specs=in_specs,
        out_specs=pl.BlockSpec((tm, n), lambda i: (i + oblk, 0)),
        out_shape=jax.ShapeDtypeStruct((t, n) if into is None else into.shape, F32),
        input_output_aliases={} if into is None else {3: 0},
        compiler_params=_cparams(("parallel",)),
        name="matmul_res",
    )(*args)


def _causal_conv(x, hist, w_ref, width, seq8):
    r = x.shape[0]
    taps = [w_ref[j:j + 1, :] for j in range(width)]

    def head(x8, h8):
        n = x8.shape[0]
        t = lax.broadcasted_iota(I32, (n, 1), 0) % SUBLANES
        y = taps[width - 1] * x8
        for s in range(1, width):
            prev = pltpu.roll(h8, (n + s - (width - 1)) % n, 0) if s != width - 1 else h8
            y = y + taps[width - 1 - s] * jnp.where(t >= s, pltpu.roll(x8, s, 0), prev)
        return y

    if seq8:
        return head(x, hist)
    y = taps[width - 1] * x
    for s in range(1, width):
        y = y + taps[width - 1 - s] * pltpu.roll(x, s, 0)
    return jnp.concatenate([head(x[:SUBLANES], hist), y[SUBLANES:]], axis=0)


def _unit_lower_inverse(ms, c, ri, ci):
    base = min(c, 16)
    eye = jnp.where(ri == ci, 1.0, 0.0).astype(F32)
    blk = (ri // base) == (ci // base)
    ds = [jnp.where(blk, m, 0.0) for m in ms]
    ps = [eye - d for d in ds]
    k = 2
    while k < base:
        ds = [_dot(d, d) for d in ds]
        ps = [_dot(p, eye + d) for p, d in zip(ps, ds)]
        k *= 2
    s = base
    while s < c:
        sel = ((ri // (2 * s)) == (ci // (2 * s))) & ((ri // s) != (ci // s))
        ts = [_dot(jnp.where(sel, m, 0.0), p) for m, p in zip(ms, ps)]
        ps = [p - _dot(p, t) for p, t in zip(ps, ts)]
        s *= 2
    return ps


def _gdn_prepare(units, c):
    n = GDN_STACK
    ri = lax.broadcasted_iota(I32, (n, n), 0)
    ci = lax.broadcasted_iota(I32, (n, n), 1)
    same = (ri // c) == (ci // c)
    incl = same & (ri >= ci)
    strict = same & (ri > ci)
    pre = []
    for q, k, v, bfull, gfull in units:
        g2 = jnp.concatenate([gfull, gfull], axis=1)
        g_row = jnp.sum(jnp.where(ri == ci, g2, 0.0), axis=0, keepdims=True)
        gc_col = jnp.sum(jnp.where(incl, g_row, 0.0), axis=1, keepdims=True)
        gc_row = jnp.sum(jnp.where(same & (ri <= ci), g2, 0.0), axis=0, keepdims=True)
        gl_col = jnp.sum(jnp.where(same, g_row, 0.0), axis=1, keepdims=True)
        decay = jnp.where(incl, jnp.exp(jnp.where(incl, gc_col - gc_row, 0.0)), 0.0)
        egc = jnp.exp(gc_col)
        kb = k * bfull
        pre.append(dict(decay=decay, kb=kb, rhs=jnp.concatenate([v * bfull, kb * egc], axis=1),
                        qd=q * egc, kd=k * jnp.exp(gl_col - gc_col), egl=jnp.exp(gl_col)))
    mms = [jnp.where(strict, _dot(e["kb"], u[1], trans_b=True) * e["decay"], 0.0) for e, u in zip(pre, units)]
    qks = [_dot(u[0], u[1], trans_b=True) * e["decay"] for e, u in zip(pre, units)]
    tinvs = _unit_lower_inverse(mms, c, ri, ci)
    uws = [_dot(t, e["rhs"]) for t, e in zip(tinvs, pre)]
    return [dict(u=uw[:, :DV_A], w=uw[:, DV_A:], qk=qk, qd=e["qd"], kd=e["kd"], egl=e["egl"])
            for uw, qk, e in zip(uws, qks, pre)]


def _gdn_recur(e, states, c):
    nprob = GDN_STACK // c
    ws, qs = [], []
    for p in range(nprob):
        sl = slice(p * c, (p + 1) * c)
        ws.append(_dot(e["w"][sl], states[p]))
        qs.append(_dot(e["qd"][sl], states[p]))
    vn = e["u"] - jnp.concatenate(ws, axis=0)
    o = _dot(e["qk"], vn) + jnp.concatenate(qs, axis=0)
    new_states = []
    for p in range(nprob):
        sl = slice(p * c, (p + 1) * c)
        new_states.append(states[p] * e["egl"][p * c:p * c + 1, :] + _dot(e["kd"][sl], vn[sl], trans_a=True))
    return o, new_states


def _gdn_compute(x, z, u_sc, scb, ba, hq, hs, wq_ref, ws_ref, prm_ref, gn, states, *, seq8, c, nu):
    unit = GDN_CHUNK
    xc = _silu(_causal_conv(x, hq, wq_ref, CONV_A, seq8))
    yb = scb * _causal_conv(u_sc, hs, ws_ref, CONV_B, seq8)
    beta_all = _sigmoid(ba)
    sp = jnp.maximum(ba + prm_ref[1:2, :], 0.0) + jnp.log1p(jnp.exp(-jnp.abs(ba + prm_ref[1:2, :])))
    g_all = -jnp.exp(prm_ref[0:1, :]) * sp

    def head_cols(a, base):
        return a[:, base * DK_A:(base + 1) * DK_A]

    cat = lambda xs: jnp.concatenate(xs, axis=0)
    units = []
    for ui in range(nu):
        rs = slice(ui * unit, (ui + 1) * unit)
        qs, ks, vs, bs, gs = [], [], [], [], []
        for h in range(H_A):
            qh = head_cols(xc, h)[rs]
            kh = head_cols(xc, H_A + h)[rs]
            qs.append(qh * lax.rsqrt(jnp.sum(qh * qh, axis=-1, keepdims=True) + EPS) * (DK_A ** -0.5))
            ks.append(kh * lax.rsqrt(jnp.sum(kh * kh, axis=-1, keepdims=True) + EPS))
            vs.append(head_cols(xc, 2 * H_A + h)[rs])
            bs.append(jnp.broadcast_to(beta_all[rs, h:h + 1], (unit, LANES)))
            gs.append(jnp.broadcast_to(g_all[rs, H_A + h:H_A + h + 1], (unit, LANES)))
        units.append((cat(qs), cat(ks), cat(vs), cat(bs), cat(gs)))
    prepared = _gdn_prepare(units, c)

    outs = []
    for ui in range(nu):
        rs = slice(ui * unit, (ui + 1) * unit)
        o, states = _gdn_recur(prepared[ui], states, c)
        zst = cat([z[rs, h * DV_A:(h + 1) * DV_A] for h in range(H_A)])
        ms = jnp.mean(o * o, axis=-1, keepdims=True)
        og = o * lax.rsqrt(ms + EPS) * gn * _silu(zst)
        outs.append(jnp.concatenate([og[h * unit:(h + 1) * unit] for h in range(H_A)], axis=1))
    o_all = outs[0] if nu == 1 else cat(outs)
    return jnp.concatenate([o_all, yb], axis=1), states


def _gdn_kernel(qkv_ref, z_ref, sch_ref, scb_ref, scc_ref, ba_ref, hq_ref, hs_ref, wq_ref, ws_ref,
                prm_ref, gn_ref, s0_ref, o_ref, sn_ref, cq_ref, cs_ref):
    x = qkv_ref[...]
    rows = x.shape[0]
    u_sc = scc_ref[...] * sch_ref[...]
    cq_ref[...] = pltpu.roll(x, rows - SUBLANES + CONV_A - 1, 0)
    cs_ref[...] = pltpu.roll(u_sc, rows - SUBLANES + CONV_B - 1, 0)
    nprob = GDN_STACK // SUBLANES
    states = [s0_ref[p % SUBLANES, p // SUBLANES] for p in range(nprob)]
    mix, states = _gdn_compute(x, z_ref[...], u_sc, scb_ref[...], ba_ref[...], hq_ref[...], hs_ref[...],
                               wq_ref, ws_ref, prm_ref, gn_ref[...], states, seq8=True, c=SUBLANES, nu=1)
    o_ref[...] = mix.astype(o_ref.dtype)
    for p in range(nprob):
        sn_ref[p % SUBLANES, p // SUBLANES] = states[p]


def _gdn_block_kernel(x_ref, g_ref, wi_ref, wo_ref, hq_ref, hs_ref, wq_ref, ws_ref, prm_ref, gn_ref, s0_ref,
                      o_ref, sn_ref, cq_ref, cs_ref, s_scr, hq_scr, hs_scr, *, nu, nl):
    l = pl.program_id(1)

    @pl.when(l == 0)
    def _():
        hq_scr[...] = hq_ref[0]
        hs_scr[...] = hs_ref[0]
        s_scr[...] = s0_ref[0]

    xres = x_ref[...]
    rows = xres.shape[0]
    ms = jnp.mean(xres * xres, axis=-1, keepdims=True)
    xn = (xres * lax.rsqrt(ms + EPS) * g_ref[...]).astype(BF16)
    proj = lambda lo, width: jnp.dot(xn, wi_ref[:, lo:lo + width], preferred_element_type=F32)
    x = proj(0, W_QKV_A)
    z = proj(W_QKV_A, H_A * DV_A)
    u_sc = proj(PA_SC + 2 * D_B, D_B) * proj(PA_SC, D_B)
    scb = proj(PA_SC + D_B, D_B)
    ba = proj(PA_BA, LANES)
    hq = hq_scr[...]
    hs = hs_scr[...]
    hq_scr[...] = pltpu.roll(x[rows - SUBLANES:], CONV_A - 1, 0)
    hs_scr[...] = pltpu.roll(u_sc[rows - SUBLANES:], CONV_B - 1, 0)
    states = [s_scr[p] for p in range(H_A)]
    mix, states = _gdn_compute(x, z, u_sc, scb, ba, hq, hs, wq_ref, ws_ref, prm_ref, gn_ref[...], states,
                               seq8=False, c=GDN_CHUNK, nu=nu)
    for p in range(H_A):
        s_scr[p] = states[p]
    o_ref[...] = xres + jnp.dot(mix.astype(BF16), wo_ref[...], preferred_element_type=F32)

    @pl.when(l == nl - 1)
    def _():
        sn_ref[0] = s_scr[...]
        cq_ref[0] = hq_scr[...]
        cs_ref[0] = hs_scr[...]


def gdn_core(p, hist_q, hist_s, w_conv_qkv, w_conv_sc, prm, gn, s0, *, batch, seqlen):
    t = batch * seqlen
    assert seqlen == SUBLANES
    rows = GDN_CHUNK
    nb = rows // seqlen
    hq_spec = pl.BlockSpec((rows, W_QKV_A), lambda i: (i, 0))
    hs_spec = pl.BlockSpec((rows, D_B), lambda i: (i, 0))
    s_spec = pl.BlockSpec((nb, H_A, DK_A, DV_A), lambda i: (i, 0, 0, 0))
    col = lambda width, blk: pl.BlockSpec((rows, width), lambda i: (i, blk))
    const = lambda shape: pl.BlockSpec(shape, lambda i: (0,) * len(shape))
    return pl.pallas_call(
        _gdn_kernel,
        grid=(batch // nb,),
        in_specs=[col(W_QKV_A, 0), col(D_B, 3), col(D_B, 4), col(D_B, 5), col(D_B, 6),
                  col(LANES, PA_BA // LANES), hq_spec, hs_spec,
                  const((CONV_A, W_QKV_A)), const((CONV_B, D_B)), const((SUBLANES, LANES)),
                  const((1, DV_A)), s_spec],
        out_specs=[pl.BlockSpec((rows, D_MODEL), lambda i: (i, 0)), s_spec, hq_spec, hs_spec],
        out_shape=[jax.ShapeDtypeStruct((t, D_MODEL), BF16),
                   jax.ShapeDtypeStruct((batch, H_A, DK_A, DV_A), F32),
                   jax.ShapeDtypeStruct((t, W_QKV_A), F32),
                   jax.ShapeDtypeStruct((t, D_B), F32)],
        compiler_params=_cparams(("parallel",)),
        name="gdn_core",
    )(p, p, p, p, p, p, hist_q.reshape(t, W_QKV_A), hist_s.reshape(t, D_B), w_conv_qkv, w_conv_sc, prm,
      gn.reshape(1, DV_A), s0)


def _resident(shape):
    return pl.BlockSpec(shape, lambda *ix: (0,) * len(shape), pipeline_mode=pl.Buffered(1))


def gdn_block(x, g, w_in, w_out, hist_q, hist_s, w_conv_qkv, w_conv_sc, prm, gn, s0, *, batch, seqlen):
    t, d = batch * seqlen, x.shape[1]
    rows = min(seqlen, 256)
    nu = rows // GDN_CHUNK
    nl = seqlen // rows
    per_b = lambda shape: pl.BlockSpec((1,) + shape, lambda b, l: (b,) + (0,) * len(shape))
    row_spec = pl.BlockSpec((rows, d), lambda b, l: (b * nl + l, 0))
    kern = functools.partial(_gdn_block_kernel, nu=nu, nl=nl)
    return pl.pallas_call(
        kern,
        grid=(batch, nl),
        in_specs=[row_spec, _resident((1, d)), _resident(w_in.shape), _resident(w_out.shape),
                  per_b((SUBLANES, W_QKV_A)), per_b((SUBLANES, D_B)),
                  _resident((CONV_A, W_QKV_A)), _resident((CONV_B, D_B)), _resident((SUBLANES, LANES)),
                  _resident((1, DV_A)), per_b((H_A, DK_A, DV_A))],
        out_specs=[row_spec, per_b((H_A, DK_A, DV_A)), per_b((SUBLANES, W_QKV_A)), per_b((SUBLANES, D_B))],
        out_shape=[jax.ShapeDtypeStruct((t, d), F32),
                   jax.ShapeDtypeStruct((batch, H_A, DK_A, DV_A), F32),
                   jax.ShapeDtypeStruct((batch, SUBLANES, W_QKV_A), F32),
                   jax.ShapeDtypeStruct((batch, SUBLANES, D_B), F32)],
        scratch_shapes=[pltpu.VMEM((H_A, DK_A, DV_A), F32),
                        pltpu.VMEM((SUBLANES, W_QKV_A), F32),
                        pltpu.VMEM((SUBLANES, D_B), F32)],
        compiler_params=_cparams(("parallel", "arbitrary")),
        name="gdn_block",
    )(x, g.reshape(1, d), w_in, w_out, hist_q, hist_s, w_conv_qkv, w_conv_sc, prm, gn.reshape(1, DV_A), s0)


def _ret_compute(get_q, get_k, get_v, get_gate, cos, sin, dm_ref, qd_ref, kd_ref, cd_ref, rn_ref, r_scr,
                 *, nseq, c):
    half = DK_C // 2

    def rot(x):
        x1, x2 = x[:, :half], x[:, half:]
        return jnp.concatenate([x1 * cos - x2 * sin, x1 * sin + x2 * cos], axis=1)

    heads = range(H_C)
    qs = [rot(get_q(h)) for h in heads]
    ks = [rot(get_k(h)) * (DK_C ** -0.5) for h in heads]
    vs = [get_v(h).astype(BF16) for h in heads]
    ss = [_dot(qs[h], ks[h], trans_b=True) * dm_ref[h] for h in heads]
    inters = []
    for h in heads:
        qdh = qs[h] * qd_ref[h]
        parts = [_dot(qdh[sq * c:(sq + 1) * c], r_scr[sq, h]) for sq in range(nseq)]
        inters.append(parts[0] if nseq == 1 else jnp.concatenate(parts, axis=0))
    outs = [_dot(ss[h], vs[h]) + inters[h] for h in heads]
    for h in heads:
        kdh = ks[h] * kd_ref[h]
        cd = cd_ref[h][0:1, 0:1]
        for sq in range(nseq):
            sl = slice(sq * c, (sq + 1) * c)
            r_scr[sq, h] = r_scr[sq, h] * cd + _dot(kdh[sl], vs[h][sl], trans_a=True)
    gated = []
    for h in heads:
        o = outs[h]
        ms = jnp.mean(o * o, axis=-1, keepdims=True)
        on = o * lax.rsqrt(ms + EPS) * rn_ref[:, h * DV_C:(h + 1) * DV_C]
        gated.append(_silu(get_gate(h)) * on)
    return gated


def _ret_kernel(q_ref, k_ref, v_ref, gate_ref, cos_ref, sin_ref, dm_ref, qd_ref, kd_ref, cd_ref,
                rn_ref, r0_ref, o_ref, rnew_ref, r_scr, *, nseq, c, nl):
    l = pl.program_id(1)

    @pl.when(l == 0)
    def _():
        r_scr[...] = r0_ref[...]

    gated = _ret_compute(lambda h: q_ref[:, h * DK_C:(h + 1) * DK_C], lambda h: k_ref[:, h * DK_C:(h + 1) * DK_C],
                         lambda h: v_ref[:, h * DV_C:(h + 1) * DV_C], lambda h: gate_ref[:, h * DV_C:(h + 1) * DV_C],
                         cos_ref[...], sin_ref[...], dm_ref, qd_ref, kd_ref, cd_ref, rn_ref, r_scr, nseq=nseq, c=c)
    for h in range(H_C):
        o_ref[:, h * DV_C:(h + 1) * DV_C] = gated[h].astype(o_ref.dtype)

    @pl.when(l == nl - 1)
    def _():
        rnew_ref[...] = r_scr[...]


def _ret_block_kernel(x_ref, g_ref, wi_ref, wo_ref, cos_ref, sin_ref, dm_ref, qd_ref, kd_ref, cd_ref,
                      rn_ref, r0_ref, o_ref, rnew_ref, r_scr, *, c, nl):
    l = pl.program_id(1)

    @pl.when(l == 0)
    def _():
        r_scr[...] = r0_ref[...]

    xres = x_ref[...]
    ms = jnp.mean(xres * xres, axis=-1, keepdims=True)
    xn = (xres * lax.rsqrt(ms + EPS) * g_ref[...]).astype(BF16)
    proj = lambda lo, width: jnp.dot(xn, wi_ref[:, lo:lo + width], preferred_element_type=F32)
    hk, hv = H_C * DK_C, H_C * DV_C
    gated = _ret_compute(lambda h: proj(h * DK_C, DK_C), lambda h: proj(hk + h * DK_C, DK_C),
                         lambda h: proj(2 * hk + h * DV_C, DV_C), lambda h: proj(2 * hk + hv + h * DV_C, DV_C),
                         cos_ref[...], sin_ref[...], dm_ref, qd_ref, kd_ref, cd_ref, rn_ref, r_scr, nseq=1, c=c)
    y = xres
    for h in range(H_C):
        y = y + jnp.dot(gated[h].astype(BF16), wo_ref[h * DV_C:(h + 1) * DV_C, :], preferred_element_type=F32)
    o_ref[...] = y

    @pl.when(l == nl - 1)
    def _():
        rnew_ref[...] = r_scr[...]


def ret_core(p, pos, ret_norm, r0, *, batch, seqlen):
    t = batch * seqlen
    assert seqlen == SUBLANES
    nseq, c = 2, seqlen
    rows = nseq * c
    cos, sin, dmat, qd, kd, cd = _ret_tables(pos, nseq, c)
    const = lambda shape: pl.BlockSpec(shape, lambda b, l: (0,) * len(shape))
    kern = functools.partial(_ret_kernel, nseq=nseq, c=c, nl=1)
    hk = H_C * DK_C
    hv = H_C * DV_C
    return pl.pallas_call(
        kern,
        grid=(batch // nseq, 1),
        in_specs=[pl.BlockSpec((rows, hk), lambda b, l: (b, 0)),
                  pl.BlockSpec((rows, hk), lambda b, l: (b, 1)),
                  pl.BlockSpec((rows, hv), lambda b, l: (b, 1)),
                  pl.BlockSpec((rows, hv), lambda b, l: (b, 2)),
                  const((rows, DK_C // 2)), const((rows, DK_C // 2)),
                  const((H_C, rows, rows)), const((H_C, rows, DK_C)), const((H_C, rows, DK_C)),
                  const((H_C, SUBLANES, LANES)), const((1, hv)),
                  pl.BlockSpec((nseq, H_C, DK_C, DV_C), lambda b, l: (b, 0, 0, 0))],
        out_specs=[pl.BlockSpec((rows, hv), lambda b, l: (b, 0)),
                   pl.BlockSpec((nseq, H_C, DK_C, DV_C), lambda b, l: (b, 0, 0, 0))],
        out_shape=[jax.ShapeDtypeStruct((t, hv), BF16),
                   jax.ShapeDtypeStruct((batch, H_C, DK_C, DV_C), F32)],
        scratch_shapes=[pltpu.VMEM((nseq, H_C, DK_C, DV_C), F32)],
        compiler_params=_cparams(("parallel", "arbitrary")),
        name="ret_core",
    )(p, p, p, p, cos, sin, dmat, qd, kd, cd, ret_norm.reshape(1, hv), r0)


def _ret_tables(pos, nseq, c):
    half = DK_C // 2
    inv = ROPE_BASE ** (-jnp.arange(half, dtype=F32) / half)
    ang = pos.astype(F32)[:, None] * inv[None, :]
    cos, sin = jnp.cos(ang), jnp.sin(ang)
    if nseq > 1:
        cos, sin = jnp.tile(cos, (nseq, 1)), jnp.tile(sin, (nseq, 1))
    lg = jnp.log(1.0 - 2.0 ** (-5.0 - jnp.arange(H_C, dtype=F32)))[:, None]
    i = jnp.arange(c, dtype=F32)
    incl = i[:, None] >= i[None, :]
    dmat = jnp.exp(jnp.where(incl[None], (i[:, None] - i[None, :])[None] * lg[..., None], -jnp.inf))
    if nseq > 1:
        dmat = jnp.kron(jnp.eye(nseq, dtype=F32)[None], dmat)
    qd = jnp.tile(jnp.exp((i + 1.0)[None] * lg), (1, nseq))[..., None] * jnp.ones((1, 1, DK_C), F32)
    kd = jnp.tile(jnp.exp((c - 1.0 - i)[None] * lg), (1, nseq))[..., None] * jnp.ones((1, 1, DK_C), F32)
    cd = jnp.exp(c * lg)[..., None] * jnp.ones((1, SUBLANES, LANES), F32)
    return cos, sin, dmat, qd, kd, cd


def ret_block(x, g, w_in, w_out, pos, ret_norm, r0, *, batch, seqlen):
    t, d = batch * seqlen, x.shape[1]
    c = min(seqlen, 256)
    nl = seqlen // c
    cos, sin, dmat, qd, kd, cd = _ret_tables(pos, 1, c)
    hv = H_C * DV_C
    row_spec = pl.BlockSpec((c, d), lambda b, l: (b * nl + l, 0))
    trig_spec = pl.BlockSpec((c, DK_C // 2), lambda b, l: (l, 0))
    state_spec = pl.BlockSpec((1, H_C, DK_C, DV_C), lambda b, l: (b, 0, 0, 0))
    kern = functools.partial(_ret_block_kernel, c=c, nl=nl)
    return pl.pallas_call(
        kern,
        grid=(batch, nl),
        in_specs=[row_spec, _resident((1, d)), _resident(w_in.shape), _resident(w_out.shape),
                  trig_spec, trig_spec,
                  _resident((H_C, c, c)), _resident((H_C, c, DK_C)), _resident((H_C, c, DK_C)),
                  _resident((H_C, SUBLANES, LANES)), _resident((1, hv)), state_spec],
        out_specs=[row_spec, state_spec],
        out_shape=[jax.ShapeDtypeStruct((t, d), F32),
                   jax.ShapeDtypeStruct((batch, H_C, DK_C, DV_C), F32)],
        scratch_shapes=[pltpu.VMEM((1, H_C, DK_C, DV_C), F32)],
        compiler_params=_cparams(("parallel", "arbitrary")),
        name="ret_block",
    )(x, g.reshape(1, d), w_in, w_out, cos, sin, dmat, qd, kd, cd, ret_norm.reshape(1, hv), r0)


def _xattn_fetch(mk_hbm, mv_hbm, kbuf, vbuf, sem, *, layer, nb, nsteps):
    i = pl.program_id(0)
    l = pl.program_id(1)
    slot = i % 2

    def copies(step, slot_):
        out = []
        for b in range(nb):
            for h in range(H_X):
                out.append(pltpu.make_async_copy(mk_hbm.at[layer, step * nb + b, :, h, :],
                                                 kbuf.at[slot_, b, h], sem.at[slot_]))
                out.append(pltpu.make_async_copy(mv_hbm.at[layer, step * nb + b, :, h, :],
                                                 vbuf.at[slot_, b, h], sem.at[slot_]))
        return out

    @pl.when(l == 0)
    def _():
        @pl.when(i == 0)
        def _():
            for c in copies(i, slot):
                c.start()

        @pl.when(i + 1 < nsteps)
        def _():
            for c in copies(i + 1, 1 - slot):
                c.start()

        for c in copies(i, slot):
            c.wait()

    return slot


def _xattn_heads(q_of, kbuf, vbuf, slot, probs):
    ss = [_dot(q_of(b, h), kbuf[slot, b, h], trans_b=True) * (HD_X ** -0.5) for b, h in probs]
    ps = []
    for s in ss:
        e = jnp.exp(s - jnp.max(s, axis=-1, keepdims=True))
        ps.append(e / jnp.sum(e, axis=-1, keepdims=True))
    return [_dot(p, vbuf[slot, b, h]) for p, (b, h) in zip(ps, probs)]


def _xattn_kernel(q_ref, mk_hbm, mv_hbm, o_ref, kbuf, vbuf, sem, *, layer, nb, lq, nsteps):
    slot = _xattn_fetch(mk_hbm, mv_hbm, kbuf, vbuf, sem, layer=layer, nb=nb, nsteps=nsteps)
    probs = [(b, h) for b in range(nb) for h in range(H_X)]
    win = lambda b, h: (slice(b * lq, (b + 1) * lq), slice(h * HD_X, (h + 1) * HD_X))
    outs = _xattn_heads(lambda b, h: q_ref[win(b, h)], kbuf, vbuf, slot, probs)
    for o, (b, h) in zip(outs, probs):
        o_ref[win(b, h)] = o.astype(o_ref.dtype)


def _xattn_block_kernel(x_ref, g_ref, wq_ref, wo_ref, mk_hbm, mv_hbm, into_hbm, o_ref, kbuf, vbuf, sem,
                        *, layer, nsteps):
    del into_hbm
    slot = _xattn_fetch(mk_hbm, mv_hbm, kbuf, vbuf, sem, layer=layer, nb=1, nsteps=nsteps)
    xres = x_ref[...]
    ms = jnp.mean(xres * xres, axis=-1, keepdims=True)
    xn = (xres * lax.rsqrt(ms + EPS) * g_ref[...]).astype(BF16)
    probs = [(0, h) for h in range(H_X)]
    qs = [jnp.dot(xn, wq_ref[:, h * HD_X:(h + 1) * HD_X], preferred_element_type=F32) for h in range(H_X)]
    outs = _xattn_heads(lambda b, h: qs[h], kbuf, vbuf, slot, probs)
    y = xres
    for h in range(H_X):
        y = y + jnp.dot(outs[h].astype(BF16), wo_ref[h * HD_X:(h + 1) * HD_X, :], preferred_element_type=F32)
    o_ref[...] = y


def xattn_core(q, mk, mv, layer, *, batch, seqlen):
    t = batch * seqlen
    d = H_X * HD_X
    assert seqlen == SUBLANES
    nb, lq = 2, seqlen
    rows = nb * lq
    nsteps = batch // nb
    kern = functools.partial(_xattn_kernel, layer=layer, nb=nb, lq=lq, nsteps=nsteps)
    return pl.pallas_call(
        kern,
        grid=(nsteps, 1),
        in_specs=[pl.BlockSpec((rows, d), lambda b, l: (b, 0)),
                  pl.BlockSpec(memory_space=pl.ANY), pl.BlockSpec(memory_space=pl.ANY)],
        out_specs=pl.BlockSpec((rows, d), lambda b, l: (b, 0)),
        out_shape=jax.ShapeDtypeStruct((t, d), q.dtype),
        scratch_shapes=[pltpu.VMEM((2, nb, H_X, N_MEM, HD_X), F32),
                        pltpu.VMEM((2, nb, H_X, N_MEM, HD_X), F32),
                        pltpu.SemaphoreType.DMA((2,))],
        compiler_params=_cparams(("arbitrary", "arbitrary")),
        name="xattn_core",
    )(q, mk, mv)


def xattn_block(x, g, w_q, w_o, mk, mv, layer, into, *, batch, seqlen):
    d = x.shape[1]
    lq = min(seqlen, 512)
    nl = seqlen // lq
    row_spec = pl.BlockSpec((lq, d), lambda b, l: (b * nl + l, 0))
    kern = functools.partial(_xattn_block_kernel, layer=layer, nsteps=batch)
    return pl.pallas_call(
        kern,
        grid=(batch, nl),
        in_specs=[row_spec, _resident((1, d)), _resident(w_q.shape), _resident(w_o.shape),
                  pl.BlockSpec(memory_space=pl.ANY), pl.BlockSpec(memory_space=pl.ANY),
                  pl.BlockSpec(memory_space=pl.ANY)],
        out_specs=row_spec,
        out_shape=jax.ShapeDtypeStruct(into.shape, F32),
        scratch_shapes=[pltpu.VMEM((2, 1, H_X, N_MEM, HD_X), F32),
                        pltpu.VMEM((2, 1, H_X, N_MEM, HD_X), F32),
                        pltpu.SemaphoreType.DMA((2,))],
        input_output_aliases={6: 0},
        compiler_params=_cparams(("arbitrary", "arbitrary")),
        name="xattn_block",
    )(x, g.reshape(1, d), w_q, w_o, mk, mv, into)


def _route_kernel(x_ref, g_ref, wr_ref, br_ref, h_ref, meta_ref, wts_ref, cnt_ref, cnt_scr):
    i = pl.program_id(0)

    @pl.when(i == 0)
    def _():
        cnt_scr[...] = jnp.zeros_like(cnt_scr)

    x = x_ref[...]
    tm = x.shape[0]
    ms = jnp.mean(x * x, axis=-1, keepdims=True)
    h = x * lax.rsqrt(ms + EPS) * g_ref[...]
    hb = h.astype(BF16)
    bits = pltpu.bitcast(hb.astype(F32), jnp.uint32)
    half_d = bits.shape[1] // 2
    h_ref[...] = bits[:, :half_d] | (bits[:, half_d:] >> 16)
    logits = jnp.dot(hb, wr_ref[...], preferred_element_type=F32) + br_ref[...]
    lane_i = lax.broadcasted_iota(I32, (tm, LANES), 1)
    lane = lane_i.astype(F32)
    neg = jnp.float32(-3.0e38)
    big = jnp.float32(LANES)
    is_g = lane_i < N_GROUPS
    gl = jnp.where(is_g, logits, neg)
    gmax = jnp.max(gl, axis=1, keepdims=True)
    grp = jnp.min(jnp.where(gl == gmax, lane, big), axis=1, keepdims=True)
    gsum = jnp.sum(jnp.where(is_g, jnp.exp(jnp.where(is_g, logits - gmax, 0.0)), 0.0), axis=1, keepdims=True)
    p_grp = 1.0 / gsum
    in_grp = ((lane_i >= R_E0) & (lane_i < R_E0 + N_EXPERTS)
              & (jnp.floor((lane - R_E0) * (1.0 / E_PER_GROUP)) == grp))
    el = jnp.where(in_grp, logits, neg)
    m1 = jnp.max(el, axis=1, keepdims=True)
    i1 = jnp.min(jnp.where(el == m1, lane, big), axis=1, keepdims=True)
    el2 = jnp.where(lane == i1, neg, el)
    m2 = jnp.max(el2, axis=1, keepdims=True)
    i2 = jnp.min(jnp.where(el2 == m2, lane, big), axis=1, keepdims=True)
    esum = jnp.sum(jnp.where(in_grp, jnp.exp(jnp.where(in_grp, logits - m1, 0.0)), 0.0), axis=1, keepdims=True)
    p1 = 1.0 / esum
    p2 = jnp.exp(m2 - m1) / esum
    tot = p1 + p2
    w1 = p_grp * (p1 / tot)
    w2 = p_grp * (p2 / tot)
    wts_ref[...] = jnp.where(lane_i == 0, w1, jnp.where(lane_i == 1, w2, 0.0))

    oh1 = (lane == i1).astype(F32)
    oh2 = (lane == i2).astype(F32)
    rr = lax.broadcasted_iota(I32, (tm, tm), 0)
    cc = lax.broadcasted_iota(I32, (tm, tm), 1)
    tri = (rr > cc).astype(BF16)
    base = cnt_scr[0:1, :]
    c1 = jnp.sum(oh1, axis=0, keepdims=True)
    c2 = jnp.sum(oh2, axis=0, keepdims=True)
    r1 = jnp.sum(oh1 * (jnp.dot(tri, oh1.astype(BF16), preferred_element_type=F32) + base), axis=1, keepdims=True)
    r2 = jnp.sum(oh2 * (jnp.dot(tri, oh2.astype(BF16), preferred_element_type=F32) + base + c1), axis=1, keepdims=True)
    new_cnt = base + c1 + c2
    cnt_scr[...] = jnp.broadcast_to(new_cnt, cnt_scr.shape)
    cnt_ref[...] = jnp.broadcast_to(new_cnt, cnt_ref.shape)
    meta = jnp.where(lane_i == 0, i1, jnp.where(lane_i == 1, i2, 0.0))
    meta = jnp.where(lane_i == 2, r1, jnp.where(lane_i == 3, r2, meta))
    meta_ref[...] = meta.astype(I32)


def moe_route(x, g, wr, br, *, tm=256):
    t, d = x.shape
    tm = min(tm, t)
    return pl.pallas_call(
        _route_kernel,
        grid=(t // tm,),
        in_specs=[pl.BlockSpec((tm, d), lambda i: (i, 0)),
                  pl.BlockSpec((1, d), lambda i: (0, 0)),
                  pl.BlockSpec((d, LANES), lambda i: (0, 0)),
                  pl.BlockSpec((1, LANES), lambda i: (0, 0))],
        out_specs=[pl.BlockSpec((tm, d // 2), lambda i: (i, 0)),
                   pl.BlockSpec((tm, LANES), lambda i: (i, 0)),
                   pl.BlockSpec((tm, LANES), lambda i: (i, 0)),
                   pl.BlockSpec((SUBLANES, LANES), lambda i: (0, 0))],
        out_shape=[jax.ShapeDtypeStruct((t, d // 2), jnp.uint32),
                   jax.ShapeDtypeStruct((t, LANES), I32),
                   jax.ShapeDtypeStruct((t, LANES), F32),
                   jax.ShapeDtypeStruct((SUBLANES, LANES), F32)],
        scratch_shapes=[pltpu.VMEM((SUBLANES, LANES), F32)],
        compiler_params=_cparams(("arbitrary",)),
        name="moe_route",
    )(x, g.reshape(1, d), wr, br)


def _plan_kernel(cnt_ref, meta_ref, dest_ref, ex_ref, be_ref, *, nblk_pad):
    cnt = cnt_ref[...]
    lane8 = lax.broadcasted_iota(I32, (SUBLANES, LANES), 1)
    padded = jnp.ceil(cnt / MOE_BLK) * MOE_BLK
    pend = padded
    s = 1
    while s < LANES:
        pend = pend + jnp.where(lane8 >= s, pltpu.roll(pend, s, 1), 0.0)
        s *= 2
    pstart = (pend - padded)[0:1, :]
    meta = meta_ref[...].astype(F32)
    tm = meta.shape[0]
    lane_i = lax.broadcasted_iota(I32, (tm, LANES), 1)
    lane = lane_i.astype(F32)
    col = lambda j: jnp.sum(jnp.where(lane_i == j, meta, 0.0), axis=1, keepdims=True)
    e1, e2, r1, r2 = col(0), col(1), col(2), col(3)
    d1 = jnp.sum(jnp.where(lane == e1, pstart, 0.0), axis=1, keepdims=True) + r1
    d2 = jnp.sum(jnp.where(lane == e2, pstart, 0.0), axis=1, keepdims=True) + r2
    dest_ref[...] = jnp.where(lane_i == 0, d1, jnp.where(lane_i == 1, d2, 0.0)).astype(I32)
    sub8 = lax.broadcasted_iota(I32, (SUBLANES, LANES), 0)
    zrow = jnp.floor((pend - padded + cnt) / SUBLANES) * SUBLANES
    ex = jnp.where(sub8 == 0, pend - padded, padded / MOE_BLK)
    ex = jnp.where(sub8 == 2, zrow, jnp.where(sub8 == 3, (pend - zrow) / SUBLANES, ex))
    ex_ref[...] = jnp.where(sub8 == 4, pend, ex).astype(I32)
    bi = (lax.broadcasted_iota(I32, (nblk_pad, LANES), 0) * MOE_BLK).astype(F32)
    lane_b = lax.broadcasted_iota(I32, (nblk_pad, LANES), 1)
    is_e = (lane_b >= R_E0) & (lane_b < R_E0 + N_EXPERTS)
    nfull = jnp.sum(jnp.where(is_e & (bi >= pend[0:1, :]), 1.0, 0.0), axis=1, keepdims=True)
    nused = jnp.max(pend[0:1, :], axis=1, keepdims=True) / MOE_BLK
    row = lax.broadcasted_iota(I32, (nblk_pad, LANES), 0)
    be_ref[...] = jnp.where(row == nblk_pad - 1, nused, jnp.minimum(nfull, N_EXPERTS - 1.0)).astype(I32)


def moe_plan(cnt, meta, *, nblk_pad):
    t = meta.shape[0]
    tm = math.gcd(t, 1024)
    kern = functools.partial(_plan_kernel, nblk_pad=nblk_pad)
    return pl.pallas_call(
        kern,
        grid=(t // tm,),
        in_specs=[pl.BlockSpec((SUBLANES, LANES), lambda i: (0, 0)),
                  pl.BlockSpec((tm, LANES), lambda i: (i, 0))],
        out_specs=[pl.BlockSpec((tm, LANES), lambda i: (i, 0)),
                   pl.BlockSpec((SUBLANES, LANES), lambda i: (0, 0)),
                   pl.BlockSpec((nblk_pad, LANES), lambda i: (0, 0))],
        out_shape=[jax.ShapeDtypeStruct((t, LANES), I32),
                   jax.ShapeDtypeStruct((SUBLANES, LANES), I32),
                   jax.ShapeDtypeStruct((nblk_pad, LANES), I32)],
        compiler_params=_cparams(("arbitrary",)),
        name="moe_plan",
    )(cnt, meta)


DMA_UNROLL = 8
DISPATCH_SLOTS = 3


def _dispatch_kernel(dest_ref, zrow_ref, zcnt_ref, end_ref, h_hbm, xd_hbm, hbuf, zbuf, zblk, lsem, ssem, zsem,
                     *, tm, nsteps, nrows):
    i = pl.program_id(0)
    slot = i % DISPATCH_SLOTS

    def load(step, slot_):
        return pltpu.make_async_copy(h_hbm.at[pl.ds(step * tm, tm)], hbuf.at[slot_], lsem.at[slot_])

    def rows_done(slot_):
        return pltpu.make_async_copy(hbuf.at[slot_], xd_hbm.at[pl.ds(0, tm)], ssem.at[slot_])

    def zero_group(e, g):
        row = pl.multiple_of(zrow_ref[e] + g * SUBLANES, SUBLANES)
        return pltpu.make_async_copy(zbuf, xd_hbm.at[pl.ds(row, SUBLANES)], zsem)

    @pl.when(i == 0)
    def _():
        load(0, 0).start()
        if nsteps > 1:
            load(1, 1).start()
        zbuf[...] = jnp.zeros(zbuf.shape, zbuf.dtype)

        def per_expert(fn):
            def body(e, carry):
                lax.fori_loop(0, zcnt_ref[e], lambda g, c: (fn(zero_group(e, g)), c)[1], 0)
                return carry
            lax.fori_loop(0, N_EXPERTS, body, 0)

        per_expert(lambda cp: cp.start())
        per_expert(lambda cp: cp.wait())

        zblk[...] = jnp.zeros(zblk.shape, zblk.dtype)
        ntail = (nrows - end_ref[0]) // MOE_BLK

        def tail_block(j):
            row = pl.multiple_of(end_ref[0] + j * MOE_BLK, MOE_BLK)
            return pltpu.make_async_copy(zblk, xd_hbm.at[pl.ds(row, MOE_BLK)], zsem)

        lax.fori_loop(0, ntail, lambda j, c: (tail_block(j).start(), c)[1], 0)
        lax.fori_loop(0, ntail, lambda j, c: (tail_block(j).wait(), c)[1], 0)

    load(i, slot).wait()

    def start(t, carry):
        for k in range(2):
            pltpu.make_async_copy(hbuf.at[slot, pl.ds(t, 1)], xd_hbm.at[pl.ds(dest_ref[2 * t + k], 1)],
                                  ssem.at[slot]).start(priority=k)
        return carry

    lax.fori_loop(0, tm, start, 0, unroll=DMA_UNROLL)

    @pl.when(i >= 1)
    def _():
        prev = (i + DISPATCH_SLOTS - 1) % DISPATCH_SLOTS
        rows_done(prev).wait()
        rows_done(prev).wait()

    @pl.when(i + 2 < nsteps)
    def _():
        load(i + 2, (i + 2) % DISPATCH_SLOTS).start()

    @pl.when(i == nsteps - 1)
    def _():
        rows_done(slot).wait()
        rows_done(slot).wait()


def moe_dispatch(dest_flat, zrow, zcnt, end, h, nrows, *, tm=512):
    t, d = h.shape
    tm = min(tm, t)
    nsteps = t // tm
    kern = functools.partial(_dispatch_kernel, tm=tm, nsteps=nsteps, nrows=nrows)
    smem_all = pl.BlockSpec(memory_space=pltpu.SMEM)
    return pl.pallas_call(
        kern,
        grid=(nsteps,),
        in_specs=[pl.BlockSpec((2 * tm,), lambda i: (i,), memory_space=pltpu.SMEM), smem_all, smem_all, smem_all,
                  pl.BlockSpec(memory_space=pl.ANY)],
        out_specs=pl.BlockSpec(memory_space=pl.ANY),
        out_shape=jax.ShapeDtypeStruct((nrows, d), h.dtype),
        scratch_shapes=[pltpu.VMEM((DISPATCH_SLOTS, tm, d), h.dtype),
                        pltpu.VMEM((SUBLANES, d), h.dtype),
                        pltpu.VMEM((MOE_BLK, d), h.dtype),
                        pltpu.SemaphoreType.DMA((DISPATCH_SLOTS,)),
                        pltpu.SemaphoreType.DMA((DISPATCH_SLOTS,)),
                        pltpu.SemaphoreType.DMA],
        compiler_params=_cparams(("arbitrary",)),
        name="moe_dispatch",
    )(dest_flat, zrow, zcnt, end, h)


def _experts_kernel(be_ref, x_ref, wg_ref, wu_ref, wd_ref, o_ref, wg_s, wu_s, wd_s, *, nblk_pad):
    i = pl.program_id(0)
    nused = be_ref[nblk_pad - 1]

    @pl.when(i < nused)
    def _():
        prev = be_ref[jnp.maximum(i - 1, 0)]

        @pl.when((i == 0) | (be_ref[i] != prev))
        def _():
            wg_s[...] = wg_ref[0].astype(BF16)
            wu_s[...] = wu_ref[0].astype(BF16)
            wd_s[...] = wd_ref[0].astype(BF16)

        half = MOE_BLK // 2

        def unpack(w):
            hi = pltpu.bitcast(w & jnp.uint32(0xFFFF0000), F32)
            lo = pltpu.bitcast(w << 16, F32)
            return jnp.concatenate([hi, lo], axis=1).astype(BF16)

        xs = [unpack(x_ref[r * half:(r + 1) * half, :]) for r in range(2)]
        gs = [jnp.dot(x, wg_s[...], preferred_element_type=F32) for x in xs]
        us = [jnp.dot(x, wu_s[...], preferred_element_type=F32) for x in xs]
        acts = [(_silu(g) * u).astype(BF16) for g, u in zip(gs, us)]
        for r in range(2):
            o_ref[r * half:(r + 1) * half, :] = jnp.dot(acts[r], wd_s[...], preferred_element_type=F32)

    @pl.when(i >= nused)
    def _():
        o_ref[...] = jnp.zeros_like(o_ref)


def moe_experts(be_flat, xd, w_gate, w_up, w_down, e0, *, nblk, nblk_pad):
    d, f = w_gate.shape[1], w_gate.shape[2]
    kern = functools.partial(_experts_kernel, nblk_pad=nblk_pad)
    grid_spec = pltpu.PrefetchScalarGridSpec(
        num_scalar_prefetch=1,
        grid=(nblk,),
        in_specs=[pl.BlockSpec((MOE_BLK, d // 2), lambda i, be: (i, 0)),
                  pl.BlockSpec((1, d, f), lambda i, be: (e0 + be[i], 0, 0)),
                  pl.BlockSpec((1, d, f), lambda i, be: (e0 + be[i], 0, 0)),
                  pl.BlockSpec((1, f, d), lambda i, be: (e0 + be[i], 0, 0))],
        out_specs=pl.BlockSpec((MOE_BLK, d), lambda i, be: (i, 0)),
        scratch_shapes=[pltpu.VMEM((d, f), BF16), pltpu.VMEM((d, f), BF16), pltpu.VMEM((f, d), BF16)],
    )
    return pl.pallas_call(
        kern,
        grid_spec=grid_spec,
        out_shape=jax.ShapeDtypeStruct((xd.shape[0], d), F32),
        compiler_params=_cparams(("arbitrary",)),
        name="moe_experts",
    )(be_flat, xd, w_gate, w_up, w_down)


def _combine_kernel(dest_ref, dest_next_ref, x_ref, wts_ref, gf_ref, yd_hbm, o_ref, rbuf, sem,
                    *, tm, nsteps, final_norm):
    i = pl.program_id(0)
    slot = i % 2

    def gather(dref, slot_):
        def start(t, carry):
            for k in range(2):
                pltpu.make_async_copy(yd_hbm.at[pl.ds(dref[2 * t + k], 1)], rbuf.at[slot_, k, pl.ds(t, 1)],
                                      sem.at[slot_]).start(priority=k)
            return carry
        lax.fori_loop(0, tm, start, 0, unroll=DMA_UNROLL)

    @pl.when(i == 0)
    def _():
        gather(dest_ref, 0)

    @pl.when(i + 1 < nsteps)
    def _():
        gather(dest_next_ref, 1 - slot)

    for k in range(2):
        pltpu.make_async_copy(yd_hbm.at[pl.ds(0, tm)], rbuf.at[slot, k], sem.at[slot]).wait()
    w = wts_ref[...]
    y = x_ref[...] + rbuf[slot, 0] * w[:, 0:1] + rbuf[slot, 1] * w[:, 1:2]
    if final_norm:
        y = y * lax.rsqrt(jnp.mean(y * y, axis=-1, keepdims=True) + EPS) * gf_ref[...]
    o_ref[...] = y


def moe_combine(dest_flat, x, wts, yd, g_final, *, row0=0, rows=None, tm=512):
    t, d = (rows or x.shape[0]), x.shape[1]
    tm = min(tm, t)
    nsteps = t // tm
    blk0 = row0 // tm
    final_norm = g_final is not None
    gf = g_final.reshape(1, d) if final_norm else jnp.ones((1, d), F32)
    kern = functools.partial(_combine_kernel, tm=tm, nsteps=nsteps, final_norm=final_norm)
    return pl.pallas_call(
        kern,
        grid=(nsteps,),
        in_specs=[pl.BlockSpec((2 * tm,), lambda i: (i + blk0,), memory_space=pltpu.SMEM),
                  pl.BlockSpec((2 * tm,), lambda i: (jnp.minimum(i + 1, nsteps - 1) + blk0,),
                               memory_space=pltpu.SMEM),
                  pl.BlockSpec((tm, d), lambda i: (i + blk0, 0)),
                  pl.BlockSpec((tm, LANES), lambda i: (i + blk0, 0)),
                  pl.BlockSpec((1, d), lambda i: (0, 0)),
                  pl.BlockSpec(memory_space=pl.ANY)],
        out_specs=pl.BlockSpec((tm, d), lambda i: (i, 0)),
        out_shape=jax.ShapeDtypeStruct((t, d), F32),
        scratch_shapes=[pltpu.VMEM((2, 2, tm, d), F32), pltpu.SemaphoreType.DMA((2,))],
        compiler_params=_cparams(("arbitrary",)),
        name="moe_combine",
    )(dest_flat, dest_flat, x, wts, gf, yd)


def moe_block(x, g, wr, br, w_gate, w_up, w_down, e0, g_final=None, splits=None):
    t, d = x.shape
    nblk = (2 * t) // MOE_BLK + N_EXPERTS
    nblk_pad = -(-(nblk + 1) // SUBLANES) * SUBLANES
    h, meta, wts, cnt = moe_route(x, g, wr, br)
    dest, ex, be = moe_plan(cnt, meta, nblk_pad=nblk_pad)
    dest_flat = dest[:, :2].reshape(2 * t)
    per_expert = lambda r: ex[r, R_E0:R_E0 + N_EXPERTS]
    end_last = ex[4, R_E0 + N_EXPERTS - 1:R_E0 + N_EXPERTS]
    xd = moe_dispatch(dest_flat, per_expert(2), per_expert(3), end_last, h, nblk * MOE_BLK)
    yd = moe_experts(be[:, 0], xd, w_gate, w_up, w_down, e0, nblk=nblk, nblk_pad=nblk_pad)
    if splits is None:
        return moe_combine(dest_flat, x, wts, yd, g_final)
    return [moe_combine(dest_flat, x, wts, yd, g_final, row0=r0, rows=n) for r0, n in splits]


def _pad_rows(buf):
    return jnp.pad(buf, ((0, 0), (0, SUBLANES - buf.shape[1]), (0, 0)))


def _forward(x_long, x_short, grp_long, grp_short, wts):
    depth = grp_long["mem_k"].shape[0]
    bl, ll = grp_long["batch"], grp_long["seqlen"]
    bs, ls = grp_short["batch"], grp_short["seqlen"]
    t_long, t_short = bl * ll, bs * ls
    d = x_long.shape[1]
    new = {id(grp_long): dict(gdn=[], conv=[], sc=[], ret=[]), id(grp_short): dict(gdn=[], conv=[], sc=[], ret=[])}

    def record_even(grp, s_new, cq, cs):
        rec = new[id(grp)]
        rec["gdn"].append(s_new)
        rec["conv"].append(cq.reshape(grp["batch"], SUBLANES, W_QKV_A)[:, :CONV_A - 1])
        rec["sc"].append(cs.reshape(grp["batch"], SUBLANES, D_B)[:, :CONV_B - 1])

    x_all = None
    for layer in range(depth):
        src_long = x_long if x_all is None else x_all
        src_short, row0 = (x_short, 0) if x_all is None else (x_all, t_long)
        if layer % 2 == 0:
            i = layer // 2
            hist = lambda grp: (_pad_rows(grp["conv"][i]), _pad_rows(grp["sc"][i]), wts["w_conv_qkv"][i],
                                wts["w_conv_sc"][i], wts["gdn_prm"][i], wts["gdn_norm"][i], grp["gdn"][i])
            p = rms_matmul(src_short, wts["norm_mix"][layer], wts["w_in_a"][i], tn=768, rows=t_short, row0=row0)
            mix, s_new, cq, cs = gdn_core(p, *hist(grp_short), batch=bs, seqlen=ls)
            xs = matmul_res(mix, wts["w_out_a"][i], src_short, res_row0=row0)
            record_even(grp_short, s_new, cq, cs)
            xl, s_new, cq, cs = gdn_block(src_long, wts["norm_mix"][layer], wts["w_in_a"][i], wts["w_out_a"][i],
                                          *hist(grp_long), batch=bl, seqlen=ll)
            record_even(grp_long, s_new, cq, cs)
        else:
            j = layer // 2
            p = rms_matmul(src_short, wts["norm_mix"][layer], wts["w_in_c"][j], tn=768, rows=t_short, row0=row0)
            ret, r_new = ret_core(p, grp_short["pos"], wts["ret_norm"][j], grp_short["ret"][j], batch=bs, seqlen=ls)
            xs = matmul_res(ret, wts["w_out_c"][j], src_short, res_row0=row0)
            new[id(grp_short)]["ret"].append(r_new)
            xl, r_new = ret_block(src_long, wts["norm_mix"][layer], wts["w_in_c"][j], wts["w_out_c"][j],
                                  grp_long["pos"], wts["ret_norm"][j], grp_long["ret"][j], batch=bl, seqlen=ll)
            new[id(grp_long)]["ret"].append(r_new)
        q = rms_matmul(xs, wts["norm_x"][layer], wts["w_xq"][layer], tn=D_MODEL)
        att = xattn_core(q, grp_short["mem_k"], grp_short["mem_v"], layer, batch=bs, seqlen=ls)
        joint = jnp.zeros((t_long + t_short, d), F32) if x_all is None else x_all
        joint = matmul_res(att, wts["w_xo"][layer], xs, into=joint, out_row0=t_long)
        joint = xattn_block(xl, wts["norm_x"][layer], wts["w_xq"][layer], wts["w_xo"][layer],
                            grp_long["mem_k"], grp_long["mem_v"], layer, joint, batch=bl, seqlen=ll)
        last = layer == depth - 1
        x_all = moe_block(joint, wts["norm_ffn"][layer], wts["w_route"][layer], wts["b_route"][layer],
                          wts["w_exp_gate"], wts["w_exp_up"], wts["w_exp_down"], layer * N_EXPERTS,
                          g_final=wts["norm_final"] if last else None,
                          splits=[(0, t_long), (t_long, t_short)] if last else None)
    y_long, y_short = x_all
    stack = lambda grp: tuple(jnp.stack(new[id(grp)][k]) for k in ("gdn", "conv", "sc", "ret"))
    return (y_long,) + stack(grp_long), (y_short,) + stack(grp_short)


def kernel(x_prompt, x_sample, state_gdn, state_gdn_conv, state_sconv, state_ret, cache_mem_k, cache_mem_v, mem_prompt, norm_mix, norm_x, norm_ffn, norm_final, norm_mem, w_in_a, w_conv_qkv, a_log, dt_bias, gdn_norm, w_conv_sc, w_out_a, w_in_c, ret_norm, w_out_c, w_xq, w_xk, w_xv, w_xo, w_group, b_group, w_router, b_router, w_exp_gate, w_exp_up, w_exp_down):
    bp, lp, d = x_prompt.shape
    bs, ls, _ = x_sample.shape
    depth = norm_mix.shape[0]
    n_even = w_in_a.shape[0]
    n_mem = mem_prompt.shape[1]

    qkv_w = 2 * H_A * DK_A + H_A * DV_A
    o_z = qkv_w
    o_b = o_z + H_A * DV_A
    o_a = o_b + H_A
    o_sc = o_a + H_A
    w_a = jnp.concatenate([w_in_a[:, :, :o_b], w_in_a[:, :, o_sc:], w_in_a[:, :, o_b:o_sc],
                           jnp.zeros((n_even, d, PA_COLS - PA_BA - 2 * H_A), F32)], axis=-1).astype(BF16)
    prm = jnp.zeros((n_even, SUBLANES, LANES), F32)
    prm = prm.at[:, 0, H_A:2 * H_A].set(a_log).at[:, 1, H_A:2 * H_A].set(dt_bias)
    w_route = jnp.concatenate([w_group, w_router, jnp.zeros((depth, d, LANES - N_GROUPS - N_EXPERTS), F32)],
                              axis=-1).astype(BF16)
    b_route = jnp.concatenate([b_group, b_router, jnp.zeros((depth, LANES - N_GROUPS - N_EXPERTS), F32)],
                              axis=-1).reshape(depth, 1, LANES)
    wts = dict(norm_mix=norm_mix, norm_x=norm_x, norm_ffn=norm_ffn, norm_final=norm_final,
               w_in_a=w_a, w_conv_qkv=w_conv_qkv, gdn_prm=prm, gdn_norm=gdn_norm, w_conv_sc=w_conv_sc,
               w_out_a=w_out_a.astype(BF16), w_in_c=w_in_c.astype(BF16), ret_norm=ret_norm,
               w_out_c=w_out_c.astype(BF16), w_xq=w_xq.astype(BF16), w_xo=w_xo.astype(BF16),
               w_route=w_route, b_route=b_route,
               w_exp_gate=w_exp_gate.reshape((depth * N_EXPERTS,) + w_exp_gate.shape[2:]),
               w_exp_up=w_exp_up.reshape((depth * N_EXPERTS,) + w_exp_up.shape[2:]),
               w_exp_down=w_exp_down.reshape((depth * N_EXPERTS,) + w_exp_down.shape[2:]))

    memf = mem_prompt.reshape(bp * n_mem, d)
    w_kv = jnp.concatenate([w_xk, w_xv], axis=-1).astype(BF16)
    mk_p, mv_p = kv_proj(memf, norm_mem, w_kv)
    p_cache_mem_k = mk_p.reshape(depth, bp, n_mem, H_X, HD_X)
    p_cache_mem_v = mv_p.reshape(depth, bp, n_mem, H_X, HD_X)

    n_odd = w_in_c.shape[0]
    z_gdn = jnp.zeros((n_even, bp, H_A, DK_A, DV_A), F32)
    z_conv = jnp.zeros((n_even, bp, CONV_A - 1, qkv_w), F32)
    z_sc = jnp.zeros((n_even, bp, CONV_B - 1, D_B), F32)
    z_ret = jnp.zeros((n_odd, bp, H_C, DK_C, DV_C), F32)
    pos_p = jnp.arange(lp, dtype=I32)
    pos_s = 16384 + jnp.arange(ls, dtype=I32)

    grp_p = dict(batch=bp, seqlen=lp, pos=pos_p, gdn=z_gdn, conv=z_conv, sc=z_sc, ret=z_ret,
                 mem_k=p_cache_mem_k, mem_v=p_cache_mem_v)
    grp_s = dict(batch=bs, seqlen=ls, pos=pos_s, gdn=state_gdn, conv=state_gdn_conv, sc=state_sconv,
                 ret=state_ret, mem_k=cache_mem_k, mem_v=cache_mem_v)
    (y_p, p_gdn, p_conv, p_sc, p_ret), (y_s, s_gdn, s_conv, s_sc, s_ret) = _forward(
        x_prompt.reshape(bp * lp, d), x_sample.reshape(bs * ls, d), grp_p, grp_s, wts)
    return (y_p.reshape(bp, lp, d), y_s.reshape(bs, ls, d), p_gdn, p_conv, p_sc, p_ret, p_cache_mem_k,
            p_cache_mem_v, s_gdn, s_conv, s_sc, s_ret)
```

```python
import functools
import math

import jax
import jax.numpy as jnp
import numpy as np
from jax import lax
from jax.experimental import pallas as pl
from jax.experimental.pallas import tpu as pltpu

F32 = jnp.float32
BF16 = jnp.bfloat16
I32 = jnp.int32

EPS = 1e-6
ROPE_BASE = 10000.0

D_MODEL = 1024
H_A, DK_A, DV_A, CONV_A = 4, 128, 128, 4
W_QKV_A = 3 * H_A * DK_A
D_B, CONV_B = D_MODEL // 2, 3
H_C, DK_C, DV_C = 4, 256, 512
H_X, HD_X, N_MEM = 4, 256, 256
N_GROUPS, E_PER_GROUP, N_EXPERTS, D_EXPERT = 4, 8, 32, 512
GDN_CHUNK = 64

LANES = 128
SUBLANES = 8
GDN_STACK = 256
VMEM_LIMIT = 56 * 1024 * 1024

PA_COLS = 3840
PA_SC = 2048
PA_BA = 3584
R_E0 = N_GROUPS
MOE_BLK = 256


def _cparams(sem):
    return pltpu.CompilerParams(dimension_semantics=sem, vmem_limit_bytes=VMEM_LIMIT)


def _dot(a, b, trans_a=False, trans_b=False):
    dn = (((0 if trans_a else 1,), (1 if trans_b else 0,)), ((), ()))
    return lax.dot_general(a.astype(BF16), b.astype(BF16), dn, preferred_element_type=F32)


def _silu(x):
    return x * (1.0 / (1.0 + jnp.exp(-x)))


def _sigmoid(x):
    return 1.0 / (1.0 + jnp.exp(-x))


def _rms_matmul_kernel(x_ref, g_ref, w_ref, o_ref, xn_ref):
    @pl.when(pl.program_id(1) == 0)
    def _():
        x = x_ref[...]
        ms = jnp.mean(x * x, axis=-1, keepdims=True)
        xn_ref[...] = (x * lax.rsqrt(ms + EPS) * g_ref[...]).astype(BF16)

    o_ref[...] = jnp.dot(xn_ref[...], w_ref[...], preferred_element_type=F32).astype(o_ref.dtype)


def rms_matmul(x, g, w, *, tn, rows=None, row0=0, out_dtype=F32, tm=1024):
    t, d = (rows or x.shape[0]), x.shape[1]
    n = w.shape[1]
    tm = min(tm, t)
    blk0 = row0 // tm
    return pl.pallas_call(
        _rms_matmul_kernel,
        grid=(t // tm, n // tn),
        in_specs=[pl.BlockSpec((tm, d), lambda i, j: (i + blk0, 0)),
                  pl.BlockSpec((1, d), lambda i, j: (0, 0)),
                  pl.BlockSpec((d, tn), lambda i, j: (0, j))],
        out_specs=pl.BlockSpec((tm, tn), lambda i, j: (i, j)),
        out_shape=jax.ShapeDtypeStruct((t, n), out_dtype),
        scratch_shapes=[pltpu.VMEM((tm, d), BF16)],
        compiler_params=_cparams(("parallel", "arbitrary")),
        name="rms_matmul",
    )(x, g.reshape(1, d), w)


def _kv_proj_kernel(x_ref, g_ref, w_ref, k_ref, v_ref):
    x = x_ref[...]
    ms = jnp.mean(x * x, axis=-1, keepdims=True)
    xn = (x * lax.rsqrt(ms + EPS) * g_ref[0]).astype(BF16)
    d = x.shape[1]
    k_ref[0] = jnp.dot(xn, w_ref[0, :, :d], preferred_element_type=F32)
    v_ref[0] = jnp.dot(xn, w_ref[0, :, d:], preferred_element_type=F32)


def kv_proj(x, g, w_kv, *, tm=1024):
    t, d = x.shape
    depth = g.shape[0]
    tm = min(tm, t)
    out_spec = pl.BlockSpec((1, tm, d), lambda l, i: (l, i, 0))
    return pl.pallas_call(
        _kv_proj_kernel,
        grid=(depth, t // tm),
        in_specs=[pl.BlockSpec((tm, d), lambda l, i: (i, 0)),
                  pl.BlockSpec((1, 1, d), lambda l, i: (l, 0, 0)),
                  pl.BlockSpec((1, d, 2 * d), lambda l, i: (l, 0, 0))],
        out_specs=[out_spec, out_spec],
        out_shape=[jax.ShapeDtypeStruct((depth, t, d), F32), jax.ShapeDtypeStruct((depth, t, d), F32)],
        compiler_params=_cparams(("parallel", "parallel")),
        name="kv_proj",
    )(x, g.reshape(depth, 1, d), w_kv)


def _matmul_res_kernel(a_ref, w_ref, r_ref, *rest):
    o_ref = rest[-1]
    o_ref[...] = r_ref[...] + jnp.dot(a_ref[...].astype(BF16), w_ref[...], preferred_element_type=F32)


def matmul_res(a, w, res, *, res_row0=0, into=None, out_row0=0, tm=512):
    t, k = a.shape
    n = w.shape[1]
    tm = min(tm, t)
    rblk, oblk = res_row0 // tm, out_row0 // tm
    in_specs = [pl.BlockSpec((tm, k), lambda i: (i, 0)),
                pl.BlockSpec((k, n), lambda i: (0, 0)),
                pl.BlockSpec((tm, n), lambda i: (i + rblk, 0))]
    args = (a, w, res)
    if into is not None:
        in_specs.append(pl.BlockSpec(memory_space=pl.ANY))
        args += (into,)
    return pl.pallas_call(
        _matmul_res_kernel,
        grid=(t // tm,),
        in_specs=in_specs,
        out_specs=pl.BlockSpec((tm, n), lambda i: (i + oblk, 0)),
        out_shape=jax.ShapeDtypeStruct((t, n) if into is None else into.shape, F32),
        input_output_aliases={} if into is None else {3: 0},
        compiler_params=_cparams(("parallel",)),
        name="matmul_res",
    )(*args)


def _causal_conv(x, hist, w_ref, width, seq8):
    r = x.shape[0]
    taps = [w_ref[j:j + 1, :] for j in range(width)]

    def head(x8, h8):
        n = x8.shape[0]
        t = lax.broadcasted_iota(I32, (n, 1), 0) % SUBLANES
        y = taps[width - 1] * x8
        for s in range(1, width):
            prev = pltpu.roll(h8, (n + s - (width - 1)) % n, 0) if s != width - 1 else h8
            y = y + taps[width - 1 - s] * jnp.where(t >= s, pltpu.roll(x8, s, 0), prev)
        return y

    if seq8:
        return head(x, hist)
    y = taps[width - 1] * x
    for s in range(1, width):
        y = y + taps[width - 1 - s] * pltpu.roll(x, s, 0)
    return jnp.concatenate([head(x[:SUBLANES], hist), y[SUBLANES:]], axis=0)


def _unit_lower_inverse(ms, c, ri, ci):
    base = min(c, 16)
    eye = jnp.where(ri == ci, 1.0, 0.0).astype(F32)
    blk = (ri // base) == (ci // base)
    ds = [jnp.where(blk, m, 0.0) for m in ms]
    ps = [eye - d for d in ds]
    k = 2
    while k < base:
        ds = [_dot(d, d) for d in ds]
        ps = [_dot(p, eye + d) for p, d in zip(ps, ds)]
        k *= 2
    s = base
    while s < c:
        sel = ((ri // (2 * s)) == (ci // (2 * s))) & ((ri // s) != (ci // s))
        ts = [_dot(jnp.where(sel, m, 0.0), p) for m, p in zip(ms, ps)]
        ps = [p - _dot(p, t) for p, t in zip(ps, ts)]
        s *= 2
    return ps


def _gdn_prepare(units, c):
    n = GDN_STACK
    ri = lax.broadcasted_iota(I32, (n, n), 0)
    ci = lax.broadcasted_iota(I32, (n, n), 1)
    same = (ri // c) == (ci // c)
    incl = same & (ri >= ci)
    strict = same & (ri > ci)
    pre = []
    for q, k, v, bfull, gfull in units:
        g2 = jnp.concatenate([gfull, gfull], axis=1)
        g_row = jnp.sum(jnp.where(ri == ci, g2, 0.0), axis=0, keepdims=True)
        gc_col = jnp.sum(jnp.where(incl, g_row, 0.0), axis=1, keepdims=True)
        gc_row = jnp.sum(jnp.where(same & (ri <= ci), g2, 0.0), axis=0, keepdims=True)
        gl_col = jnp.sum(jnp.where(same, g_row, 0.0), axis=1, keepdims=True)
        decay = jnp.where(incl, jnp.exp(jnp.where(incl, gc_col - gc_row, 0.0)), 0.0)
        egc = jnp.exp(gc_col)
        kb = k * bfull
        pre.append(dict(decay=decay, kb=kb, rhs=jnp.concatenate([v * bfull, kb * egc], axis=1),
                        qd=q * egc, kd=k * jnp.exp(gl_col - gc_col), egl=jnp.exp(gl_col)))
    mms = [jnp.where(strict, _dot(e["kb"], u[1], trans_b=True) * e["decay"], 0.0) for e, u in zip(pre, units)]
    qks = [_dot(u[0], u[1], trans_b=True) * e["decay"] for e, u in zip(pre, units)]
    tinvs = _unit_lower_inverse(mms, c, ri, ci)
    uws = [_dot(t, e["rhs"]) for t, e in zip(tinvs, pre)]
    return [dict(u=uw[:, :DV_A], w=uw[:, DV_A:], qk=qk, qd=e["qd"], kd=e["kd"], egl=e["egl"])
            for uw, qk, e in zip(uws, qks, pre)]


def _gdn_recur(e, states, c):
    nprob = GDN_STACK // c
    ws, qs = [], []
    for p in range(nprob):
        sl = slice(p * c, (p + 1) * c)
        ws.append(_dot(e["w"][sl], states[p]))
        qs.append(_dot(e["qd"][sl], states[p]))
    vn = e["u"] - jnp.concatenate(ws, axis=0)
    o = _dot(e["qk"], vn) + jnp.concatenate(qs, axis=0)
    new_states = []
    for p in range(nprob):
        sl = slice(p * c, (p + 1) * c)
        new_states.append(states[p] * e["egl"][p * c:p * c + 1, :] + _dot(e["kd"][sl], vn[sl], trans_a=True))
    return o, new_states


def _gdn_compute(x, z, u_sc, scb, ba, hq, hs, wq_ref, ws_ref, prm_ref, gn, states, *, seq8, c, nu):
    unit = GDN_CHUNK
    xc = _silu(_causal_conv(x, hq, wq_ref, CONV_A, seq8))
    yb = scb * _causal_conv(u_sc, hs, ws_ref, CONV_B, seq8)
    beta_all = _sigmoid(ba)
    sp = jnp.maximum(ba + prm_ref[1:2, :], 0.0) + jnp.log1p(jnp.exp(-jnp.abs(ba + prm_ref[1:2, :])))
    g_all = -jnp.exp(prm_ref[0:1, :]) * sp

    def head_cols(a, base):
        return a[:, base * DK_A:(base + 1) * DK_A]

    cat = lambda xs: jnp.concatenate(xs, axis=0)
    units = []
    for ui in range(nu):
        rs = slice(ui * unit, (ui + 1) * unit)
        qs, ks, vs, bs, gs = [], [], [], [], []
        for h in range(H_A):
            qh = head_cols(xc, h)[rs]
            kh = head_cols(xc, H_A + h)[rs]
            qs.append(qh * lax.rsqrt(jnp.sum(qh * qh, axis=-1, keepdims=True) + EPS) * (DK_A ** -0.5))
            ks.append(kh * lax.rsqrt(jnp.sum(kh * kh, axis=-1, keepdims=True) + EPS))
            vs.append(head_cols(xc, 2 * H_A + h)[rs])
            bs.append(jnp.broadcast_to(beta_all[rs, h:h + 1], (unit, LANES)))
            gs.append(jnp.broadcast_to(g_all[rs, H_A + h:H_A + h + 1], (unit, LANES)))
        units.append((cat(qs), cat(ks), cat(vs), cat(bs), cat(gs)))
    prepared = _gdn_prepare(units, c)

    outs = []
    for ui in range(nu):
        rs = slice(ui * unit, (ui + 1) * unit)
        o, states = _gdn_recur(prepared[ui], states, c)
        zst = cat([z[rs, h * DV_A:(h + 1) * DV_A] for h in range(H_A)])
        ms = jnp.mean(o * o, axis=-1, keepdims=True)
        og = o * lax.rsqrt(ms + EPS) * gn * _silu(zst)
        outs.append(jnp.concatenate([og[h * unit:(h + 1) * unit] for h in range(H_A)], axis=1))
    o_all = outs[0] if nu == 1 else cat(outs)
    return jnp.concatenate([o_all, yb], axis=1), states


def _gdn_kernel(qkv_ref, z_ref, sch_ref, scb_ref, scc_ref, ba_ref, hq_ref, hs_ref, wq_ref, ws_ref,
                prm_ref, gn_ref, s0_ref, o_ref, sn_ref, cq_ref, cs_ref):
    x = qkv_ref[...]
    rows = x.shape[0]
    u_sc = scc_ref[...] * sch_ref[...]
    cq_ref[...] = pltpu.roll(x, rows - SUBLANES + CONV_A - 1, 0)
    cs_ref[...] = pltpu.roll(u_sc, rows - SUBLANES + CONV_B - 1, 0)
    nprob = GDN_STACK // SUBLANES
    states = [s0_ref[p % SUBLANES, p // SUBLANES] for p in range(nprob)]
    mix, states = _gdn_compute(x, z_ref[...], u_sc, scb_ref[...], ba_ref[...], hq_ref[...], hs_ref[...],
                               wq_ref, ws_ref, prm_ref, gn_ref[...], states, seq8=True, c=SUBLANES, nu=1)
    o_ref[...] = mix.astype(o_ref.dtype)
    for p in range(nprob):
        sn_ref[p % SUBLANES, p // SUBLANES] = states[p]


def _gdn_block_kernel(x_ref, g_ref, wi_ref, wo_ref, hq_ref, hs_ref, wq_ref, ws_ref, prm_ref, gn_ref, s0_ref,
                      o_ref, sn_ref, cq_ref, cs_ref, s_scr, hq_scr, hs_scr, *, nu, nl):
    l = pl.program_id(1)

    @pl.when(l == 0)
    def _():
        hq_scr[...] = hq_ref[0]
        hs_scr[...] = hs_ref[0]
        s_scr[...] = s0_ref[0]

    xres = x_ref[...]
    rows = xres.shape[0]
    ms = jnp.mean(xres * xres, axis=-1, keepdims=True)
    xn = (xres * lax.rsqrt(ms + EPS) * g_ref[...]).astype(BF16)
    proj = lambda lo, width: jnp.dot(xn, wi_ref[:, lo:lo + width], preferred_element_type=F32)
    x = proj(0, W_QKV_A)
    z = proj(W_QKV_A, H_A * DV_A)
    u_sc = proj(PA_SC + 2 * D_B, D_B) * proj(PA_SC, D_B)
    scb = proj(PA_SC + D_B, D_B)
    ba = proj(PA_BA, LANES)
    hq = hq_scr[...]
    hs = hs_scr[...]
    hq_scr[...] = pltpu.roll(x[rows - SUBLANES:], CONV_A - 1, 0)
    hs_scr[...] = pltpu.roll(u_sc[rows - SUBLANES:], CONV_B - 1, 0)
    states = [s_scr[p] for p in range(H_A)]
    mix, states = _gdn_compute(x, z, u_sc, scb, ba, hq, hs, wq_ref, ws_ref, prm_ref, gn_ref[...], states,
                               seq8=False, c=GDN_CHUNK, nu=nu)
    for p in range(H_A):
        s_scr[p] = states[p]
    o_ref[...] = xres + jnp.dot(mix.astype(BF16), wo_ref[...], preferred_element_type=F32)

    @pl.when(l == nl - 1)
    def _():
        sn_ref[0] = s_scr[...]
        cq_ref[0] = hq_scr[...]
        cs_ref[0] = hs_scr[...]


def gdn_core(p, hist_q, hist_s, w_conv_qkv, w_conv_sc, prm, gn, s0, *, batch, seqlen):
    t = batch * seqlen
    assert seqlen == SUBLANES
    rows = GDN_CHUNK
    nb = rows // seqlen
    hq_spec = pl.BlockSpec((rows, W_QKV_A), lambda i: (i, 0))
    hs_spec = pl.BlockSpec((rows, D_B), lambda i: (i, 0))
    s_spec = pl.BlockSpec((nb, H_A, DK_A, DV_A), lambda i: (i, 0, 0, 0))
    col = lambda width, blk: pl.BlockSpec((rows, width), lambda i: (i, blk))
    const = lambda shape: pl.BlockSpec(shape, lambda i: (0,) * len(shape))
    return pl.pallas_call(
        _gdn_kernel,
        grid=(batch // nb,),
        in_specs=[col(W_QKV_A, 0), col(D_B, 3), col(D_B, 4), col(D_B, 5), col(D_B, 6),
                  col(LANES, PA_BA // LANES), hq_spec, hs_spec,
                  const((CONV_A, W_QKV_A)), const((CONV_B, D_B)), const((SUBLANES, LANES)),
                  const((1, DV_A)), s_spec],
        out_specs=[pl.BlockSpec((rows, D_MODEL), lambda i: (i, 0)), s_spec, hq_spec, hs_spec],
        out_shape=[jax.ShapeDtypeStruct((t, D_MODEL), BF16),
                   jax.ShapeDtypeStruct((batch, H_A, DK_A, DV_A), F32),
                   jax.ShapeDtypeStruct((t, W_QKV_A), F32),
                   jax.ShapeDtypeStruct((t, D_B), F32)],
        compiler_params=_cparams(("parallel",)),
        name="gdn_core",
    )(p, p, p, p, p, p, hist_q.reshape(t, W_QKV_A), hist_s.reshape(t, D_B), w_conv_qkv, w_conv_sc, prm,
      gn.reshape(1, DV_A), s0)


def _resident(shape):
    return pl.BlockSpec(shape, lambda *ix: (0,) * len(shape), pipeline_mode=pl.Buffered(1))


def gdn_block(x, g, w_in, w_out, hist_q, hist_s, w_conv_qkv, w_conv_sc, prm, gn, s0, *, batch, seqlen):
    t, d = batch * seqlen, x.shape[1]
    rows = min(seqlen, 256)
    nu = rows // GDN_CHUNK
    nl = seqlen // rows
    per_b = lambda shape: pl.BlockSpec((1,) + shape, lambda b, l: (b,) + (0,) * len(shape))
    row_spec = pl.BlockSpec((rows, d), lambda b, l: (b * nl + l, 0))
    kern = functools.partial(_gdn_block_kernel, nu=nu, nl=nl)
    return pl.pallas_call(
        kern,
        grid=(batch, nl),
        in_specs=[row_spec, _resident((1, d)), _resident(w_in.shape), _resident(w_out.shape),
                  per_b((SUBLANES, W_QKV_A)), per_b((SUBLANES, D_B)),
                  _resident((CONV_A, W_QKV_A)), _resident((CONV_B, D_B)), _resident((SUBLANES, LANES)),
                  _resident((1, DV_A)), per_b((H_A, DK_A, DV_A))],
        out_specs=[row_spec, per_b((H_A, DK_A, DV_A)), per_b((SUBLANES, W_QKV_A)), per_b((SUBLANES, D_B))],
        out_shape=[jax.ShapeDtypeStruct((t, d), F32),
                   jax.ShapeDtypeStruct((batch, H_A, DK_A, DV_A), F32),
                   jax.ShapeDtypeStruct((batch, SUBLANES, W_QKV_A), F32),
                   jax.ShapeDtypeStruct((batch, SUBLANES, D_B), F32)],
        scratch_shapes=[pltpu.VMEM((H_A, DK_A, DV_A), F32),
                        pltpu.VMEM((SUBLANES, W_QKV_A), F32),
                        pltpu.VMEM((SUBLANES, D_B), F32)],
        compiler_params=_cparams(("parallel", "arbitrary")),
        name="gdn_block",
    )(x, g.reshape(1, d), w_in, w_out, hist_q, hist_s, w_conv_qkv, w_conv_sc, prm, gn.reshape(1, DV_A), s0)


def _ret_compute(get_q, get_k, get_v, get_gate, cos, sin, dm_ref, qd_ref, kd_ref, cd_ref, rn_ref, r_scr,
                 *, nseq, c):
    half = DK_C // 2

    def rot(x):
        x1, x2 = x[:, :half], x[:, half:]
        return jnp.concatenate([x1 * cos - x2 * sin, x1 * sin + x2 * cos], axis=1)

    heads = range(H_C)
    qs = [rot(get_q(h)) for h in heads]
    ks = [rot(get_k(h)) * (DK_C ** -0.5) for h in heads]
    vs = [get_v(h).astype(BF16) for h in heads]
    ss = [_dot(qs[h], ks[h], trans_b=True) * dm_ref[h] for h in heads]
    inters = []
    for h in heads:
        qdh = qs[h] * qd_ref[h]
        parts = [_dot(qdh[sq * c:(sq + 1) * c], r_scr[sq, h]) for sq in range(nseq)]
        inters.append(parts[0] if nseq == 1 else jnp.concatenate(parts, axis=0))
    outs = [_dot(ss[h], vs[h]) + inters[h] for h in heads]
    for h in heads:
        kdh = ks[h] * kd_ref[h]
        cd = cd_ref[h][0:1, 0:1]
        for sq in range(nseq):
            sl = slice(sq * c, (sq + 1) * c)
            r_scr[sq, h] = r_scr[sq, h] * cd + _dot(kdh[sl], vs[h][sl], trans_a=True)
    gated = []
    for h in heads:
        o = outs[h]
        ms = jnp.mean(o * o, axis=-1, keepdims=True)
        on = o * lax.rsqrt(ms + EPS) * rn_ref[:, h * DV_C:(h + 1) * DV_C]
        gated.append(_silu(get_gate(h)) * on)
    return gated


def _ret_kernel(q_ref, k_ref, v_ref, gate_ref, cos_ref, sin_ref, dm_ref, qd_ref, kd_ref, cd_ref,
                rn_ref, r0_ref, o_ref, rnew_ref, r_scr, *, nseq, c, nl):
    l = pl.program_id(1)

    @pl.when(l == 0)
    def _():
        r_scr[...] = r0_ref[...]

    gated = _ret_compute(lambda h: q_ref[:, h * DK_C:(h + 1) * DK_C], lambda h: k_ref[:, h * DK_C:(h + 1) * DK_C],
                         lambda h: v_ref[:, h * DV_C:(h + 1) * DV_C], lambda h: gate_ref[:, h * DV_C:(h + 1) * DV_C],
                         cos_ref[...], sin_ref[...], dm_ref, qd_ref, kd_ref, cd_ref, rn_ref, r_scr, nseq=nseq, c=c)
    for h in range(H_C):
        o_ref[:, h * DV_C:(h + 1) * DV_C] = gated[h].astype(o_ref.dtype)

    @pl.when(l == nl - 1)
    def _():
        rnew_ref[...] = r_scr[...]


def _ret_block_kernel(x_ref, g_ref, wi_ref, wo_ref, cos_ref, sin_ref, dm_ref, qd_ref, kd_ref, cd_ref,
                      rn_ref, r0_ref, o_ref, rnew_ref, r_scr, *, c, nl):
    l = pl.program_id(1)

    @pl.when(l == 0)
    def _():
        r_scr[...] = r0_ref[...]

    xres = x_ref[...]
    ms = jnp.mean(xres * xres, axis=-1, keepdims=True)
    xn = (xres * lax.rsqrt(ms + EPS) * g_ref[...]).astype(BF16)
    proj = lambda lo, width: jnp.dot(xn, wi_ref[:, lo:lo + width], preferred_element_type=F32)
    hk, hv = H_C * DK_C, H_C * DV_C
    gated = _ret_compute(lambda h: proj(h * DK_C, DK_C), lambda h: proj(hk + h * DK_C, DK_C),
                         lambda h: proj(2 * hk + h * DV_C, DV_C), lambda h: proj(2 * hk + hv + h * DV_C, DV_C),
                         cos_ref[...], sin_ref[...], dm_ref, qd_ref, kd_ref, cd_ref, rn_ref, r_scr, nseq=1, c=c)
    y = xres
    for h in range(H_C):
        y = y + jnp.dot(gated[h].astype(BF16), wo_ref[h * DV_C:(h + 1) * DV_C, :], preferred_element_type=F32)
    o_ref[...] = y

    @pl.when(l == nl - 1)
    def _():
        rnew_ref[...] = r_scr[...]


def ret_core(p, pos, ret_norm, r0, *, batch, seqlen):
    t = batch * seqlen
    assert seqlen == SUBLANES
    nseq, c = 2, seqlen
    rows = nseq * c
    cos, sin, dmat, qd, kd, cd = _ret_tables(pos, nseq, c)
    const = lambda shape: pl.BlockSpec(shape, lambda b, l: (0,) * len(shape))
    kern = functools.partial(_ret_kernel, nseq=nseq, c=c, nl=1)
    hk = H_C * DK_C
    hv = H_C * DV_C
    return pl.pallas_call(
        kern,
        grid=(batch // nseq, 1),
        in_specs=[pl.BlockSpec((rows, hk), lambda b, l: (b, 0)),
                  pl.BlockSpec((rows, hk), lambda b, l: (b, 1)),
                  pl.BlockSpec((rows, hv), lambda b, l: (b, 1)),
                  pl.BlockSpec((rows, hv), lambda b, l: (b, 2)),
                  const((rows, DK_C // 2)), const((rows, DK_C // 2)),
                  const((H_C, rows, rows)), const((H_C, rows, DK_C)), const((H_C, rows, DK_C)),
                  const((H_C, SUBLANES, LANES)), const((1, hv)),
                  pl.BlockSpec((nseq, H_C, DK_C, DV_C), lambda b, l: (b, 0, 0, 0))],
        out_specs=[pl.BlockSpec((rows, hv), lambda b, l: (b, 0)),
                   pl.BlockSpec((nseq, H_C, DK_C, DV_C), lambda b, l: (b, 0, 0, 0))],
        out_shape=[jax.ShapeDtypeStruct((t, hv), BF16),
                   jax.ShapeDtypeStruct((batch, H_C, DK_C, DV_C), F32)],
        scratch_shapes=[pltpu.VMEM((nseq, H_C, DK_C, DV_C), F32)],
        compiler_params=_cparams(("parallel", "arbitrary")),
        name="ret_core",
    )(p, p, p, p, cos, sin, dmat, qd, kd, cd, ret_norm.reshape(1, hv), r0)


def _ret_tables(pos, nseq, c):
    half = DK_C // 2
    inv = ROPE_BASE ** (-jnp.arange(half, dtype=F32) / half)
    ang = pos.astype(F32)[:, None] * inv[None, :]
    cos, sin = jnp.cos(ang), jnp.sin(ang)
    if nseq > 1:
        cos, sin = jnp.tile(cos, (nseq, 1)), jnp.tile(sin, (nseq, 1))
    lg = jnp.log(1.0 - 2.0 ** (-5.0 - jnp.arange(H_C, dtype=F32)))[:, None]
    i = jnp.arange(c, dtype=F32)
    incl = i[:, None] >= i[None, :]
    dmat = jnp.exp(jnp.where(incl[None], (i[:, None] - i[None, :])[None] * lg[..., None], -jnp.inf))
    if nseq > 1:
        dmat = jnp.kron(jnp.eye(nseq, dtype=F32)[None], dmat)
    qd = jnp.tile(jnp.exp((i + 1.0)[None] * lg), (1, nseq))[..., None] * jnp.ones((1, 1, DK_C), F32)
    kd = jnp.tile(jnp.exp((c - 1.0 - i)[None] * lg), (1, nseq))[..., None] * jnp.ones((1, 1, DK_C), F32)
    cd = jnp.exp(c * lg)[..., None] * jnp.ones((1, SUBLANES, LANES), F32)
    return cos, sin, dmat, qd, kd, cd


def ret_block(x, g, w_in, w_out, pos, ret_norm, r0, *, batch, seqlen):
    t, d = batch * seqlen, x.shape[1]
    c = min(seqlen, 256)
    nl = seqlen // c
    cos, sin, dmat, qd, kd, cd = _ret_tables(pos, 1, c)
    hv = H_C * DV_C
    row_spec = pl.BlockSpec((c, d), lambda b, l: (b * nl + l, 0))
    trig_spec = pl.BlockSpec((c, DK_C // 2), lambda b, l: (l, 0))
    state_spec = pl.BlockSpec((1, H_C, DK_C, DV_C), lambda b, l: (b, 0, 0, 0))
    kern = functools.partial(_ret_block_kernel, c=c, nl=nl)
    return pl.pallas_call(
        kern,
        grid=(batch, nl),
        in_specs=[row_spec, _resident((1, d)), _resident(w_in.shape), _resident(w_out.shape),
                  trig_spec, trig_spec,
                  _resident((H_C, c, c)), _resident((H_C, c, DK_C)), _resident((H_C, c, DK_C)),
                  _resident((H_C, SUBLANES, LANES)), _resident((1, hv)), state_spec],
        out_specs=[row_spec, state_spec],
        out_shape=[jax.ShapeDtypeStruct((t, d), F32),
                   jax.ShapeDtypeStruct((batch, H_C, DK_C, DV_C), F32)],
        scratch_shapes=[pltpu.VMEM((1, H_C, DK_C, DV_C), F32)],
        compiler_params=_cparams(("parallel", "arbitrary")),
        name="ret_block",
    )(x, g.reshape(1, d), w_in, w_out, cos, sin, dmat, qd, kd, cd, ret_norm.reshape(1, hv), r0)


def _xattn_fetch(mk_hbm, mv_hbm, kbuf, vbuf, sem, *, layer, nb, nsteps):
    i = pl.program_id(0)
    l = pl.program_id(1)
    slot = i % 2

    def copies(step, slot_):
        out = []
        for b in range(nb):
            for h in range(H_X):
                out.append(pltpu.make_async_copy(mk_hbm.at[layer, step * nb + b, :, h, :],
                                                 kbuf.at[slot_, b, h], sem.at[slot_]))
                out.append(pltpu.make_async_copy(mv_hbm.at[layer, step * nb + b, :, h, :],
                                                 vbuf.at[slot_, b, h], sem.at[slot_]))
        return out

    @pl.when(l == 0)
    def _():
        @pl.when(i == 0)
        def _():
            for c in copies(i, slot):
                c.start()

        @pl.when(i + 1 < nsteps)
        def _():
            for c in copies(i + 1, 1 - slot):
                c.start()

        for c in copies(i, slot):
            c.wait()

    return slot


def _xattn_heads(q_of, kbuf, vbuf, slot, probs):
    ss = [_dot(q_of(b, h), kbuf[slot, b, h], trans_b=True) * (HD_X ** -0.5) for b, h in probs]
    ps = []
    for s in ss:
        e = jnp.exp(s - jnp.max(s, axis=-1, keepdims=True))
        ps.append(e / jnp.sum(e, axis=-1, keepdims=True))
    return [_dot(p, vbuf[slot, b, h]) for p, (b, h) in zip(ps, probs)]


def _xattn_kernel(q_ref, mk_hbm, mv_hbm, o_ref, kbuf, vbuf, sem, *, layer, nb, lq, nsteps):
    slot = _xattn_fetch(mk_hbm, mv_hbm, kbuf, vbuf, sem, layer=layer, nb=nb, nsteps=nsteps)
    probs = [(b, h) for b in range(nb) for h in range(H_X)]
    win = lambda b, h: (slice(b * lq, (b + 1) * lq), slice(h * HD_X, (h + 1) * HD_X))
    outs = _xattn_heads(lambda b, h: q_ref[win(b, h)], kbuf, vbuf, slot, probs)
    for o, (b, h) in zip(outs, probs):
        o_ref[win(b, h)] = o.astype(o_ref.dtype)


def _xattn_block_kernel(x_ref, g_ref, wq_ref, wo_ref, mk_hbm, mv_hbm, into_hbm, o_ref, kbuf, vbuf, sem,
                        *, layer, nsteps):
    del into_hbm
    slot = _xattn_fetch(mk_hbm, mv_hbm, kbuf, vbuf, sem, layer=layer, nb=1, nsteps=nsteps)
    xres = x_ref[...]
    ms = jnp.mean(xres * xres, axis=-1, keepdims=True)
    xn = (xres * lax.rsqrt(ms + EPS) * g_ref[...]).astype(BF16)
    probs = [(0, h) for h in range(H_X)]
    qs = [jnp.dot(xn, wq_ref[:, h * HD_X:(h + 1) * HD_X], preferred_element_type=F32) for h in range(H_X)]
    outs = _xattn_heads(lambda b, h: qs[h], kbuf, vbuf, slot, probs)
    y = xres
    for h in range(H_X):
        y = y + jnp.dot(outs[h].astype(BF16), wo_ref[h * HD_X:(h + 1) * HD_X, :], preferred_element_type=F32)
    o_ref[...] = y


def xattn_core(q, mk, mv, layer, *, batch, seqlen):
    t = batch * seqlen
    d = H_X * HD_X
    assert seqlen == SUBLANES
    nb, lq = 2, seqlen
    rows = nb * lq
    nsteps = batch // nb
    kern = functools.partial(_xattn_kernel, layer=layer, nb=nb, lq=lq, nsteps=nsteps)
    return pl.pallas_call(
        kern,
        grid=(nsteps, 1),
        in_specs=[pl.BlockSpec((rows, d), lambda b, l: (b, 0)),
                  pl.BlockSpec(memory_space=pl.ANY), pl.BlockSpec(memory_space=pl.ANY)],
        out_specs=pl.BlockSpec((rows, d), lambda b, l: (b, 0)),
        out_shape=jax.ShapeDtypeStruct((t, d), q.dtype),
        scratch_shapes=[pltpu.VMEM((2, nb, H_X, N_MEM, HD_X), F32),
                        pltpu.VMEM((2, nb, H_X, N_MEM, HD_X), F32),
                        pltpu.SemaphoreType.DMA((2,))],
        compiler_params=_cparams(("arbitrary", "arbitrary")),
        name="xattn_core",
    )(q, mk, mv)


def xattn_block(x, g, w_q, w_o, mk, mv, layer, into, *, batch, seqlen):
    d = x.shape[1]
    lq = min(seqlen, 512)
    nl = seqlen // lq
    row_spec = pl.BlockSpec((lq, d), lambda b, l: (b * nl + l, 0))
    kern = functools.partial(_xattn_block_kernel, layer=layer, nsteps=batch)
    return pl.pallas_call(
        kern,
        grid=(batch, nl),
        in_specs=[row_spec, _resident((1, d)), _resident(w_q.shape), _resident(w_o.shape),
                  pl.BlockSpec(memory_space=pl.ANY), pl.BlockSpec(memory_space=pl.ANY),
                  pl.BlockSpec(memory_space=pl.ANY)],
        out_specs=row_spec,
        out_shape=jax.ShapeDtypeStruct(into.shape, F32),
        scratch_shapes=[pltpu.VMEM((2, 1, H_X, N_MEM, HD_X), F32),
                        pltpu.VMEM((2, 1, H_X, N_MEM, HD_X), F32),
                        pltpu.SemaphoreType.DMA((2,))],
        input_output_aliases={6: 0},
        compiler_params=_cparams(("arbitrary", "arbitrary")),
        name="xattn_block",
    )(x, g.reshape(1, d), w_q, w_o, mk, mv, into)


def _route_kernel(x_ref, g_ref, wr_ref, br_ref, h_ref, meta_ref, wts_ref, cnt_ref, cnt_scr):
    i = pl.program_id(0)

    @pl.when(i == 0)
    def _():
        cnt_scr[...] = jnp.zeros_like(cnt_scr)

    x = x_ref[...]
    tm = x.shape[0]
    ms = jnp.mean(x * x, axis=-1, keepdims=True)
    h = x * lax.rsqrt(ms + EPS) * g_ref[...]
    hb = h.astype(BF16)
    bits = pltpu.bitcast(hb.astype(F32), jnp.uint32)
    half_d = bits.shape[1] // 2
    h_ref[...] = bits[:, :half_d] | (bits[:, half_d:] >> 16)
    logits = jnp.dot(hb, wr_ref[...], preferred_element_type=F32) + br_ref[...]
    lane_i = lax.broadcasted_iota(I32, (tm, LANES), 1)
    lane = lane_i.astype(F32)
    neg = jnp.float32(-3.0e38)
    big = jnp.float32(LANES)
    is_g = lane_i < N_GROUPS
    gl = jnp.where(is_g, logits, neg)
    gmax = jnp.max(gl, axis=1, keepdims=True)
    grp = jnp.min(jnp.where(gl == gmax, lane, big), axis=1, keepdims=True)
    gsum = jnp.sum(jnp.where(is_g, jnp.exp(jnp.where(is_g, logits - gmax, 0.0)), 0.0), axis=1, keepdims=True)
    p_grp = 1.0 / gsum
    in_grp = ((lane_i >= R_E0) & (lane_i < R_E0 + N_EXPERTS)
              & (jnp.floor((lane - R_E0) * (1.0 / E_PER_GROUP)) == grp))
    el = jnp.where(in_grp, logits, neg)
    m1 = jnp.max(el, axis=1, keepdims=True)
    i1 = jnp.min(jnp.where(el == m1, lane, big), axis=1, keepdims=True)
    el2 = jnp.where(lane == i1, neg, el)
    m2 = jnp.max(el2, axis=1, keepdims=True)
    i2 = jnp.min(jnp.where(el2 == m2, lane, big), axis=1, keepdims=True)
    esum = jnp.sum(jnp.where(in_grp, jnp.exp(jnp.where(in_grp, logits - m1, 0.0)), 0.0), axis=1, keepdims=True)
    p1 = 1.0 / esum
    p2 = jnp.exp(m2 - m1) / esum
    tot = p1 + p2
    w1 = p_grp * (p1 / tot)
    w2 = p_grp * (p2 / tot)
    wts_ref[...] = jnp.where(lane_i == 0, w1, jnp.where(lane_i == 1, w2, 0.0))

    oh1 = (lane == i1).astype(F32)
    oh2 = (lane == i2).astype(F32)
    rr = lax.broadcasted_iota(I32, (tm, tm), 0)
    cc = lax.broadcasted_iota(I32, (tm, tm), 1)
    tri = (rr > cc).astype(BF16)
    base = cnt_scr[0:1, :]
    c1 = jnp.sum(oh1, axis=0, keepdims=True)
    c2 = jnp.sum(oh2, axis=0, keepdims=True)
    r1 = jnp.sum(oh1 * (jnp.dot(tri, oh1.astype(BF16), preferred_element_type=F32) + base), axis=1, keepdims=True)
    r2 = jnp.sum(oh2 * (jnp.dot(tri, oh2.astype(BF16), preferred_element_type=F32) + base + c1), axis=1, keepdims=True)
    new_cnt = base + c1 + c2
    cnt_scr[...] = jnp.broadcast_to(new_cnt, cnt_scr.shape)
    cnt_ref[...] = jnp.broadcast_to(new_cnt, cnt_ref.shape)
    meta = jnp.where(lane_i == 0, i1, jnp.where(lane_i == 1, i2, 0.0))
    meta = jnp.where(lane_i == 2, r1, jnp.where(lane_i == 3, r2, meta))
    meta_ref[...] = meta.astype(I32)


def moe_route(x, g, wr, br, *, tm=256):
    t, d = x.shape
    tm = min(tm, t)
    return pl.pallas_call(
        _route_kernel,
        grid=(t // tm,),
        in_specs=[pl.BlockSpec((tm, d), lambda i: (i, 0)),
                  pl.BlockSpec((1, d), lambda i: (0, 0)),
                  pl.BlockSpec((d, LANES), lambda i: (0, 0)),
                  pl.BlockSpec((1, LANES), lambda i: (0, 0))],
        out_specs=[pl.BlockSpec((tm, d // 2), lambda i: (i, 0)),
                   pl.BlockSpec((tm, LANES), lambda i: (i, 0)),
                   pl.BlockSpec((tm, LANES), lambda i: (i, 0)),
                   pl.BlockSpec((SUBLANES, LANES), lambda i: (0, 0))],
        out_shape=[jax.ShapeDtypeStruct((t, d // 2), jnp.uint32),
                   jax.ShapeDtypeStruct((t, LANES), I32),
                   jax.ShapeDtypeStruct((t, LANES), F32),
                   jax.ShapeDtypeStruct((SUBLANES, LANES), F32)],
        scratch_shapes=[pltpu.VMEM((SUBLANES, LANES), F32)],
        compiler_params=_cparams(("arbitrary",)),
        name="moe_route",
    )(x, g.reshape(1, d), wr, br)


def _plan_kernel(cnt_ref, meta_ref, dest_ref, ex_ref, be_ref, *, nblk_pad):
    cnt = cnt_ref[...]
    lane8 = lax.broadcasted_iota(I32, (SUBLANES, LANES), 1)
    padded = jnp.ceil(cnt / MOE_BLK) * MOE_BLK
    pend = padded
    s = 1
    while s < LANES:
        pend = pend + jnp.where(lane8 >= s, pltpu.roll(pend, s, 1), 0.0)
        s *= 2
    pstart = (pend - padded)[0:1, :]
    meta = meta_ref[...].astype(F32)
    tm = meta.shape[0]
    lane_i = lax.broadcasted_iota(I32, (tm, LANES), 1)
    lane = lane_i.astype(F32)
    col = lambda j: jnp.sum(jnp.where(lane_i == j, meta, 0.0), axis=1, keepdims=True)
    e1, e2, r1, r2 = col(0), col(1), col(2), col(3)
    d1 = jnp.sum(jnp.where(lane == e1, pstart, 0.0), axis=1, keepdims=True) + r1
    d2 = jnp.sum(jnp.where(lane == e2, pstart, 0.0), axis=1, keepdims=True) + r2
    dest_ref[...] = jnp.where(lane_i == 0, d1, jnp.where(lane_i == 1, d2, 0.0)).astype(I32)
    sub8 = lax.broadcasted_iota(I32, (SUBLANES, LANES), 0)
    zrow = jnp.floor((pend - padded + cnt) / SUBLANES) * SUBLANES
    ex = jnp.where(sub8 == 0, pend - padded, padded / MOE_BLK)
    ex = jnp.where(sub8 == 2, zrow, jnp.where(sub8 == 3, (pend - zrow) / SUBLANES, ex))
    ex_ref[...] = jnp.where(sub8 == 4, pend, ex).astype(I32)
    bi = (lax.broadcasted_iota(I32, (nblk_pad, LANES), 0) * MOE_BLK).astype(F32)
    lane_b = lax.broadcasted_iota(I32, (nblk_pad, LANES), 1)
    is_e = (lane_b >= R_E0) & (lane_b < R_E0 + N_EXPERTS)
    nfull = jnp.sum(jnp.where(is_e & (bi >= pend[0:1, :]), 1.0, 0.0), axis=1, keepdims=True)
    nused = jnp.max(pend[0:1, :], axis=1, keepdims=True) / MOE_BLK
    row = lax.broadcasted_iota(I32, (nblk_pad, LANES), 0)
    be_ref[...] = jnp.where(row == nblk_pad - 1, nused, jnp.minimum(nfull, N_EXPERTS - 1.0)).astype(I32)


def moe_plan(cnt, meta, *, nblk_pad):
    t = meta.shape[0]
    tm = math.gcd(t, 1024)
    kern = functools.partial(_plan_kernel, nblk_pad=nblk_pad)
    return pl.pallas_call(
        kern,
        grid=(t // tm,),
        in_specs=[pl.BlockSpec((SUBLANES, LANES), lambda i: (0, 0)),
                  pl.BlockSpec((tm, LANES), lambda i: (i, 0))],
        out_specs=[pl.BlockSpec((tm, LANES), lambda i: (i, 0)),
                   pl.BlockSpec((SUBLANES, LANES), lambda i: (0, 0)),
                   pl.BlockSpec((nblk_pad, LANES), lambda i: (0, 0))],
        out_shape=[jax.ShapeDtypeStruct((t, LANES), I32),
                   jax.ShapeDtypeStruct((SUBLANES, LANES), I32),
                   jax.ShapeDtypeStruct((nblk_pad, LANES), I32)],
        compiler_params=_cparams(("arbitrary",)),
        name="moe_plan",
    )(cnt, meta)


DMA_UNROLL = 8
DISPATCH_SLOTS = 3


def _dispatch_kernel(dest_ref, zrow_ref, zcnt_ref, end_ref, h_hbm, xd_hbm, hbuf, zbuf, zblk, lsem, ssem, zsem,
                     *, tm, nsteps, nrows):
    i = pl.program_id(0)
    slot = i % DISPATCH_SLOTS

    def load(step, slot_):
        return pltpu.make_async_copy(h_hbm.at[pl.ds(step * tm, tm)], hbuf.at[slot_], lsem.at[slot_])

    def rows_done(slot_):
        return pltpu.make_async_copy(hbuf.at[slot_], xd_hbm.at[pl.ds(0, tm)], ssem.at[slot_])

    def zero_group(e, g):
        row = pl.multiple_of(zrow_ref[e] + g * SUBLANES, SUBLANES)
        return pltpu.make_async_copy(zbuf, xd_hbm.at[pl.ds(row, SUBLANES)], zsem)

    @pl.when(i == 0)
    def _():
        load(0, 0).start()
        if nsteps > 1:
            load(1, 1).start()
        zbuf[...] = jnp.zeros(zbuf.shape, zbuf.dtype)

        def per_expert(fn):
            def body(e, carry):
                lax.fori_loop(0, zcnt_ref[e], lambda g, c: (fn(zero_group(e, g)), c)[1], 0)
                return carry
            lax.fori_loop(0, N_EXPERTS, body, 0)

        per_expert(lambda cp: cp.start())
        per_expert(lambda cp: cp.wait())

        zblk[...] = jnp.zeros(zblk.shape, zblk.dtype)
        ntail = (nrows - end_ref[0]) // MOE_BLK

        def tail_block(j):
            row = pl.multiple_of(end_ref[0] + j * MOE_BLK, MOE_BLK)
            return pltpu.make_async_copy(zblk, xd_hbm.at[pl.ds(row, MOE_BLK)], zsem)

        lax.fori_loop(0, ntail, lambda j, c: (tail_block(j).start(), c)[1], 0)
        lax.fori_loop(0, ntail, lambda j, c: (tail_block(j).wait(), c)[1], 0)

    load(i, slot).wait()

    def start(t, carry):
        for k in range(2):
            pltpu.make_async_copy(hbuf.at[slot, pl.ds(t, 1)], xd_hbm.at[pl.ds(dest_ref[2 * t + k], 1)],
                                  ssem.at[slot]).start(priority=k)
        return carry

    lax.fori_loop(0, tm, start, 0, unroll=DMA_UNROLL)

    @pl.when(i >= 1)
    def _():
        prev = (i + DISPATCH_SLOTS - 1) % DISPATCH_SLOTS
        rows_done(prev).wait()
        rows_done(prev).wait()

    @pl.when(i + 2 < nsteps)
    def _():
        load(i + 2, (i + 2) % DISPATCH_SLOTS).start()

    @pl.when(i == nsteps - 1)
    def _():
        rows_done(slot).wait()
        rows_done(slot).wait()


def moe_dispatch(dest_flat, zrow, zcnt, end, h, nrows, *, tm=1024):
    t, d = h.shape
    tm = min(tm, t)
    nsteps = t // tm
    kern = functools.partial(_dispatch_kernel, tm=tm, nsteps=nsteps, nrows=nrows)
    smem_all = pl.BlockSpec(memory_space=pltpu.SMEM)
    return pl.pallas_call(
        kern,
        grid=(nsteps,),
        in_specs=[pl.BlockSpec((2 * tm,), lambda i: (i,), memory_space=pltpu.SMEM), smem_all, smem_all, smem_all,
                  pl.BlockSpec(memory_space=pl.ANY)],
        out_specs=pl.BlockSpec(memory_space=pl.ANY),
        out_shape=jax.ShapeDtypeStruct((nrows, d), h.dtype),
        scratch_shapes=[pltpu.VMEM((DISPATCH_SLOTS, tm, d), h.dtype),
                        pltpu.VMEM((SUBLANES, d), h.dtype),
                        pltpu.VMEM((MOE_BLK, d), h.dtype),
                        pltpu.SemaphoreType.DMA((DISPATCH_SLOTS,)),
                        pltpu.SemaphoreType.DMA((DISPATCH_SLOTS,)),
                        pltpu.SemaphoreType.DMA],
        compiler_params=_cparams(("arbitrary",)),
        name="moe_dispatch",
    )(dest_flat, zrow, zcnt, end, h)


def _experts_kernel(be_ref, x_ref, wg_ref, wu_ref, wd_ref, o_ref, wg_s, wu_s, wd_s, *, nblk_pad):
    i = pl.program_id(0)
    nused = be_ref[nblk_pad - 1]

    @pl.when(i < nused)
    def _():
        prev = be_ref[jnp.maximum(i - 1, 0)]

        @pl.when((i == 0) | (be_ref[i] != prev))
        def _():
            wg_s[...] = wg_ref[0].astype(BF16)
            wu_s[...] = wu_ref[0].astype(BF16)
            wd_s[...] = wd_ref[0].astype(BF16)

        half = MOE_BLK // 2

        def unpack(w):
            hi = pltpu.bitcast(w & jnp.uint32(0xFFFF0000), F32)
            lo = pltpu.bitcast(w << 16, F32)
            return jnp.concatenate([hi, lo], axis=1).astype(BF16)

        xs = [unpack(x_ref[r * half:(r + 1) * half, :]) for r in range(2)]
        gs = [jnp.dot(x, wg_s[...], preferred_element_type=F32) for x in xs]
        us = [jnp.dot(x, wu_s[...], preferred_element_type=F32) for x in xs]
        acts = [(_silu(g) * u).astype(BF16) for g, u in zip(gs, us)]
        for r in range(2):
            o_ref[r * half:(r + 1) * half, :] = jnp.dot(acts[r], wd_s[...], preferred_element_type=F32)

    @pl.when(i >= nused)
    def _():
        o_ref[...] = jnp.zeros_like(o_ref)


def moe_experts(be_flat, xd, w_gate, w_up, w_down, e0, *, nblk, nblk_pad):
    d, f = w_gate.shape[1], w_gate.shape[2]
    kern = functools.partial(_experts_kernel, nblk_pad=nblk_pad)
    grid_spec = pltpu.PrefetchScalarGridSpec(
        num_scalar_prefetch=1,
        grid=(nblk,),
        in_specs=[pl.BlockSpec((MOE_BLK, d // 2), lambda i, be: (i, 0)),
                  pl.BlockSpec((1, d, f), lambda i, be: (e0 + be[i], 0, 0)),
                  pl.BlockSpec((1, d, f), lambda i, be: (e0 + be[i], 0, 0)),
                  pl.BlockSpec((1, f, d), lambda i, be: (e0 + be[i], 0, 0))],
        out_specs=pl.BlockSpec((MOE_BLK, d), lambda i, be: (i, 0)),
        scratch_shapes=[pltpu.VMEM((d, f), BF16), pltpu.VMEM((d, f), BF16), pltpu.VMEM((f, d), BF16)],
    )
    return pl.pallas_call(
        kern,
        grid_spec=grid_spec,
        out_shape=jax.ShapeDtypeStruct((xd.shape[0], d), F32),
        compiler_params=_cparams(("arbitrary",)),
        name="moe_experts",
    )(be_flat, xd, w_gate, w_up, w_down)


def _combine_kernel(dest_ref, dest_next_ref, x_ref, wts_ref, gf_ref, yd_hbm, o_ref, rbuf, sem,
                    *, tm, nsteps, final_norm):
    i = pl.program_id(0)
    slot = i % 2

    def gather(dref, slot_):
        def start(t, carry):
            for k in range(2):
                pltpu.make_async_copy(yd_hbm.at[pl.ds(dref[2 * t + k], 1)], rbuf.at[slot_, k, pl.ds(t, 1)],
                                      sem.at[slot_]).start(priority=k)
            return carry
        lax.fori_loop(0, tm, start, 0, unroll=DMA_UNROLL)

    @pl.when(i == 0)
    def _():
        gather(dest_ref, 0)

    @pl.when(i + 1 < nsteps)
    def _():
        gather(dest_next_ref, 1 - slot)

    for k in range(2):
        pltpu.make_async_copy(yd_hbm.at[pl.ds(0, tm)], rbuf.at[slot, k], sem.at[slot]).wait()
    w = wts_ref[...]
    y = x_ref[...] + rbuf[slot, 0] * w[:, 0:1] + rbuf[slot, 1] * w[:, 1:2]
    if final_norm:
        y = y * lax.rsqrt(jnp.mean(y * y, axis=-1, keepdims=True) + EPS) * gf_ref[...]
    o_ref[...] = y


def moe_combine(dest_flat, x, wts, yd, g_final, *, row0=0, rows=None, tm=1024):
    t, d = (rows or x.shape[0]), x.shape[1]
    tm = min(tm, t)
    nsteps = t // tm
    blk0 = row0 // tm
    final_norm = g_final is not None
    gf = g_final.reshape(1, d) if final_norm else jnp.ones((1, d), F32)
    kern = functools.partial(_combine_kernel, tm=tm, nsteps=nsteps, final_norm=final_norm)
    return pl.pallas_call(
        kern,
        grid=(nsteps,),
        in_specs=[pl.BlockSpec((2 * tm,), lambda i: (i + blk0,), memory_space=pltpu.SMEM),
                  pl.BlockSpec((2 * tm,), lambda i: (jnp.minimum(i + 1, nsteps - 1) + blk0,),
                               memory_space=pltpu.SMEM),
                  pl.BlockSpec((tm, d), lambda i: (i + blk0, 0)),
                  pl.BlockSpec((tm, LANES), lambda i: (i + blk0, 0)),
                  pl.BlockSpec((1, d), lambda i: (0, 0)),
                  pl.BlockSpec(memory_space=pl.ANY)],
        out_specs=pl.BlockSpec((tm, d), lambda i: (i, 0)),
        out_shape=jax.ShapeDtypeStruct((t, d), F32),
        scratch_shapes=[pltpu.VMEM((2, 2, tm, d), F32), pltpu.SemaphoreType.DMA((2,))],
        compiler_params=_cparams(("arbitrary",)),
        name="moe_combine",
    )(dest_flat, dest_flat, x, wts, gf, yd)


def moe_block(x, g, wr, br, w_gate, w_up, w_down, e0, g_final=None, splits=None):
    t, d = x.shape
    nblk = (2 * t) // MOE_BLK + N_EXPERTS
    nblk_pad = -(-(nblk + 1) // SUBLANES) * SUBLANES
    h, meta, wts, cnt = moe_route(x, g, wr, br)
    dest, ex, be = moe_plan(cnt, meta, nblk_pad=nblk_pad)
    dest_flat = dest[:, :2].reshape(2 * t)
    per_expert = lambda r: ex[r, R_E0:R_E0 + N_EXPERTS]
    end_last = ex[4, R_E0 + N_EXPERTS - 1:R_E0 + N_EXPERTS]
    xd = moe_dispatch(dest_flat, per_expert(2), per_expert(3), end_last, h, nblk * MOE_BLK)
    yd = moe_experts(be[:, 0], xd, w_gate, w_up, w_down, e0, nblk=nblk, nblk_pad=nblk_pad)
    if splits is None:
        return moe_combine(dest_flat, x, wts, yd, g_final)
    return [moe_combine(dest_flat, x, wts, yd, g_final, row0=r0, rows=n) for r0, n in splits]


def _pad_rows(buf):
    return jnp.pad(buf, ((0, 0), (0, SUBLANES - buf.shape[1]), (0, 0)))


def _forward(x_long, x_short, grp_long, grp_short, wts):
    depth = grp_long["mem_k"].shape[0]
    bl, ll = grp_long["batch"], grp_long["seqlen"]
    bs, ls = grp_short["batch"], grp_short["seqlen"]
    t_long, t_short = bl * ll, bs * ls
    d = x_long.shape[1]
    new = {id(grp_long): dict(gdn=[], conv=[], sc=[], ret=[]), id(grp_short): dict(gdn=[], conv=[], sc=[], ret=[])}

    def record_even(grp, s_new, cq, cs):
        rec = new[id(grp)]
        rec["gdn"].append(s_new)
        rec["conv"].append(cq.reshape(grp["batch"], SUBLANES, W_QKV_A)[:, :CONV_A - 1])
        rec["sc"].append(cs.reshape(grp["batch"], SUBLANES, D_B)[:, :CONV_B - 1])

    x_all = None
    for layer in range(depth):
        src_long = x_long if x_all is None else x_all
        src_short, row0 = (x_short, 0) if x_all is None else (x_all, t_long)
        if layer % 2 == 0:
            i = layer // 2
            hist = lambda grp: (_pad_rows(grp["conv"][i]), _pad_rows(grp["sc"][i]), wts["w_conv_qkv"][i],
                                wts["w_conv_sc"][i], wts["gdn_prm"][i], wts["gdn_norm"][i], grp["gdn"][i])
            p = rms_matmul(src_short, wts["norm_mix"][layer], wts["w_in_a"][i], tn=768, rows=t_short, row0=row0)
            mix, s_new, cq, cs = gdn_core(p, *hist(grp_short), batch=bs, seqlen=ls)
            xs = matmul_res(mix, wts["w_out_a"][i], src_short, res_row0=row0)
            record_even(grp_short, s_new, cq, cs)
            xl, s_new, cq, cs = gdn_block(src_long, wts["norm_mix"][layer], wts["w_in_a"][i], wts["w_out_a"][i],
                                          *hist(grp_long), batch=bl, seqlen=ll)
            record_even(grp_long, s_new, cq, cs)
        else:
            j = layer // 2
            p = rms_matmul(src_short, wts["norm_mix"][layer], wts["w_in_c"][j], tn=768, rows=t_short, row0=row0)
            ret, r_new = ret_core(p, grp_short["pos"], wts["ret_norm"][j], grp_short["ret"][j], batch=bs, seqlen=ls)
            xs = matmul_res(ret, wts["w_out_c"][j], src_short, res_row0=row0)
            new[id(grp_short)]["ret"].append(r_new)
            xl, r_new = ret_block(src_long, wts["norm_mix"][layer], wts["w_in_c"][j], wts["w_out_c"][j],
                                  grp_long["pos"], wts["ret_norm"][j], grp_long["ret"][j], batch=bl, seqlen=ll)
            new[id(grp_long)]["ret"].append(r_new)
        q = rms_matmul(xs, wts["norm_x"][layer], wts["w_xq"][layer], tn=D_MODEL)
        att = xattn_core(q, grp_short["mem_k"], grp_short["mem_v"], layer, batch=bs, seqlen=ls)
        joint = jnp.zeros((t_long + t_short, d), F32) if x_all is None else x_all
        joint = matmul_res(att, wts["w_xo"][layer], xs, into=joint, out_row0=t_long)
        joint = xattn_block(xl, wts["norm_x"][layer], wts["w_xq"][layer], wts["w_xo"][layer],
                            grp_long["mem_k"], grp_long["mem_v"], layer, joint, batch=bl, seqlen=ll)
        last = layer == depth - 1
        x_all = moe_block(joint, wts["norm_ffn"][layer], wts["w_route"][layer], wts["b_route"][layer],
                          wts["w_exp_gate"], wts["w_exp_up"], wts["w_exp_down"], layer * N_EXPERTS,
                          g_final=wts["norm_final"] if last else None,
                          splits=[(0, t_long), (t_long, t_short)] if last else None)
    y_long, y_short = x_all
    stack = lambda grp: tuple(jnp.stack(new[id(grp)][k]) for k in ("gdn", "conv", "sc", "ret"))
    return (y_long,) + stack(grp_long), (y_short,) + stack(grp_short)


def kernel(x_prompt, x_sample, state_gdn, state_gdn_conv, state_sconv, state_ret, cache_mem_k, cache_mem_v, mem_prompt, norm_mix, norm_x, norm_ffn, norm_final, norm_mem, w_in_a, w_conv_qkv, a_log, dt_bias, gdn_norm, w_conv_sc, w_out_a, w_in_c, ret_norm, w_out_c, w_xq, w_xk, w_xv, w_xo, w_group, b_group, w_router, b_router, w_exp_gate, w_exp_up, w_exp_down):
    bp, lp, d = x_prompt.shape
    bs, ls, _ = x_sample.shape
    depth = norm_mix.shape[0]
    n_even = w_in_a.shape[0]
    n_mem = mem_prompt.shape[1]

    qkv_w = 2 * H_A * DK_A + H_A * DV_A
    o_z = qkv_w
    o_b = o_z + H_A * DV_A
    o_a = o_b + H_A
    o_sc = o_a + H_A
    w_a = jnp.concatenate([w_in_a[:, :, :o_b], w_in_a[:, :, o_sc:], w_in_a[:, :, o_b:o_sc],
                           jnp.zeros((n_even, d, PA_COLS - PA_BA - 2 * H_A), F32)], axis=-1).astype(BF16)
    prm = jnp.zeros((n_even, SUBLANES, LANES), F32)
    prm = prm.at[:, 0, H_A:2 * H_A].set(a_log).at[:, 1, H_A:2 * H_A].set(dt_bias)
    w_route = jnp.concatenate([w_group, w_router, jnp.zeros((depth, d, LANES - N_GROUPS - N_EXPERTS), F32)],
                              axis=-1).astype(BF16)
    b_route = jnp.concatenate([b_group, b_router, jnp.zeros((depth, LANES - N_GROUPS - N_EXPERTS), F32)],
                              axis=-1).reshape(depth, 1, LANES)
    wts = dict(norm_mix=norm_mix, norm_x=norm_x, norm_ffn=norm_ffn, norm_final=norm_final,
               w_in_a=w_a, w_conv_qkv=w_conv_qkv, gdn_prm=prm, gdn_norm=gdn_norm, w_conv_sc=w_conv_sc,
               w_out_a=w_out_a.astype(BF16), w_in_c=w_in_c.astype(BF16), ret_norm=ret_norm,
               w_out_c=w_out_c.astype(BF16), w_xq=w_xq.astype(BF16), w_xo=w_xo.astype(BF16),
               w_route=w_route, b_route=b_route,
               w_exp_gate=w_exp_gate.reshape((depth * N_EXPERTS,) + w_exp_gate.shape[2:]),
               w_exp_up=w_exp_up.reshape((depth * N_EXPERTS,) + w_exp_up.shape[2:]),
               w_exp_down=w_exp_down.reshape((depth * N_EXPERTS,) + w_exp_down.shape[2:]))

    memf = mem_prompt.reshape(bp * n_mem, d)
    w_kv = jnp.concatenate([w_xk, w_xv], axis=-1).astype(BF16)
    mk_p, mv_p = kv_proj(memf, norm_mem, w_kv)
    p_cache_mem_k = mk_p.reshape(depth, bp, n_mem, H_X, HD_X)
    p_cache_mem_v = mv_p.reshape(depth, bp, n_mem, H_X, HD_X)

    n_odd = w_in_c.shape[0]
    z_gdn = jnp.zeros((n_even, bp, H_A, DK_A, DV_A), F32)
    z_conv = jnp.zeros((n_even, bp, CONV_A - 1, qkv_w), F32)
    z_sc = jnp.zeros((n_even, bp, CONV_B - 1, D_B), F32)
    z_ret = jnp.zeros((n_odd, bp, H_C, DK_C, DV_C), F32)
    pos_p = jnp.arange(lp, dtype=I32)
    pos_s = 16384 + jnp.arange(ls, dtype=I32)

    grp_p = dict(batch=bp, seqlen=lp, pos=pos_p, gdn=z_gdn, conv=z_conv, sc=z_sc, ret=z_ret,
                 mem_k=p_cache_mem_k, mem_v=p_cache_mem_v)
    grp_s = dict(batch=bs, seqlen=ls, pos=pos_s, gdn=state_gdn, conv=state_gdn_conv, sc=state_sconv,
                 ret=state_ret, mem_k=cache_mem_k, mem_v=cache_mem_v)
    (y_p, p_gdn, p_conv, p_sc, p_ret), (y_s, s_gdn, s_conv, s_sc, s_ret) = _forward(
        x_prompt.reshape(bp * lp, d), x_sample.reshape(bs * ls, d), grp_p, grp_s, wts)
    return (y_p.reshape(bp, lp, d), y_s.reshape(bs, ls, d), p_gdn, p_conv, p_sc, p_ret, p_cache_mem_k,
            p_cache_mem_v, s_gdn, s_conv, s_sc, s_ret)
```

```python
import functools
import math

import jax
import jax.numpy as jnp
from jax import lax
from jax.experimental import pallas as pl
from jax.experimental.pallas import tpu as pltpu

F32 = jnp.float32
BF16 = jnp.bfloat16
I32 = jnp.int32

EPS = 1e-6
ROPE_BASE = 10000.0

D_MODEL = 1024
H_A, DK_A, DV_A, CONV_A = 4, 128, 128, 4
W_QKV_A = 3 * H_A * DK_A
D_B, CONV_B = D_MODEL // 2, 3
H_C, DK_C, DV_C = 4, 256, 512
H_X, HD_X, N_MEM = 4, 256, 256
N_GROUPS, E_PER_GROUP, N_EXPERTS, D_EXPERT = 4, 8, 32, 512
GDN_CHUNK = 64

LANES = 128
SUBLANES = 8
GDN_STACK = 256
VMEM_LIMIT = 56 * 1024 * 1024

PA_COLS = 3840
PA_SC = 2048
PA_BA = 3584
R_E0 = N_GROUPS
MOE_BLK = 256


def _cparams(sem):
    return pltpu.CompilerParams(dimension_semantics=sem, vmem_limit_bytes=VMEM_LIMIT)


def _dot(a, b, trans_a=False, trans_b=False):
    dn = (((0 if trans_a else 1,), (1 if trans_b else 0,)), ((), ()))
    return lax.dot_general(a.astype(BF16), b.astype(BF16), dn, preferred_element_type=F32)


def _silu(x):
    return x * (1.0 / (1.0 + jnp.exp(-x)))


def _sigmoid(x):
    return 1.0 / (1.0 + jnp.exp(-x))


def _rms_matmul_kernel(x_ref, g_ref, w_ref, o_ref, xn_ref):
    @pl.when(pl.program_id(1) == 0)
    def _():
        x = x_ref[...]
        ms = jnp.mean(x * x, axis=-1, keepdims=True)
        xn_ref[...] = (x * lax.rsqrt(ms + EPS) * g_ref[...]).astype(BF16)

    o_ref[...] = jnp.dot(xn_ref[...], w_ref[...], preferred_element_type=F32).astype(o_ref.dtype)


def rms_matmul(x, g, w, *, tn, rows=None, row0=0, out_dtype=F32, tm=1024):
    t, d = (rows or x.shape[0]), x.shape[1]
    n = w.shape[1]
    tm = min(tm, t)
    blk0 = row0 // tm
    return pl.pallas_call(
        _rms_matmul_kernel,
        grid=(t // tm, n // tn),
        in_specs=[pl.BlockSpec((tm, d), lambda i, j: (i + blk0, 0)),
                  pl.BlockSpec((1, d), lambda i, j: (0, 0)),
                  pl.BlockSpec((d, tn), lambda i, j: (0, j))],
        out_specs=pl.BlockSpec((tm, tn), lambda i, j: (i, j)),
        out_shape=jax.ShapeDtypeStruct((t, n), out_dtype),
        scratch_shapes=[pltpu.VMEM((tm, d), BF16)],
        compiler_params=_cparams(("parallel", "arbitrary")),
        name="rms_matmul",
    )(x, g.reshape(1, d), w)


def _kv_proj_kernel(x_ref, g_ref, w_ref, k_ref, v_ref):
    x = x_ref[...]
    ms = jnp.mean(x * x, axis=-1, keepdims=True)
    xn = (x * lax.rsqrt(ms + EPS) * g_ref[0]).astype(BF16)
    d = x.shape[1]
    k_ref[0] = jnp.dot(xn, w_ref[0, :, :d], preferred_element_type=F32)
    v_ref[0] = jnp.dot(xn, w_ref[0, :, d:], preferred_element_type=F32)


def kv_proj(x, g, w_kv, *, tm=1024):
    t, d = x.shape
    depth = g.shape[0]
    tm = min(tm, t)
    out_spec = pl.BlockSpec((1, tm, d), lambda l, i: (l, i, 0))
    return pl.pallas_call(
        _kv_proj_kernel,
        grid=(depth, t // tm),
        in_specs=[pl.BlockSpec((tm, d), lambda l, i: (i, 0)),
                  pl.BlockSpec((1, 1, d), lambda l, i: (l, 0, 0)),
                  pl.BlockSpec((1, d, 2 * d), lambda l, i: (l, 0, 0))],
        out_specs=[out_spec, out_spec],
        out_shape=[jax.ShapeDtypeStruct((depth, t, d), F32), jax.ShapeDtypeStruct((depth, t, d), F32)],
        compiler_params=_cparams(("parallel", "parallel")),
        name="kv_proj",
    )(x, g.reshape(depth, 1, d), w_kv)


def _matmul_res_kernel(a_ref, w_ref, r_ref, *rest):
    o_ref = rest[-1]
    o_ref[...] = r_ref[...] + jnp.dot(a_ref[...].astype(BF16), w_ref[...], preferred_element_type=F32)


def matmul_res(a, w, res, *, res_row0=0, into=None, out_row0=0, tm=512):
    t, k = a.shape
    n = w.shape[1]
    tm = min(tm, t)
    rblk, oblk = res_row0 // tm, out_row0 // tm
    in_specs = [pl.BlockSpec((tm, k), lambda i: (i, 0)),
                pl.BlockSpec((k, n), lambda i: (0, 0)),
                pl.BlockSpec((tm, n), lambda i: (i + rblk, 0))]
    args = (a, w, res)
    if into is not None:
        in_specs.append(pl.BlockSpec(memory_space=pl.ANY))
        args += (into,)
    return pl.pallas_call(
        _matmul_res_kernel,
        grid=(t // tm,),
        in_specs=in_specs,
        out_specs=pl.BlockSpec((tm, n), lambda i: (i + oblk, 0)),
        out_shape=jax.ShapeDtypeStruct((t, n) if into is None else into.shape, F32),
        input_output_aliases={} if into is None else {3: 0},
        compiler_params=_cparams(("parallel",)),
        name="matmul_res",
    )(*args)


def _causal_conv(x, hist, w_ref, width, seq8):
    r = x.shape[0]
    taps = [w_ref[j:j + 1, :] for j in range(width)]

    def head(x8, h8):
        n = x8.shape[0]
        t = lax.broadcasted_iota(I32, (n, 1), 0) % SUBLANES
        y = taps[width - 1] * x8
        for s in range(1, width):
            prev = pltpu.roll(h8, (n + s - (width - 1)) % n, 0) if s != width - 1 else h8
            y = y + taps[width - 1 - s] * jnp.where(t >= s, pltpu.roll(x8, s, 0), prev)
        return y

    if seq8:
        return head(x, hist)
    y = taps[width - 1] * x
    for s in range(1, width):
        y = y + taps[width - 1 - s] * pltpu.roll(x, s, 0)
    return jnp.concatenate([head(x[:SUBLANES], hist), y[SUBLANES:]], axis=0)


def _unit_lower_inverse(ms, c, ri, ci):
    base = min(c, 16)
    eye = jnp.where(ri == ci, 1.0, 0.0).astype(F32)
    blk = (ri // base) == (ci // base)
    ds = [jnp.where(blk, m, 0.0) for m in ms]
    ps = [eye - d for d in ds]
    k = 2
    while k < base:
        ds = [_dot(d, d) for d in ds]
        ps = [_dot(p, eye + d) for p, d in zip(ps, ds)]
        k *= 2
    s = base
    while s < c:
        sel = ((ri // (2 * s)) == (ci // (2 * s))) & ((ri // s) != (ci // s))
        ts = [_dot(jnp.where(sel, m, 0.0), p) for m, p in zip(ms, ps)]
        ps = [p - _dot(p, t) for p, t in zip(ps, ts)]
        s *= 2
    return ps


def _gdn_prepare(units, c):
    n = GDN_STACK
    ri = lax.broadcasted_iota(I32, (n, n), 0)
    ci = lax.broadcasted_iota(I32, (n, n), 1)
    same = (ri // c) == (ci // c)
    incl = same & (ri >= ci)
    strict = same & (ri > ci)
    pre = []
    for q, k, v, bfull, gfull in units:
        g2 = jnp.concatenate([gfull, gfull], axis=1)
        g_row = jnp.sum(jnp.where(ri == ci, g2, 0.0), axis=0, keepdims=True)
        gc_col = jnp.sum(jnp.where(incl, g_row, 0.0), axis=1, keepdims=True)
        gc_row = jnp.sum(jnp.where(same & (ri <= ci), g2, 0.0), axis=0, keepdims=True)
        gl_col = jnp.sum(jnp.where(same, g_row, 0.0), axis=1, keepdims=True)
        decay = jnp.where(incl, jnp.exp(jnp.where(incl, gc_col - gc_row, 0.0)), 0.0)
        egc = jnp.exp(gc_col)
        kb = k * bfull
        pre.append(dict(decay=decay, kb=kb, rhs=jnp.concatenate([v * bfull, kb * egc], axis=1),
                        qd=q * egc, kd=k * jnp.exp(gl_col - gc_col), egl=jnp.exp(gl_col)))
    mms = [jnp.where(strict, _dot(e["kb"], u[1], trans_b=True) * e["decay"], 0.0) for e, u in zip(pre, units)]
    qks = [_dot(u[0], u[1], trans_b=True) * e["decay"] for e, u in zip(pre, units)]
    tinvs = _unit_lower_inverse(mms, c, ri, ci)
    uws = [_dot(t, e["rhs"]) for t, e in zip(tinvs, pre)]
    return [dict(u=uw[:, :DV_A], w=uw[:, DV_A:], qk=qk, qd=e["qd"], kd=e["kd"], egl=e["egl"])
            for uw, qk, e in zip(uws, qks, pre)]


def _gdn_recur(e, states, c):
    nprob = GDN_STACK // c
    ws, qs = [], []
    for p in range(nprob):
        sl = slice(p * c, (p + 1) * c)
        ws.append(_dot(e["w"][sl], states[p]))
        qs.append(_dot(e["qd"][sl], states[p]))
    vn = e["u"] - jnp.concatenate(ws, axis=0)
    o = _dot(e["qk"], vn) + jnp.concatenate(qs, axis=0)
    new_states = []
    for p in range(nprob):
        sl = slice(p * c, (p + 1) * c)
        new_states.append(states[p] * e["egl"][p * c:p * c + 1, :] + _dot(e["kd"][sl], vn[sl], trans_a=True))
    return o, new_states


def _gdn_compute(x, z, u_sc, scb, ba, hq, hs, wq_ref, ws_ref, prm_ref, gn, states, *, seq8, c, nu):
    unit = GDN_CHUNK
    xc = _silu(_causal_conv(x, hq, wq_ref, CONV_A, seq8))
    yb = scb * _causal_conv(u_sc, hs, ws_ref, CONV_B, seq8)
    beta_all = _sigmoid(ba)
    sp = jnp.maximum(ba + prm_ref[1:2, :], 0.0) + jnp.log1p(jnp.exp(-jnp.abs(ba + prm_ref[1:2, :])))
    g_all = -jnp.exp(prm_ref[0:1, :]) * sp

    def head_cols(a, base):
        return a[:, base * DK_A:(base + 1) * DK_A]

    cat = lambda xs: jnp.concatenate(xs, axis=0)
    units = []
    for ui in range(nu):
        rs = slice(ui * unit, (ui + 1) * unit)
        qs, ks, vs, bs, gs = [], [], [], [], []
        for h in range(H_A):
            qh = head_cols(xc, h)[rs]
            kh = head_cols(xc, H_A + h)[rs]
            qs.append(qh * lax.rsqrt(jnp.sum(qh * qh, axis=-1, keepdims=True) + EPS) * (DK_A ** -0.5))
            ks.append(kh * lax.rsqrt(jnp.sum(kh * kh, axis=-1, keepdims=True) + EPS))
            vs.append(head_cols(xc, 2 * H_A + h)[rs])
            bs.append(jnp.broadcast_to(beta_all[rs, h:h + 1], (unit, LANES)))
            gs.append(jnp.broadcast_to(g_all[rs, H_A + h:H_A + h + 1], (unit, LANES)))
        units.append((cat(qs), cat(ks), cat(vs), cat(bs), cat(gs)))
    prepared = _gdn_prepare(units, c)

    outs = []
    for ui in range(nu):
        rs = slice(ui * unit, (ui + 1) * unit)
        o, states = _gdn_recur(prepared[ui], states, c)
        zst = cat([z[rs, h * DV_A:(h + 1) * DV_A] for h in range(H_A)])
        ms = jnp.mean(o * o, axis=-1, keepdims=True)
        og = o * lax.rsqrt(ms + EPS) * gn * _silu(zst)
        outs.append(jnp.concatenate([og[h * unit:(h + 1) * unit] for h in range(H_A)], axis=1))
    o_all = outs[0] if nu == 1 else cat(outs)
    return jnp.concatenate([o_all, yb], axis=1), states


def _gdn_kernel(qkv_ref, z_ref, sch_ref, scb_ref, scc_ref, ba_ref, hq_ref, hs_ref, wq_ref, ws_ref,
                prm_ref, gn_ref, s0_ref, o_ref, sn_ref, cq_ref, cs_ref):
    x = qkv_ref[...]
    rows = x.shape[0]
    u_sc = scc_ref[...] * sch_ref[...]
    cq_ref[...] = pltpu.roll(x, rows - SUBLANES + CONV_A - 1, 0)
    cs_ref[...] = pltpu.roll(u_sc, rows - SUBLANES + CONV_B - 1, 0)
    nprob = GDN_STACK // SUBLANES
    states = [s0_ref[p % SUBLANES, p // SUBLANES] for p in range(nprob)]
    mix, states = _gdn_compute(x, z_ref[...], u_sc, scb_ref[...], ba_ref[...], hq_ref[...], hs_ref[...],
                               wq_ref, ws_ref, prm_ref, gn_ref[...], states, seq8=True, c=SUBLANES, nu=1)
    o_ref[...] = mix.astype(o_ref.dtype)
    for p in range(nprob):
        sn_ref[p % SUBLANES, p // SUBLANES] = states[p]


def _gdn_block_kernel(x_ref, g_ref, wi_ref, wo_ref, hq_ref, hs_ref, wq_ref, ws_ref, prm_ref, gn_ref, s0_ref,
                      o_ref, sn_ref, cq_ref, cs_ref, s_scr, hq_scr, hs_scr, *, nu, nl):
    l = pl.program_id(1)

    @pl.when(l == 0)
    def _():
        hq_scr[...] = hq_ref[0]
        hs_scr[...] = hs_ref[0]
        s_scr[...] = s0_ref[0]

    xres = x_ref[...]
    rows = xres.shape[0]
    ms = jnp.mean(xres * xres, axis=-1, keepdims=True)
    xn = (xres * lax.rsqrt(ms + EPS) * g_ref[...]).astype(BF16)
    proj = lambda lo, width: jnp.dot(xn, wi_ref[:, lo:lo + width], preferred_element_type=F32)
    x = proj(0, W_QKV_A)
    z = proj(W_QKV_A, H_A * DV_A)
    u_sc = proj(PA_SC + 2 * D_B, D_B) * proj(PA_SC, D_B)
    scb = proj(PA_SC + D_B, D_B)
    ba = proj(PA_BA, LANES)
    hq = hq_scr[...]
    hs = hs_scr[...]
    hq_scr[...] = pltpu.roll(x[rows - SUBLANES:], CONV_A - 1, 0)
    hs_scr[...] = pltpu.roll(u_sc[rows - SUBLANES:], CONV_B - 1, 0)
    states = [s_scr[p] for p in range(H_A)]
    mix, states = _gdn_compute(x, z, u_sc, scb, ba, hq, hs, wq_ref, ws_ref, prm_ref, gn_ref[...], states,
                               seq8=False, c=GDN_CHUNK, nu=nu)
    for p in range(H_A):
        s_scr[p] = states[p]
    o_ref[...] = xres + jnp.dot(mix.astype(BF16), wo_ref[...], preferred_element_type=F32)

    @pl.when(l == nl - 1)
    def _():
        sn_ref[0] = s_scr[...]
        cq_ref[0] = hq_scr[...]
        cs_ref[0] = hs_scr[...]


def gdn_core(p, hist_q, hist_s, w_conv_qkv, w_conv_sc, prm, gn, s0, *, batch, seqlen):
    t = batch * seqlen
    assert seqlen == SUBLANES
    rows = GDN_CHUNK
    nb = rows // seqlen
    hq_spec = pl.BlockSpec((rows, W_QKV_A), lambda i: (i, 0))
    hs_spec = pl.BlockSpec((rows, D_B), lambda i: (i, 0))
    s_spec = pl.BlockSpec((nb, H_A, DK_A, DV_A), lambda i: (i, 0, 0, 0))
    col = lambda width, blk: pl.BlockSpec((rows, width), lambda i: (i, blk))
    const = lambda shape: pl.BlockSpec(shape, lambda i: (0,) * len(shape))
    return pl.pallas_call(
        _gdn_kernel,
        grid=(batch // nb,),
        in_specs=[col(W_QKV_A, 0), col(D_B, 3), col(D_B, 4), col(D_B, 5), col(D_B, 6),
                  col(LANES, PA_BA // LANES), hq_spec, hs_spec,
                  const((CONV_A, W_QKV_A)), const((CONV_B, D_B)), const((SUBLANES, LANES)),
                  const((1, DV_A)), s_spec],
        out_specs=[pl.BlockSpec((rows, D_MODEL), lambda i: (i, 0)), s_spec, hq_spec, hs_spec],
        out_shape=[jax.ShapeDtypeStruct((t, D_MODEL), BF16),
                   jax.ShapeDtypeStruct((batch, H_A, DK_A, DV_A), F32),
                   jax.ShapeDtypeStruct((t, W_QKV_A), F32),
                   jax.ShapeDtypeStruct((t, D_B), F32)],
        compiler_params=_cparams(("parallel",)),
        name="gdn_core",
    )(p, p, p, p, p, p, hist_q.reshape(t, W_QKV_A), hist_s.reshape(t, D_B), w_conv_qkv, w_conv_sc, prm,
      gn.reshape(1, DV_A), s0)


def _resident(shape):
    return pl.BlockSpec(shape, lambda *ix: (0,) * len(shape), pipeline_mode=pl.Buffered(1))


def gdn_block(x, g, w_in, w_out, hist_q, hist_s, w_conv_qkv, w_conv_sc, prm, gn, s0, *, batch, seqlen):
    t, d = batch * seqlen, x.shape[1]
    rows = min(seqlen, 256)
    nu = rows // GDN_CHUNK
    nl = seqlen // rows
    per_b = lambda shape: pl.BlockSpec((1,) + shape, lambda b, l: (b,) + (0,) * len(shape))
    row_spec = pl.BlockSpec((rows, d), lambda b, l: (b * nl + l, 0))
    kern = functools.partial(_gdn_block_kernel, nu=nu, nl=nl)
    return pl.pallas_call(
        kern,
        grid=(batch, nl),
        in_specs=[row_spec, _resident((1, d)), _resident(w_in.shape), _resident(w_out.shape),
                  per_b((SUBLANES, W_QKV_A)), per_b((SUBLANES, D_B)),
                  _resident((CONV_A, W_QKV_A)), _resident((CONV_B, D_B)), _resident((SUBLANES, LANES)),
                  _resident((1, DV_A)), per_b((H_A, DK_A, DV_A))],
        out_specs=[row_spec, per_b((H_A, DK_A, DV_A)), per_b((SUBLANES, W_QKV_A)), per_b((SUBLANES, D_B))],
        out_shape=[jax.ShapeDtypeStruct((t, d), F32),
                   jax.ShapeDtypeStruct((batch, H_A, DK_A, DV_A), F32),
                   jax.ShapeDtypeStruct((batch, SUBLANES, W_QKV_A), F32),
                   jax.ShapeDtypeStruct((batch, SUBLANES, D_B), F32)],
        scratch_shapes=[pltpu.VMEM((H_A, DK_A, DV_A), F32),
                        pltpu.VMEM((SUBLANES, W_QKV_A), F32),
                        pltpu.VMEM((SUBLANES, D_B), F32)],
        compiler_params=_cparams(("parallel", "arbitrary")),
        name="gdn_block",
    )(x, g.reshape(1, d), w_in, w_out, hist_q, hist_s, w_conv_qkv, w_conv_sc, prm, gn.reshape(1, DV_A), s0)


def _ret_compute(get_q, get_k, get_v, get_gate, cos, sin, dm_ref, qd_ref, kd_ref, cd_ref, rn_ref, r_scr,
                 *, nseq, c):
    half = DK_C // 2

    def rot(x):
        x1, x2 = x[:, :half], x[:, half:]
        return jnp.concatenate([x1 * cos - x2 * sin, x1 * sin + x2 * cos], axis=1)

    heads = range(H_C)
    qs = [rot(get_q(h)) for h in heads]
    ks = [rot(get_k(h)) * (DK_C ** -0.5) for h in heads]
    vs = [get_v(h).astype(BF16) for h in heads]
    ss = [_dot(qs[h], ks[h], trans_b=True) * dm_ref[h] for h in heads]
    inters = []
    for h in heads:
        qdh = qs[h] * qd_ref[h]
        parts = [_dot(qdh[sq * c:(sq + 1) * c], r_scr[sq, h]) for sq in range(nseq)]
        inters.append(parts[0] if nseq == 1 else jnp.concatenate(parts, axis=0))
    outs = [_dot(ss[h], vs[h]) + inters[h] for h in heads]
    for h in heads:
        kdh = ks[h] * kd_ref[h]
        cd = cd_ref[h][0:1, 0:1]
        for sq in range(nseq):
            sl = slice(sq * c, (sq + 1) * c)
            r_scr[sq, h] = r_scr[sq, h] * cd + _dot(kdh[sl], vs[h][sl], trans_a=True)
    gated = []
    for h in heads:
        o = outs[h]
        ms = jnp.mean(o * o, axis=-1, keepdims=True)
        on = o * lax.rsqrt(ms + EPS) * rn_ref[:, h * DV_C:(h + 1) * DV_C]
        gated.append(_silu(get_gate(h)) * on)
    return gated


def _ret_kernel(q_ref, k_ref, v_ref, gate_ref, cos_ref, sin_ref, dm_ref, qd_ref, kd_ref, cd_ref,
                rn_ref, r0_ref, o_ref, rnew_ref, r_scr, *, nseq, c, nl):
    l = pl.program_id(1)

    @pl.when(l == 0)
    def _():
        r_scr[...] = r0_ref[...]

    gated = _ret_compute(lambda h: q_ref[:, h * DK_C:(h + 1) * DK_C], lambda h: k_ref[:, h * DK_C:(h + 1) * DK_C],
                         lambda h: v_ref[:, h * DV_C:(h + 1) * DV_C], lambda h: gate_ref[:, h * DV_C:(h + 1) * DV_C],
                         cos_ref[...], sin_ref[...], dm_ref, qd_ref, kd_ref, cd_ref, rn_ref, r_scr, nseq=nseq, c=c)
    for h in range(H_C):
        o_ref[:, h * DV_C:(h + 1) * DV_C] = gated[h].astype(o_ref.dtype)

    @pl.when(l == nl - 1)
    def _():
        rnew_ref[...] = r_scr[...]


def _ret_block_kernel(x_ref, g_ref, wi_ref, wo_ref, cos_ref, sin_ref, dm_ref, qd_ref, kd_ref, cd_ref,
                      rn_ref, r0_ref, o_ref, rnew_ref, r_scr, *, c, nl):
    l = pl.program_id(1)

    @pl.when(l == 0)
    def _():
        r_scr[...] = r0_ref[...]

    xres = x_ref[...]
    ms = jnp.mean(xres * xres, axis=-1, keepdims=True)
    xn = (xres * lax.rsqrt(ms + EPS) * g_ref[...]).astype(BF16)
    proj = lambda lo, width: jnp.dot(xn, wi_ref[:, lo:lo + width], preferred_element_type=F32)
    hk, hv = H_C * DK_C, H_C * DV_C
    gated = _ret_compute(lambda h: proj(h * DK_C, DK_C), lambda h: proj(hk + h * DK_C, DK_C),
                         lambda h: proj(2 * hk + h * DV_C, DV_C), lambda h: proj(2 * hk + hv + h * DV_C, DV_C),
                         cos_ref[...], sin_ref[...], dm_ref, qd_ref, kd_ref, cd_ref, rn_ref, r_scr, nseq=1, c=c)
    y = xres
    for h in range(H_C):
        y = y + jnp.dot(gated[h].astype(BF16), wo_ref[h * DV_C:(h + 1) * DV_C, :], preferred_element_type=F32)
    o_ref[...] = y

    @pl.when(l == nl - 1)
    def _():
        rnew_ref[...] = r_scr[...]


def ret_core(p, pos, ret_norm, r0, *, batch, seqlen):
    t = batch * seqlen
    assert seqlen == SUBLANES
    nseq, c = 2, seqlen
    rows = nseq * c
    cos, sin, dmat, qd, kd, cd = _ret_tables(pos, nseq, c)
    const = lambda shape: pl.BlockSpec(shape, lambda b, l: (0,) * len(shape))
    kern = functools.partial(_ret_kernel, nseq=nseq, c=c, nl=1)
    hk = H_C * DK_C
    hv = H_C * DV_C
    return pl.pallas_call(
        kern,
        grid=(batch // nseq, 1),
        in_specs=[pl.BlockSpec((rows, hk), lambda b, l: (b, 0)),
                  pl.BlockSpec((rows, hk), lambda b, l: (b, 1)),
                  pl.BlockSpec((rows, hv), lambda b, l: (b, 1)),
                  pl.BlockSpec((rows, hv), lambda b, l: (b, 2)),
                  const((rows, DK_C // 2)), const((rows, DK_C // 2)),
                  const((H_C, rows, rows)), const((H_C, rows, DK_C)), const((H_C, rows, DK_C)),
                  const((H_C, SUBLANES, LANES)), const((1, hv)),
                  pl.BlockSpec((nseq, H_C, DK_C, DV_C), lambda b, l: (b, 0, 0, 0))],
        out_specs=[pl.BlockSpec((rows, hv), lambda b, l: (b, 0)),
                   pl.BlockSpec((nseq, H_C, DK_C, DV_C), lambda b, l: (b, 0, 0, 0))],
        out_shape=[jax.ShapeDtypeStruct((t, hv), BF16),
                   jax.ShapeDtypeStruct((batch, H_C, DK_C, DV_C), F32)],
        scratch_shapes=[pltpu.VMEM((nseq, H_C, DK_C, DV_C), F32)],
        compiler_params=_cparams(("parallel", "arbitrary")),
        name="ret_core",
    )(p, p, p, p, cos, sin, dmat, qd, kd, cd, ret_norm.reshape(1, hv), r0)


def _ret_tables(pos, nseq, c):
    half = DK_C // 2
    inv = ROPE_BASE ** (-jnp.arange(half, dtype=F32) / half)
    ang = pos.astype(F32)[:, None] * inv[None, :]
    cos, sin = jnp.cos(ang), jnp.sin(ang)
    if nseq > 1:
        cos, sin = jnp.tile(cos, (nseq, 1)), jnp.tile(sin, (nseq, 1))
    lg = jnp.log(1.0 - 2.0 ** (-5.0 - jnp.arange(H_C, dtype=F32)))[:, None]
    i = jnp.arange(c, dtype=F32)
    incl = i[:, None] >= i[None, :]
    dmat = jnp.exp(jnp.where(incl[None], (i[:, None] - i[None, :])[None] * lg[..., None], -jnp.inf))
    if nseq > 1:
        dmat = jnp.kron(jnp.eye(nseq, dtype=F32)[None], dmat)
    qd = jnp.tile(jnp.exp((i + 1.0)[None] * lg), (1, nseq))[..., None] * jnp.ones((1, 1, DK_C), F32)
    kd = jnp.tile(jnp.exp((c - 1.0 - i)[None] * lg), (1, nseq))[..., None] * jnp.ones((1, 1, DK_C), F32)
    cd = jnp.exp(c * lg)[..., None] * jnp.ones((1, SUBLANES, LANES), F32)
    return cos, sin, dmat, qd, kd, cd


def ret_block(x, g, w_in, w_out, pos, ret_norm, r0, *, batch, seqlen):
    t, d = batch * seqlen, x.shape[1]
    c = min(seqlen, 256)
    nl = seqlen // c
    cos, sin, dmat, qd, kd, cd = _ret_tables(pos, 1, c)
    hv = H_C * DV_C
    row_spec = pl.BlockSpec((c, d), lambda b, l: (b * nl + l, 0))
    trig_spec = pl.BlockSpec((c, DK_C // 2), lambda b, l: (l, 0))
    state_spec = pl.BlockSpec((1, H_C, DK_C, DV_C), lambda b, l: (b, 0, 0, 0))
    kern = functools.partial(_ret_block_kernel, c=c, nl=nl)
    return pl.pallas_call(
        kern,
        grid=(batch, nl),
        in_specs=[row_spec, _resident((1, d)), _resident(w_in.shape), _resident(w_out.shape),
                  trig_spec, trig_spec,
                  _resident((H_C, c, c)), _resident((H_C, c, DK_C)), _resident((H_C, c, DK_C)),
                  _resident((H_C, SUBLANES, LANES)), _resident((1, hv)), state_spec],
        out_specs=[row_spec, state_spec],
        out_shape=[jax.ShapeDtypeStruct((t, d), F32),
                   jax.ShapeDtypeStruct((batch, H_C, DK_C, DV_C), F32)],
        scratch_shapes=[pltpu.VMEM((1, H_C, DK_C, DV_C), F32)],
        compiler_params=_cparams(("parallel", "arbitrary")),
        name="ret_block",
    )(x, g.reshape(1, d), w_in, w_out, cos, sin, dmat, qd, kd, cd, ret_norm.reshape(1, hv), r0)


def _xattn_fetch(mk_hbm, mv_hbm, kbuf, vbuf, sem, *, layer, nb, nsteps):
    i = pl.program_id(0)
    l = pl.program_id(1)
    slot = i % 2

    def copies(step, slot_):
        out = []
        for b in range(nb):
            for h in range(H_X):
                out.append(pltpu.make_async_copy(mk_hbm.at[layer, step * nb + b, :, h, :],
                                                 kbuf.at[slot_, b, h], sem.at[slot_]))
                out.append(pltpu.make_async_copy(mv_hbm.at[layer, step * nb + b, :, h, :],
                                                 vbuf.at[slot_, b, h], sem.at[slot_]))
        return out

    @pl.when(l == 0)
    def _():
        @pl.when(i == 0)
        def _():
            for c in copies(i, slot):
                c.start()

        @pl.when(i + 1 < nsteps)
        def _():
            for c in copies(i + 1, 1 - slot):
                c.start()

        for c in copies(i, slot):
            c.wait()

    return slot


def _xattn_heads(q_of, kbuf, vbuf, slot, probs):
    ss = [_dot(q_of(b, h), kbuf[slot, b, h], trans_b=True) * (HD_X ** -0.5) for b, h in probs]
    ps = []
    for s in ss:
        e = jnp.exp(s - jnp.max(s, axis=-1, keepdims=True))
        ps.append(e / jnp.sum(e, axis=-1, keepdims=True))
    return [_dot(p, vbuf[slot, b, h]) for p, (b, h) in zip(ps, probs)]


def _xattn_kernel(q_ref, mk_hbm, mv_hbm, o_ref, kbuf, vbuf, sem, *, layer, nb, lq, nsteps):
    slot = _xattn_fetch(mk_hbm, mv_hbm, kbuf, vbuf, sem, layer=layer, nb=nb, nsteps=nsteps)
    probs = [(b, h) for b in range(nb) for h in range(H_X)]
    win = lambda b, h: (slice(b * lq, (b + 1) * lq), slice(h * HD_X, (h + 1) * HD_X))
    outs = _xattn_heads(lambda b, h: q_ref[win(b, h)], kbuf, vbuf, slot, probs)
    for o, (b, h) in zip(outs, probs):
        o_ref[win(b, h)] = o.astype(o_ref.dtype)


def _xattn_block_kernel(x_ref, g_ref, wq_ref, wo_ref, mk_hbm, mv_hbm, into_hbm, o_ref, kbuf, vbuf, sem,
                        *, layer, nsteps):
    del into_hbm
    slot = _xattn_fetch(mk_hbm, mv_hbm, kbuf, vbuf, sem, layer=layer, nb=1, nsteps=nsteps)
    xres = x_ref[...]
    ms = jnp.mean(xres * xres, axis=-1, keepdims=True)
    xn = (xres * lax.rsqrt(ms + EPS) * g_ref[...]).astype(BF16)
    probs = [(0, h) for h in range(H_X)]
    qs = [jnp.dot(xn, wq_ref[:, h * HD_X:(h + 1) * HD_X], preferred_element_type=F32) for h in range(H_X)]
    outs = _xattn_heads(lambda b, h: qs[h], kbuf, vbuf, slot, probs)
    y = xres
    for h in range(H_X):
        y = y + jnp.dot(outs[h].astype(BF16), wo_ref[h * HD_X:(h + 1) * HD_X, :], preferred_element_type=F32)
    o_ref[...] = y


def xattn_core(q, mk, mv, layer, *, batch, seqlen):
    t = batch * seqlen
    d = H_X * HD_X
    assert seqlen == SUBLANES
    nb, lq = 2, seqlen
    rows = nb * lq
    nsteps = batch // nb
    kern = functools.partial(_xattn_kernel, layer=layer, nb=nb, lq=lq, nsteps=nsteps)
    return pl.pallas_call(
        kern,
        grid=(nsteps, 1),
        in_specs=[pl.BlockSpec((rows, d), lambda b, l: (b, 0)),
                  pl.BlockSpec(memory_space=pl.ANY), pl.BlockSpec(memory_space=pl.ANY)],
        out_specs=pl.BlockSpec((rows, d), lambda b, l: (b, 0)),
        out_shape=jax.ShapeDtypeStruct((t, d), q.dtype),
        scratch_shapes=[pltpu.VMEM((2, nb, H_X, N_MEM, HD_X), F32),
                        pltpu.VMEM((2, nb, H_X, N_MEM, HD_X), F32),
                        pltpu.SemaphoreType.DMA((2,))],
        compiler_params=_cparams(("arbitrary", "arbitrary")),
        name="xattn_core",
    )(q, mk, mv)


def xattn_block(x, g, w_q, w_o, mk, mv, layer, into, *, batch, seqlen):
    d = x.shape[1]
    lq = min(seqlen, 512)
    nl = seqlen // lq
    row_spec = pl.BlockSpec((lq, d), lambda b, l: (b * nl + l, 0))
    kern = functools.partial(_xattn_block_kernel, layer=layer, nsteps=batch)
    return pl.pallas_call(
        kern,
        grid=(batch, nl),
        in_specs=[row_spec, _resident((1, d)), _resident(w_q.shape), _resident(w_o.shape),
                  pl.BlockSpec(memory_space=pl.ANY), pl.BlockSpec(memory_space=pl.ANY),
                  pl.BlockSpec(memory_space=pl.ANY)],
        out_specs=row_spec,
        out_shape=jax.ShapeDtypeStruct(into.shape, F32),
        scratch_shapes=[pltpu.VMEM((2, 1, H_X, N_MEM, HD_X), F32),
                        pltpu.VMEM((2, 1, H_X, N_MEM, HD_X), F32),
                        pltpu.SemaphoreType.DMA((2,))],
        input_output_aliases={6: 0},
        compiler_params=_cparams(("arbitrary", "arbitrary")),
        name="xattn_block",
    )(x, g.reshape(1, d), w_q, w_o, mk, mv, into)


def _route_kernel(x_ref, g_ref, wr_ref, br_ref, h_ref, meta_ref, wts_ref, cnt_ref, cnt_scr):
    i = pl.program_id(0)

    @pl.when(i == 0)
    def _():
        cnt_scr[...] = jnp.zeros_like(cnt_scr)

    x = x_ref[...]
    tm = x.shape[0]
    ms = jnp.mean(x * x, axis=-1, keepdims=True)
    h = x * lax.rsqrt(ms + EPS) * g_ref[...]
    hb = h.astype(BF16)
    bits = pltpu.bitcast(hb.astype(F32), jnp.uint32)
    half_d = bits.shape[1] // 2
    h_ref[...] = bits[:, :half_d] | (bits[:, half_d:] >> 16)
    logits = jnp.dot(hb, wr_ref[...], preferred_element_type=F32) + br_ref[...]
    lane_i = lax.broadcasted_iota(I32, (tm, LANES), 1)
    lane = lane_i.astype(F32)
    neg = jnp.float32(-3.0e38)
    big = jnp.float32(LANES)
    is_g = lane_i < N_GROUPS
    gl = jnp.where(is_g, logits, neg)
    gmax = jnp.max(gl, axis=1, keepdims=True)
    grp = jnp.min(jnp.where(gl == gmax, lane, big), axis=1, keepdims=True)
    gsum = jnp.sum(jnp.where(is_g, jnp.exp(jnp.where(is_g, logits - gmax, 0.0)), 0.0), axis=1, keepdims=True)
    p_grp = 1.0 / gsum
    in_grp = ((lane_i >= R_E0) & (lane_i < R_E0 + N_EXPERTS)
              & (jnp.floor((lane - R_E0) * (1.0 / E_PER_GROUP)) == grp))
    el = jnp.where(in_grp, logits, neg)
    m1 = jnp.max(el, axis=1, keepdims=True)
    i1 = jnp.min(jnp.where(el == m1, lane, big), axis=1, keepdims=True)
    el2 = jnp.where(lane == i1, neg, el)
    m2 = jnp.max(el2, axis=1, keepdims=True)
    i2 = jnp.min(jnp.where(el2 == m2, lane, big), axis=1, keepdims=True)
    esum = jnp.sum(jnp.where(in_grp, jnp.exp(jnp.where(in_grp, logits - m1, 0.0)), 0.0), axis=1, keepdims=True)
    p1 = 1.0 / esum
    p2 = jnp.exp(m2 - m1) / esum
    tot = p1 + p2
    w1 = p_grp * (p1 / tot)
    w2 = p_grp * (p2 / tot)
    wts_ref[...] = jnp.where(lane_i == 0, w1, jnp.where(lane_i == 1, w2, 0.0))

    oh1 = (lane == i1).astype(F32)
    oh2 = (lane == i2).astype(F32)
    rr = lax.broadcasted_iota(I32, (tm, tm), 0)
    cc = lax.broadcasted_iota(I32, (tm, tm), 1)
    tri = (rr > cc).astype(BF16)
    base = cnt_scr[0:1, :]
    c1 = jnp.sum(oh1, axis=0, keepdims=True)
    c2 = jnp.sum(oh2, axis=0, keepdims=True)
    r1 = jnp.sum(oh1 * (jnp.dot(tri, oh1.astype(BF16), preferred_element_type=F32) + base), axis=1, keepdims=True)
    r2 = jnp.sum(oh2 * (jnp.dot(tri, oh2.astype(BF16), preferred_element_type=F32) + base + c1), axis=1, keepdims=True)
    new_cnt = base + c1 + c2
    cnt_scr[...] = jnp.broadcast_to(new_cnt, cnt_scr.shape)
    cnt_ref[...] = jnp.broadcast_to(new_cnt, cnt_ref.shape)
    meta = jnp.where(lane_i == 0, i1, jnp.where(lane_i == 1, i2, 0.0))
    meta = jnp.where(lane_i == 2, r1, jnp.where(lane_i == 3, r2, meta))
    meta_ref[...] = meta.astype(I32)


def moe_route(x, g, wr, br, *, tm=256):
    t, d = x.shape
    tm = min(tm, t)
    return pl.pallas_call(
        _route_kernel,
        grid=(t // tm,),
        in_specs=[pl.BlockSpec((tm, d), lambda i: (i, 0)),
                  pl.BlockSpec((1, d), lambda i: (0, 0)),
                  pl.BlockSpec((d, LANES), lambda i: (0, 0)),
                  pl.BlockSpec((1, LANES), lambda i: (0, 0))],
        out_specs=[pl.BlockSpec((tm, d // 2), lambda i: (i, 0)),
                   pl.BlockSpec((tm, LANES), lambda i: (i, 0)),
                   pl.BlockSpec((tm, LANES), lambda i: (i, 0)),
                   pl.BlockSpec((SUBLANES, LANES), lambda i: (0, 0))],
        out_shape=[jax.ShapeDtypeStruct((t, d // 2), jnp.uint32),
                   jax.ShapeDtypeStruct((t, LANES), I32),
                   jax.ShapeDtypeStruct((t, LANES), F32),
                   jax.ShapeDtypeStruct((SUBLANES, LANES), F32)],
        scratch_shapes=[pltpu.VMEM((SUBLANES, LANES), F32)],
        compiler_params=_cparams(("arbitrary",)),
        name="moe_route",
    )(x, g.reshape(1, d), wr, br)


def _plan_kernel(cnt_ref, meta_ref, dest_ref, ex_ref, be_ref, *, nblk_pad):
    cnt = cnt_ref[...]
    lane8 = lax.broadcasted_iota(I32, (SUBLANES, LANES), 1)
    padded = jnp.ceil(cnt / MOE_BLK) * MOE_BLK
    pend = padded
    s = 1
    while s < LANES:
        pend = pend + jnp.where(lane8 >= s, pltpu.roll(pend, s, 1), 0.0)
        s *= 2
    pstart = (pend - padded)[0:1, :]
    meta = meta_ref[...].astype(F32)
    tm = meta.shape[0]
    lane_i = lax.broadcasted_iota(I32, (tm, LANES), 1)
    lane = lane_i.astype(F32)
    col = lambda j: jnp.sum(jnp.where(lane_i == j, meta, 0.0), axis=1, keepdims=True)
    e1, e2, r1, r2 = col(0), col(1), col(2), col(3)
    d1 = jnp.sum(jnp.where(lane == e1, pstart, 0.0), axis=1, keepdims=True) + r1
    d2 = jnp.sum(jnp.where(lane == e2, pstart, 0.0), axis=1, keepdims=True) + r2
    dest_ref[...] = jnp.where(lane_i == 0, d1, jnp.where(lane_i == 1, d2, 0.0)).astype(I32)
    sub8 = lax.broadcasted_iota(I32, (SUBLANES, LANES), 0)
    zrow = jnp.floor((pend - padded + cnt) / SUBLANES) * SUBLANES
    ex_ref[...] = jnp.where(sub8 == 0, zrow, jnp.where(sub8 == 1, (pend - zrow) / SUBLANES, pend)).astype(I32)
    bi = (lax.broadcasted_iota(I32, (nblk_pad, LANES), 0) * MOE_BLK).astype(F32)
    lane_b = lax.broadcasted_iota(I32, (nblk_pad, LANES), 1)
    is_e = (lane_b >= R_E0) & (lane_b < R_E0 + N_EXPERTS)
    nfull = jnp.sum(jnp.where(is_e & (bi >= pend[0:1, :]), 1.0, 0.0), axis=1, keepdims=True)
    nused = jnp.max(pend[0:1, :], axis=1, keepdims=True) / MOE_BLK
    row = lax.broadcasted_iota(I32, (nblk_pad, LANES), 0)
    be_ref[...] = jnp.where(row == nblk_pad - 1, nused, jnp.minimum(nfull, N_EXPERTS - 1.0)).astype(I32)


def moe_plan(cnt, meta, *, nblk_pad):
    t = meta.shape[0]
    tm = math.gcd(t, 1024)
    kern = functools.partial(_plan_kernel, nblk_pad=nblk_pad)
    return pl.pallas_call(
        kern,
        grid=(t // tm,),
        in_specs=[pl.BlockSpec((SUBLANES, LANES), lambda i: (0, 0)),
                  pl.BlockSpec((tm, LANES), lambda i: (i, 0))],
        out_specs=[pl.BlockSpec((tm, LANES), lambda i: (i, 0)),
                   pl.BlockSpec((SUBLANES, LANES), lambda i: (0, 0)),
                   pl.BlockSpec((nblk_pad, LANES), lambda i: (0, 0))],
        out_shape=[jax.ShapeDtypeStruct((t, LANES), I32),
                   jax.ShapeDtypeStruct((SUBLANES, LANES), I32),
                   jax.ShapeDtypeStruct((nblk_pad, LANES), I32)],
        compiler_params=_cparams(("arbitrary",)),
        name="moe_plan",
    )(cnt, meta)


DMA_UNROLL = 8
DISPATCH_SLOTS = 3


def _dispatch_kernel(dest_ref, zrow_ref, zcnt_ref, end_ref, h_hbm, xd_hbm, hbuf, zbuf, zblk, lsem, ssem, zsem,
                     *, tm, nsteps, nrows):
    i = pl.program_id(0)
    slot = i % DISPATCH_SLOTS

    def load(step, slot_):
        return pltpu.make_async_copy(h_hbm.at[pl.ds(step * tm, tm)], hbuf.at[slot_], lsem.at[slot_])

    def rows_done(slot_):
        return pltpu.make_async_copy(hbuf.at[slot_], xd_hbm.at[pl.ds(0, tm)], ssem.at[slot_])

    def zero_group(e, g):
        row = pl.multiple_of(zrow_ref[e] + g * SUBLANES, SUBLANES)
        return pltpu.make_async_copy(zbuf, xd_hbm.at[pl.ds(row, SUBLANES)], zsem)

    @pl.when(i == 0)
    def _():
        load(0, 0).start()
        if nsteps > 1:
            load(1, 1).start()
        zbuf[...] = jnp.zeros(zbuf.shape, zbuf.dtype)

        def per_expert(fn):
            def body(e, carry):
                lax.fori_loop(0, zcnt_ref[e], lambda g, c: (fn(zero_group(e, g)), c)[1], 0)
                return carry
            lax.fori_loop(0, N_EXPERTS, body, 0)

        per_expert(lambda cp: cp.start())
        per_expert(lambda cp: cp.wait())

        zblk[...] = jnp.zeros(zblk.shape, zblk.dtype)
        ntail = (nrows - end_ref[0]) // MOE_BLK

        def tail_block(j):
            row = pl.multiple_of(end_ref[0] + j * MOE_BLK, MOE_BLK)
            return pltpu.make_async_copy(zblk, xd_hbm.at[pl.ds(row, MOE_BLK)], zsem)

        lax.fori_loop(0, ntail, lambda j, c: (tail_block(j).start(), c)[1], 0)
        lax.fori_loop(0, ntail, lambda j, c: (tail_block(j).wait(), c)[1], 0)

    load(i, slot).wait()

    def start(t, carry):
        for k in range(2):
            pltpu.make_async_copy(hbuf.at[slot, pl.ds(t, 1)], xd_hbm.at[pl.ds(dest_ref[2 * t + k], 1)],
                                  ssem.at[slot]).start(priority=k)
        return carry

    lax.fori_loop(0, tm, start, 0, unroll=DMA_UNROLL)

    @pl.when(i >= 1)
    def _():
        prev = (i + DISPATCH_SLOTS - 1) % DISPATCH_SLOTS
        rows_done(prev).wait()
        rows_done(prev).wait()

    @pl.when(i + 2 < nsteps)
    def _():
        load(i + 2, (i + 2) % DISPATCH_SLOTS).start()

    @pl.when(i == nsteps - 1)
    def _():
        rows_done(slot).wait()
        rows_done(slot).wait()


def moe_dispatch(dest_flat, zrow, zcnt, end, h, nrows, *, tm=1024):
    t, d = h.shape
    tm = min(tm, t)
    nsteps = t // tm
    kern = functools.partial(_dispatch_kernel, tm=tm, nsteps=nsteps, nrows=nrows)
    smem_all = pl.BlockSpec(memory_space=pltpu.SMEM)
    return pl.pallas_call(
        kern,
        grid=(nsteps,),
        in_specs=[pl.BlockSpec((2 * tm,), lambda i: (i,), memory_space=pltpu.SMEM), smem_all, smem_all, smem_all,
                  pl.BlockSpec(memory_space=pl.ANY)],
        out_specs=pl.BlockSpec(memory_space=pl.ANY),
        out_shape=jax.ShapeDtypeStruct((nrows, d), h.dtype),
        scratch_shapes=[pltpu.VMEM((DISPATCH_SLOTS, tm, d), h.dtype),
                        pltpu.VMEM((SUBLANES, d), h.dtype),
                        pltpu.VMEM((MOE_BLK, d), h.dtype),
                        pltpu.SemaphoreType.DMA((DISPATCH_SLOTS,)),
                        pltpu.SemaphoreType.DMA((DISPATCH_SLOTS,)),
                        pltpu.SemaphoreType.DMA],
        compiler_params=_cparams(("arbitrary",)),
        name="moe_dispatch",
    )(dest_flat, zrow, zcnt, end, h)


def _experts_kernel(be_ref, x_ref, wg_ref, wu_ref, wd_ref, o_ref, wg_s, wu_s, wd_s, *, nblk_pad):
    i = pl.program_id(0)
    nused = be_ref[nblk_pad - 1]

    @pl.when(i < nused)
    def _():
        prev = be_ref[jnp.maximum(i - 1, 0)]

        @pl.when((i == 0) | (be_ref[i] != prev))
        def _():
            wg_s[...] = wg_ref[0].astype(BF16)
            wu_s[...] = wu_ref[0].astype(BF16)
            wd_s[...] = wd_ref[0].astype(BF16)

        half = MOE_BLK // 2

        def unpack(w):
            hi = pltpu.bitcast(w & jnp.uint32(0xFFFF0000), F32)
            lo = pltpu.bitcast(w << 16, F32)
            return jnp.concatenate([hi, lo], axis=1).astype(BF16)

        xs = [unpack(x_ref[r * half:(r + 1) * half, :]) for r in range(2)]
        gs = [jnp.dot(x, wg_s[...], preferred_element_type=F32) for x in xs]
        us = [jnp.dot(x, wu_s[...], preferred_element_type=F32) for x in xs]
        acts = [(_silu(g) * u).astype(BF16) for g, u in zip(gs, us)]
        for r in range(2):
            o_ref[r * half:(r + 1) * half, :] = jnp.dot(acts[r], wd_s[...], preferred_element_type=F32)

    @pl.when(i >= nused)
    def _():
        o_ref[...] = jnp.zeros_like(o_ref)


def moe_experts(be_flat, xd, w_gate, w_up, w_down, e0, *, nblk, nblk_pad):
    d, f = w_gate.shape[1], w_gate.shape[2]
    kern = functools.partial(_experts_kernel, nblk_pad=nblk_pad)
    grid_spec = pltpu.PrefetchScalarGridSpec(
        num_scalar_prefetch=1,
        grid=(nblk,),
        in_specs=[pl.BlockSpec((MOE_BLK, d // 2), lambda i, be: (i, 0)),
                  pl.BlockSpec((1, d, f), lambda i, be: (e0 + be[i], 0, 0)),
                  pl.BlockSpec((1, d, f), lambda i, be: (e0 + be[i], 0, 0)),
                  pl.BlockSpec((1, f, d), lambda i, be: (e0 + be[i], 0, 0))],
        out_specs=pl.BlockSpec((MOE_BLK, d), lambda i, be: (i, 0)),
        scratch_shapes=[pltpu.VMEM((d, f), BF16), pltpu.VMEM((d, f), BF16), pltpu.VMEM((f, d), BF16)],
    )
    return pl.pallas_call(
        kern,
        grid_spec=grid_spec,
        out_shape=jax.ShapeDtypeStruct((xd.shape[0], d), F32),
        compiler_params=_cparams(("arbitrary",)),
        name="moe_experts",
    )(be_flat, xd, w_gate, w_up, w_down)


def _combine_kernel(dest_ref, dest_next_ref, x_ref, wts_ref, gf_ref, yd_hbm, o_ref, rbuf, sem,
                    *, tm, nsteps, final_norm):
    i = pl.program_id(0)
    slot = i % 2

    def gather(dref, slot_):
        def start(t, carry):
            for k in range(2):
                pltpu.make_async_copy(yd_hbm.at[pl.ds(dref[2 * t + k], 1)], rbuf.at[slot_, k, pl.ds(t, 1)],
                                      sem.at[slot_]).start(priority=k)
            return carry
        lax.fori_loop(0, tm, start, 0, unroll=DMA_UNROLL)

    @pl.when(i == 0)
    def _():
        gather(dest_ref, 0)

    @pl.when(i + 1 < nsteps)
    def _():
        gather(dest_next_ref, 1 - slot)

    for k in range(2):
        pltpu.make_async_copy(yd_hbm.at[pl.ds(0, tm)], rbuf.at[slot, k], sem.at[slot]).wait()
    w = wts_ref[...]
    y = x_ref[...] + rbuf[slot, 0] * w[:, 0:1] + rbuf[slot, 1] * w[:, 1:2]
    if final_norm:
        y = y * lax.rsqrt(jnp.mean(y * y, axis=-1, keepdims=True) + EPS) * gf_ref[...]
    o_ref[...] = y


def moe_combine(dest_flat, x, wts, yd, g_final, *, row0=0, rows=None, tm=1024):
    t, d = (rows or x.shape[0]), x.shape[1]
    tm = min(tm, t)
    nsteps = t // tm
    blk0 = row0 // tm
    final_norm = g_final is not None
    gf = g_final.reshape(1, d) if final_norm else jnp.ones((1, d), F32)
    kern = functools.partial(_combine_kernel, tm=tm, nsteps=nsteps, final_norm=final_norm)
    return pl.pallas_call(
        kern,
        grid=(nsteps,),
        in_specs=[pl.BlockSpec((2 * tm,), lambda i: (i + blk0,), memory_space=pltpu.SMEM),
                  pl.BlockSpec((2 * tm,), lambda i: (jnp.minimum(i + 1, nsteps - 1) + blk0,),
                               memory_space=pltpu.SMEM),
                  pl.BlockSpec((tm, d), lambda i: (i + blk0, 0)),
                  pl.BlockSpec((tm, LANES), lambda i: (i + blk0, 0)),
                  pl.BlockSpec((1, d), lambda i: (0, 0)),
                  pl.BlockSpec(memory_space=pl.ANY)],
        out_specs=pl.BlockSpec((tm, d), lambda i: (i, 0)),
        out_shape=jax.ShapeDtypeStruct((t, d), F32),
        scratch_shapes=[pltpu.VMEM((2, 2, tm, d), F32), pltpu.SemaphoreType.DMA((2,))],
        compiler_params=_cparams(("arbitrary",)),
        name="moe_combine",
    )(dest_flat, dest_flat, x, wts, gf, yd)


def moe_block(x, g, wr, br, w_gate, w_up, w_down, e0, g_final=None, splits=None):
    t, d = x.shape
    nblk = (2 * t) // MOE_BLK + N_EXPERTS
    nblk_pad = -(-(nblk + 1) // SUBLANES) * SUBLANES
    h, meta, wts, cnt = moe_route(x, g, wr, br)
    dest, ex, be = moe_plan(cnt, meta, nblk_pad=nblk_pad)
    dest_flat = dest[:, :2].reshape(2 * t)
    per_expert = lambda r: ex[r, R_E0:R_E0 + N_EXPERTS]
    end_last = ex[2, R_E0 + N_EXPERTS - 1:R_E0 + N_EXPERTS]
    xd = moe_dispatch(dest_flat, per_expert(0), per_expert(1), end_last, h, nblk * MOE_BLK)
    yd = moe_experts(be[:, 0], xd, w_gate, w_up, w_down, e0, nblk=nblk, nblk_pad=nblk_pad)
    if splits is None:
        return moe_combine(dest_flat, x, wts, yd, g_final)
    return [moe_combine(dest_flat, x, wts, yd, g_final, row0=r0, rows=n) for r0, n in splits]


def _pad_rows(buf):
    return jnp.pad(buf, ((0, 0), (0, SUBLANES - buf.shape[1]), (0, 0)))


def _forward(x_long, x_short, grp_long, grp_short, wts):
    depth = grp_long["mem_k"].shape[0]
    bl, ll = grp_long["batch"], grp_long["seqlen"]
    bs, ls = grp_short["batch"], grp_short["seqlen"]
    t_long, t_short = bl * ll, bs * ls
    d = x_long.shape[1]
    new = {id(grp_long): dict(gdn=[], conv=[], sc=[], ret=[]), id(grp_short): dict(gdn=[], conv=[], sc=[], ret=[])}

    def record_even(grp, s_new, cq, cs):
        rec = new[id(grp)]
        rec["gdn"].append(s_new)
        rec["conv"].append(cq.reshape(grp["batch"], SUBLANES, W_QKV_A)[:, :CONV_A - 1])
        rec["sc"].append(cs.reshape(grp["batch"], SUBLANES, D_B)[:, :CONV_B - 1])

    x_all = None
    for layer in range(depth):
        src_long = x_long if x_all is None else x_all
        src_short, row0 = (x_short, 0) if x_all is None else (x_all, t_long)
        if layer % 2 == 0:
            i = layer // 2
            hist = lambda grp: (_pad_rows(grp["conv"][i]), _pad_rows(grp["sc"][i]), wts["w_conv_qkv"][i],
                                wts["w_conv_sc"][i], wts["gdn_prm"][i], wts["gdn_norm"][i], grp["gdn"][i])
            p = rms_matmul(src_short, wts["norm_mix"][layer], wts["w_in_a"][i], tn=768, rows=t_short, row0=row0)
            mix, s_new, cq, cs = gdn_core(p, *hist(grp_short), batch=bs, seqlen=ls)
            xs = matmul_res(mix, wts["w_out_a"][i], src_short, res_row0=row0)
            record_even(grp_short, s_new, cq, cs)
            xl, s_new, cq, cs = gdn_block(src_long, wts["norm_mix"][layer], wts["w_in_a"][i], wts["w_out_a"][i],
                                          *hist(grp_long), batch=bl, seqlen=ll)
            record_even(grp_long, s_new, cq, cs)
        else:
            j = layer // 2
            p = rms_matmul(src_short, wts["norm_mix"][layer], wts["w_in_c"][j], tn=768, rows=t_short, row0=row0)
            ret, r_new = ret_core(p, grp_short["pos"], wts["ret_norm"][j], grp_short["ret"][j], batch=bs, seqlen=ls)
            xs = matmul_res(ret, wts["w_out_c"][j], src_short, res_row0=row0)
            new[id(grp_short)]["ret"].append(r_new)
            xl, r_new = ret_block(src_long, wts["norm_mix"][layer], wts["w_in_c"][j], wts["w_out_c"][j],
                                  grp_long["pos"], wts["ret_norm"][j], grp_long["ret"][j], batch=bl, seqlen=ll)
            new[id(grp_long)]["ret"].append(r_new)
        q = rms_matmul(xs, wts["norm_x"][layer], wts["w_xq"][layer], tn=D_MODEL)
        att = xattn_core(q, grp_short["mem_k"], grp_short["mem_v"], layer, batch=bs, seqlen=ls)
        joint = jnp.zeros((t_long + t_short, d), F32) if x_all is None else x_all
        joint = matmul_res(att, wts["w_xo"][layer], xs, into=joint, out_row0=t_long)
        joint = xattn_block(xl, wts["norm_x"][layer], wts["w_xq"][layer], wts["w_xo"][layer],
                            grp_long["mem_k"], grp_long["mem_v"], layer, joint, batch=bl, seqlen=ll)
        last = layer == depth - 1
        x_all = moe_block(joint, wts["norm_ffn"][layer], wts["w_route"][layer], wts["b_route"][layer],
                          wts["w_exp_gate"], wts["w_exp_up"], wts["w_exp_down"], layer * N_EXPERTS,
                          g_final=wts["norm_final"] if last else None,
                          splits=[(0, t_long), (t_long, t_short)] if last else None)
    y_long, y_short = x_all
    stack = lambda grp: tuple(jnp.stack(new[id(grp)][k]) for k in ("gdn", "conv", "sc", "ret"))
    return (y_long,) + stack(grp_long), (y_short,) + stack(grp_short)


def kernel(x_prompt, x_sample, state_gdn, state_gdn_conv, state_sconv, state_ret, cache_mem_k, cache_mem_v, mem_prompt, norm_mix, norm_x, norm_ffn, norm_final, norm_mem, w_in_a, w_conv_qkv, a_log, dt_bias, gdn_norm, w_conv_sc, w_out_a, w_in_c, ret_norm, w_out_c, w_xq, w_xk, w_xv, w_xo, w_group, b_group, w_router, b_router, w_exp_gate, w_exp_up, w_exp_down):
    bp, lp, d = x_prompt.shape
    bs, ls, _ = x_sample.shape
    depth = norm_mix.shape[0]
    n_even = w_in_a.shape[0]
    n_mem = mem_prompt.shape[1]

    qkv_w = 2 * H_A * DK_A + H_A * DV_A
    o_z = qkv_w
    o_b = o_z + H_A * DV_A
    o_a = o_b + H_A
    o_sc = o_a + H_A
    w_a = jnp.concatenate([w_in_a[:, :, :o_b], w_in_a[:, :, o_sc:], w_in_a[:, :, o_b:o_sc],
                           jnp.zeros((n_even, d, PA_COLS - PA_BA - 2 * H_A), F32)], axis=-1).astype(BF16)
    prm = jnp.zeros((n_even, SUBLANES, LANES), F32)
    prm = prm.at[:, 0, H_A:2 * H_A].set(a_log).at[:, 1, H_A:2 * H_A].set(dt_bias)
    w_route = jnp.concatenate([w_group, w_router, jnp.zeros((depth, d, LANES - N_GROUPS - N_EXPERTS), F32)],
                              axis=-1).astype(BF16)
    b_route = jnp.concatenate([b_group, b_router, jnp.zeros((depth, LANES - N_GROUPS - N_EXPERTS), F32)],
                              axis=-1).reshape(depth, 1, LANES)
    wts = dict(norm_mix=norm_mix, norm_x=norm_x, norm_ffn=norm_ffn, norm_final=norm_final,
               w_in_a=w_a, w_conv_qkv=w_conv_qkv, gdn_prm=prm, gdn_norm=gdn_norm, w_conv_sc=w_conv_sc,
               w_out_a=w_out_a.astype(BF16), w_in_c=w_in_c.astype(BF16), ret_norm=ret_norm,
               w_out_c=w_out_c.astype(BF16), w_xq=w_xq.astype(BF16), w_xo=w_xo.astype(BF16),
               w_route=w_route, b_route=b_route,
               w_exp_gate=w_exp_gate.reshape((depth * N_EXPERTS,) + w_exp_gate.shape[2:]),
               w_exp_up=w_exp_up.reshape((depth * N_EXPERTS,) + w_exp_up.shape[2:]),
               w_exp_down=w_exp_down.reshape((depth * N_EXPERTS,) + w_exp_down.shape[2:]))

    memf = mem_prompt.reshape(bp * n_mem, d)
    w_kv = jnp.concatenate([w_xk, w_xv], axis=-1).astype(BF16)
    mk_p, mv_p = kv_proj(memf, norm_mem, w_kv)
    p_cache_mem_k = mk_p.reshape(depth, bp, n_mem, H_X, HD_X)
    p_cache_mem_v = mv_p.reshape(depth, bp, n_mem, H_X, HD_X)

    n_odd = w_in_c.shape[0]
    z_gdn = jnp.zeros((n_even, bp, H_A, DK_A, DV_A), F32)
    z_conv = jnp.zeros((n_even, bp, CONV_A - 1, qkv_w), F32)
    z_sc = jnp.zeros((n_even, bp, CONV_B - 1, D_B), F32)
    z_ret = jnp.zeros((n_odd, bp, H_C, DK_C, DV_C), F32)
    pos_p = jnp.arange(lp, dtype=I32)
    pos_s = 16384 + jnp.arange(ls, dtype=I32)

    grp_p = dict(batch=bp, seqlen=lp, pos=pos_p, gdn=z_gdn, conv=z_conv, sc=z_sc, ret=z_ret,
                 mem_k=p_cache_mem_k, mem_v=p_cache_mem_v)
    grp_s = dict(batch=bs, seqlen=ls, pos=pos_s, gdn=state_gdn, conv=state_gdn_conv, sc=state_sconv,
                 ret=state_ret, mem_k=cache_mem_k, mem_v=cache_mem_v)
    (y_p, p_gdn, p_conv, p_sc, p_ret), (y_s, s_gdn, s_conv, s_sc, s_ret) = _forward(
        x_prompt.reshape(bp * lp, d), x_sample.reshape(bs * ls, d), grp_p, grp_s, wts)
    return (y_p.reshape(bp, lp, d), y_s.reshape(bs, ls, d), p_gdn, p_conv, p_sc, p_ret, p_cache_mem_k,
            p_cache_mem_v, s_gdn, s_conv, s_sc, s_ret)
```

```python
import functools
import math

import jax
import jax.numpy as jnp
from jax import lax
from jax.experimental import pallas as pl
from jax.experimental.pallas import tpu as pltpu

F32 = jnp.float32
BF16 = jnp.bfloat16
I32 = jnp.int32

EPS = 1e-6
ROPE_BASE = 10000.0

D_MODEL = 1024
H_A, DK_A, DV_A, CONV_A = 4, 128, 128, 4
W_QKV_A = 3 * H_A * DK_A
D_B, CONV_B = D_MODEL // 2, 3
H_C, DK_C, DV_C = 4, 256, 512
H_X, HD_X, N_MEM = 4, 256, 256
N_GROUPS, E_PER_GROUP, N_EXPERTS, D_EXPERT = 4, 8, 32, 512
GDN_CHUNK = 64

LANES = 128
SUBLANES = 8
GDN_STACK = 256
VMEM_LIMIT = 56 * 1024 * 1024

PA_COLS = 3840
PA_SC = 2048
PA_BA = 3584
R_E0 = N_GROUPS
MOE_BLK = 256


def _cparams(sem):
    return pltpu.CompilerParams(dimension_semantics=sem, vmem_limit_bytes=VMEM_LIMIT)


def _dot(a, b, trans_a=False, trans_b=False):
    dn = (((0 if trans_a else 1,), (1 if trans_b else 0,)), ((), ()))
    return lax.dot_general(a.astype(BF16), b.astype(BF16), dn, preferred_element_type=F32)


def _silu(x):
    return x * (1.0 / (1.0 + jnp.exp(-x)))


def _sigmoid(x):
    return 1.0 / (1.0 + jnp.exp(-x))


def _rms_matmul_kernel(x_ref, g_ref, w_ref, o_ref, xn_ref):
    @pl.when(pl.program_id(1) == 0)
    def _():
        x = x_ref[...]
        ms = jnp.mean(x * x, axis=-1, keepdims=True)
        xn_ref[...] = (x * lax.rsqrt(ms + EPS) * g_ref[...]).astype(BF16)

    o_ref[...] = jnp.dot(xn_ref[...], w_ref[...], preferred_element_type=F32).astype(o_ref.dtype)


def rms_matmul(x, g, w, *, tn, rows=None, row0=0, out_dtype=F32, tm=1024):
    t, d = (rows or x.shape[0]), x.shape[1]
    n = w.shape[1]
    tm = min(tm, t)
    blk0 = row0 // tm
    return pl.pallas_call(
        _rms_matmul_kernel,
        grid=(t // tm, n // tn),
        in_specs=[pl.BlockSpec((tm, d), lambda i, j: (i + blk0, 0)),
                  pl.BlockSpec((1, d), lambda i, j: (0, 0)),
                  pl.BlockSpec((d, tn), lambda i, j: (0, j))],
        out_specs=pl.BlockSpec((tm, tn), lambda i, j: (i, j)),
        out_shape=jax.ShapeDtypeStruct((t, n), out_dtype),
        scratch_shapes=[pltpu.VMEM((tm, d), BF16)],
        compiler_params=_cparams(("parallel", "arbitrary")),
        name="rms_matmul",
    )(x, g.reshape(1, d), w)


def _kv_proj_kernel(x_ref, g_ref, w_ref, k_ref, v_ref):
    x = x_ref[...]
    ms = jnp.mean(x * x, axis=-1, keepdims=True)
    xn = (x * lax.rsqrt(ms + EPS) * g_ref[0]).astype(BF16)
    d = x.shape[1]
    k_ref[0] = jnp.dot(xn, w_ref[0, :, :d], preferred_element_type=F32)
    v_ref[0] = jnp.dot(xn, w_ref[0, :, d:], preferred_element_type=F32)


def kv_proj(x, g, w_kv, *, tm=1024):
    t, d = x.shape
    depth = g.shape[0]
    tm = min(tm, t)
    out_spec = pl.BlockSpec((1, tm, d), lambda l, i: (l, i, 0))
    return pl.pallas_call(
        _kv_proj_kernel,
        grid=(depth, t // tm),
        in_specs=[pl.BlockSpec((tm, d), lambda l, i: (i, 0)),
                  pl.BlockSpec((1, 1, d), lambda l, i: (l, 0, 0)),
                  pl.BlockSpec((1, d, 2 * d), lambda l, i: (l, 0, 0))],
        out_specs=[out_spec, out_spec],
        out_shape=[jax.ShapeDtypeStruct((depth, t, d), F32), jax.ShapeDtypeStruct((depth, t, d), F32)],
        compiler_params=_cparams(("parallel", "parallel")),
        name="kv_proj",
    )(x, g.reshape(depth, 1, d), w_kv)


def _matmul_res_kernel(a_ref, w_ref, r_ref, *rest):
    o_ref = rest[-1]
    o_ref[...] = r_ref[...] + jnp.dot(a_ref[...].astype(BF16), w_ref[...], preferred_element_type=F32)


def matmul_res(a, w, res, *, res_row0=0, into=None, out_row0=0, tm=512):
    t, k = a.shape
    n = w.shape[1]
    tm = min(tm, t)
    rblk, oblk = res_row0 // tm, out_row0 // tm
    in_specs = [pl.BlockSpec((tm, k), lambda i: (i, 0)),
                pl.BlockSpec((k, n), lambda i: (0, 0)),
                pl.BlockSpec((tm, n), lambda i: (i + rblk, 0))]
    args = (a, w, res)
    if into is not None:
        in_specs.append(pl.BlockSpec(memory_space=pl.ANY))
        args += (into,)
    return pl.pallas_call(
        _matmul_res_kernel,
        grid=(t // tm,),
        in_specs=in_specs,
        out_specs=pl.BlockSpec((tm, n), lambda i: (i + oblk, 0)),
        out_shape=jax.ShapeDtypeStruct((t, n) if into is None else into.shape, F32),
        input_output_aliases={} if into is None else {3: 0},
        compiler_params=_cparams(("parallel",)),
        name="matmul_res",
    )(*args)


def _causal_conv(x, hist, w_ref, width, seq8):
    r = x.shape[0]
    taps = [w_ref[j:j + 1, :] for j in range(width)]

    def head(x8, h8):
        n = x8.shape[0]
        t = lax.broadcasted_iota(I32, (n, 1), 0) % SUBLANES
        y = taps[width - 1] * x8
        for s in range(1, width):
            prev = pltpu.roll(h8, (n + s - (width - 1)) % n, 0) if s != width - 1 else h8
            y = y + taps[width - 1 - s] * jnp.where(t >= s, pltpu.roll(x8, s, 0), prev)
        return y

    if seq8:
        return head(x, hist)
    y = taps[width - 1] * x
    for s in range(1, width):
        y = y + taps[width - 1 - s] * pltpu.roll(x, s, 0)
    return jnp.concatenate([head(x[:SUBLANES], hist), y[SUBLANES:]], axis=0)


def _unit_lower_inverse(ms, c, ri, ci):
    base = min(c, 16)
    eye = jnp.where(ri == ci, 1.0, 0.0).astype(F32)
    blk = (ri // base) == (ci // base)
    ds = [jnp.where(blk, m, 0.0) for m in ms]
    ps = [eye - d for d in ds]
    k = 2
    while k < base:
        ds = [_dot(d, d) for d in ds]
        ps = [_dot(p, eye + d) for p, d in zip(ps, ds)]
        k *= 2
    s = base
    while s < c:
        sel = ((ri // (2 * s)) == (ci // (2 * s))) & ((ri // s) != (ci // s))
        ts = [_dot(jnp.where(sel, m, 0.0), p) for m, p in zip(ms, ps)]
        ps = [p - _dot(p, t) for p, t in zip(ps, ts)]
        s *= 2
    return ps


def _gdn_prepare(units, c):
    n = GDN_STACK
    ri = lax.broadcasted_iota(I32, (n, n), 0)
    ci = lax.broadcasted_iota(I32, (n, n), 1)
    same = (ri // c) == (ci // c)
    incl = same & (ri >= ci)
    strict = same & (ri > ci)
    pre = []
    for q, k, v, bfull, gfull in units:
        g2 = jnp.concatenate([gfull, gfull], axis=1)
        g_row = jnp.sum(jnp.where(ri == ci, g2, 0.0), axis=0, keepdims=True)
        gc_col = jnp.sum(jnp.where(incl, g_row, 0.0), axis=1, keepdims=True)
        gc_row = jnp.sum(jnp.where(same & (ri <= ci), g2, 0.0), axis=0, keepdims=True)
        gl_col = jnp.sum(jnp.where(same, g_row, 0.0), axis=1, keepdims=True)
        decay = jnp.where(incl, jnp.exp(jnp.where(incl, gc_col - gc_row, 0.0)), 0.0)
        egc = jnp.exp(gc_col)
        kb = k * bfull
        pre.append(dict(decay=decay, kb=kb, rhs=jnp.concatenate([v * bfull, kb * egc], axis=1),
                        qd=q * egc, kd=k * jnp.exp(gl_col - gc_col), egl=jnp.exp(gl_col)))
    mms = [jnp.where(strict, _dot(e["kb"], u[1], trans_b=True) * e["decay"], 0.0) for e, u in zip(pre, units)]
    qks = [_dot(u[0], u[1], trans_b=True) * e["decay"] for e, u in zip(pre, units)]
    tinvs = _unit_lower_inverse(mms, c, ri, ci)
    uws = [_dot(t, e["rhs"]) for t, e in zip(tinvs, pre)]
    return [dict(u=uw[:, :DV_A], w=uw[:, DV_A:], qk=qk, qd=e["qd"], kd=e["kd"], egl=e["egl"])
            for uw, qk, e in zip(uws, qks, pre)]


def _gdn_recur(e, states, c):
    nprob = GDN_STACK // c
    ws, qs = [], []
    for p in range(nprob):
        sl = slice(p * c, (p + 1) * c)
        ws.append(_dot(e["w"][sl], states[p]))
        qs.append(_dot(e["qd"][sl], states[p]))
    vn = e["u"] - jnp.concatenate(ws, axis=0)
    o = _dot(e["qk"], vn) + jnp.concatenate(qs, axis=0)
    new_states = []
    for p in range(nprob):
        sl = slice(p * c, (p + 1) * c)
        new_states.append(states[p] * e["egl"][p * c:p * c + 1, :] + _dot(e["kd"][sl], vn[sl], trans_a=True))
    return o, new_states


def _gdn_compute(x, z, u_sc, scb, ba, hq, hs, wq_ref, ws_ref, prm_ref, gn, states, *, seq8, c, nu):
    unit = GDN_CHUNK
    xc = _silu(_causal_conv(x, hq, wq_ref, CONV_A, seq8))
    yb = scb * _causal_conv(u_sc, hs, ws_ref, CONV_B, seq8)
    beta_all = _sigmoid(ba)
    sp = jnp.maximum(ba + prm_ref[1:2, :], 0.0) + jnp.log1p(jnp.exp(-jnp.abs(ba + prm_ref[1:2, :])))
    g_all = -jnp.exp(prm_ref[0:1, :]) * sp

    def head_cols(a, base):
        return a[:, base * DK_A:(base + 1) * DK_A]

    cat = lambda xs: jnp.concatenate(xs, axis=0)
    units = []
    for ui in range(nu):
        rs = slice(ui * unit, (ui + 1) * unit)
        qs, ks, vs, bs, gs = [], [], [], [], []
        for h in range(H_A):
            qh = head_cols(xc, h)[rs]
            kh = head_cols(xc, H_A + h)[rs]
            qs.append(qh * lax.rsqrt(jnp.sum(qh * qh, axis=-1, keepdims=True) + EPS) * (DK_A ** -0.5))
            ks.append(kh * lax.rsqrt(jnp.sum(kh * kh, axis=-1, keepdims=True) + EPS))
            vs.append(head_cols(xc, 2 * H_A + h)[rs])
            bs.append(jnp.broadcast_to(beta_all[rs, h:h + 1], (unit, LANES)))
            gs.append(jnp.broadcast_to(g_all[rs, H_A + h:H_A + h + 1], (unit, LANES)))
        units.append((cat(qs), cat(ks), cat(vs), cat(bs), cat(gs)))
    prepared = _gdn_prepare(units, c)

    outs = []
    for ui in range(nu):
        rs = slice(ui * unit, (ui + 1) * unit)
        o, states = _gdn_recur(prepared[ui], states, c)
        zst = cat([z[rs, h * DV_A:(h + 1) * DV_A] for h in range(H_A)])
        ms = jnp.mean(o * o, axis=-1, keepdims=True)
        og = o * lax.rsqrt(ms + EPS) * gn * _silu(zst)
        outs.append(jnp.concatenate([og[h * unit:(h + 1) * unit] for h in range(H_A)], axis=1))
    o_all = outs[0] if nu == 1 else cat(outs)
    return jnp.concatenate([o_all, yb], axis=1), states


def _gdn_kernel(qkv_ref, z_ref, sch_ref, scb_ref, scc_ref, ba_ref, hq_ref, hs_ref, wq_ref, ws_ref,
                prm_ref, gn_ref, s0_ref, o_ref, sn_ref, cq_ref, cs_ref):
    x = qkv_ref[...]
    rows = x.shape[0]
    u_sc = scc_ref[...] * sch_ref[...]
    cq_ref[...] = pltpu.roll(x, rows - SUBLANES + CONV_A - 1, 0)
    cs_ref[...] = pltpu.roll(u_sc, rows - SUBLANES + CONV_B - 1, 0)
    nprob = GDN_STACK // SUBLANES
    states = [s0_ref[p % SUBLANES, p // SUBLANES] for p in range(nprob)]
    mix, states = _gdn_compute(x, z_ref[...], u_sc, scb_ref[...], ba_ref[...], hq_ref[...], hs_ref[...],
                               wq_ref, ws_ref, prm_ref, gn_ref[...], states, seq8=True, c=SUBLANES, nu=1)
    o_ref[...] = mix.astype(o_ref.dtype)
    for p in range(nprob):
        sn_ref[p % SUBLANES, p // SUBLANES] = states[p]


def _gdn_block_kernel(x_ref, g_ref, wi_ref, wo_ref, hq_ref, hs_ref, wq_ref, ws_ref, prm_ref, gn_ref, s0_ref,
                      o_ref, sn_ref, cq_ref, cs_ref, s_scr, hq_scr, hs_scr, *, nu, nl):
    l = pl.program_id(1)

    @pl.when(l == 0)
    def _():
        hq_scr[...] = hq_ref[0]
        hs_scr[...] = hs_ref[0]
        s_scr[...] = s0_ref[0]

    xres = x_ref[...]
    rows = xres.shape[0]
    ms = jnp.mean(xres * xres, axis=-1, keepdims=True)
    xn = (xres * lax.rsqrt(ms + EPS) * g_ref[...]).astype(BF16)
    proj = lambda lo, width: jnp.dot(xn, wi_ref[:, lo:lo + width], preferred_element_type=F32)
    x = proj(0, W_QKV_A)
    z = proj(W_QKV_A, H_A * DV_A)
    u_sc = proj(PA_SC + 2 * D_B, D_B) * proj(PA_SC, D_B)
    scb = proj(PA_SC + D_B, D_B)
    ba = proj(PA_BA, LANES)
    hq = hq_scr[...]
    hs = hs_scr[...]
    hq_scr[...] = pltpu.roll(x[rows - SUBLANES:], CONV_A - 1, 0)
    hs_scr[...] = pltpu.roll(u_sc[rows - SUBLANES:], CONV_B - 1, 0)
    states = [s_scr[p] for p in range(H_A)]
    mix, states = _gdn_compute(x, z, u_sc, scb, ba, hq, hs, wq_ref, ws_ref, prm_ref, gn_ref[...], states,
                               seq8=False, c=GDN_CHUNK, nu=nu)
    for p in range(H_A):
        s_scr[p] = states[p]
    o_ref[...] = xres + jnp.dot(mix.astype(BF16), wo_ref[...], preferred_element_type=F32)

    @pl.when(l == nl - 1)
    def _():
        sn_ref[0] = s_scr[...]
        cq_ref[0] = hq_scr[...]
        cs_ref[0] = hs_scr[...]


def gdn_core(p, hist_q, hist_s, w_conv_qkv, w_conv_sc, prm, gn, s0, *, batch, seqlen):
    t = batch * seqlen
    assert seqlen == SUBLANES
    rows = GDN_CHUNK
    nb = rows // seqlen
    hq_spec = pl.BlockSpec((rows, W_QKV_A), lambda i: (i, 0))
    hs_spec = pl.BlockSpec((rows, D_B), lambda i: (i, 0))
    s_spec = pl.BlockSpec((nb, H_A, DK_A, DV_A), lambda i: (i, 0, 0, 0))
    col = lambda width, blk: pl.BlockSpec((rows, width), lambda i: (i, blk))
    const = lambda shape: pl.BlockSpec(shape, lambda i: (0,) * len(shape))
    return pl.pallas_call(
        _gdn_kernel,
        grid=(batch // nb,),
        in_specs=[col(W_QKV_A, 0), col(D_B, 3), col(D_B, 4), col(D_B, 5), col(D_B, 6),
                  col(LANES, PA_BA // LANES), hq_spec, hs_spec,
                  const((CONV_A, W_QKV_A)), const((CONV_B, D_B)), const((SUBLANES, LANES)),
                  const((1, DV_A)), s_spec],
        out_specs=[pl.BlockSpec((rows, D_MODEL), lambda i: (i, 0)), s_spec, hq_spec, hs_spec],
        out_shape=[jax.ShapeDtypeStruct((t, D_MODEL), BF16),
                   jax.ShapeDtypeStruct((batch, H_A, DK_A, DV_A), F32),
                   jax.ShapeDtypeStruct((t, W_QKV_A), F32),
                   jax.ShapeDtypeStruct((t, D_B), F32)],
        compiler_params=_cparams(("parallel",)),
        name="gdn_core",
    )(p, p, p, p, p, p, hist_q.reshape(t, W_QKV_A), hist_s.reshape(t, D_B), w_conv_qkv, w_conv_sc, prm,
      gn.reshape(1, DV_A), s0)


def _resident(shape):
    return pl.BlockSpec(shape, lambda *ix: (0,) * len(shape), pipeline_mode=pl.Buffered(1))


def gdn_block(x, g, w_in, w_out, hist_q, hist_s, w_conv_qkv, w_conv_sc, prm, gn, s0, *, batch, seqlen):
    t, d = batch * seqlen, x.shape[1]
    rows = min(seqlen, 256)
    nu = rows // GDN_CHUNK
    nl = seqlen // rows
    per_b = lambda shape: pl.BlockSpec((1,) + shape, lambda b, l: (b,) + (0,) * len(shape))
    row_spec = pl.BlockSpec((rows, d), lambda b, l: (b * nl + l, 0))
    kern = functools.partial(_gdn_block_kernel, nu=nu, nl=nl)
    return pl.pallas_call(
        kern,
        grid=(batch, nl),
        in_specs=[row_spec, _resident((1, d)), _resident(w_in.shape), _resident(w_out.shape),
                  per_b((SUBLANES, W_QKV_A)), per_b((SUBLANES, D_B)),
                  _resident((CONV_A, W_QKV_A)), _resident((CONV_B, D_B)), _resident((SUBLANES, LANES)),
                  _resident((1, DV_A)), per_b((H_A, DK_A, DV_A))],
        out_specs=[row_spec, per_b((H_A, DK_A, DV_A)), per_b((SUBLANES, W_QKV_A)), per_b((SUBLANES, D_B))],
        out_shape=[jax.ShapeDtypeStruct((t, d), F32),
                   jax.ShapeDtypeStruct((batch, H_A, DK_A, DV_A), F32),
                   jax.ShapeDtypeStruct((batch, SUBLANES, W_QKV_A), F32),
                   jax.ShapeDtypeStruct((batch, SUBLANES, D_B), F32)],
        scratch_shapes=[pltpu.VMEM((H_A, DK_A, DV_A), F32),
                        pltpu.VMEM((SUBLANES, W_QKV_A), F32),
                        pltpu.VMEM((SUBLANES, D_B), F32)],
        compiler_params=_cparams(("parallel", "arbitrary")),
        name="gdn_block",
    )(x, g.reshape(1, d), w_in, w_out, hist_q, hist_s, w_conv_qkv, w_conv_sc, prm, gn.reshape(1, DV_A), s0)


def _ret_compute(get_q, get_k, get_v, get_gate, cos, sin, dm_ref, qd_ref, kd_ref, cd_ref, rn_ref, r_scr,
                 *, nseq, c):
    half = DK_C // 2

    def rot(x):
        x1, x2 = x[:, :half], x[:, half:]
        return jnp.concatenate([x1 * cos - x2 * sin, x1 * sin + x2 * cos], axis=1)

    heads = range(H_C)
    qs = [rot(get_q(h)) for h in heads]
    ks = [rot(get_k(h)) * (DK_C ** -0.5) for h in heads]
    vs = [get_v(h).astype(BF16) for h in heads]
    ss = [_dot(qs[h], ks[h], trans_b=True) * dm_ref[h] for h in heads]
    inters = []
    for h in heads:
        qdh = qs[h] * qd_ref[h]
        parts = [_dot(qdh[sq * c:(sq + 1) * c], r_scr[sq, h]) for sq in range(nseq)]
        inters.append(parts[0] if nseq == 1 else jnp.concatenate(parts, axis=0))
    outs = [_dot(ss[h], vs[h]) + inters[h] for h in heads]
    for h in heads:
        kdh = ks[h] * kd_ref[h]
        cd = cd_ref[h][0:1, 0:1]
        for sq in range(nseq):
            sl = slice(sq * c, (sq + 1) * c)
            r_scr[sq, h] = r_scr[sq, h] * cd + _dot(kdh[sl], vs[h][sl], trans_a=True)
    gated = []
    for h in heads:
        o = outs[h]
        ms = jnp.mean(o * o, axis=-1, keepdims=True)
        on = o * lax.rsqrt(ms + EPS) * rn_ref[:, h * DV_C:(h + 1) * DV_C]
        gated.append(_silu(get_gate(h)) * on)
    return gated


def _ret_kernel(q_ref, k_ref, v_ref, gate_ref, cos_ref, sin_ref, dm_ref, qd_ref, kd_ref, cd_ref,
                rn_ref, r0_ref, o_ref, rnew_ref, r_scr, *, nseq, c, nl):
    l = pl.program_id(1)

    @pl.when(l == 0)
    def _():
        r_scr[...] = r0_ref[...]

    gated = _ret_compute(lambda h: q_ref[:, h * DK_C:(h + 1) * DK_C], lambda h: k_ref[:, h * DK_C:(h + 1) * DK_C],
                         lambda h: v_ref[:, h * DV_C:(h + 1) * DV_C], lambda h: gate_ref[:, h * DV_C:(h + 1) * DV_C],
                         cos_ref[...], sin_ref[...], dm_ref, qd_ref, kd_ref, cd_ref, rn_ref, r_scr, nseq=nseq, c=c)
    for h in range(H_C):
        o_ref[:, h * DV_C:(h + 1) * DV_C] = gated[h].astype(o_ref.dtype)

    @pl.when(l == nl - 1)
    def _():
        rnew_ref[...] = r_scr[...]


def _ret_block_kernel(x_ref, g_ref, wi_ref, wo_ref, cos_ref, sin_ref, dm_ref, qd_ref, kd_ref, cd_ref,
                      rn_ref, r0_ref, o_ref, rnew_ref, r_scr, *, c, nl):
    l = pl.program_id(1)

    @pl.when(l == 0)
    def _():
        r_scr[...] = r0_ref[...]

    xres = x_ref[...]
    ms = jnp.mean(xres * xres, axis=-1, keepdims=True)
    xn = (xres * lax.rsqrt(ms + EPS) * g_ref[...]).astype(BF16)
    proj = lambda lo, width: jnp.dot(xn, wi_ref[:, lo:lo + width], preferred_element_type=F32)
    hk, hv = H_C * DK_C, H_C * DV_C
    gated = _ret_compute(lambda h: proj(h * DK_C, DK_C), lambda h: proj(hk + h * DK_C, DK_C),
                         lambda h: proj(2 * hk + h * DV_C, DV_C), lambda h: proj(2 * hk + hv + h * DV_C, DV_C),
                         cos_ref[...], sin_ref[...], dm_ref, qd_ref, kd_ref, cd_ref, rn_ref, r_scr, nseq=1, c=c)
    y = xres
    for h in range(H_C):
        y = y + jnp.dot(gated[h].astype(BF16), wo_ref[h * DV_C:(h + 1) * DV_C, :], preferred_element_type=F32)
    o_ref[...] = y

    @pl.when(l == nl - 1)
    def _():
        rnew_ref[...] = r_scr[...]


def ret_core(p, pos, ret_norm, r0, *, batch, seqlen):
    t = batch * seqlen
    assert seqlen == SUBLANES
    nseq, c = 2, seqlen
    rows = nseq * c
    cos, sin, dmat, qd, kd, cd = _ret_tables(pos, nseq, c)
    const = lambda shape: pl.BlockSpec(shape, lambda b, l: (0,) * len(shape))
    kern = functools.partial(_ret_kernel, nseq=nseq, c=c, nl=1)
    hk = H_C * DK_C
    hv = H_C * DV_C
    return pl.pallas_call(
        kern,
        grid=(batch // nseq, 1),
        in_specs=[pl.BlockSpec((rows, hk), lambda b, l: (b, 0)),
                  pl.BlockSpec((rows, hk), lambda b, l: (b, 1)),
                  pl.BlockSpec((rows, hv), lambda b, l: (b, 1)),
                  pl.BlockSpec((rows, hv), lambda b, l: (b, 2)),
                  const((rows, DK_C // 2)), const((rows, DK_C // 2)),
                  const((H_C, rows, rows)), const((H_C, rows, DK_C)), const((H_C, rows, DK_C)),
                  const((H_C, SUBLANES, LANES)), const((1, hv)),
                  pl.BlockSpec((nseq, H_C, DK_C, DV_C), lambda b, l: (b, 0, 0, 0))],
        out_specs=[pl.BlockSpec((rows, hv), lambda b, l: (b, 0)),
                   pl.BlockSpec((nseq, H_C, DK_C, DV_C), lambda b, l: (b, 0, 0, 0))],
        out_shape=[jax.ShapeDtypeStruct((t, hv), BF16),
                   jax.ShapeDtypeStruct((batch, H_C, DK_C, DV_C), F32)],
        scratch_shapes=[pltpu.VMEM((nseq, H_C, DK_C, DV_C), F32)],
        compiler_params=_cparams(("parallel", "arbitrary")),
        name="ret_core",
    )(p, p, p, p, cos, sin, dmat, qd, kd, cd, ret_norm.reshape(1, hv), r0)


def _ret_tables(pos, nseq, c):
    half = DK_C // 2
    inv = ROPE_BASE ** (-jnp.arange(half, dtype=F32) / half)
    ang = pos.astype(F32)[:, None] * inv[None, :]
    cos, sin = jnp.cos(ang), jnp.sin(ang)
    if nseq > 1:
        cos, sin = jnp.tile(cos, (nseq, 1)), jnp.tile(sin, (nseq, 1))
    lg = jnp.log(1.0 - 2.0 ** (-5.0 - jnp.arange(H_C, dtype=F32)))[:, None]
    i = jnp.arange(c, dtype=F32)
    incl = i[:, None] >= i[None, :]
    dmat = jnp.exp(jnp.where(incl[None], (i[:, None] - i[None, :])[None] * lg[..., None], -jnp.inf))
    if nseq > 1:
        dmat = jnp.kron(jnp.eye(nseq, dtype=F32)[None], dmat)
    qd = jnp.tile(jnp.exp((i + 1.0)[None] * lg), (1, nseq))[..., None] * jnp.ones((1, 1, DK_C), F32)
    kd = jnp.tile(jnp.exp((c - 1.0 - i)[None] * lg), (1, nseq))[..., None] * jnp.ones((1, 1, DK_C), F32)
    cd = jnp.exp(c * lg)[..., None] * jnp.ones((1, SUBLANES, LANES), F32)
    return cos, sin, dmat, qd, kd, cd


def ret_block(x, g, w_in, w_out, pos, ret_norm, r0, *, batch, seqlen):
    t, d = batch * seqlen, x.shape[1]
    c = min(seqlen, 256)
    nl = seqlen // c
    cos, sin, dmat, qd, kd, cd = _ret_tables(pos, 1, c)
    hv = H_C * DV_C
    row_spec = pl.BlockSpec((c, d), lambda b, l: (b * nl + l, 0))
    trig_spec = pl.BlockSpec((c, DK_C // 2), lambda b, l: (l, 0))
    state_spec = pl.BlockSpec((1, H_C, DK_C, DV_C), lambda b, l: (b, 0, 0, 0))
    kern = functools.partial(_ret_block_kernel, c=c, nl=nl)
    return pl.pallas_call(
        kern,
        grid=(batch, nl),
        in_specs=[row_spec, _resident((1, d)), _resident(w_in.shape), _resident(w_out.shape),
                  trig_spec, trig_spec,
                  _resident((H_C, c, c)), _resident((H_C, c, DK_C)), _resident((H_C, c, DK_C)),
                  _resident((H_C, SUBLANES, LANES)), _resident((1, hv)), state_spec],
        out_specs=[row_spec, state_spec],
        out_shape=[jax.ShapeDtypeStruct((t, d), F32),
                   jax.ShapeDtypeStruct((batch, H_C, DK_C, DV_C), F32)],
        scratch_shapes=[pltpu.VMEM((1, H_C, DK_C, DV_C), F32)],
        compiler_params=_cparams(("parallel", "arbitrary")),
        name="ret_block",
    )(x, g.reshape(1, d), w_in, w_out, cos, sin, dmat, qd, kd, cd, ret_norm.reshape(1, hv), r0)


def _xattn_fetch(mk_hbm, mv_hbm, kbuf, vbuf, sem, *, layer, nb, nsteps):
    i = pl.program_id(0)
    l = pl.program_id(1)
    slot = i % 2

    def copies(step, slot_):
        out = []
        for b in range(nb):
            for h in range(H_X):
                out.append(pltpu.make_async_copy(mk_hbm.at[layer, step * nb + b, :, h, :],
                                                 kbuf.at[slot_, b, h], sem.at[slot_]))
                out.append(pltpu.make_async_copy(mv_hbm.at[layer, step * nb + b, :, h, :],
                                                 vbuf.at[slot_, b, h], sem.at[slot_]))
        return out

    @pl.when(l == 0)
    def _():
        @pl.when(i == 0)
        def _():
            for c in copies(i, slot):
                c.start()

        @pl.when(i + 1 < nsteps)
        def _():
            for c in copies(i + 1, 1 - slot):
                c.start()

        for c in copies(i, slot):
            c.wait()

    return slot


def _xattn_heads(q_of, kbuf, vbuf, slot, probs):
    ss = [_dot(q_of(b, h), kbuf[slot, b, h], trans_b=True) * (HD_X ** -0.5) for b, h in probs]
    ps = []
    for s in ss:
        e = jnp.exp(s - jnp.max(s, axis=-1, keepdims=True))
        ps.append(e / jnp.sum(e, axis=-1, keepdims=True))
    return [_dot(p, vbuf[slot, b, h]) for p, (b, h) in zip(ps, probs)]


def _xattn_kernel(q_ref, mk_hbm, mv_hbm, o_ref, kbuf, vbuf, sem, *, layer, nb, lq, nsteps):
    slot = _xattn_fetch(mk_hbm, mv_hbm, kbuf, vbuf, sem, layer=layer, nb=nb, nsteps=nsteps)
    probs = [(b, h) for b in range(nb) for h in range(H_X)]
    win = lambda b, h: (slice(b * lq, (b + 1) * lq), slice(h * HD_X, (h + 1) * HD_X))
    outs = _xattn_heads(lambda b, h: q_ref[win(b, h)], kbuf, vbuf, slot, probs)
    for o, (b, h) in zip(outs, probs):
        o_ref[win(b, h)] = o.astype(o_ref.dtype)


def _xattn_block_kernel(x_ref, g_ref, wq_ref, wo_ref, mk_hbm, mv_hbm, into_hbm, o_ref, kbuf, vbuf, sem,
                        *, layer, nsteps):
    del into_hbm
    slot = _xattn_fetch(mk_hbm, mv_hbm, kbuf, vbuf, sem, layer=layer, nb=1, nsteps=nsteps)
    xres = x_ref[...]
    ms = jnp.mean(xres * xres, axis=-1, keepdims=True)
    xn = (xres * lax.rsqrt(ms + EPS) * g_ref[...]).astype(BF16)
    probs = [(0, h) for h in range(H_X)]
    qs = [jnp.dot(xn, wq_ref[:, h * HD_X:(h + 1) * HD_X], preferred_element_type=F32) for h in range(H_X)]
    outs = _xattn_heads(lambda b, h: qs[h], kbuf, vbuf, slot, probs)
    y = xres
    for h in range(H_X):
        y = y + jnp.dot(outs[h].astype(BF16), wo_ref[h * HD_X:(h + 1) * HD_X, :], preferred_element_type=F32)
    o_ref[...] = y


def xattn_core(q, mk, mv, layer, *, batch, seqlen):
    t = batch * seqlen
    d = H_X * HD_X
    assert seqlen == SUBLANES
    nb, lq = 2, seqlen
    rows = nb * lq
    nsteps = batch // nb
    kern = functools.partial(_xattn_kernel, layer=layer, nb=nb, lq=lq, nsteps=nsteps)
    return pl.pallas_call(
        kern,
        grid=(nsteps, 1),
        in_specs=[pl.BlockSpec((rows, d), lambda b, l: (b, 0)),
                  pl.BlockSpec(memory_space=pl.ANY), pl.BlockSpec(memory_space=pl.ANY)],
        out_specs=pl.BlockSpec((rows, d), lambda b, l: (b, 0)),
        out_shape=jax.ShapeDtypeStruct((t, d), q.dtype),
        scratch_shapes=[pltpu.VMEM((2, nb, H_X, N_MEM, HD_X), F32),
                        pltpu.VMEM((2, nb, H_X, N_MEM, HD_X), F32),
                        pltpu.SemaphoreType.DMA((2,))],
        compiler_params=_cparams(("arbitrary", "arbitrary")),
        name="xattn_core",
    )(q, mk, mv)


def xattn_block(x, g, w_q, w_o, mk, mv, layer, into, *, batch, seqlen):
    d = x.shape[1]
    lq = min(seqlen, 512)
    nl = seqlen // lq
    row_spec = pl.BlockSpec((lq, d), lambda b, l: (b * nl + l, 0))
    kern = functools.partial(_xattn_block_kernel, layer=layer, nsteps=batch)
    return pl.pallas_call(
        kern,
        grid=(batch, nl),
        in_specs=[row_spec, _resident((1, d)), _resident(w_q.shape), _resident(w_o.shape),
                  pl.BlockSpec(memory_space=pl.ANY), pl.BlockSpec(memory_space=pl.ANY),
                  pl.BlockSpec(memory_space=pl.ANY)],
        out_specs=row_spec,
        out_shape=jax.ShapeDtypeStruct(into.shape, F32),
        scratch_shapes=[pltpu.VMEM((2, 1, H_X, N_MEM, HD_X), F32),
                        pltpu.VMEM((2, 1, H_X, N_MEM, HD_X), F32),
                        pltpu.SemaphoreType.DMA((2,))],
        input_output_aliases={6: 0},
        compiler_params=_cparams(("arbitrary", "arbitrary")),
        name="xattn_block",
    )(x, g.reshape(1, d), w_q, w_o, mk, mv, into)


def _route_kernel(x_ref, g_ref, wr_ref, br_ref, h_ref, meta_ref, wts_ref, cnt_ref, cnt_scr):
    i = pl.program_id(0)

    @pl.when(i == 0)
    def _():
        cnt_scr[...] = jnp.zeros_like(cnt_scr)

    x = x_ref[...]
    tm = x.shape[0]
    ms = jnp.mean(x * x, axis=-1, keepdims=True)
    h = x * lax.rsqrt(ms + EPS) * g_ref[...]
    hb = h.astype(BF16)
    bits = pltpu.bitcast(hb.astype(F32), jnp.uint32)
    half_d = bits.shape[1] // 2
    h_ref[...] = bits[:, :half_d] | (bits[:, half_d:] >> 16)
    logits = jnp.dot(hb, wr_ref[...], preferred_element_type=F32) + br_ref[...]
    lane_i = lax.broadcasted_iota(I32, (tm, LANES), 1)
    lane = lane_i.astype(F32)
    neg = jnp.float32(-3.0e38)
    big = jnp.float32(LANES)
    is_g = lane_i < N_GROUPS
    gl = jnp.where(is_g, logits, neg)
    gmax = jnp.max(gl, axis=1, keepdims=True)
    grp = jnp.min(jnp.where(gl == gmax, lane, big), axis=1, keepdims=True)
    gsum = jnp.sum(jnp.where(is_g, jnp.exp(jnp.where(is_g, logits - gmax, 0.0)), 0.0), axis=1, keepdims=True)
    p_grp = 1.0 / gsum
    in_grp = ((lane_i >= R_E0) & (lane_i < R_E0 + N_EXPERTS)
              & (jnp.floor((lane - R_E0) * (1.0 / E_PER_GROUP)) == grp))
    el = jnp.where(in_grp, logits, neg)
    m1 = jnp.max(el, axis=1, keepdims=True)
    i1 = jnp.min(jnp.where(el == m1, lane, big), axis=1, keepdims=True)
    el2 = jnp.where(lane == i1, neg, el)
    m2 = jnp.max(el2, axis=1, keepdims=True)
    i2 = jnp.min(jnp.where(el2 == m2, lane, big), axis=1, keepdims=True)
    esum = jnp.sum(jnp.where(in_grp, jnp.exp(jnp.where(in_grp, logits - m1, 0.0)), 0.0), axis=1, keepdims=True)
    p1 = 1.0 / esum
    p2 = jnp.exp(m2 - m1) / esum
    tot = p1 + p2
    w1 = p_grp * (p1 / tot)
    w2 = p_grp * (p2 / tot)
    wts_ref[...] = jnp.where(lane_i == 0, w1, jnp.where(lane_i == 1, w2, 0.0))

    oh1 = (lane == i1).astype(F32)
    oh2 = (lane == i2).astype(F32)
    rr = lax.broadcasted_iota(I32, (tm, tm), 0)
    cc = lax.broadcasted_iota(I32, (tm, tm), 1)
    tri = (rr > cc).astype(BF16)
    base = cnt_scr[0:1, :]
    c1 = jnp.sum(oh1, axis=0, keepdims=True)
    c2 = jnp.sum(oh2, axis=0, keepdims=True)
    r1 = jnp.sum(oh1 * (jnp.dot(tri, oh1.astype(BF16), preferred_element_type=F32) + base), axis=1, keepdims=True)
    r2 = jnp.sum(oh2 * (jnp.dot(tri, oh2.astype(BF16), preferred_element_type=F32) + base + c1), axis=1, keepdims=True)
    new_cnt = base + c1 + c2
    cnt_scr[...] = jnp.broadcast_to(new_cnt, cnt_scr.shape)
    cnt_ref[...] = jnp.broadcast_to(new_cnt, cnt_ref.shape)
    meta = jnp.where(lane_i == 0, i1, jnp.where(lane_i == 1, i2, 0.0))
    meta = jnp.where(lane_i == 2, r1, jnp.where(lane_i == 3, r2, meta))
    meta_ref[...] = meta.astype(I32)


def moe_route(x, g, wr, br, *, tm=512):
    t, d = x.shape
    tm = min(tm, t)
    return pl.pallas_call(
        _route_kernel,
        grid=(t // tm,),
        in_specs=[pl.BlockSpec((tm, d), lambda i: (i, 0)),
                  pl.BlockSpec((1, d), lambda i: (0, 0)),
                  pl.BlockSpec((d, LANES), lambda i: (0, 0)),
                  pl.BlockSpec((1, LANES), lambda i: (0, 0))],
        out_specs=[pl.BlockSpec((tm, d // 2), lambda i: (i, 0)),
                   pl.BlockSpec((tm, LANES), lambda i: (i, 0)),
                   pl.BlockSpec((tm, LANES), lambda i: (i, 0)),
                   pl.BlockSpec((SUBLANES, LANES), lambda i: (0, 0))],
        out_shape=[jax.ShapeDtypeStruct((t, d // 2), jnp.uint32),
                   jax.ShapeDtypeStruct((t, LANES), I32),
                   jax.ShapeDtypeStruct((t, LANES), F32),
                   jax.ShapeDtypeStruct((SUBLANES, LANES), F32)],
        scratch_shapes=[pltpu.VMEM((SUBLANES, LANES), F32)],
        compiler_params=_cparams(("arbitrary",)),
        name="moe_route",
    )(x, g.reshape(1, d), wr, br)


def _plan_kernel(cnt_ref, meta_ref, dest_ref, ex_ref, be_ref, *, nblk_pad):
    cnt = cnt_ref[...]
    lane8 = lax.broadcasted_iota(I32, (SUBLANES, LANES), 1)
    padded = jnp.ceil(cnt / MOE_BLK) * MOE_BLK
    pend = padded
    s = 1
    while s < LANES:
        pend = pend + jnp.where(lane8 >= s, pltpu.roll(pend, s, 1), 0.0)
        s *= 2
    pstart = (pend - padded)[0:1, :]
    meta = meta_ref[...].astype(F32)
    tm = meta.shape[0]
    lane_i = lax.broadcasted_iota(I32, (tm, LANES), 1)
    lane = lane_i.astype(F32)
    col = lambda j: jnp.sum(jnp.where(lane_i == j, meta, 0.0), axis=1, keepdims=True)
    e1, e2, r1, r2 = col(0), col(1), col(2), col(3)
    d1 = jnp.sum(jnp.where(lane == e1, pstart, 0.0), axis=1, keepdims=True) + r1
    d2 = jnp.sum(jnp.where(lane == e2, pstart, 0.0), axis=1, keepdims=True) + r2
    dest_ref[...] = jnp.where(lane_i == 0, d1, jnp.where(lane_i == 1, d2, 0.0)).astype(I32)
    sub8 = lax.broadcasted_iota(I32, (SUBLANES, LANES), 0)
    zrow = jnp.floor((pend - padded + cnt) / SUBLANES) * SUBLANES
    ex_ref[...] = jnp.where(sub8 == 0, zrow, jnp.where(sub8 == 1, (pend - zrow) / SUBLANES, pend)).astype(I32)
    bi = (lax.broadcasted_iota(I32, (nblk_pad, LANES), 0) * MOE_BLK).astype(F32)
    lane_b = lax.broadcasted_iota(I32, (nblk_pad, LANES), 1)
    is_e = (lane_b >= R_E0) & (lane_b < R_E0 + N_EXPERTS)
    nfull = jnp.sum(jnp.where(is_e & (bi >= pend[0:1, :]), 1.0, 0.0), axis=1, keepdims=True)
    nused = jnp.max(pend[0:1, :], axis=1, keepdims=True) / MOE_BLK
    row = lax.broadcasted_iota(I32, (nblk_pad, LANES), 0)
    be_ref[...] = jnp.where(row == nblk_pad - 1, nused, jnp.minimum(nfull, N_EXPERTS - 1.0)).astype(I32)


def moe_plan(cnt, meta, *, nblk_pad):
    t = meta.shape[0]
    tm = math.gcd(t, 1024)
    kern = functools.partial(_plan_kernel, nblk_pad=nblk_pad)
    return pl.pallas_call(
        kern,
        grid=(t // tm,),
        in_specs=[pl.BlockSpec((SUBLANES, LANES), lambda i: (0, 0)),
                  pl.BlockSpec((tm, LANES), lambda i: (i, 0))],
        out_specs=[pl.BlockSpec((tm, LANES), lambda i: (i, 0)),
                   pl.BlockSpec((SUBLANES, LANES), lambda i: (0, 0)),
                   pl.BlockSpec((nblk_pad, LANES), lambda i: (0, 0))],
        out_shape=[jax.ShapeDtypeStruct((t, LANES), I32),
                   jax.ShapeDtypeStruct((SUBLANES, LANES), I32),
                   jax.ShapeDtypeStruct((nblk_pad, LANES), I32)],
        compiler_params=_cparams(("arbitrary",)),
        name="moe_plan",
    )(cnt, meta)


DMA_UNROLL = 8
DISPATCH_SLOTS = 3


def _dispatch_kernel(dest_ref, zrow_ref, zcnt_ref, end_ref, h_hbm, xd_hbm, hbuf, zbuf, zblk, lsem, ssem, zsem,
                     *, tm, nsteps, nrows):
    i = pl.program_id(0)
    slot = i % DISPATCH_SLOTS

    def load(step, slot_):
        return pltpu.make_async_copy(h_hbm.at[pl.ds(step * tm, tm)], hbuf.at[slot_], lsem.at[slot_])

    def rows_done(slot_):
        return pltpu.make_async_copy(hbuf.at[slot_], xd_hbm.at[pl.ds(0, tm)], ssem.at[slot_])

    def zero_group(e, g):
        row = pl.multiple_of(zrow_ref[e] + g * SUBLANES, SUBLANES)
        return pltpu.make_async_copy(zbuf, xd_hbm.at[pl.ds(row, SUBLANES)], zsem)

    @pl.when(i == 0)
    def _():
        load(0, 0).start()
        if nsteps > 1:
            load(1, 1).start()
        zbuf[...] = jnp.zeros(zbuf.shape, zbuf.dtype)

        def per_expert(fn):
            def body(e, carry):
                lax.fori_loop(0, zcnt_ref[e], lambda g, c: (fn(zero_group(e, g)), c)[1], 0)
                return carry
            lax.fori_loop(0, N_EXPERTS, body, 0)

        per_expert(lambda cp: cp.start())
        per_expert(lambda cp: cp.wait())

        zblk[...] = jnp.zeros(zblk.shape, zblk.dtype)
        ntail = (nrows - end_ref[0]) // MOE_BLK

        def tail_block(j):
            row = pl.multiple_of(end_ref[0] + j * MOE_BLK, MOE_BLK)
            return pltpu.make_async_copy(zblk, xd_hbm.at[pl.ds(row, MOE_BLK)], zsem)

        lax.fori_loop(0, ntail, lambda j, c: (tail_block(j).start(), c)[1], 0)
        lax.fori_loop(0, ntail, lambda j, c: (tail_block(j).wait(), c)[1], 0)

    load(i, slot).wait()

    def start(t, carry):
        for k in range(2):
            pltpu.make_async_copy(hbuf.at[slot, pl.ds(t, 1)], xd_hbm.at[pl.ds(dest_ref[2 * t + k], 1)],
                                  ssem.at[slot]).start(priority=k)
        return carry

    lax.fori_loop(0, tm, start, 0, unroll=DMA_UNROLL)

    @pl.when(i >= 1)
    def _():
        prev = (i + DISPATCH_SLOTS - 1) % DISPATCH_SLOTS
        rows_done(prev).wait()
        rows_done(prev).wait()

    @pl.when(i + 2 < nsteps)
    def _():
        load(i + 2, (i + 2) % DISPATCH_SLOTS).start()

    @pl.when(i == nsteps - 1)
    def _():
        rows_done(slot).wait()
        rows_done(slot).wait()


def moe_dispatch(dest_flat, zrow, zcnt, end, h, nrows, *, tm=1024):
    t, d = h.shape
    tm = min(tm, t)
    nsteps = t // tm
    kern = functools.partial(_dispatch_kernel, tm=tm, nsteps=nsteps, nrows=nrows)
    smem_all = pl.BlockSpec(memory_space=pltpu.SMEM)
    return pl.pallas_call(
        kern,
        grid=(nsteps,),
        in_specs=[pl.BlockSpec((2 * tm,), lambda i: (i,), memory_space=pltpu.SMEM), smem_all, smem_all, smem_all,
                  pl.BlockSpec(memory_space=pl.ANY)],
        out_specs=pl.BlockSpec(memory_space=pl.ANY),
        out_shape=jax.ShapeDtypeStruct((nrows, d), h.dtype),
        scratch_shapes=[pltpu.VMEM((DISPATCH_SLOTS, tm, d), h.dtype),
                        pltpu.VMEM((SUBLANES, d), h.dtype),
                        pltpu.VMEM((MOE_BLK, d), h.dtype),
                        pltpu.SemaphoreType.DMA((DISPATCH_SLOTS,)),
                        pltpu.SemaphoreType.DMA((DISPATCH_SLOTS,)),
                        pltpu.SemaphoreType.DMA],
        compiler_params=_cparams(("arbitrary",)),
        name="moe_dispatch",
    )(dest_flat, zrow, zcnt, end, h)


def _experts_kernel(be_ref, x_ref, wg_ref, wu_ref, wd_ref, o_ref, wg_s, wu_s, wd_s, *, nblk_pad):
    i = pl.program_id(0)
    nused = be_ref[nblk_pad - 1]

    @pl.when(i < nused)
    def _():
        prev = be_ref[jnp.maximum(i - 1, 0)]

        @pl.when((i == 0) | (be_ref[i] != prev))
        def _():
            wg_s[...] = wg_ref[0].astype(BF16)
            wu_s[...] = wu_ref[0].astype(BF16)
            wd_s[...] = wd_ref[0].astype(BF16)

        half = MOE_BLK // 2

        def unpack(w):
            hi = pltpu.bitcast(w & jnp.uint32(0xFFFF0000), F32)
            lo = pltpu.bitcast(w << 16, F32)
            return jnp.concatenate([hi, lo], axis=1).astype(BF16)

        xs = [unpack(x_ref[r * half:(r + 1) * half, :]) for r in range(2)]
        gs = [jnp.dot(x, wg_s[...], preferred_element_type=F32) for x in xs]
        us = [jnp.dot(x, wu_s[...], preferred_element_type=F32) for x in xs]
        acts = [(_silu(g) * u).astype(BF16) for g, u in zip(gs, us)]
        for r in range(2):
            o_ref[r * half:(r + 1) * half, :] = jnp.dot(acts[r], wd_s[...], preferred_element_type=F32)

    @pl.when(i >= nused)
    def _():
        o_ref[...] = jnp.zeros_like(o_ref)


def moe_experts(be_flat, xd, w_gate, w_up, w_down, e0, *, nblk, nblk_pad):
    d, f = w_gate.shape[1], w_gate.shape[2]
    kern = functools.partial(_experts_kernel, nblk_pad=nblk_pad)
    grid_spec = pltpu.PrefetchScalarGridSpec(
        num_scalar_prefetch=1,
        grid=(nblk,),
        in_specs=[pl.BlockSpec((MOE_BLK, d // 2), lambda i, be: (i, 0)),
                  pl.BlockSpec((1, d, f), lambda i, be: (e0 + be[i], 0, 0)),
                  pl.BlockSpec((1, d, f), lambda i, be: (e0 + be[i], 0, 0)),
                  pl.BlockSpec((1, f, d), lambda i, be: (e0 + be[i], 0, 0))],
        out_specs=pl.BlockSpec((MOE_BLK, d), lambda i, be: (i, 0)),
        scratch_shapes=[pltpu.VMEM((d, f), BF16), pltpu.VMEM((d, f), BF16), pltpu.VMEM((f, d), BF16)],
    )
    return pl.pallas_call(
        kern,
        grid_spec=grid_spec,
        out_shape=jax.ShapeDtypeStruct((xd.shape[0], d), F32),
        compiler_params=_cparams(("arbitrary",)),
        name="moe_experts",
    )(be_flat, xd, w_gate, w_up, w_down)


def _combine_kernel(dest_ref, dest_next_ref, x_ref, wts_ref, gf_ref, yd_hbm, o_ref, rbuf, sem,
                    *, tm, nsteps, final_norm):
    i = pl.program_id(0)
    slot = i % 2

    def gather(dref, slot_):
        def start(t, carry):
            for k in range(2):
                pltpu.make_async_copy(yd_hbm.at[pl.ds(dref[2 * t + k], 1)], rbuf.at[slot_, k, pl.ds(t, 1)],
                                      sem.at[slot_]).start(priority=k)
            return carry
        lax.fori_loop(0, tm, start, 0, unroll=DMA_UNROLL)

    @pl.when(i == 0)
    def _():
        gather(dest_ref, 0)

    @pl.when(i + 1 < nsteps)
    def _():
        gather(dest_next_ref, 1 - slot)

    for k in range(2):
        pltpu.make_async_copy(yd_hbm.at[pl.ds(0, tm)], rbuf.at[slot, k], sem.at[slot]).wait()
    w = wts_ref[...]
    y = x_ref[...] + rbuf[slot, 0] * w[:, 0:1] + rbuf[slot, 1] * w[:, 1:2]
    if final_norm:
        y = y * lax.rsqrt(jnp.mean(y * y, axis=-1, keepdims=True) + EPS) * gf_ref[...]
    o_ref[...] = y


def moe_combine(dest_flat, x, wts, yd, g_final, *, row0=0, rows=None, tm=1024):
    t, d = (rows or x.shape[0]), x.shape[1]
    tm = min(tm, t)
    nsteps = t // tm
    blk0 = row0 // tm
    final_norm = g_final is not None
    gf = g_final.reshape(1, d) if final_norm else jnp.ones((1, d), F32)
    kern = functools.partial(_combine_kernel, tm=tm, nsteps=nsteps, final_norm=final_norm)
    return pl.pallas_call(
        kern,
        grid=(nsteps,),
        in_specs=[pl.BlockSpec((2 * tm,), lambda i: (i + blk0,), memory_space=pltpu.SMEM),
                  pl.BlockSpec((2 * tm,), lambda i: (jnp.minimum(i + 1, nsteps - 1) + blk0,),
                               memory_space=pltpu.SMEM),
                  pl.BlockSpec((tm, d), lambda i: (i + blk0, 0)),
                  pl.BlockSpec((tm, LANES), lambda i: (i + blk0, 0)),
                  pl.BlockSpec((1, d), lambda i: (0, 0)),
                  pl.BlockSpec(memory_space=pl.ANY)],
        out_specs=pl.BlockSpec((tm, d), lambda i: (i, 0)),
        out_shape=jax.ShapeDtypeStruct((t, d), F32),
        scratch_shapes=[pltpu.VMEM((2, 2, tm, d), F32), pltpu.SemaphoreType.DMA((2,))],
        compiler_params=_cparams(("arbitrary",)),
        name="moe_combine",
    )(dest_flat, dest_flat, x, wts, gf, yd)


def moe_block(x, g, wr, br, w_gate, w_up, w_down, e0, g_final=None, splits=None):
    t, d = x.shape
    nblk = (2 * t) // MOE_BLK + N_EXPERTS
    nblk_pad = -(-(nblk + 1) // SUBLANES) * SUBLANES
    h, meta, wts, cnt = moe_route(x, g, wr, br)
    dest, ex, be = moe_plan(cnt, meta, nblk_pad=nblk_pad)
    dest_flat = dest[:, :2].reshape(2 * t)
    per_expert = lambda r: ex[r, R_E0:R_E0 + N_EXPERTS]
    end_last = ex[2, R_E0 + N_EXPERTS - 1:R_E0 + N_EXPERTS]
    xd = moe_dispatch(dest_flat, per_expert(0), per_expert(1), end_last, h, nblk * MOE_BLK)
    yd = moe_experts(be[:, 0], xd, w_gate, w_up, w_down, e0, nblk=nblk, nblk_pad=nblk_pad)
    if splits is None:
        return moe_combine(dest_flat, x, wts, yd, g_final)
    return [moe_combine(dest_flat, x, wts, yd, g_final, row0=r0, rows=n) for r0, n in splits]


def _pad_rows(buf):
    return jnp.pad(buf, ((0, 0), (0, SUBLANES - buf.shape[1]), (0, 0)))


def _forward(x_long, x_short, grp_long, grp_short, wts):
    depth = grp_long["mem_k"].shape[0]
    bl, ll = grp_long["batch"], grp_long["seqlen"]
    bs, ls = grp_short["batch"], grp_short["seqlen"]
    t_long, t_short = bl * ll, bs * ls
    d = x_long.shape[1]
    new = {id(grp_long): dict(gdn=[], conv=[], sc=[], ret=[]), id(grp_short): dict(gdn=[], conv=[], sc=[], ret=[])}

    def record_even(grp, s_new, cq, cs):
        rec = new[id(grp)]
        rec["gdn"].append(s_new)
        rec["conv"].append(cq.reshape(grp["batch"], SUBLANES, W_QKV_A)[:, :CONV_A - 1])
        rec["sc"].append(cs.reshape(grp["batch"], SUBLANES, D_B)[:, :CONV_B - 1])

    x_all = None
    for layer in range(depth):
        src_long = x_long if x_all is None else x_all
        src_short, row0 = (x_short, 0) if x_all is None else (x_all, t_long)
        if layer % 2 == 0:
            i = layer // 2
            hist = lambda grp: (_pad_rows(grp["conv"][i]), _pad_rows(grp["sc"][i]), wts["w_conv_qkv"][i],
                                wts["w_conv_sc"][i], wts["gdn_prm"][i], wts["gdn_norm"][i], grp["gdn"][i])
            p = rms_matmul(src_short, wts["norm_mix"][layer], wts["w_in_a"][i], tn=768, rows=t_short, row0=row0)
            mix, s_new, cq, cs = gdn_core(p, *hist(grp_short), batch=bs, seqlen=ls)
            xs = matmul_res(mix, wts["w_out_a"][i], src_short, res_row0=row0)
            record_even(grp_short, s_new, cq, cs)
            xl, s_new, cq, cs = gdn_block(src_long, wts["norm_mix"][layer], wts["w_in_a"][i], wts["w_out_a"][i],
                                          *hist(grp_long), batch=bl, seqlen=ll)
            record_even(grp_long, s_new, cq, cs)
        else:
            j = layer // 2
            p = rms_matmul(src_short, wts["norm_mix"][layer], wts["w_in_c"][j], tn=768, rows=t_short, row0=row0)
            ret, r_new = ret_core(p, grp_short["pos"], wts["ret_norm"][j], grp_short["ret"][j], batch=bs, seqlen=ls)
            xs = matmul_res(ret, wts["w_out_c"][j], src_short, res_row0=row0)
            new[id(grp_short)]["ret"].append(r_new)
            xl, r_new = ret_block(src_long, wts["norm_mix"][layer], wts["w_in_c"][j], wts["w_out_c"][j],
                                  grp_long["pos"], wts["ret_norm"][j], grp_long["ret"][j], batch=bl, seqlen=ll)
            new[id(grp_long)]["ret"].append(r_new)
        q = rms_matmul(xs, wts["norm_x"][layer], wts["w_xq"][layer], tn=D_MODEL)
        att = xattn_core(q, grp_short["mem_k"], grp_short["mem_v"], layer, batch=bs, seqlen=ls)
        joint = jnp.zeros((t_long + t_short, d), F32) if x_all is None else x_all
        joint = matmul_res(att, wts["w_xo"][layer], xs, into=joint, out_row0=t_long)
        joint = xattn_block(xl, wts["norm_x"][layer], wts["w_xq"][layer], wts["w_xo"][layer],
                            grp_long["mem_k"], grp_long["mem_v"], layer, joint, batch=bl, seqlen=ll)
        last = layer == depth - 1
        x_all = moe_block(joint, wts["norm_ffn"][layer], wts["w_route"][layer], wts["b_route"][layer],
                          wts["w_exp_gate"], wts["w_exp_up"], wts["w_exp_down"], layer * N_EXPERTS,
                          g_final=wts["norm_final"] if last else None,
                          splits=[(0, t_long), (t_long, t_short)] if last else None)
    y_long, y_short = x_all
    stack = lambda grp: tuple(jnp.stack(new[id(grp)][k]) for k in ("gdn", "conv", "sc", "ret"))
    return (y_long,) + stack(grp_long), (y_short,) + stack(grp_short)


def kernel(x_prompt, x_sample, state_gdn, state_gdn_conv, state_sconv, state_ret, cache_mem_k, cache_mem_v, mem_prompt, norm_mix, norm_x, norm_ffn, norm_final, norm_mem, w_in_a, w_conv_qkv, a_log, dt_bias, gdn_norm, w_conv_sc, w_out_a, w_in_c, ret_norm, w_out_c, w_xq, w_xk, w_xv, w_xo, w_group, b_group, w_router, b_router, w_exp_gate, w_exp_up, w_exp_down):
    bp, lp, d = x_prompt.shape
    bs, ls, _ = x_sample.shape
    depth = norm_mix.shape[0]
    n_even = w_in_a.shape[0]
    n_mem = mem_prompt.shape[1]

    qkv_w = 2 * H_A * DK_A + H_A * DV_A
    o_z = qkv_w
    o_b = o_z + H_A * DV_A
    o_a = o_b + H_A
    o_sc = o_a + H_A
    w_a = jnp.concatenate([w_in_a[:, :, :o_b], w_in_a[:, :, o_sc:], w_in_a[:, :, o_b:o_sc],
                           jnp.zeros((n_even, d, PA_COLS - PA_BA - 2 * H_A), F32)], axis=-1).astype(BF16)
    prm = jnp.zeros((n_even, SUBLANES, LANES), F32)
    prm = prm.at[:, 0, H_A:2 * H_A].set(a_log).at[:, 1, H_A:2 * H_A].set(dt_bias)
    w_route = jnp.concatenate([w_group, w_router, jnp.zeros((depth, d, LANES - N_GROUPS - N_EXPERTS), F32)],
                              axis=-1).astype(BF16)
    b_route = jnp.concatenate([b_group, b_router, jnp.zeros((depth, LANES - N_GROUPS - N_EXPERTS), F32)],
                              axis=-1).reshape(depth, 1, LANES)
    wts = dict(norm_mix=norm_mix, norm_x=norm_x, norm_ffn=norm_ffn, norm_final=norm_final,
               w_in_a=w_a, w_conv_qkv=w_conv_qkv, gdn_prm=prm, gdn_norm=gdn_norm, w_conv_sc=w_conv_sc,
               w_out_a=w_out_a.astype(BF16), w_in_c=w_in_c.astype(BF16), ret_norm=ret_norm,
               w_out_c=w_out_c.astype(BF16), w_xq=w_xq.astype(BF16), w_xo=w_xo.astype(BF16),
               w_route=w_route, b_route=b_route,
               w_exp_gate=w_exp_gate.reshape((depth * N_EXPERTS,) + w_exp_gate.shape[2:]),
               w_exp_up=w_exp_up.reshape((depth * N_EXPERTS,) + w_exp_up.shape[2:]),
               w_exp_down=w_exp_down.reshape((depth * N_EXPERTS,) + w_exp_down.shape[2:]))

    memf = mem_prompt.reshape(bp * n_mem, d)
    w_kv = jnp.concatenate([w_xk, w_xv], axis=-1).astype(BF16)
    mk_p, mv_p = kv_proj(memf, norm_mem, w_kv)
    p_cache_mem_k = mk_p.reshape(depth, bp, n_mem, H_X, HD_X)
    p_cache_mem_v = mv_p.reshape(depth, bp, n_mem, H_X, HD_X)

    n_odd = w_in_c.shape[0]
    z_gdn = jnp.zeros((n_even, bp, H_A, DK_A, DV_A), F32)
    z_conv = jnp.zeros((n_even, bp, CONV_A - 1, qkv_w), F32)
    z_sc = jnp.zeros((n_even, bp, CONV_B - 1, D_B), F32)
    z_ret = jnp.zeros((n_odd, bp, H_C, DK_C, DV_C), F32)
    pos_p = jnp.arange(lp, dtype=I32)
    pos_s = 16384 + jnp.arange(ls, dtype=I32)

    grp_p = dict(batch=bp, seqlen=lp, pos=pos_p, gdn=z_gdn, conv=z_conv, sc=z_sc, ret=z_ret,
                 mem_k=p_cache_mem_k, mem_v=p_cache_mem_v)
    grp_s = dict(batch=bs, seqlen=ls, pos=pos_s, gdn=state_gdn, conv=state_gdn_conv, sc=state_sconv,
                 ret=state_ret, mem_k=cache_mem_k, mem_v=cache_mem_v)
    (y_p, p_gdn, p_conv, p_sc, p_ret), (y_s, s_gdn, s_conv, s_sc, s_ret) = _forward(
        x_prompt.reshape(bp * lp, d), x_sample.reshape(bs * ls, d), grp_p, grp_s, wts)
    return (y_p.reshape(bp, lp, d), y_s.reshape(bs, ls, d), p_gdn, p_conv, p_sc, p_ret, p_cache_mem_k,
            p_cache_mem_v, s_gdn, s_conv, s_sc, s_ret)
```

```python
import functools

import jax
import jax.numpy as jnp
from jax import lax
from jax.experimental import pallas as pl
from jax.experimental.pallas import tpu as pltpu

F32 = jnp.float32
BF16 = jnp.bfloat16
I32 = jnp.int32

EPS = 1e-6
ROPE_BASE = 10000.0

D_MODEL = 1024
H_A, DK_A, DV_A, CONV_A = 4, 128, 128, 4
W_QKV_A = 3 * H_A * DK_A
D_B, CONV_B = D_MODEL // 2, 3
H_C, DK_C, DV_C = 4, 256, 512
H_X, HD_X, N_MEM = 4, 256, 256
N_GROUPS, E_PER_GROUP, N_EXPERTS, D_EXPERT = 4, 8, 32, 512
GDN_CHUNK = 64

LANES = 128
SUBLANES = 8
GDN_STACK = 256
VMEM_LIMIT = 56 * 1024 * 1024

PA_COLS = 3840
PA_SC = 2048
PA_BA = 3584
R_E0 = N_GROUPS
MOE_BLK = 256


def _cparams(sem):
    return pltpu.CompilerParams(dimension_semantics=sem, vmem_limit_bytes=VMEM_LIMIT)


def _dot(a, b, trans_a=False, trans_b=False):
    dn = (((0 if trans_a else 1,), (1 if trans_b else 0,)), ((), ()))
    return lax.dot_general(a.astype(BF16), b.astype(BF16), dn, preferred_element_type=F32)


def _silu(x):
    return x * (1.0 / (1.0 + jnp.exp(-x)))


def _sigmoid(x):
    return 1.0 / (1.0 + jnp.exp(-x))


def _rms_matmul_kernel(x_ref, g_ref, w_ref, o_ref, xn_ref):
    @pl.when(pl.program_id(1) == 0)
    def _():
        x = x_ref[...]
        ms = jnp.mean(x * x, axis=-1, keepdims=True)
        xn_ref[...] = (x * lax.rsqrt(ms + EPS) * g_ref[...]).astype(BF16)

    o_ref[...] = jnp.dot(xn_ref[...], w_ref[...], preferred_element_type=F32).astype(o_ref.dtype)


def rms_matmul(x, g, w, *, tn, rows=None, row0=0, out_dtype=F32, tm=1024):
    t, d = (rows or x.shape[0]), x.shape[1]
    n = w.shape[1]
    tm = min(tm, t)
    blk0 = row0 // tm
    return pl.pallas_call(
        _rms_matmul_kernel,
        grid=(t // tm, n // tn),
        in_specs=[pl.BlockSpec((tm, d), lambda i, j: (i + blk0, 0)),
                  pl.BlockSpec((1, d), lambda i, j: (0, 0)),
                  pl.BlockSpec((d, tn), lambda i, j: (0, j))],
        out_specs=pl.BlockSpec((tm, tn), lambda i, j: (i, j)),
        out_shape=jax.ShapeDtypeStruct((t, n), out_dtype),
        scratch_shapes=[pltpu.VMEM((tm, d), BF16)],
        compiler_params=_cparams(("parallel", "arbitrary")),
        name="rms_matmul",
    )(x, g.reshape(1, d), w)


def _kv_proj_kernel(x_ref, g_ref, w_ref, k_ref, v_ref):
    x = x_ref[...]
    ms = jnp.mean(x * x, axis=-1, keepdims=True)
    xn = (x * lax.rsqrt(ms + EPS) * g_ref[0]).astype(BF16)
    d = x.shape[1]
    k_ref[0] = jnp.dot(xn, w_ref[0, :, :d], preferred_element_type=F32)
    v_ref[0] = jnp.dot(xn, w_ref[0, :, d:], preferred_element_type=F32)


def kv_proj(x, g, w_kv, *, tm=1024):
    t, d = x.shape
    depth = g.shape[0]
    tm = min(tm, t)
    out_spec = pl.BlockSpec((1, tm, d), lambda l, i: (l, i, 0))
    return pl.pallas_call(
        _kv_proj_kernel,
        grid=(depth, t // tm),
        in_specs=[pl.BlockSpec((tm, d), lambda l, i: (i, 0)),
                  pl.BlockSpec((1, 1, d), lambda l, i: (l, 0, 0)),
                  pl.BlockSpec((1, d, 2 * d), lambda l, i: (l, 0, 0))],
        out_specs=[out_spec, out_spec],
        out_shape=[jax.ShapeDtypeStruct((depth, t, d), F32), jax.ShapeDtypeStruct((depth, t, d), F32)],
        compiler_params=_cparams(("parallel", "parallel")),
        name="kv_proj",
    )(x, g.reshape(depth, 1, d), w_kv)


def _matmul_res_kernel(a_ref, w_ref, r_ref, *rest):
    o_ref = rest[-1]
    o_ref[...] = r_ref[...] + jnp.dot(a_ref[...].astype(BF16), w_ref[...], preferred_element_type=F32)


def matmul_res(a, w, res, *, res_row0=0, into=None, out_row0=0, tm=512):
    t, k = a.shape
    n = w.shape[1]
    tm = min(tm, t)
    rblk, oblk = res_row0 // tm, out_row0 // tm
    in_specs = [pl.BlockSpec((tm, k), lambda i: (i, 0)),
                pl.BlockSpec((k, n), lambda i: (0, 0)),
                pl.BlockSpec((tm, n), lambda i: (i + rblk, 0))]
    args = (a, w, res)
    if into is not None:
        in_specs.append(pl.BlockSpec(memory_space=pl.ANY))
        args += (into,)
    return pl.pallas_call(
        _matmul_res_kernel,
        grid=(t // tm,),
        in_specs=in_specs,
        out_specs=pl.BlockSpec((tm, n), lambda i: (i + oblk, 0)),
        out_shape=jax.ShapeDtypeStruct((t, n) if into is None else into.shape, F32),
        input_output_aliases={} if into is None else {3: 0},
        compiler_params=_cparams(("parallel",)),
        name="matmul_res",
    )(*args)


def _causal_conv(x, hist, w_ref, width, seq8):
    r = x.shape[0]
    taps = [w_ref[j:j + 1, :] for j in range(width)]

    def head(x8, h8):
        n = x8.shape[0]
        t = lax.broadcasted_iota(I32, (n, 1), 0) % SUBLANES
        y = taps[width - 1] * x8
        for s in range(1, width):
            prev = pltpu.roll(h8, (n + s - (width - 1)) % n, 0) if s != width - 1 else h8
            y = y + taps[width - 1 - s] * jnp.where(t >= s, pltpu.roll(x8, s, 0), prev)
        return y

    if seq8:
        return head(x, hist)
    y = taps[width - 1] * x
    for s in range(1, width):
        y = y + taps[width - 1 - s] * pltpu.roll(x, s, 0)
    return jnp.concatenate([head(x[:SUBLANES], hist), y[SUBLANES:]], axis=0)


def _unit_lower_inverse(ms, c, ri, ci):
    base = min(c, 16)
    eye = jnp.where(ri == ci, 1.0, 0.0).astype(F32)
    blk = (ri // base) == (ci // base)
    ds = [jnp.where(blk, m, 0.0) for m in ms]
    ps = [eye - d for d in ds]
    k = 2
    while k < base:
        ds = [_dot(d, d) for d in ds]
        ps = [_dot(p, eye + d) for p, d in zip(ps, ds)]
        k *= 2
    s = base
    while s < c:
        sel = ((ri // (2 * s)) == (ci // (2 * s))) & ((ri // s) != (ci // s))
        ts = [_dot(jnp.where(sel, m, 0.0), p) for m, p in zip(ms, ps)]
        ps = [p - _dot(p, t) for p, t in zip(ps, ts)]
        s *= 2
    return ps


def _gdn_prepare(units, c):
    n = GDN_STACK
    ri = lax.broadcasted_iota(I32, (n, n), 0)
    ci = lax.broadcasted_iota(I32, (n, n), 1)
    same = (ri // c) == (ci // c)
    incl = same & (ri >= ci)
    strict = same & (ri > ci)
    pre = []
    for q, k, v, bfull, gfull in units:
        g2 = jnp.concatenate([gfull, gfull], axis=1)
        g_row = jnp.sum(jnp.where(ri == ci, g2, 0.0), axis=0, keepdims=True)
        gc_col = jnp.sum(jnp.where(incl, g_row, 0.0), axis=1, keepdims=True)
        gc_row = jnp.sum(jnp.where(same & (ri <= ci), g2, 0.0), axis=0, keepdims=True)
        gl_col = jnp.sum(jnp.where(same, g_row, 0.0), axis=1, keepdims=True)
        decay = jnp.where(incl, jnp.exp(jnp.where(incl, gc_col - gc_row, 0.0)), 0.0)
        egc = jnp.exp(gc_col)
        kb = k * bfull
        pre.append(dict(decay=decay, kb=kb, rhs=jnp.concatenate([v * bfull, kb * egc], axis=1),
                        qd=q * egc, kd=k * jnp.exp(gl_col - gc_col), egl=jnp.exp(gl_col)))
    mms = [jnp.where(strict, _dot(e["kb"], u[1], trans_b=True) * e["decay"], 0.0) for e, u in zip(pre, units)]
    qks = [_dot(u[0], u[1], trans_b=True) * e["decay"] for e, u in zip(pre, units)]
    tinvs = _unit_lower_inverse(mms, c, ri, ci)
    uws = [_dot(t, e["rhs"]) for t, e in zip(tinvs, pre)]
    return [dict(u=uw[:, :DV_A], w=uw[:, DV_A:], qk=qk, qd=e["qd"], kd=e["kd"], egl=e["egl"])
            for uw, qk, e in zip(uws, qks, pre)]


def _gdn_recur(e, states, c):
    nprob = GDN_STACK // c
    ws, qs = [], []
    for p in range(nprob):
        sl = slice(p * c, (p + 1) * c)
        ws.append(_dot(e["w"][sl], states[p]))
        qs.append(_dot(e["qd"][sl], states[p]))
    vn = e["u"] - jnp.concatenate(ws, axis=0)
    o = _dot(e["qk"], vn) + jnp.concatenate(qs, axis=0)
    new_states = []
    for p in range(nprob):
        sl = slice(p * c, (p + 1) * c)
        new_states.append(states[p] * e["egl"][p * c:p * c + 1, :] + _dot(e["kd"][sl], vn[sl], trans_a=True))
    return o, new_states


def _gdn_compute(x, z, u_sc, scb, ba, hq, hs, wq_ref, ws_ref, prm_ref, gn, states, *, seq8, c, nu):
    unit = GDN_CHUNK
    xc = _silu(_causal_conv(x, hq, wq_ref, CONV_A, seq8))
    yb = scb * _causal_conv(u_sc, hs, ws_ref, CONV_B, seq8)
    beta_all = _sigmoid(ba)
    sp = jnp.maximum(ba + prm_ref[1:2, :], 0.0) + jnp.log1p(jnp.exp(-jnp.abs(ba + prm_ref[1:2, :])))
    g_all = -jnp.exp(prm_ref[0:1, :]) * sp

    def head_cols(a, base):
        return a[:, base * DK_A:(base + 1) * DK_A]

    cat = lambda xs: jnp.concatenate(xs, axis=0)
    units = []
    for ui in range(nu):
        rs = slice(ui * unit, (ui + 1) * unit)
        qs, ks, vs, bs, gs = [], [], [], [], []
        for h in range(H_A):
            qh = head_cols(xc, h)[rs]
            kh = head_cols(xc, H_A + h)[rs]
            qs.append(qh * lax.rsqrt(jnp.sum(qh * qh, axis=-1, keepdims=True) + EPS) * (DK_A ** -0.5))
            ks.append(kh * lax.rsqrt(jnp.sum(kh * kh, axis=-1, keepdims=True) + EPS))
            vs.append(head_cols(xc, 2 * H_A + h)[rs])
            bs.append(jnp.broadcast_to(beta_all[rs, h:h + 1], (unit, LANES)))
            gs.append(jnp.broadcast_to(g_all[rs, H_A + h:H_A + h + 1], (unit, LANES)))
        units.append((cat(qs), cat(ks), cat(vs), cat(bs), cat(gs)))
    prepared = _gdn_prepare(units, c)

    outs = []
    for ui in range(nu):
        rs = slice(ui * unit, (ui + 1) * unit)
        o, states = _gdn_recur(prepared[ui], states, c)
        zst = cat([z[rs, h * DV_A:(h + 1) * DV_A] for h in range(H_A)])
        ms = jnp.mean(o * o, axis=-1, keepdims=True)
        og = o * lax.rsqrt(ms + EPS) * gn * _silu(zst)
        outs.append(jnp.concatenate([og[h * unit:(h + 1) * unit] for h in range(H_A)], axis=1))
    o_all = outs[0] if nu == 1 else cat(outs)
    return jnp.concatenate([o_all, yb], axis=1), states


def _gdn_kernel(qkv_ref, z_ref, sch_ref, scb_ref, scc_ref, ba_ref, hq_ref, hs_ref, wq_ref, ws_ref,
                prm_ref, gn_ref, s0_ref, o_ref, sn_ref, cq_ref, cs_ref):
    x = qkv_ref[...]
    rows = x.shape[0]
    u_sc = scc_ref[...] * sch_ref[...]
    cq_ref[...] = pltpu.roll(x, rows - SUBLANES + CONV_A - 1, 0)
    cs_ref[...] = pltpu.roll(u_sc, rows - SUBLANES + CONV_B - 1, 0)
    nprob = GDN_STACK // SUBLANES
    states = [s0_ref[p % SUBLANES, p // SUBLANES] for p in range(nprob)]
    mix, states = _gdn_compute(x, z_ref[...], u_sc, scb_ref[...], ba_ref[...], hq_ref[...], hs_ref[...],
                               wq_ref, ws_ref, prm_ref, gn_ref[...], states, seq8=True, c=SUBLANES, nu=1)
    o_ref[...] = mix.astype(o_ref.dtype)
    for p in range(nprob):
        sn_ref[p % SUBLANES, p // SUBLANES] = states[p]


def _gdn_block_kernel(x_ref, g_ref, wi_ref, wo_ref, hq_ref, hs_ref, wq_ref, ws_ref, prm_ref, gn_ref, s0_ref,
                      o_ref, sn_ref, cq_ref, cs_ref, s_scr, hq_scr, hs_scr, *, nu, nl):
    l = pl.program_id(1)

    @pl.when(l == 0)
    def _():
        hq_scr[...] = hq_ref[0]
        hs_scr[...] = hs_ref[0]
        s_scr[...] = s0_ref[0]

    xres = x_ref[...]
    rows = xres.shape[0]
    ms = jnp.mean(xres * xres, axis=-1, keepdims=True)
    xn = (xres * lax.rsqrt(ms + EPS) * g_ref[...]).astype(BF16)
    proj = lambda lo, width: jnp.dot(xn, wi_ref[:, lo:lo + width], preferred_element_type=F32)
    x = proj(0, W_QKV_A)
    z = proj(W_QKV_A, H_A * DV_A)
    u_sc = proj(PA_SC + 2 * D_B, D_B) * proj(PA_SC, D_B)
    scb = proj(PA_SC + D_B, D_B)
    ba = proj(PA_BA, LANES)
    hq = hq_scr[...]
    hs = hs_scr[...]
    hq_scr[...] = pltpu.roll(x[rows - SUBLANES:], CONV_A - 1, 0)
    hs_scr[...] = pltpu.roll(u_sc[rows - SUBLANES:], CONV_B - 1, 0)
    states = [s_scr[p] for p in range(H_A)]
    mix, states = _gdn_compute(x, z, u_sc, scb, ba, hq, hs, wq_ref, ws_ref, prm_ref, gn_ref[...], states,
                               seq8=False, c=GDN_CHUNK, nu=nu)
    for p in range(H_A):
        s_scr[p] = states[p]
    o_ref[...] = xres + jnp.dot(mix.astype(BF16), wo_ref[...], preferred_element_type=F32)

    @pl.when(l == nl - 1)
    def _():
        sn_ref[0] = s_scr[...]
        cq_ref[0] = hq_scr[...]
        cs_ref[0] = hs_scr[...]


def gdn_core(p, hist_q, hist_s, w_conv_qkv, w_conv_sc, prm, gn, s0, *, batch, seqlen):
    t = batch * seqlen
    assert seqlen == SUBLANES
    rows = GDN_CHUNK
    nb = rows // seqlen
    hq_spec = pl.BlockSpec((rows, W_QKV_A), lambda i: (i, 0))
    hs_spec = pl.BlockSpec((rows, D_B), lambda i: (i, 0))
    s_spec = pl.BlockSpec((nb, H_A, DK_A, DV_A), lambda i: (i, 0, 0, 0))
    col = lambda width, blk: pl.BlockSpec((rows, width), lambda i: (i, blk))
    const = lambda shape: pl.BlockSpec(shape, lambda i: (0,) * len(shape))
    return pl.pallas_call(
        _gdn_kernel,
        grid=(batch // nb,),
        in_specs=[col(W_QKV_A, 0), col(D_B, 3), col(D_B, 4), col(D_B, 5), col(D_B, 6),
                  col(LANES, PA_BA // LANES), hq_spec, hs_spec,
                  const((CONV_A, W_QKV_A)), const((CONV_B, D_B)), const((SUBLANES, LANES)),
                  const((1, DV_A)), s_spec],
        out_specs=[pl.BlockSpec((rows, D_MODEL), lambda i: (i, 0)), s_spec, hq_spec, hs_spec],
        out_shape=[jax.ShapeDtypeStruct((t, D_MODEL), BF16),
                   jax.ShapeDtypeStruct((batch, H_A, DK_A, DV_A), F32),
                   jax.ShapeDtypeStruct((t, W_QKV_A), F32),
                   jax.ShapeDtypeStruct((t, D_B), F32)],
        compiler_params=_cparams(("parallel",)),
        name="gdn_core",
    )(p, p, p, p, p, p, hist_q.reshape(t, W_QKV_A), hist_s.reshape(t, D_B), w_conv_qkv, w_conv_sc, prm,
      gn.reshape(1, DV_A), s0)


def _resident(shape):
    return pl.BlockSpec(shape, lambda *ix: (0,) * len(shape), pipeline_mode=pl.Buffered(1))


def gdn_block(x, g, w_in, w_out, hist_q, hist_s, w_conv_qkv, w_conv_sc, prm, gn, s0, *, batch, seqlen):
    t, d = batch * seqlen, x.shape[1]
    rows = min(seqlen, 256)
    nu = rows // GDN_CHUNK
    nl = seqlen // rows
    per_b = lambda shape: pl.BlockSpec((1,) + shape, lambda b, l: (b,) + (0,) * len(shape))
    row_spec = pl.BlockSpec((rows, d), lambda b, l: (b * nl + l, 0))
    kern = functools.partial(_gdn_block_kernel, nu=nu, nl=nl)
    return pl.pallas_call(
        kern,
        grid=(batch, nl),
        in_specs=[row_spec, _resident((1, d)), _resident(w_in.shape), _resident(w_out.shape),
                  per_b((SUBLANES, W_QKV_A)), per_b((SUBLANES, D_B)),
                  _resident((CONV_A, W_QKV_A)), _resident((CONV_B, D_B)), _resident((SUBLANES, LANES)),
                  _resident((1, DV_A)), per_b((H_A, DK_A, DV_A))],
        out_specs=[row_spec, per_b((H_A, DK_A, DV_A)), per_b((SUBLANES, W_QKV_A)), per_b((SUBLANES, D_B))],
        out_shape=[jax.ShapeDtypeStruct((t, d), F32),
                   jax.ShapeDtypeStruct((batch, H_A, DK_A, DV_A), F32),
                   jax.ShapeDtypeStruct((batch, SUBLANES, W_QKV_A), F32),
                   jax.ShapeDtypeStruct((batch, SUBLANES, D_B), F32)],
        scratch_shapes=[pltpu.VMEM((H_A, DK_A, DV_A), F32),
                        pltpu.VMEM((SUBLANES, W_QKV_A), F32),
                        pltpu.VMEM((SUBLANES, D_B), F32)],
        compiler_params=_cparams(("parallel", "arbitrary")),
        name="gdn_block",
    )(x, g.reshape(1, d), w_in, w_out, hist_q, hist_s, w_conv_qkv, w_conv_sc, prm, gn.reshape(1, DV_A), s0)


def _ret_compute(get_q, get_k, get_v, get_gate, cos, sin, dm_ref, qd_ref, kd_ref, cd_ref, rn_ref, r_scr,
                 *, nseq, c):
    half = DK_C // 2

    def rot(x):
        x1, x2 = x[:, :half], x[:, half:]
        return jnp.concatenate([x1 * cos - x2 * sin, x1 * sin + x2 * cos], axis=1)

    heads = range(H_C)
    qs = [rot(get_q(h)) for h in heads]
    ks = [rot(get_k(h)) * (DK_C ** -0.5) for h in heads]
    vs = [get_v(h).astype(BF16) for h in heads]
    ss = [_dot(qs[h], ks[h], trans_b=True) * dm_ref[h] for h in heads]
    inters = []
    for h in heads:
        qdh = qs[h] * qd_ref[h]
        parts = [_dot(qdh[sq * c:(sq + 1) * c], r_scr[sq, h]) for sq in range(nseq)]
        inters.append(parts[0] if nseq == 1 else jnp.concatenate(parts, axis=0))
    outs = [_dot(ss[h], vs[h]) + inters[h] for h in heads]
    for h in heads:
        kdh = ks[h] * kd_ref[h]
        cd = cd_ref[h][0:1, 0:1]
        for sq in range(nseq):
            sl = slice(sq * c, (sq + 1) * c)
            r_scr[sq, h] = r_scr[sq, h] * cd + _dot(kdh[sl], vs[h][sl], trans_a=True)
    gated = []
    for h in heads:
        o = outs[h]
        ms = jnp.mean(o * o, axis=-1, keepdims=True)
        on = o * lax.rsqrt(ms + EPS) * rn_ref[:, h * DV_C:(h + 1) * DV_C]
        gated.append(_silu(get_gate(h)) * on)
    return gated


def _ret_kernel(q_ref, k_ref, v_ref, gate_ref, cos_ref, sin_ref, dm_ref, qd_ref, kd_ref, cd_ref,
                rn_ref, r0_ref, o_ref, rnew_ref, r_scr, *, nseq, c, nl):
    l = pl.program_id(1)

    @pl.when(l == 0)
    def _():
        r_scr[...] = r0_ref[...]

    gated = _ret_compute(lambda h: q_ref[:, h * DK_C:(h + 1) * DK_C], lambda h: k_ref[:, h * DK_C:(h + 1) * DK_C],
                         lambda h: v_ref[:, h * DV_C:(h + 1) * DV_C], lambda h: gate_ref[:, h * DV_C:(h + 1) * DV_C],
                         cos_ref[...], sin_ref[...], dm_ref, qd_ref, kd_ref, cd_ref, rn_ref, r_scr, nseq=nseq, c=c)
    for h in range(H_C):
        o_ref[:, h * DV_C:(h + 1) * DV_C] = gated[h].astype(o_ref.dtype)

    @pl.when(l == nl - 1)
    def _():
        rnew_ref[...] = r_scr[...]


def _ret_block_kernel(x_ref, g_ref, wi_ref, wo_ref, cos_ref, sin_ref, dm_ref, qd_ref, kd_ref, cd_ref,
                      rn_ref, r0_ref, o_ref, rnew_ref, r_scr, *, c, nl):
    l = pl.program_id(1)

    @pl.when(l == 0)
    def _():
        r_scr[...] = r0_ref[...]

    xres = x_ref[...]
    ms = jnp.mean(xres * xres, axis=-1, keepdims=True)
    xn = (xres * lax.rsqrt(ms + EPS) * g_ref[...]).astype(BF16)
    proj = lambda lo, width: jnp.dot(xn, wi_ref[:, lo:lo + width], preferred_element_type=F32)
    hk, hv = H_C * DK_C, H_C * DV_C
    gated = _ret_compute(lambda h: proj(h * DK_C, DK_C), lambda h: proj(hk + h * DK_C, DK_C),
                         lambda h: proj(2 * hk + h * DV_C, DV_C), lambda h: proj(2 * hk + hv + h * DV_C, DV_C),
                         cos_ref[...], sin_ref[...], dm_ref, qd_ref, kd_ref, cd_ref, rn_ref, r_scr, nseq=1, c=c)
    y = xres
    for h in range(H_C):
        y = y + jnp.dot(gated[h].astype(BF16), wo_ref[h * DV_C:(h + 1) * DV_C, :], preferred_element_type=F32)
    o_ref[...] = y

    @pl.when(l == nl - 1)
    def _():
        rnew_ref[...] = r_scr[...]


def ret_core(p, pos, ret_norm, r0, *, batch, seqlen):
    t = batch * seqlen
    assert seqlen == SUBLANES
    nseq, c = 2, seqlen
    rows = nseq * c
    cos, sin, dmat, qd, kd, cd = _ret_tables(pos, nseq, c)
    const = lambda shape: pl.BlockSpec(shape, lambda b, l: (0,) * len(shape))
    kern = functools.partial(_ret_kernel, nseq=nseq, c=c, nl=1)
    hk = H_C * DK_C
    hv = H_C * DV_C
    return pl.pallas_call(
        kern,
        grid=(batch // nseq, 1),
        in_specs=[pl.BlockSpec((rows, hk), lambda b, l: (b, 0)),
                  pl.BlockSpec((rows, hk), lambda b, l: (b, 1)),
                  pl.BlockSpec((rows, hv), lambda b, l: (b, 1)),
                  pl.BlockSpec((rows, hv), lambda b, l: (b, 2)),
                  const((rows, DK_C // 2)), const((rows, DK_C // 2)),
                  const((H_C, rows, rows)), const((H_C, rows, DK_C)), const((H_C, rows, DK_C)),
                  const((H_C, SUBLANES, LANES)), const((1, hv)),
                  pl.BlockSpec((nseq, H_C, DK_C, DV_C), lambda b, l: (b, 0, 0, 0))],
        out_specs=[pl.BlockSpec((rows, hv), lambda b, l: (b, 0)),
                   pl.BlockSpec((nseq, H_C, DK_C, DV_C), lambda b, l: (b, 0, 0, 0))],
        out_shape=[jax.ShapeDtypeStruct((t, hv), BF16),
                   jax.ShapeDtypeStruct((batch, H_C, DK_C, DV_C), F32)],
        scratch_shapes=[pltpu.VMEM((nseq, H_C, DK_C, DV_C), F32)],
        compiler_params=_cparams(("parallel", "arbitrary")),
        name="ret_core",
    )(p, p, p, p, cos, sin, dmat, qd, kd, cd, ret_norm.reshape(1, hv), r0)


def _ret_tables(pos, nseq, c):
    half = DK_C // 2
    inv = ROPE_BASE ** (-jnp.arange(half, dtype=F32) / half)
    ang = pos.astype(F32)[:, None] * inv[None, :]
    cos, sin = jnp.cos(ang), jnp.sin(ang)
    if nseq > 1:
        cos, sin = jnp.tile(cos, (nseq, 1)), jnp.tile(sin, (nseq, 1))
    lg = jnp.log(1.0 - 2.0 ** (-5.0 - jnp.arange(H_C, dtype=F32)))[:, None]
    i = jnp.arange(c, dtype=F32)
    incl = i[:, None] >= i[None, :]
    dmat = jnp.exp(jnp.where(incl[None], (i[:, None] - i[None, :])[None] * lg[..., None], -jnp.inf))
    if nseq > 1:
        dmat = jnp.kron(jnp.eye(nseq, dtype=F32)[None], dmat)
    qd = jnp.tile(jnp.exp((i + 1.0)[None] * lg), (1, nseq))[..., None] * jnp.ones((1, 1, DK_C), F32)
    kd = jnp.tile(jnp.exp((c - 1.0 - i)[None] * lg), (1, nseq))[..., None] * jnp.ones((1, 1, DK_C), F32)
    cd = jnp.exp(c * lg)[..., None] * jnp.ones((1, SUBLANES, LANES), F32)
    return cos, sin, dmat, qd, kd, cd


def ret_block(x, g, w_in, w_out, pos, ret_norm, r0, *, batch, seqlen):
    t, d = batch * seqlen, x.shape[1]
    c = min(seqlen, 256)
    nl = seqlen // c
    cos, sin, dmat, qd, kd, cd = _ret_tables(pos, 1, c)
    hv = H_C * DV_C
    row_spec = pl.BlockSpec((c, d), lambda b, l: (b * nl + l, 0))
    trig_spec = pl.BlockSpec((c, DK_C // 2), lambda b, l: (l, 0))
    state_spec = pl.BlockSpec((1, H_C, DK_C, DV_C), lambda b, l: (b, 0, 0, 0))
    kern = functools.partial(_ret_block_kernel, c=c, nl=nl)
    return pl.pallas_call(
        kern,
        grid=(batch, nl),
        in_specs=[row_spec, _resident((1, d)), _resident(w_in.shape), _resident(w_out.shape),
                  trig_spec, trig_spec,
                  _resident((H_C, c, c)), _resident((H_C, c, DK_C)), _resident((H_C, c, DK_C)),
                  _resident((H_C, SUBLANES, LANES)), _resident((1, hv)), state_spec],
        out_specs=[row_spec, state_spec],
        out_shape=[jax.ShapeDtypeStruct((t, d), F32),
                   jax.ShapeDtypeStruct((batch, H_C, DK_C, DV_C), F32)],
        scratch_shapes=[pltpu.VMEM((1, H_C, DK_C, DV_C), F32)],
        compiler_params=_cparams(("parallel", "arbitrary")),
        name="ret_block",
    )(x, g.reshape(1, d), w_in, w_out, cos, sin, dmat, qd, kd, cd, ret_norm.reshape(1, hv), r0)


def _xattn_fetch(mk_hbm, mv_hbm, kbuf, vbuf, sem, *, layer, nb, nsteps):
    i = pl.program_id(0)
    l = pl.program_id(1)
    slot = i % 2

    def copies(step, slot_):
        out = []
        for b in range(nb):
            for h in range(H_X):
                out.append(pltpu.make_async_copy(mk_hbm.at[layer, step * nb + b, :, h, :],
                                                 kbuf.at[slot_, b, h], sem.at[slot_]))
                out.append(pltpu.make_async_copy(mv_hbm.at[layer, step * nb + b, :, h, :],
                                                 vbuf.at[slot_, b, h], sem.at[slot_]))
        return out

    @pl.when(l == 0)
    def _():
        @pl.when(i == 0)
        def _():
            for c in copies(i, slot):
                c.start()

        @pl.when(i + 1 < nsteps)
        def _():
            for c in copies(i + 1, 1 - slot):
                c.start()

        for c in copies(i, slot):
            c.wait()

    return slot


def _xattn_heads(q_of, kbuf, vbuf, slot, probs):
    ss = [_dot(q_of(b, h), kbuf[slot, b, h], trans_b=True) * (HD_X ** -0.5) for b, h in probs]
    ps = []
    for s in ss:
        e = jnp.exp(s - jnp.max(s, axis=-1, keepdims=True))
        ps.append(e / jnp.sum(e, axis=-1, keepdims=True))
    return [_dot(p, vbuf[slot, b, h]) for p, (b, h) in zip(ps, probs)]


def _xattn_kernel(q_ref, mk_hbm, mv_hbm, o_ref, kbuf, vbuf, sem, *, layer, nb, lq, nsteps):
    slot = _xattn_fetch(mk_hbm, mv_hbm, kbuf, vbuf, sem, layer=layer, nb=nb, nsteps=nsteps)
    probs = [(b, h) for b in range(nb) for h in range(H_X)]
    win = lambda b, h: (slice(b * lq, (b + 1) * lq), slice(h * HD_X, (h + 1) * HD_X))
    outs = _xattn_heads(lambda b, h: q_ref[win(b, h)], kbuf, vbuf, slot, probs)
    for o, (b, h) in zip(outs, probs):
        o_ref[win(b, h)] = o.astype(o_ref.dtype)


def _xattn_block_kernel(x_ref, g_ref, wq_ref, wo_ref, mk_hbm, mv_hbm, into_hbm, o_ref, kbuf, vbuf, sem,
                        *, layer, nsteps):
    del into_hbm
    slot = _xattn_fetch(mk_hbm, mv_hbm, kbuf, vbuf, sem, layer=layer, nb=1, nsteps=nsteps)
    xres = x_ref[...]
    ms = jnp.mean(xres * xres, axis=-1, keepdims=True)
    xn = (xres * lax.rsqrt(ms + EPS) * g_ref[...]).astype(BF16)
    probs = [(0, h) for h in range(H_X)]
    qs = [jnp.dot(xn, wq_ref[:, h * HD_X:(h + 1) * HD_X], preferred_element_type=F32) for h in range(H_X)]
    outs = _xattn_heads(lambda b, h: qs[h], kbuf, vbuf, slot, probs)
    y = xres
    for h in range(H_X):
        y = y + jnp.dot(outs[h].astype(BF16), wo_ref[h * HD_X:(h + 1) * HD_X, :], preferred_element_type=F32)
    o_ref[...] = y


def xattn_core(q, mk, mv, layer, *, batch, seqlen):
    t = batch * seqlen
    d = H_X * HD_X
    assert seqlen == SUBLANES
    nb, lq = 2, seqlen
    rows = nb * lq
    nsteps = batch // nb
    kern = functools.partial(_xattn_kernel, layer=layer, nb=nb, lq=lq, nsteps=nsteps)
    return pl.pallas_call(
        kern,
        grid=(nsteps, 1),
        in_specs=[pl.BlockSpec((rows, d), lambda b, l: (b, 0)),
                  pl.BlockSpec(memory_space=pl.ANY), pl.BlockSpec(memory_space=pl.ANY)],
        out_specs=pl.BlockSpec((rows, d), lambda b, l: (b, 0)),
        out_shape=jax.ShapeDtypeStruct((t, d), q.dtype),
        scratch_shapes=[pltpu.VMEM((2, nb, H_X, N_MEM, HD_X), F32),
                        pltpu.VMEM((2, nb, H_X, N_MEM, HD_X), F32),
                        pltpu.SemaphoreType.DMA((2,))],
        compiler_params=_cparams(("arbitrary", "arbitrary")),
        name="xattn_core",
    )(q, mk, mv)


def xattn_block(x, g, w_q, w_o, mk, mv, layer, into, *, batch, seqlen):
    d = x.shape[1]
    lq = min(seqlen, 512)
    nl = seqlen // lq
    row_spec = pl.BlockSpec((lq, d), lambda b, l: (b * nl + l, 0))
    kern = functools.partial(_xattn_block_kernel, layer=layer, nsteps=batch)
    return pl.pallas_call(
        kern,
        grid=(batch, nl),
        in_specs=[row_spec, _resident((1, d)), _resident(w_q.shape), _resident(w_o.shape),
                  pl.BlockSpec(memory_space=pl.ANY), pl.BlockSpec(memory_space=pl.ANY),
                  pl.BlockSpec(memory_space=pl.ANY)],
        out_specs=row_spec,
        out_shape=jax.ShapeDtypeStruct(into.shape, F32),
        scratch_shapes=[pltpu.VMEM((2, 1, H_X, N_MEM, HD_X), F32),
                        pltpu.VMEM((2, 1, H_X, N_MEM, HD_X), F32),
                        pltpu.SemaphoreType.DMA((2,))],
        input_output_aliases={6: 0},
        compiler_params=_cparams(("arbitrary", "arbitrary")),
        name="xattn_block",
    )(x, g.reshape(1, d), w_q, w_o, mk, mv, into)


def _route_kernel(x_ref, g_ref, wr_ref, br_ref, h_ref, meta_ref, wts_ref, cnt_ref, cnt_scr):
    i = pl.program_id(0)

    @pl.when(i == 0)
    def _():
        cnt_scr[...] = jnp.zeros_like(cnt_scr)

    x = x_ref[...]
    tm = x.shape[0]
    ms = jnp.mean(x * x, axis=-1, keepdims=True)
    h = x * lax.rsqrt(ms + EPS) * g_ref[...]
    hb = h.astype(BF16)
    bits = pltpu.bitcast(hb.astype(F32), jnp.uint32)
    half_d = bits.shape[1] // 2
    h_ref[...] = bits[:, :half_d] | (bits[:, half_d:] >> 16)
    logits = jnp.dot(hb, wr_ref[...], preferred_element_type=F32) + br_ref[...]
    lane_i = lax.broadcasted_iota(I32, (tm, LANES), 1)
    lane = lane_i.astype(F32)
    neg = jnp.float32(-3.0e38)
    big = jnp.float32(LANES)
    is_g = lane_i < N_GROUPS
    gl = jnp.where(is_g, logits, neg)
    gmax = jnp.max(gl, axis=1, keepdims=True)
    grp = jnp.min(jnp.where(gl == gmax, lane, big), axis=1, keepdims=True)
    gsum = jnp.sum(jnp.where(is_g, jnp.exp(jnp.where(is_g, logits - gmax, 0.0)), 0.0), axis=1, keepdims=True)
    p_grp = 1.0 / gsum
    in_grp = ((lane_i >= R_E0) & (lane_i < R_E0 + N_EXPERTS)
              & (jnp.floor((lane - R_E0) * (1.0 / E_PER_GROUP)) == grp))
    el = jnp.where(in_grp, logits, neg)
    m1 = jnp.max(el, axis=1, keepdims=True)
    i1 = jnp.min(jnp.where(el == m1, lane, big), axis=1, keepdims=True)
    el2 = jnp.where(lane == i1, neg, el)
    m2 = jnp.max(el2, axis=1, keepdims=True)
    i2 = jnp.min(jnp.where(el2 == m2, lane, big), axis=1, keepdims=True)
    esum = jnp.sum(jnp.where(in_grp, jnp.exp(jnp.where(in_grp, logits - m1, 0.0)), 0.0), axis=1, keepdims=True)
    p1 = 1.0 / esum
    p2 = jnp.exp(m2 - m1) / esum
    tot = p1 + p2
    w1 = p_grp * (p1 / tot)
    w2 = p_grp * (p2 / tot)
    wts_ref[...] = jnp.where(lane_i == 0, w1, jnp.where(lane_i == 1, w2, 0.0))

    oh1 = (lane == i1).astype(F32)
    oh2 = (lane == i2).astype(F32)
    rr = lax.broadcasted_iota(I32, (tm, tm), 0)
    cc = lax.broadcasted_iota(I32, (tm, tm), 1)
    tri = (rr > cc).astype(BF16)
    base = cnt_scr[0:1, :]
    c1 = jnp.sum(oh1, axis=0, keepdims=True)
    c2 = jnp.sum(oh2, axis=0, keepdims=True)
    r1 = jnp.sum(oh1 * (jnp.dot(tri, oh1.astype(BF16), preferred_element_type=F32) + base), axis=1, keepdims=True)
    r2 = jnp.sum(oh2 * (jnp.dot(tri, oh2.astype(BF16), preferred_element_type=F32) + base + c1), axis=1, keepdims=True)
    new_cnt = base + c1 + c2
    cnt_scr[...] = jnp.broadcast_to(new_cnt, cnt_scr.shape)
    cnt_ref[...] = jnp.broadcast_to(new_cnt, cnt_ref.shape)
    meta = jnp.where(lane_i == 0, i1, jnp.where(lane_i == 1, i2, 0.0))
    meta = jnp.where(lane_i == 2, r1, jnp.where(lane_i == 3, r2, meta))
    meta_ref[...] = meta.astype(I32)


def moe_route(x, g, wr, br, *, tm=512):
    t, d = x.shape
    tm = min(tm, t)
    return pl.pallas_call(
        _route_kernel,
        grid=(t // tm,),
        in_specs=[pl.BlockSpec((tm, d), lambda i: (i, 0)),
                  pl.BlockSpec((1, d), lambda i: (0, 0)),
                  pl.BlockSpec((d, LANES), lambda i: (0, 0)),
                  pl.BlockSpec((1, LANES), lambda i: (0, 0))],
        out_specs=[pl.BlockSpec((tm, d // 2), lambda i: (i, 0)),
                   pl.BlockSpec((tm, LANES), lambda i: (i, 0)),
                   pl.BlockSpec((tm, LANES), lambda i: (i, 0)),
                   pl.BlockSpec((SUBLANES, LANES), lambda i: (0, 0))],
        out_shape=[jax.ShapeDtypeStruct((t, d // 2), jnp.uint32),
                   jax.ShapeDtypeStruct((t, LANES), I32),
                   jax.ShapeDtypeStruct((t, LANES), F32),
                   jax.ShapeDtypeStruct((SUBLANES, LANES), F32)],
        scratch_shapes=[pltpu.VMEM((SUBLANES, LANES), F32)],
        compiler_params=_cparams(("arbitrary",)),
        name="moe_route",
    )(x, g.reshape(1, d), wr, br)


def _plan_kernel(cnt_ref, meta_ref, dest_ref, ex_ref, be_ref, *, nblk_pad):
    cnt = cnt_ref[...]
    lane8 = lax.broadcasted_iota(I32, (SUBLANES, LANES), 1)
    padded = jnp.ceil(cnt / MOE_BLK) * MOE_BLK
    pend = padded
    s = 1
    while s < LANES:
        pend = pend + jnp.where(lane8 >= s, pltpu.roll(pend, s, 1), 0.0)
        s *= 2
    pstart = (pend - padded)[0:1, :]
    meta = meta_ref[...].astype(F32)
    tm = meta.shape[0]
    lane_i = lax.broadcasted_iota(I32, (tm, LANES), 1)
    lane = lane_i.astype(F32)
    col = lambda j: jnp.sum(jnp.where(lane_i == j, meta, 0.0), axis=1, keepdims=True)
    e1, e2, r1, r2 = col(0), col(1), col(2), col(3)
    d1 = jnp.sum(jnp.where(lane == e1, pstart, 0.0), axis=1, keepdims=True) + r1
    d2 = jnp.sum(jnp.where(lane == e2, pstart, 0.0), axis=1, keepdims=True) + r2
    dest_ref[...] = jnp.where(lane_i == 0, d1, jnp.where(lane_i == 1, d2, 0.0)).astype(I32)
    sub8 = lax.broadcasted_iota(I32, (SUBLANES, LANES), 0)
    zrow = jnp.floor((pend - padded + cnt) / SUBLANES) * SUBLANES
    ex_ref[...] = jnp.where(sub8 == 0, zrow, jnp.where(sub8 == 1, (pend - zrow) / SUBLANES, pend)).astype(I32)
    bi = (lax.broadcasted_iota(I32, (nblk_pad, LANES), 0) * MOE_BLK).astype(F32)
    lane_b = lax.broadcasted_iota(I32, (nblk_pad, LANES), 1)
    is_e = (lane_b >= R_E0) & (lane_b < R_E0 + N_EXPERTS)
    nfull = jnp.sum(jnp.where(is_e & (bi >= pend[0:1, :]), 1.0, 0.0), axis=1, keepdims=True)
    nused = jnp.max(pend[0:1, :], axis=1, keepdims=True) / MOE_BLK
    row = lax.broadcasted_iota(I32, (nblk_pad, LANES), 0)
    be_ref[...] = jnp.where(row == nblk_pad - 1, nused, jnp.minimum(nfull, N_EXPERTS - 1.0)).astype(I32)


def moe_plan(cnt, meta, *, nblk_pad):
    t = meta.shape[0]
    tm = max(m for m in range(SUBLANES, min(t, 4096) + 1, SUBLANES) if t % m == 0)
    kern = functools.partial(_plan_kernel, nblk_pad=nblk_pad)
    return pl.pallas_call(
        kern,
        grid=(t // tm,),
        in_specs=[pl.BlockSpec((SUBLANES, LANES), lambda i: (0, 0)),
                  pl.BlockSpec((tm, LANES), lambda i: (i, 0))],
        out_specs=[pl.BlockSpec((tm, LANES), lambda i: (i, 0)),
                   pl.BlockSpec((SUBLANES, LANES), lambda i: (0, 0)),
                   pl.BlockSpec((nblk_pad, LANES), lambda i: (0, 0))],
        out_shape=[jax.ShapeDtypeStruct((t, LANES), I32),
                   jax.ShapeDtypeStruct((SUBLANES, LANES), I32),
                   jax.ShapeDtypeStruct((nblk_pad, LANES), I32)],
        compiler_params=_cparams(("arbitrary",)),
        name="moe_plan",
    )(cnt, meta)


DMA_UNROLL = 8
DISPATCH_SLOTS = 3


def _dispatch_kernel(dest_ref, zrow_ref, zcnt_ref, end_ref, h_hbm, xd_hbm, hbuf, zbuf, zblk, lsem, ssem, zsem,
                     *, tm, nsteps, nrows):
    i = pl.program_id(0)
    slot = i % DISPATCH_SLOTS

    def load(step, slot_):
        return pltpu.make_async_copy(h_hbm.at[pl.ds(step * tm, tm)], hbuf.at[slot_], lsem.at[slot_])

    def rows_done(slot_):
        return pltpu.make_async_copy(hbuf.at[slot_], xd_hbm.at[pl.ds(0, tm)], ssem.at[slot_])

    def zero_group(e, g):
        row = pl.multiple_of(zrow_ref[e] + g * SUBLANES, SUBLANES)
        return pltpu.make_async_copy(zbuf, xd_hbm.at[pl.ds(row, SUBLANES)], zsem)

    @pl.when(i == 0)
    def _():
        load(0, 0).start()
        if nsteps > 1:
            load(1, 1).start()
        zbuf[...] = jnp.zeros(zbuf.shape, zbuf.dtype)

        def per_expert(fn):
            def body(e, carry):
                lax.fori_loop(0, zcnt_ref[e], lambda g, c: (fn(zero_group(e, g)), c)[1], 0)
                return carry
            lax.fori_loop(0, N_EXPERTS, body, 0)

        per_expert(lambda cp: cp.start())
        per_expert(lambda cp: cp.wait())

        zblk[...] = jnp.zeros(zblk.shape, zblk.dtype)
        ntail = (nrows - end_ref[0]) // MOE_BLK

        def tail_block(j):
            row = pl.multiple_of(end_ref[0] + j * MOE_BLK, MOE_BLK)
            return pltpu.make_async_copy(zblk, xd_hbm.at[pl.ds(row, MOE_BLK)], zsem)

        lax.fori_loop(0, ntail, lambda j, c: (tail_block(j).start(), c)[1], 0)
        lax.fori_loop(0, ntail, lambda j, c: (tail_block(j).wait(), c)[1], 0)

    load(i, slot).wait()

    def start(t, carry):
        for k in range(2):
            pltpu.make_async_copy(hbuf.at[slot, pl.ds(t, 1)], xd_hbm.at[pl.ds(dest_ref[2 * t + k], 1)],
                                  ssem.at[slot]).start(priority=k)
        return carry

    lax.fori_loop(0, tm, start, 0, unroll=DMA_UNROLL)

    @pl.when(i >= 1)
    def _():
        prev = (i + DISPATCH_SLOTS - 1) % DISPATCH_SLOTS
        rows_done(prev).wait()
        rows_done(prev).wait()

    @pl.when(i + 2 < nsteps)
    def _():
        load(i + 2, (i + 2) % DISPATCH_SLOTS).start()

    @pl.when(i == nsteps - 1)
    def _():
        rows_done(slot).wait()
        rows_done(slot).wait()


def moe_dispatch(dest_flat, zrow, zcnt, end, h, nrows, *, tm=1024):
    t, d = h.shape
    tm = min(tm, t)
    nsteps = t // tm
    kern = functools.partial(_dispatch_kernel, tm=tm, nsteps=nsteps, nrows=nrows)
    smem_all = pl.BlockSpec(memory_space=pltpu.SMEM)
    return pl.pallas_call(
        kern,
        grid=(nsteps,),
        in_specs=[pl.BlockSpec((2 * tm,), lambda i: (i,), memory_space=pltpu.SMEM), smem_all, smem_all, smem_all,
                  pl.BlockSpec(memory_space=pl.ANY)],
        out_specs=pl.BlockSpec(memory_space=pl.ANY),
        out_shape=jax.ShapeDtypeStruct((nrows, d), h.dtype),
        scratch_shapes=[pltpu.VMEM((DISPATCH_SLOTS, tm, d), h.dtype),
                        pltpu.VMEM((SUBLANES, d), h.dtype),
                        pltpu.VMEM((MOE_BLK, d), h.dtype),
                        pltpu.SemaphoreType.DMA((DISPATCH_SLOTS,)),
                        pltpu.SemaphoreType.DMA((DISPATCH_SLOTS,)),
                        pltpu.SemaphoreType.DMA],
        compiler_params=_cparams(("arbitrary",)),
        name="moe_dispatch",
    )(dest_flat, zrow, zcnt, end, h)


def _experts_kernel(be_ref, x_ref, wg_ref, wu_ref, wd_ref, o_ref, wg_s, wu_s, wd_s, *, nblk_pad):
    i = pl.program_id(0)
    nused = be_ref[nblk_pad - 1]

    @pl.when(i < nused)
    def _():
        prev = be_ref[jnp.maximum(i - 1, 0)]

        @pl.when((i == 0) | (be_ref[i] != prev))
        def _():
            wg_s[...] = wg_ref[0].astype(BF16)
            wu_s[...] = wu_ref[0].astype(BF16)
            wd_s[...] = wd_ref[0].astype(BF16)

        half = MOE_BLK // 2

        def unpack(w):
            hi = pltpu.bitcast(w & jnp.uint32(0xFFFF0000), F32)
            lo = pltpu.bitcast(w << 16, F32)
            return jnp.concatenate([hi, lo], axis=1).astype(BF16)

        xs = [unpack(x_ref[r * half:(r + 1) * half, :]) for r in range(2)]
        gs = [jnp.dot(x, wg_s[...], preferred_element_type=F32) for x in xs]
        us = [jnp.dot(x, wu_s[...], preferred_element_type=F32) for x in xs]
        acts = [(_silu(g) * u).astype(BF16) for g, u in zip(gs, us)]
        for r in range(2):
            o_ref[r * half:(r + 1) * half, :] = jnp.dot(acts[r], wd_s[...], preferred_element_type=F32)

    @pl.when(i >= nused)
    def _():
        o_ref[...] = jnp.zeros_like(o_ref)


def moe_experts(be_flat, xd, w_gate, w_up, w_down, e0, *, nblk, nblk_pad):
    d, f = w_gate.shape[1], w_gate.shape[2]
    kern = functools.partial(_experts_kernel, nblk_pad=nblk_pad)
    grid_spec = pltpu.PrefetchScalarGridSpec(
        num_scalar_prefetch=1,
        grid=(nblk,),
        in_specs=[pl.BlockSpec((MOE_BLK, d // 2), lambda i, be: (i, 0)),
                  pl.BlockSpec((1, d, f), lambda i, be: (e0 + be[i], 0, 0)),
                  pl.BlockSpec((1, d, f), lambda i, be: (e0 + be[i], 0, 0)),
                  pl.BlockSpec((1, f, d), lambda i, be: (e0 + be[i], 0, 0))],
        out_specs=pl.BlockSpec((MOE_BLK, d), lambda i, be: (i, 0)),
        scratch_shapes=[pltpu.VMEM((d, f), BF16), pltpu.VMEM((d, f), BF16), pltpu.VMEM((f, d), BF16)],
    )
    return pl.pallas_call(
        kern,
        grid_spec=grid_spec,
        out_shape=jax.ShapeDtypeStruct((xd.shape[0], d), F32),
        compiler_params=_cparams(("arbitrary",)),
        name="moe_experts",
    )(be_flat, xd, w_gate, w_up, w_down)


def _combine_kernel(dest_ref, dest_next_ref, x_ref, wts_ref, gf_ref, yd_hbm, o_ref, rbuf, sem,
                    *, tm, nsteps, final_norm):
    i = pl.program_id(0)
    slot = i % 2

    def gather(dref, slot_):
        def start(t, carry):
            for k in range(2):
                pltpu.make_async_copy(yd_hbm.at[pl.ds(dref[2 * t + k], 1)], rbuf.at[slot_, k, pl.ds(t, 1)],
                                      sem.at[slot_]).start(priority=k)
            return carry
        lax.fori_loop(0, tm, start, 0, unroll=DMA_UNROLL)

    @pl.when(i == 0)
    def _():
        gather(dest_ref, 0)

    @pl.when(i + 1 < nsteps)
    def _():
        gather(dest_next_ref, 1 - slot)

    for k in range(2):
        pltpu.make_async_copy(yd_hbm.at[pl.ds(0, tm)], rbuf.at[slot, k], sem.at[slot]).wait()
    w = wts_ref[...]
    y = x_ref[...] + rbuf[slot, 0] * w[:, 0:1] + rbuf[slot, 1] * w[:, 1:2]
    if final_norm:
        y = y * lax.rsqrt(jnp.mean(y * y, axis=-1, keepdims=True) + EPS) * gf_ref[...]
    o_ref[...] = y


def moe_combine(dest_flat, x, wts, yd, g_final, *, row0=0, rows=None, tm=1024):
    t, d = (rows or x.shape[0]), x.shape[1]
    tm = min(tm, t)
    nsteps = t // tm
    blk0 = row0 // tm
    final_norm = g_final is not None
    gf = g_final.reshape(1, d) if final_norm else jnp.ones((1, d), F32)
    kern = functools.partial(_combine_kernel, tm=tm, nsteps=nsteps, final_norm=final_norm)
    return pl.pallas_call(
        kern,
        grid=(nsteps,),
        in_specs=[pl.BlockSpec((2 * tm,), lambda i: (i + blk0,), memory_space=pltpu.SMEM),
                  pl.BlockSpec((2 * tm,), lambda i: (jnp.minimum(i + 1, nsteps - 1) + blk0,),
                               memory_space=pltpu.SMEM),
                  pl.BlockSpec((tm, d), lambda i: (i + blk0, 0)),
                  pl.BlockSpec((tm, LANES), lambda i: (i + blk0, 0)),
                  pl.BlockSpec((1, d), lambda i: (0, 0)),
                  pl.BlockSpec(memory_space=pl.ANY)],
        out_specs=pl.BlockSpec((tm, d), lambda i: (i, 0)),
        out_shape=jax.ShapeDtypeStruct((t, d), F32),
        scratch_shapes=[pltpu.VMEM((2, 2, tm, d), F32), pltpu.SemaphoreType.DMA((2,))],
        compiler_params=_cparams(("arbitrary",)),
        name="moe_combine",
    )(dest_flat, dest_flat, x, wts, gf, yd)


def moe_block(x, g, wr, br, w_gate, w_up, w_down, e0, g_final=None, splits=None):
    t, d = x.shape
    nblk = (2 * t) // MOE_BLK + N_EXPERTS
    nblk_pad = -(-(nblk + 1) // SUBLANES) * SUBLANES
    h, meta, wts, cnt = moe_route(x, g, wr, br)
    dest, ex, be = moe_plan(cnt, meta, nblk_pad=nblk_pad)
    dest_flat = dest[:, :2].reshape(2 * t)
    per_expert = lambda r: ex[r, R_E0:R_E0 + N_EXPERTS]
    end_last = ex[2, R_E0 + N_EXPERTS - 1:R_E0 + N_EXPERTS]
    xd = moe_dispatch(dest_flat, per_expert(0), per_expert(1), end_last, h, nblk * MOE_BLK)
    yd = moe_experts(be[:, 0], xd, w_gate, w_up, w_down, e0, nblk=nblk, nblk_pad=nblk_pad)
    if splits is None:
        return moe_combine(dest_flat, x, wts, yd, g_final)
    return [moe_combine(dest_flat, x, wts, yd, g_final, row0=r0, rows=n) for r0, n in splits]


def _pad_rows(buf):
    return jnp.pad(buf, ((0, 0), (0, SUBLANES - buf.shape[1]), (0, 0)))


def _forward(x_long, x_short, grp_long, grp_short, wts):
    depth = grp_long["mem_k"].shape[0]
    bl, ll = grp_long["batch"], grp_long["seqlen"]
    bs, ls = grp_short["batch"], grp_short["seqlen"]
    t_long, t_short = bl * ll, bs * ls
    d = x_long.shape[1]
    new = {id(grp_long): dict(gdn=[], conv=[], sc=[], ret=[]), id(grp_short): dict(gdn=[], conv=[], sc=[], ret=[])}

    def record_even(grp, s_new, cq, cs):
        rec = new[id(grp)]
        rec["gdn"].append(s_new)
        rec["conv"].append(cq.reshape(grp["batch"], SUBLANES, W_QKV_A)[:, :CONV_A - 1])
        rec["sc"].append(cs.reshape(grp["batch"], SUBLANES, D_B)[:, :CONV_B - 1])

    x_all = None
    for layer in range(depth):
        src_long = x_long if x_all is None else x_all
        src_short, row0 = (x_short, 0) if x_all is None else (x_all, t_long)
        if layer % 2 == 0:
            i = layer // 2
            hist = lambda grp: (_pad_rows(grp["conv"][i]), _pad_rows(grp["sc"][i]), wts["w_conv_qkv"][i],
                                wts["w_conv_sc"][i], wts["gdn_prm"][i], wts["gdn_norm"][i], grp["gdn"][i])
            p = rms_matmul(src_short, wts["norm_mix"][layer], wts["w_in_a"][i], tn=768, rows=t_short, row0=row0)
            mix, s_new, cq, cs = gdn_core(p, *hist(grp_short), batch=bs, seqlen=ls)
            xs = matmul_res(mix, wts["w_out_a"][i], src_short, res_row0=row0)
            record_even(grp_short, s_new, cq, cs)
            xl, s_new, cq, cs = gdn_block(src_long, wts["norm_mix"][layer], wts["w_in_a"][i], wts["w_out_a"][i],
                                          *hist(grp_long), batch=bl, seqlen=ll)
            record_even(grp_long, s_new, cq, cs)
        else:
            j = layer // 2
            p = rms_matmul(src_short, wts["norm_mix"][layer], wts["w_in_c"][j], tn=768, rows=t_short, row0=row0)
            ret, r_new = ret_core(p, grp_short["pos"], wts["ret_norm"][j], grp_short["ret"][j], batch=bs, seqlen=ls)
            xs = matmul_res(ret, wts["w_out_c"][j], src_short, res_row0=row0)
            new[id(grp_short)]["ret"].append(r_new)
            xl, r_new = ret_block(src_long, wts["norm_mix"][layer], wts["w_in_c"][j], wts["w_out_c"][j],
                                  grp_long["pos"], wts["ret_norm"][j], grp_long["ret"][j], batch=bl, seqlen=ll)
            new[id(grp_long)]["ret"].append(r_new)
        q = rms_matmul(xs, wts["norm_x"][layer], wts["w_xq"][layer], tn=D_MODEL)
        att = xattn_core(q, grp_short["mem_k"], grp_short["mem_v"], layer, batch=bs, seqlen=ls)
        joint = jnp.zeros((t_long + t_short, d), F32) if x_all is None else x_all
        joint = matmul_res(att, wts["w_xo"][layer], xs, into=joint, out_row0=t_long)
        joint = xattn_block(xl, wts["norm_x"][layer], wts["w_xq"][layer], wts["w_xo"][layer],
                            grp_long["mem_k"], grp_long["mem_v"], layer, joint, batch=bl, seqlen=ll)
        last = layer == depth - 1
        x_all = moe_block(joint, wts["norm_ffn"][layer], wts["w_route"][layer], wts["b_route"][layer],
                          wts["w_exp_gate"], wts["w_exp_up"], wts["w_exp_down"], layer * N_EXPERTS,
                          g_final=wts["norm_final"] if last else None,
                          splits=[(0, t_long), (t_long, t_short)] if last else None)
    y_long, y_short = x_all
    stack = lambda grp: tuple(jnp.stack(new[id(grp)][k]) for k in ("gdn", "conv", "sc", "ret"))
    return (y_long,) + stack(grp_long), (y_short,) + stack(grp_short)


def kernel(x_prompt, x_sample, state_gdn, state_gdn_conv, state_sconv, state_ret, cache_mem_k, cache_mem_v, mem_prompt, norm_mix, norm_x, norm_ffn, norm_final, norm_mem, w_in_a, w_conv_qkv, a_log, dt_bias, gdn_norm, w_conv_sc, w_out_a, w_in_c, ret_norm, w_out_c, w_xq, w_xk, w_xv, w_xo, w_group, b_group, w_router, b_router, w_exp_gate, w_exp_up, w_exp_down):
    bp, lp, d = x_prompt.shape
    bs, ls, _ = x_sample.shape
    depth = norm_mix.shape[0]
    n_even = w_in_a.shape[0]
    n_mem = mem_prompt.shape[1]

    qkv_w = 2 * H_A * DK_A + H_A * DV_A
    o_z = qkv_w
    o_b = o_z + H_A * DV_A
    o_a = o_b + H_A
    o_sc = o_a + H_A
    w_a = jnp.concatenate([w_in_a[:, :, :o_b], w_in_a[:, :, o_sc:], w_in_a[:, :, o_b:o_sc],
                           jnp.zeros((n_even, d, PA_COLS - PA_BA - 2 * H_A), F32)], axis=-1).astype(BF16)
    prm = jnp.zeros((n_even, SUBLANES, LANES), F32)
    prm = prm.at[:, 0, H_A:2 * H_A].set(a_log).at[:, 1, H_A:2 * H_A].set(dt_bias)
    w_route = jnp.concatenate([w_group, w_router, jnp.zeros((depth, d, LANES - N_GROUPS - N_EXPERTS), F32)],
                              axis=-1).astype(BF16)
    b_route = jnp.concatenate([b_group, b_router, jnp.zeros((depth, LANES - N_GROUPS - N_EXPERTS), F32)],
                              axis=-1).reshape(depth, 1, LANES)
    wts = dict(norm_mix=norm_mix, norm_x=norm_x, norm_ffn=norm_ffn, norm_final=norm_final,
               w_in_a=w_a, w_conv_qkv=w_conv_qkv, gdn_prm=prm, gdn_norm=gdn_norm, w_conv_sc=w_conv_sc,
               w_out_a=w_out_a.astype(BF16), w_in_c=w_in_c.astype(BF16), ret_norm=ret_norm,
               w_out_c=w_out_c.astype(BF16), w_xq=w_xq.astype(BF16), w_xo=w_xo.astype(BF16),
               w_route=w_route, b_route=b_route,
               w_exp_gate=w_exp_gate.reshape((depth * N_EXPERTS,) + w_exp_gate.shape[2:]),
               w_exp_up=w_exp_up.reshape((depth * N_EXPERTS,) + w_exp_up.shape[2:]),
               w_exp_down=w_exp_down.reshape((depth * N_EXPERTS,) + w_exp_down.shape[2:]))

    memf = mem_prompt.reshape(bp * n_mem, d)
    w_kv = jnp.concatenate([w_xk, w_xv], axis=-1).astype(BF16)
    mk_p, mv_p = kv_proj(memf, norm_mem, w_kv)
    p_cache_mem_k = mk_p.reshape(depth, bp, n_mem, H_X, HD_X)
    p_cache_mem_v = mv_p.reshape(depth, bp, n_mem, H_X, HD_X)

    n_odd = w_in_c.shape[0]
    z_gdn = jnp.zeros((n_even, bp, H_A, DK_A, DV_A), F32)
    z_conv = jnp.zeros((n_even, bp, CONV_A - 1, qkv_w), F32)
    z_sc = jnp.zeros((n_even, bp, CONV_B - 1, D_B), F32)
    z_ret = jnp.zeros((n_odd, bp, H_C, DK_C, DV_C), F32)
    pos_p = jnp.arange(lp, dtype=I32)
    pos_s = 16384 + jnp.arange(ls, dtype=I32)

    grp_p = dict(batch=bp, seqlen=lp, pos=pos_p, gdn=z_gdn, conv=z_conv, sc=z_sc, ret=z_ret,
                 mem_k=p_cache_mem_k, mem_v=p_cache_mem_v)
    grp_s = dict(batch=bs, seqlen=ls, pos=pos_s, gdn=state_gdn, conv=state_gdn_conv, sc=state_sconv,
                 ret=state_ret, mem_k=cache_mem_k, mem_v=cache_mem_v)
    (y_p, p_gdn, p_conv, p_sc, p_ret), (y_s, s_gdn, s_conv, s_sc, s_ret) = _forward(
        x_prompt.reshape(bp * lp, d), x_sample.reshape(bs * ls, d), grp_p, grp_s, wts)
    return (y_p.reshape(bp, lp, d), y_s.reshape(bs, ls, d), p_gdn, p_conv, p_sc, p_ret, p_cache_mem_k,
            p_cache_mem_v, s_gdn, s_conv, s_sc, s_ret)
```
